```python
import math
import jax, jax.numpy as jnp
from jax import lax
import numpy as np

D_MODEL = 1024
BATCH = 4
SEQ = 4096
DEPTH = 1

CHUNK = 64
Q_BLOCK = 128
HEAD_DIM = 64
N_RET_HEADS = 8
N_FOX_HEADS = 8
RET_WIDTH = N_RET_HEADS * HEAD_DIM
FOX_WIDTH = N_FOX_HEADS * HEAD_DIM
MIX_WIDTH = RET_WIDTH + FOX_WIDTH
N_MEM = 256
N_XATTN_HEADS = 4
XATTN_HEAD_DIM = D_MODEL // N_XATTN_HEADS
D_FF = -(-8 * D_MODEL // (3 * 256)) * 256
ROPE_BASE = 10000.0
EPS = 1e-6
NEG_INF = -1e30
IN_SIZES = (RET_WIDTH, RET_WIDTH, RET_WIDTH, RET_WIDTH, FOX_WIDTH, FOX_WIDTH, FOX_WIDTH, N_FOX_HEADS)
IN_WIDTH = sum(IN_SIZES)

kernel_name = "hymba_retention_fox_hybrid_block"


def _split_points(sizes):
    pts, acc = [], 0
    for s in sizes[:-1]:
        acc += s
        pts.append(acc)
    return pts


def rmsnorm(x, g):
    xf = x.astype(jnp.float32)
    y = xf * lax.rsqrt(jnp.mean(xf * xf, axis=-1, keepdims=True) + EPS)
    return (y * g.astype(jnp.float32)).astype(x.dtype)


def head_group_norm(x, g):
    xf = x.astype(jnp.float32)
    mu = jnp.mean(xf, axis=-1, keepdims=True)
    xc = xf - mu
    var = jnp.mean(xc * xc, axis=-1, keepdims=True)
    return (xc * lax.rsqrt(var + EPS) * g.astype(jnp.float32)).astype(x.dtype)


def rotary(x, pos):
    d = x.shape[-1]
    inv_freq = ROPE_BASE ** (-jnp.arange(0, d, 2, dtype=jnp.float32) / d)
    ang = pos[:, None] * inv_freq[None, :]
    cos = jnp.cos(ang)[:, None, :].astype(x.dtype)
    sin = jnp.sin(ang)[:, None, :].astype(x.dtype)
    x1, x2 = x[..., : d // 2], x[..., d // 2:]
    return jnp.concatenate([x1 * cos - x2 * sin, x1 * sin + x2 * cos], axis=-1)


def chunk_retention(q, k, v):
    B, T, H, d = q.shape
    dv = v.shape[-1]
    nc = T // CHUNK
    dt = q.dtype
    log_g = jnp.log(1.0 - 2.0 ** (-5.0 - jnp.arange(H, dtype=jnp.float32)))
    idx = jnp.arange(CHUNK, dtype=jnp.float32)
    intra_decay = jnp.exp(log_g[:, None, None] * jnp.abs(idx[:, None] - idx[None, :])).astype(dt)
    q_decay = jnp.exp(log_g[None, :] * (idx[:, None] + 1.0)).astype(dt)
    k_decay = jnp.exp(log_g[None, :] * (CHUNK - 1.0 - idx[:, None])).astype(dt)
    chunk_decay = jnp.exp(log_g * CHUNK).astype(dt)[:, None, None]

    qc = (q * (d ** -0.5)).reshape(B, nc, CHUNK, H, d)
    kc = k.reshape(B, nc, CHUNK, H, d)
    vc = v.reshape(B, nc, CHUNK, H, dv)

    scores = jnp.einsum('bnihd,bnjhd->bnhij', qc, kc) * intra_decay
    intra = jnp.einsum('bnhij,bnjhe->bnihe', scores, vc)

    kv = jnp.einsum('bnjhd,bnjhe->nbhde', kc * k_decay[:, :, None], vc)

    def step(state, kv_c):
        return state * chunk_decay + kv_c, state

    _, s_prev = lax.scan(step, jnp.zeros((B, H, d, dv), kv.dtype), kv)
    inter = jnp.einsum('bnihd,nbhde->bnihe', qc * q_decay[:, :, None], s_prev)
    return (intra + inter).reshape(B, T, H, dv)


def forgetting_attention(q, k, v, log_f):
    B, T, H, d = q.shape
    F = jnp.cumsum(log_f, axis=1).transpose(0, 2, 1)
    qh = (q * (d ** -0.5)).transpose(0, 2, 1, 3)
    kh = k.transpose(0, 2, 1, 3)
    vh = v.transpose(0, 2, 1, 3)
    qpos = jnp.arange(Q_BLOCK)
    outs = []
    for blk in range(T // Q_BLOCK):
        q0 = blk * Q_BLOCK
        kend = q0 + Q_BLOCK
        logits = jnp.einsum('bhqd,bhkd->bhqk', qh[:, :, q0:kend], kh[:, :, :kend]).astype(jnp.float32)
        logits = logits + F[:, :, q0:kend, None] - F[:, :, None, :kend]
        causal = (q0 + qpos)[:, None] >= jnp.arange(kend)[None, :]
        logits = jnp.where(causal, logits, NEG_INF)
        p = jax.nn.softmax(logits, axis=-1).astype(v.dtype)
        outs.append(jnp.einsum('bhqk,bhkd->bhqd', p, vh[:, :, :kend]))
    return jnp.concatenate(outs, axis=2).transpose(0, 2, 1, 3)


def memory_cross_attention(hn, mem, w_xq, w_xkv, g_mem, g_xq, g_xk, w_xo):
    B, T, _ = hn.shape
    M = mem.shape[1]
    q = (hn @ w_xq).reshape(B, T, N_XATTN_HEADS, XATTN_HEAD_DIM)
    q = rmsnorm(q, g_xq)
    kv = rmsnorm(mem, g_mem) @ w_xkv
    k, v = jnp.split(kv, 2, axis=-1)
    k = rmsnorm(k.reshape(B, M, N_XATTN_HEADS, XATTN_HEAD_DIM), g_xk)
    v = v.reshape(B, M, N_XATTN_HEADS, XATTN_HEAD_DIM)
    logits = jnp.einsum('bthd,bmhd->bhtm', q, k).astype(jnp.float32) * (XATTN_HEAD_DIM ** -0.5)
    p = jax.nn.softmax(logits, axis=-1).astype(v.dtype)
    o = jnp.einsum('bhtm,bmhd->bthd', p, v).reshape(B, T, D_MODEL)
    return o @ w_xo


def setup_inputs(seed: int = 0) -> dict:
    key = jax.random.key(seed)
    ks = jax.random.split(key, 24)
    f32 = jnp.float32
    D = D_MODEL

    def w(k, shape, fan_in):
        return jax.random.normal(k, shape, f32) * fan_in ** -0.5

    def gain(k, shape):
        return 1.0 + 0.02 * jax.random.normal(k, shape, f32)

    return {
        "x": jax.random.normal(ks[0], (BATCH, SEQ, D), f32),
        "mem": jax.random.normal(ks[1], (BATCH, N_MEM, D), f32),
        "g_mix": gain(ks[2], (DEPTH, D)),
        "w_in": w(ks[3], (DEPTH, D, IN_WIDTH), D),
        "b_forget": jax.random.uniform(ks[4], (DEPTH, N_FOX_HEADS), f32, 1.0, 4.0),
        "g_ret_out": gain(ks[5], (DEPTH, N_RET_HEADS, HEAD_DIM)),
        "g_fox_q": gain(ks[6], (DEPTH, HEAD_DIM)),
        "g_fox_k": gain(ks[7], (DEPTH, HEAD_DIM)),
        "w_out": w(ks[8], (DEPTH, MIX_WIDTH, D), MIX_WIDTH),
        "g_xattn": gain(ks[9], (DEPTH, D)),
        "w_xq": w(ks[10], (DEPTH, D, D), D),
        "w_xkv": w(ks[11], (DEPTH, D, 2 * D), D),
        "g_mem": gain(ks[12], (DEPTH, D)),
        "g_xq": gain(ks[13], (DEPTH, XATTN_HEAD_DIM)),
        "g_xk": gain(ks[14], (DEPTH, XATTN_HEAD_DIM)),
        "w_xo": w(ks[15], (DEPTH, D, D), D),
        "g_ffn": gain(ks[16], (DEPTH, D)),
        "w_gate": w(ks[17], (DEPTH, D, D_FF), D),
        "w_up": w(ks[18], (DEPTH, D, D_FF), D),
        "w_down": w(ks[19], (DEPTH, D_FF, D), D_FF),
    }


def reference(x, mem, g_mix, w_in, b_forget, g_ret_out, g_fox_q, g_fox_k, w_out,
              g_xattn, w_xq, w_xkv, g_mem, g_xq, g_xk, w_xo,
              g_ffn, w_gate, w_up, w_down):
    B, T, _ = x.shape
    pos = jnp.arange(T, dtype=jnp.float32)
    splits = _split_points(IN_SIZES)
    h = x
    for l in range(DEPTH):
        hn = rmsnorm(h, g_mix[l])
        proj = hn @ w_in[l]
        rq, rk, rv, rg, fq, fk, fv, ff = jnp.split(proj, splits, axis=-1)

        rq = rotary(rq.reshape(B, T, N_RET_HEADS, HEAD_DIM), pos)
        rk = rotary(rk.reshape(B, T, N_RET_HEADS, HEAD_DIM), pos)
        ret = chunk_retention(rq, rk, rv.reshape(B, T, N_RET_HEADS, HEAD_DIM))
        ret = head_group_norm(ret, g_ret_out[l]).reshape(B, T, RET_WIDTH)
        ret = jax.nn.silu(rg) * ret

        fq = rmsnorm(fq.reshape(B, T, N_FOX_HEADS, HEAD_DIM), g_fox_q[l])
        fk = rmsnorm(fk.reshape(B, T, N_FOX_HEADS, HEAD_DIM), g_fox_k[l])
        log_f = jax.nn.log_sigmoid(ff.astype(jnp.float32) + b_forget[l].astype(jnp.float32))
        fox = forgetting_attention(fq, fk, fv.reshape(B, T, N_FOX_HEADS, HEAD_DIM), log_f)
        fox = fox.reshape(B, T, FOX_WIDTH)

        h = h + jnp.concatenate([ret, fox], axis=-1) @ w_out[l]

        h = h + memory_cross_attention(rmsnorm(h, g_xattn[l]), mem, w_xq[l], w_xkv[l],
                                       g_mem[l], g_xq[l], g_xk[l], w_xo[l])

        hn = rmsnorm(h, g_ffn[l])
        h = h + (jax.nn.silu(hn @ w_gate[l]) * (hn @ w_up[l])) @ w_down[l]
    return h
```

```python
import functools

import numpy as np
import jax
import jax.numpy as jnp
from jax import lax
from jax.experimental import pallas as pl
from jax.experimental.pallas import tpu as pltpu

_BF16 = jnp.bfloat16
_F32 = jnp.float32

_EPS = 1e-6
_NEG_INF = -1e30
_ROPE_BASE = 10000.0
_HEAD_DIM = 64
_N_HEADS = 8
_RET_CHUNK = 64
_N_XHEADS = 4
_LANES = 128
_VMEM_LIMIT = 56 * 1024 * 1024

_AUG_HI, _AUG_MID, _AUG_LO = 64, 65, 66
_AUG_KHI, _AUG_KMID, _AUG_KLO = 67, 68, 69
_AUG_ONE = 64


def _params(sem):
    return pltpu.CompilerParams(dimension_semantics=sem, vmem_limit_bytes=_VMEM_LIMIT)


def _const_spec(shape):
    nd = len(shape)
    return pl.BlockSpec(shape, lambda *_: (0,) * nd, pipeline_mode=pl.Buffered(1))


def _rms(x, g):
    return x * lax.rsqrt(jnp.mean(x * x, axis=-1, keepdims=True) + _EPS) * g


def _split3(v):
    hi = v.astype(_BF16).astype(_F32)
    r = v - hi
    mid = r.astype(_BF16).astype(_F32)
    return hi, mid, r - mid


def _in_proj_body(x_ref, g_ref, w_ref, wff_ref, bf_ref, cos_ref, sin_ref, gq_ref, gk_ref, tri_ref,
                  rq_ref, rk_ref, rv_ref, gate_ref, qa_ref, ka_ref, va_ref, carry_ref, *, tm, width):
    i = pl.program_id(1)
    hb = _rms(x_ref[0], g_ref[...]).astype(_BF16)

    def proj(j):
        return jnp.dot(hb, w_ref[:, j * width:(j + 1) * width], preferred_element_type=_F32)

    lane = lax.broadcasted_iota(jnp.int32, (tm, _LANES), 1)
    low = lane < _HEAD_DIM
    first_half = (lane & (_HEAD_DIM // 2)) == 0
    n_pairs = width // _LANES

    cos = cos_ref[...]
    sin = sin_ref[...]
    for j, out_ref, scale in ((0, rq_ref, _HEAD_DIM ** -0.5), (1, rk_ref, None)):
        y = proj(j)
        for c in range(n_pairs):
            blk = y[:, c * _LANES:(c + 1) * _LANES]
            swapped = jnp.where(first_half, pltpu.roll(blk, _LANES - _HEAD_DIM // 2, 1),
                                pltpu.roll(blk, _HEAD_DIM // 2, 1))
            r = blk * cos + swapped * sin
            if scale is not None:
                r = r * scale
            out_ref[0, :, c * _LANES:(c + 1) * _LANES] = r.astype(_BF16)
    rv_ref[0] = proj(2).astype(_BF16)
    gate = proj(3)
    gate_ref[0] = (gate * jax.nn.sigmoid(gate)).astype(_BF16)

    z = jnp.dot(hb, wff_ref[...], preferred_element_type=_F32) + bf_ref[...]
    logf = jnp.minimum(z, 0.0) - jnp.log1p(jnp.exp(-jnp.abs(z)))
    hi, mid, lo = _split3(logf)
    tri = tri_ref[...]
    csum = (jnp.dot(tri, hi.astype(_BF16), preferred_element_type=_F32)
            + jnp.dot(tri, mid.astype(_BF16), preferred_element_type=_F32)
            + jnp.dot(tri, lo.astype(_BF16), preferred_element_type=_F32))

    @pl.when(i == 0)
    def _():
        carry_ref[...] = jnp.zeros_like(carry_ref)

    fcum = csum + carry_ref[0:1, :]
    carry_ref[...] = jnp.broadcast_to(fcum[tm - 1:tm, :], carry_ref.shape)

    fq = proj(4)
    fk = proj(5)
    fv = proj(6)
    inv_d = 1.0 / _HEAD_DIM
    for c in range(n_pairs):
        normed = []
        for src, gref in ((fq, gq_ref), (fk, gk_ref)):
            blk = src[:, c * _LANES:(c + 1) * _LANES]
            sq = blk * blk
            ms0 = jnp.sum(jnp.where(low, sq, 0.0), axis=-1, keepdims=True) * inv_d
            ms1 = jnp.sum(jnp.where(low, 0.0, sq), axis=-1, keepdims=True) * inv_d
            inv = jnp.where(low, lax.rsqrt(ms0 + _EPS), lax.rsqrt(ms1 + _EPS))
            normed.append(blk * inv * gref[...])
        vblk = fv[:, c * _LANES:(c + 1) * _LANES]
        for hh in range(2):
            h = 2 * c + hh
            qb, kb, vb = normed[0], normed[1], vblk
            if hh == 1:
                qb = pltpu.roll(qb, _HEAD_DIM, 1)
                kb = pltpu.roll(kb, _HEAD_DIM, 1)
                vb = pltpu.roll(vb, _HEAD_DIM, 1)
            fh, fm, fl = _split3(jnp.broadcast_to(fcum[:, h:h + 1], (tm, _LANES)))
            q_ext = jnp.where(lane == _AUG_HI, fh, jnp.where(lane == _AUG_MID, fm, jnp.where(
                lane == _AUG_LO, fl, jnp.where(lane <= _AUG_KLO, 1.0, 0.0))))
            k_ext = jnp.where(lane < _AUG_KHI, 1.0, jnp.where(lane == _AUG_KHI, -fh, jnp.where(
                lane == _AUG_KMID, -fm, jnp.where(lane == _AUG_KLO, -fl, 0.0))))
            v_ext = jnp.where(lane == _AUG_ONE, 1.0, 0.0)
            qa_ref[0, h] = jnp.where(low, qb * (_HEAD_DIM ** -0.5), q_ext).astype(_BF16)
            ka_ref[0, h] = jnp.where(low, kb, k_ext).astype(_BF16)
            va_ref[0, h] = jnp.where(low, vb, v_ext).astype(_BF16)


def _in_proj(x, g_mix, w_main, w_ff, b_ff, cos_t, sin_t, gq, gk, tri, *, tm):
    B, T, D = x.shape
    width = _N_HEADS * _HEAD_DIM
    tok = lambda b, i: (b, i, 0)
    head = lambda b, i: (b, 0, i, 0)
    bf_tok = jax.ShapeDtypeStruct((B, T, width), _BF16)
    bf_head = jax.ShapeDtypeStruct((B, _N_HEADS, T, _LANES), _BF16)
    return pl.pallas_call(
        functools.partial(_in_proj_body, tm=tm, width=width),
        grid=(B, T // tm),
        in_specs=[
            pl.BlockSpec((1, tm, D), tok),
            _const_spec((1, D)),
            _const_spec(w_main.shape),
            _const_spec(w_ff.shape),
            _const_spec((1, _LANES)),
            pl.BlockSpec((tm, _LANES), lambda b, i: (i, 0)),
            pl.BlockSpec((tm, _LANES), lambda b, i: (i, 0)),
            _const_spec((1, _LANES)),
            _const_spec((1, _LANES)),
            _const_spec((tm, tm)),
        ],
        out_specs=[pl.BlockSpec((1, tm, width), tok)] * 4
        + [pl.BlockSpec((1, _N_HEADS, tm, _LANES), head)] * 3,
        out_shape=[bf_tok] * 4 + [bf_head] * 3,
        scratch_shapes=[pltpu.VMEM((8, _LANES), _F32)],
        compiler_params=_params(("arbitrary", "arbitrary")),
        name="in_proj",
    )(x, g_mix, w_main, w_ff, b_ff, cos_t, sin_t, gq, gk, tri)


def _retention_body(rq_ref, rk_ref, rv_ref, gate_ref, dmask_ref, qdec_ref, kdec_ref, bdec_ref, g_ref,
                    o_ref, state_ref):
    i = pl.program_id(1)

    @pl.when(i == 0)
    def _():
        state_ref[...] = jnp.zeros_like(state_ref)

    nt = (((1,), (1,)), ((), ()))
    tn = (((0,), (0,)), ((), ()))
    for h in range(_N_HEADS):
        sl = slice(h * _HEAD_DIM, (h + 1) * _HEAD_DIM)
        qh = rq_ref[0, :, sl]
        kh = rk_ref[0, :, sl]
        vh = rv_ref[0, :, sl]
        scores = lax.dot_general(qh, kh, nt, preferred_element_type=_F32) * dmask_ref[h]
        intra = jnp.dot(scores.astype(_BF16), vh, preferred_element_type=_F32)
        state = state_ref[h]
        inter = jnp.dot(qh, state.astype(_BF16), preferred_element_type=_F32) * qdec_ref[:, sl]
        kd = (kh.astype(_F32) * kdec_ref[:, sl]).astype(_BF16)
        state_ref[h] = state * bdec_ref[h] + lax.dot_general(kd, vh, tn, preferred_element_type=_F32)
        o = intra + inter
        mu = jnp.mean(o, axis=-1, keepdims=True)
        oc = o - mu
        var = jnp.mean(oc * oc, axis=-1, keepdims=True)
        y = oc * lax.rsqrt(var + _EPS) * g_ref[:, sl]
        o_ref[0, :, sl] = (y * gate_ref[0, :, sl].astype(_F32)).astype(_BF16)


def _retention(rq, rk, rv, gate, dmask, qdec, kdec, bdec, g_ret, *, tb):
    B, T, W = rq.shape
    tok = pl.BlockSpec((1, tb, W), lambda b, i: (b, i, 0))
    return pl.pallas_call(
        _retention_body,
        grid=(B, T // tb),
        in_specs=[tok, tok, tok, tok, _const_spec(dmask.shape), _const_spec(qdec.shape),
                  _const_spec(kdec.shape), _const_spec(bdec.shape), _const_spec(g_ret.shape)],
        out_specs=tok,
        out_shape=jax.ShapeDtypeStruct((B, T, W), _BF16),
        scratch_shapes=[pltpu.VMEM((_N_HEADS, _HEAD_DIM, _HEAD_DIM), _F32)],
        compiler_params=_params(("arbitrary", "arbitrary")),
        name="retention",
    )(rq, rk, rv, gate, dmask, qdec, kdec, bdec, g_ret)


def _fox_body(qa_ref, ka_ref, va_ref, o_ref, m_ref, acc_ref, *, tq):
    qi = pl.program_id(2)
    nt = (((1,), (1,)), ((), ()))
    row = lax.broadcasted_iota(jnp.int32, (tq, tq), 0)
    col = lax.broadcasted_iota(jnp.int32, (tq, tq), 1)
    causal = row >= col
    outs = []
    for hh in range(2):
        q = qa_ref[0, hh]
        m_ref[...] = jnp.full_like(m_ref, _NEG_INF)
        acc_ref[...] = jnp.zeros_like(acc_ref)

        def step(j, masked):
            start = pl.multiple_of(j * tq, tq)
            k = ka_ref[0, hh, pl.ds(start, tq), :]
            v = va_ref[0, hh, pl.ds(start, tq), :]
            s = lax.dot_general(q, k, nt, preferred_element_type=_F32)
            if masked:
                s = jnp.where(causal, s, _NEG_INF)
            m_old = m_ref[...]
            m_new = jnp.maximum(m_old, jnp.max(s, axis=-1, keepdims=True))
            p = jnp.exp(s - m_new[:, 0:1])
            acc_ref[...] = jnp.exp(m_old - m_new) * acc_ref[...] + jnp.dot(
                p.astype(_BF16), v, preferred_element_type=_F32)
            m_ref[...] = m_new

        def off_diag(j, carry):
            step(j, False)
            return carry

        lax.fori_loop(0, qi, off_diag, 0)
        step(qi, True)
        acc = acc_ref[...]
        outs.append(acc / acc[:, _AUG_ONE:_AUG_ONE + 1])
    lane = lax.broadcasted_iota(jnp.int32, (tq, _LANES), 1)
    o_ref[0] = jnp.where(lane < _HEAD_DIM, outs[0], pltpu.roll(outs[1], _HEAD_DIM, 1)).astype(_BF16)


def _fox(qa, ka, va, *, tq):
    B, H, T, L = qa.shape
    return pl.pallas_call(
        functools.partial(_fox_body, tq=tq),
        grid=(B, H // 2, T // tq),
        in_specs=[
            pl.BlockSpec((1, 2, tq, L), lambda b, p, i: (b, p, i, 0)),
            pl.BlockSpec((1, 2, T, L), lambda b, p, i: (b, p, 0, 0)),
            pl.BlockSpec((1, 2, T, L), lambda b, p, i: (b, p, 0, 0)),
        ],
        out_specs=pl.BlockSpec((1, tq, L), lambda b, p, i: (b, i, p)),
        out_shape=jax.ShapeDtypeStruct((B, T, H * _HEAD_DIM), _BF16),
        scratch_shapes=[pltpu.VMEM((tq, L), _F32), pltpu.VMEM((tq, L), _F32)],
        compiler_params=_params(("arbitrary", "arbitrary", "arbitrary")),
        name="fox",
    )(qa, ka, va)


def _mem_kv_body(mem_ref, gm_ref, w_ref, gk_ref, k_ref, v_ref, *, d_model, xd):
    mn = _rms(mem_ref[0], gm_ref[...]).astype(_BF16)
    kv = jnp.dot(mn, w_ref[...], preferred_element_type=_F32)
    for h in range(_N_XHEADS):
        sl = slice(h * xd, (h + 1) * xd)
        k_ref[0, :, sl] = (_rms(kv[:, sl], gk_ref[...]) * (xd ** -0.5)).astype(_BF16)
    v_ref[0] = kv[:, d_model:].astype(_BF16)


def _mem_kv(mem, g_mem, w_xkv, g_xk):
    B, M, D = mem.shape
    xd = D // _N_XHEADS
    blk = pl.BlockSpec((1, M, D), lambda b: (b, 0, 0))
    out = jax.ShapeDtypeStruct((B, M, D), _BF16)
    return pl.pallas_call(
        functools.partial(_mem_kv_body, d_model=D, xd=xd),
        grid=(B,),
        in_specs=[blk, _const_spec((1, D)), _const_spec(w_xkv.shape), _const_spec((1, xd))],
        out_specs=[blk, blk],
        out_shape=[out, out],
        compiler_params=_params(("arbitrary",)),
        name="mem_kv",
    )(mem, g_mem, w_xkv, g_xk)


def _mix_xattn_body(x_ref, ret_ref, fox_ref, wo_ref, gx_ref, wq_ref, gq_ref, k_ref, v_ref, wxo_ref,
                    o_ref, *, width, xd):
    h1 = (x_ref[0]
          + jnp.dot(ret_ref[0], wo_ref[:width, :], preferred_element_type=_F32)
          + jnp.dot(fox_ref[0], wo_ref[width:, :], preferred_element_type=_F32))
    hn = _rms(h1, gx_ref[...]).astype(_BF16)
    q = jnp.dot(hn, wq_ref[...], preferred_element_type=_F32)
    nt = (((1,), (1,)), ((), ()))
    heads = []
    for h in range(_N_XHEADS):
        sl = slice(h * xd, (h + 1) * xd)
        qn = _rms(q[:, sl], gq_ref[...]).astype(_BF16)
        logits = lax.dot_general(qn, k_ref[0, :, sl], nt, preferred_element_type=_F32)
        p = jnp.exp(logits - jnp.max(logits, axis=-1, keepdims=True))
        p = p / jnp.sum(p, axis=-1, keepdims=True)
        heads.append(jnp.dot(p.astype(_BF16), v_ref[0, :, sl], preferred_element_type=_F32).astype(_BF16))
    o = jnp.concatenate(heads, axis=-1)
    o_ref[0] = h1 + jnp.dot(o, wxo_ref[...], preferred_element_type=_F32)


def _mix_xattn(x, ret, fox, w_out, g_xattn, w_xq, g_xq, k, v, w_xo, *, tm):
    B, T, D = x.shape
    W = ret.shape[-1]
    M = k.shape[1]
    xd = D // _N_XHEADS
    tok = lambda b, i: (b, i, 0)
    return pl.pallas_call(
        functools.partial(_mix_xattn_body, width=W, xd=xd),
        grid=(B, T // tm),
        in_specs=[
            pl.BlockSpec((1, tm, D), tok),
            pl.BlockSpec((1, tm, W), tok),
            pl.BlockSpec((1, tm, W), tok),
            _const_spec(w_out.shape),
            _const_spec((1, D)),
            _const_spec(w_xq.shape),
            _const_spec((1, xd)),
            pl.BlockSpec((1, M, D), lambda b, i: (b, 0, 0)),
            pl.BlockSpec((1, M, D), lambda b, i: (b, 0, 0)),
            _const_spec(w_xo.shape),
        ],
        out_specs=pl.BlockSpec((1, tm, D), tok),
        out_shape=jax.ShapeDtypeStruct((B, T, D), _F32),
        compiler_params=_params(("arbitrary", "arbitrary")),
        name="mix_xattn",
    )(x, ret, fox, w_out, g_xattn, w_xq, g_xq, k, v, w_xo)


def _ffn_body(h_ref, g_ref, wg_ref, wu_ref, wd_ref, o_ref, *, n_split, fs):
    h = h_ref[0]
    hn = _rms(h, g_ref[...]).astype(_BF16)
    acc = h
    for c in range(n_split):
        sl = slice(c * fs, (c + 1) * fs)
        gate = jnp.dot(hn, wg_ref[:, sl], preferred_element_type=_F32)
        up = jnp.dot(hn, wu_ref[:, sl], preferred_element_type=_F32)
        a = (gate * jax.nn.sigmoid(gate) * up).astype(_BF16)
        acc = acc + jnp.dot(a, wd_ref[sl, :], preferred_element_type=_F32)
    o_ref[0] = acc


def _ffn(h, g_ffn, w_gate, w_up, w_down, *, tm, n_split):
    B, T, D = h.shape
    F = w_gate.shape[1]
    tok = pl.BlockSpec((1, tm, D), lambda b, i: (b, i, 0))
    return pl.pallas_call(
        functools.partial(_ffn_body, n_split=n_split, fs=F // n_split),
        grid=(B, T // tm),
        in_specs=[tok, _const_spec((1, D)), _const_spec(w_gate.shape), _const_spec(w_up.shape),
                  _const_spec(w_down.shape)],
        out_specs=tok,
        out_shape=jax.ShapeDtypeStruct((B, T, D), _F32),
        compiler_params=_params(("arbitrary", "arbitrary")),
        name="ffn",
    )(h, g_ffn, w_gate, w_up, w_down)


def _rope_tables(T):
    half = _HEAD_DIM // 2
    inv_freq = (_ROPE_BASE ** (-np.arange(0, _HEAD_DIM, 2, dtype=np.float32) / _HEAD_DIM)).astype(np.float32)
    ang = (np.arange(T, dtype=np.float32)[:, None] * inv_freq[None, :]).astype(np.float32).astype(np.float64)
    cos, sin = np.cos(ang), np.sin(ang)
    reps = _LANES // _HEAD_DIM
    cos_t = np.tile(np.concatenate([cos, cos], axis=1), (1, reps))
    sin_t = np.tile(np.concatenate([-sin, sin], axis=1), (1, reps))
    return jnp.asarray(cos_t, _F32), jnp.asarray(sin_t, _F32)


def _retention_tables(tb):
    log_g = np.log(1.0 - 2.0 ** (-5.0 - np.arange(_N_HEADS, dtype=np.float64)))
    idx = np.arange(tb, dtype=np.float64)
    dist = np.abs(idx[:, None] - idx[None, :])
    chunk = np.arange(tb) // _RET_CHUNK
    visible = chunk[None, :] <= chunk[:, None]
    dmask = np.where(visible[None], np.exp(log_g[:, None, None] * dist[None]), 0.0)
    qdec = np.repeat(np.exp(log_g[None, :] * (idx[:, None] + 1.0)), _HEAD_DIM, axis=1)
    kdec = np.repeat(np.exp(log_g[None, :] * (tb - 1.0 - idx[:, None])), _HEAD_DIM, axis=1)
    bdec = np.broadcast_to(np.exp(log_g * tb)[:, None, None], (_N_HEADS, _HEAD_DIM, _HEAD_DIM))
    f = lambda a: jnp.asarray(a, _F32)
    return f(dmask), f(qdec), f(kdec), f(bdec)


def _pad_lanes(a):
    return jnp.pad(a, [(0, 0)] * (a.ndim - 1) + [(0, _LANES - a.shape[-1])])


def kernel(x, mem, g_mix, w_in, b_forget, g_ret_out, g_fox_q, g_fox_k, w_out, g_xattn, w_xq, w_xkv,
           g_mem, g_xq, g_xk, w_xo, g_ffn, w_gate, w_up, w_down):
    B, T, D = x.shape
    width = _N_HEADS * _HEAD_DIM
    tm_in, tb, tq, tm_mix, tm_ffn = 512, 256, 512, 512, 512
    cos_t, sin_t = _rope_tables(T)
    dmask, qdec, kdec, bdec = _retention_tables(tb)
    tri = jnp.asarray(np.tril(np.ones((tm_in, tm_in), np.float32)), _BF16)
    reps = _LANES // _HEAD_DIM
    row = lambda a: a.reshape(1, -1).astype(_F32)

    h = x
    for l in range(w_in.shape[0]):
        w_main = w_in[l, :, :7 * width].astype(_BF16)
        w_ff = _pad_lanes(w_in[l, :, 7 * width:]).astype(_BF16)
        b_ff = _pad_lanes(row(b_forget[l]))
        gq = jnp.tile(row(g_fox_q[l]), (1, reps))
        gk = jnp.tile(row(g_fox_k[l]), (1, reps))
        rq, rk, rv, gate, qa, ka, va = _in_proj(h, row(g_mix[l]), w_main, w_ff, b_ff, cos_t, sin_t,
                                                gq, gk, tri, tm=tm_in)
        ret = _retention(rq, rk, rv, gate, dmask, qdec, kdec, bdec, row(g_ret_out[l]), tb=tb)
        fox = _fox(qa, ka, va, tq=tq)
        k, v = _mem_kv(mem, row(g_mem[l]), w_xkv[l].astype(_BF16), row(g_xk[l]))
        h = _mix_xattn(h, ret, fox, w_out[l].astype(_BF16), row(g_xattn[l]), w_xq[l].astype(_BF16),
                       row(g_xq[l]), k, v, w_xo[l].astype(_BF16), tm=tm_mix)
        h = _ffn(h, row(g_ffn[l]), w_gate[l].astype(_BF16), w_up[l].astype(_BF16),
                 w_down[l].astype(_BF16), tm=tm_ffn, n_split=2)
    return h
```

```python
import functools

import numpy as np
import jax
import jax.numpy as jnp
from jax import lax
from jax.experimental import pallas as pl
from jax.experimental.pallas import tpu as pltpu

_BF16 = jnp.bfloat16
_F32 = jnp.float32

_EPS = 1e-6
_NEG_INF = -1e30
_ROPE_BASE = 10000.0
_HEAD_DIM = 64
_N_HEADS = 8
_RET_CHUNK = 64
_N_XHEADS = 4
_LANES = 128
_VMEM_LIMIT = 56 * 1024 * 1024

_AUG_HI, _AUG_MID, _AUG_LO = 64, 65, 66
_AUG_KHI, _AUG_KMID, _AUG_KLO = 67, 68, 69
_AUG_SHIFT = 70
_AUG_ONE = 64

_LOG2E = 1.4426950408889634
_MAX_FIXED_SHIFT = 32.0
_NORM_ROUNDING_SLACK = 1.01


def _params(sem):
    return pltpu.CompilerParams(dimension_semantics=sem, vmem_limit_bytes=_VMEM_LIMIT)


def _const_spec(shape):
    nd = len(shape)
    return pl.BlockSpec(shape, lambda *_: (0,) * nd, pipeline_mode=pl.Buffered(1))


def _rms(x, g):
    return x * lax.rsqrt(jnp.mean(x * x, axis=-1, keepdims=True) + _EPS) * g


def _split3(v):
    hi = v.astype(_BF16).astype(_F32)
    r = v - hi
    mid = r.astype(_BF16).astype(_F32)
    return hi, mid, r - mid


def _in_proj_body(x_ref, g_ref, w_ref, wff_ref, bf_ref, cos_ref, sin_ref, gq_ref, gk_ref, shift_ref,
                  tri_ref, rq_ref, rk_ref, rv_ref, gate_ref, qa_ref, ka_ref, va_ref, carry_ref,
                  *, tm, width):
    i = pl.program_id(1)
    hb = _rms(x_ref[0], g_ref[...]).astype(_BF16)

    def proj(j):
        return jnp.dot(hb, w_ref[:, j * width:(j + 1) * width], preferred_element_type=_F32)

    lane = lax.broadcasted_iota(jnp.int32, (tm, _LANES), 1)
    low = lane < _HEAD_DIM
    first_half = (lane & (_HEAD_DIM // 2)) == 0
    n_pairs = width // _LANES

    cos = cos_ref[...]
    sin = sin_ref[...]
    for j, out_ref, scale in ((0, rq_ref, _HEAD_DIM ** -0.5), (1, rk_ref, None)):
        y = proj(j)
        for c in range(n_pairs):
            blk = y[:, c * _LANES:(c + 1) * _LANES]
            swapped = jnp.where(first_half, pltpu.roll(blk, _LANES - _HEAD_DIM // 2, 1),
                                pltpu.roll(blk, _HEAD_DIM // 2, 1))
            r = blk * cos + swapped * sin
            if scale is not None:
                r = r * scale
            out_ref[0, :, c * _LANES:(c + 1) * _LANES] = r.astype(_BF16)
    rv_ref[0] = proj(2).astype(_BF16)
    gate = proj(3)
    gate_ref[0] = (gate * jax.nn.sigmoid(gate)).astype(_BF16)

    z = jnp.dot(hb, wff_ref[...], preferred_element_type=_F32) + bf_ref[...]
    logf = (jnp.minimum(z, 0.0) - jnp.log1p(jnp.exp(-jnp.abs(z)))) * _LOG2E
    hi, mid, lo = _split3(logf)
    tri = tri_ref[...]
    csum = (jnp.dot(tri, hi.astype(_BF16), preferred_element_type=_F32)
            + jnp.dot(tri, mid.astype(_BF16), preferred_element_type=_F32)
            + jnp.dot(tri, lo.astype(_BF16), preferred_element_type=_F32))

    @pl.when(i == 0)
    def _():
        carry_ref[...] = jnp.zeros_like(carry_ref)

    fcum = csum + carry_ref[0:1, :]
    carry_ref[...] = jnp.broadcast_to(fcum[tm - 1:tm, :], carry_ref.shape)

    neg_shift = -shift_ref[...]
    fq = proj(4)
    fk = proj(5)
    fv = proj(6)
    inv_d = 1.0 / _HEAD_DIM
    for c in range(n_pairs):
        normed = []
        for src, gref in ((fq, gq_ref), (fk, gk_ref)):
            blk = src[:, c * _LANES:(c + 1) * _LANES]
            sq = blk * blk
            ms0 = jnp.sum(jnp.where(low, sq, 0.0), axis=-1, keepdims=True) * inv_d
            ms1 = jnp.sum(jnp.where(low, 0.0, sq), axis=-1, keepdims=True) * inv_d
            inv = jnp.where(low, lax.rsqrt(ms0 + _EPS), lax.rsqrt(ms1 + _EPS))
            normed.append(blk * inv * gref[...])
        vblk = fv[:, c * _LANES:(c + 1) * _LANES]
        for hh in range(2):
            h = 2 * c + hh
            qb, kb, vb = normed[0], normed[1], vblk
            if hh == 1:
                qb = pltpu.roll(qb, _HEAD_DIM, 1)
                kb = pltpu.roll(kb, _HEAD_DIM, 1)
                vb = pltpu.roll(vb, _HEAD_DIM, 1)
            fh, fm, fl = _split3(jnp.broadcast_to(fcum[:, h:h + 1], (tm, _LANES)))
            q_ext = jnp.where(lane == _AUG_HI, fh, jnp.where(lane == _AUG_MID, fm, jnp.where(
                lane == _AUG_LO, fl, jnp.where(lane <= _AUG_KLO, 1.0, jnp.where(
                    lane == _AUG_SHIFT, neg_shift, 0.0)))))
            k_ext = jnp.where(lane < _AUG_KHI, 1.0, jnp.where(lane == _AUG_KHI, -fh, jnp.where(
                lane == _AUG_KMID, -fm, jnp.where(lane == _AUG_KLO, -fl, jnp.where(
                    lane == _AUG_SHIFT, 1.0, 0.0)))))
            v_ext = jnp.where(lane == _AUG_ONE, 1.0, 0.0)
            qa_ref[0, h] = jnp.where(low, qb, q_ext).astype(_BF16)
            ka_ref[0, h] = jnp.where(low, kb, k_ext).astype(_BF16)
            va_ref[0, h] = jnp.where(low, vb, v_ext).astype(_BF16)


def _in_proj(x, g_mix, w_main, w_ff, b_ff, cos_t, sin_t, gq, gk, shift, tri, *, tm):
    B, T, D = x.shape
    width = _N_HEADS * _HEAD_DIM
    tok = lambda b, i: (b, i, 0)
    head = lambda b, i: (b, 0, i, 0)
    bf_tok = jax.ShapeDtypeStruct((B, T, width), _BF16)
    bf_head = jax.ShapeDtypeStruct((B, _N_HEADS, T, _LANES), _BF16)
    return pl.pallas_call(
        functools.partial(_in_proj_body, tm=tm, width=width),
        grid=(B, T // tm),
        in_specs=[
            pl.BlockSpec((1, tm, D), tok),
            _const_spec((1, D)),
            _const_spec(w_main.shape),
            _const_spec(w_ff.shape),
            _const_spec((1, _LANES)),
            pl.BlockSpec((tm, _LANES), lambda b, i: (i, 0)),
            pl.BlockSpec((tm, _LANES), lambda b, i: (i, 0)),
            _const_spec((1, _LANES)),
            _const_spec((1, _LANES)),
            _const_spec((1, _LANES)),
            _const_spec((tm, tm)),
        ],
        out_specs=[pl.BlockSpec((1, tm, width), tok)] * 4
        + [pl.BlockSpec((1, _N_HEADS, tm, _LANES), head)] * 3,
        out_shape=[bf_tok] * 4 + [bf_head] * 3,
        scratch_shapes=[pltpu.VMEM((8, _LANES), _F32)],
        compiler_params=_params(("arbitrary", "arbitrary")),
        name="in_proj",
    )(x, g_mix, w_main, w_ff, b_ff, cos_t, sin_t, gq, gk, shift, tri)


def _retention_body(rq_ref, rk_ref, rv_ref, gate_ref, dmask_ref, qdec_ref, kdec_ref, bdec_ref, g_ref,
                    o_ref, state_ref):
    i = pl.program_id(1)

    @pl.when(i == 0)
    def _():
        state_ref[...] = jnp.zeros_like(state_ref)

    nt = (((1,), (1,)), ((), ()))
    tn = (((0,), (0,)), ((), ()))
    for h in range(_N_HEADS):
        sl = slice(h * _HEAD_DIM, (h + 1) * _HEAD_DIM)
        qh = rq_ref[0, :, sl]
        kh = rk_ref[0, :, sl]
        vh = rv_ref[0, :, sl]
        scores = lax.dot_general(qh, kh, nt, preferred_element_type=_F32) * dmask_ref[h]
        intra = jnp.dot(scores.astype(_BF16), vh, preferred_element_type=_F32)
        state = state_ref[h]
        inter = jnp.dot(qh, state.astype(_BF16), preferred_element_type=_F32) * qdec_ref[:, sl]
        kd = (kh.astype(_F32) * kdec_ref[:, sl]).astype(_BF16)
        state_ref[h] = state * bdec_ref[h] + lax.dot_general(kd, vh, tn, preferred_element_type=_F32)
        o = intra + inter
        mu = jnp.mean(o, axis=-1, keepdims=True)
        oc = o - mu
        var = jnp.mean(oc * oc, axis=-1, keepdims=True)
        y = oc * lax.rsqrt(var + _EPS) * g_ref[:, sl]
        o_ref[0, :, sl] = (y * gate_ref[0, :, sl].astype(_F32)).astype(_BF16)


def _retention(rq, rk, rv, gate, dmask, qdec, kdec, bdec, g_ret, *, tb):
    B, T, W = rq.shape
    tok = pl.BlockSpec((1, tb, W), lambda b, i: (b, i, 0))
    return pl.pallas_call(
        _retention_body,
        grid=(B, T // tb),
        in_specs=[tok, tok, tok, tok, _const_spec(dmask.shape), _const_spec(qdec.shape),
                  _const_spec(kdec.shape), _const_spec(bdec.shape), _const_spec(g_ret.shape)],
        out_specs=tok,
        out_shape=jax.ShapeDtypeStruct((B, T, W), _BF16),
        scratch_shapes=[pltpu.VMEM((_N_HEADS, _HEAD_DIM, _HEAD_DIM), _F32)],
        compiler_params=_params(("arbitrary", "arbitrary")),
        name="retention",
    )(rq, rk, rv, gate, dmask, qdec, kdec, bdec, g_ret)


def _causal_mask(tq):
    row = lax.broadcasted_iota(jnp.int32, (tq, tq), 0)
    col = lax.broadcasted_iota(jnp.int32, (tq, tq), 1)
    return row >= col


def _fox_finish(acc_refs, o_ref, tq):
    lane = lax.broadcasted_iota(jnp.int32, (tq, _LANES), 1)
    for c in range(len(acc_refs) // 2):
        pair = []
        for hh in (2 * c, 2 * c + 1):
            acc = acc_refs[hh][...]
            pair.append(acc / acc[:, _AUG_ONE:_AUG_ONE + 1])
        o_ref[0, :, c * _LANES:(c + 1) * _LANES] = jnp.where(
            lane < _HEAD_DIM, pair[0], pltpu.roll(pair[1], _HEAD_DIM, 1)).astype(_BF16)


def _fox_shifted_body(qa_ref, ka_ref, va_ref, o_ref, *acc_refs, tq):
    qi = pl.program_id(2)
    nt = (((1,), (1,)), ((), ()))
    n_heads = qa_ref.shape[1]
    for acc_ref in acc_refs:
        acc_ref[...] = jnp.zeros_like(acc_ref)

    def step(j, masked):
        start = pl.multiple_of(j * tq, tq)

        def qk(hh):
            return lax.dot_general(qa_ref[0, hh], ka_ref[0, hh, pl.ds(start, tq), :], nt,
                                   preferred_element_type=_F32)

        def pv(hh, s):
            if masked:
                s = jnp.where(_causal_mask(tq), s, _NEG_INF)
            acc_refs[hh][...] += jnp.dot(jnp.exp2(s).astype(_BF16), va_ref[0, hh, pl.ds(start, tq), :],
                                         preferred_element_type=_F32)

        s_prev = qk(0)
        for hh in range(1, n_heads):
            s_next = qk(hh)
            pv(hh - 1, s_prev)
            s_prev = s_next
        pv(n_heads - 1, s_prev)

    def off_diag(j, carry):
        step(j, False)
        return carry

    lax.fori_loop(0, qi, off_diag, 0)
    step(qi, True)
    _fox_finish(acc_refs, o_ref, tq)


def _fox_online_body(qa_ref, ka_ref, va_ref, o_ref, *scratch, tq):
    qi = pl.program_id(2)
    nt = (((1,), (1,)), ((), ()))
    n_heads = qa_ref.shape[1]
    m_refs, acc_refs = scratch[:n_heads], scratch[n_heads:]
    for hh in range(n_heads):
        m_refs[hh][...] = jnp.full_like(m_refs[hh], _NEG_INF)
        acc_refs[hh][...] = jnp.zeros_like(acc_refs[hh])

    def step(j, masked):
        start = pl.multiple_of(j * tq, tq)
        logits = [lax.dot_general(qa_ref[0, hh], ka_ref[0, hh, pl.ds(start, tq), :], nt,
                                  preferred_element_type=_F32) for hh in range(n_heads)]
        for hh in range(n_heads):
            m_ref, acc_ref = m_refs[hh], acc_refs[hh]
            s = logits[hh]
            if masked:
                s = jnp.where(_causal_mask(tq), s, _NEG_INF)
            m_old = m_ref[...]
            m_new = jnp.maximum(m_old, jnp.max(s, axis=-1, keepdims=True))
            p = jnp.exp2(s - m_new[:, 0:1])
            acc_ref[...] = jnp.exp2(m_old - m_new) * acc_ref[...] + jnp.dot(
                p.astype(_BF16), va_ref[0, hh, pl.ds(start, tq), :], preferred_element_type=_F32)
            m_ref[...] = m_new

    def off_diag(j, carry):
        step(j, False)
        return carry

    lax.fori_loop(0, qi, off_diag, 0)
    step(qi, True)
    _fox_finish(acc_refs, o_ref, tq)


def _fox(qa, ka, va, *, tq, hg, shifted):
    B, H, T, L = qa.shape
    body = _fox_shifted_body if shifted else _fox_online_body
    return pl.pallas_call(
        functools.partial(body, tq=tq),
        grid=(B, H // hg, T // tq),
        in_specs=[
            pl.BlockSpec((1, hg, tq, L), lambda b, p, i: (b, p, i, 0)),
            pl.BlockSpec((1, hg, T, L), lambda b, p, i: (b, p, 0, 0)),
            pl.BlockSpec((1, hg, T, L), lambda b, p, i: (b, p, 0, 0)),
        ],
        out_specs=pl.BlockSpec((1, tq, hg * _HEAD_DIM), lambda b, p, i: (b, i, p)),
        out_shape=jax.ShapeDtypeStruct((B, T, H * _HEAD_DIM), _BF16),
        scratch_shapes=[pltpu.VMEM((tq, L), _F32)] * (hg if shifted else 2 * hg),
        compiler_params=_params(("arbitrary", "arbitrary", "arbitrary")),
        name="fox_shifted" if shifted else "fox_online",
    )(qa, ka, va)


def _mem_kv_body(mem_ref, gm_ref, w_ref, gk_ref, k_ref, v_ref, *, d_model, xd):
    mn = _rms(mem_ref[0], gm_ref[...]).astype(_BF16)
    kv = jnp.dot(mn, w_ref[...], preferred_element_type=_F32)
    for h in range(_N_XHEADS):
        sl = slice(h * xd, (h + 1) * xd)
        k_ref[0, :, sl] = (_rms(kv[:, sl], gk_ref[...]) * (xd ** -0.5)).astype(_BF16)
    v_ref[0] = kv[:, d_model:].astype(_BF16)


def _mem_kv(mem, g_mem, w_xkv, g_xk):
    B, M, D = mem.shape
    xd = D // _N_XHEADS
    blk = pl.BlockSpec((1, M, D), lambda b: (b, 0, 0))
    out = jax.ShapeDtypeStruct((B, M, D), _BF16)
    return pl.pallas_call(
        functools.partial(_mem_kv_body, d_model=D, xd=xd),
        grid=(B,),
        in_specs=[blk, _const_spec((1, D)), _const_spec(w_xkv.shape), _const_spec((1, xd))],
        out_specs=[blk, blk],
        out_shape=[out, out],
        compiler_params=_params(("arbitrary",)),
        name="mem_kv",
    )(mem, g_mem, w_xkv, g_xk)


def _mix_xattn_body(x_ref, ret_ref, fox_ref, wo_ref, gx_ref, wq_ref, gq_ref, k_ref, v_ref, wxo_ref,
                    o_ref, *, width, xd):
    h1 = (x_ref[0]
          + jnp.dot(ret_ref[0], wo_ref[:width, :], preferred_element_type=_F32)
          + jnp.dot(fox_ref[0], wo_ref[width:, :], preferred_element_type=_F32))
    hn = _rms(h1, gx_ref[...]).astype(_BF16)
    q = jnp.dot(hn, wq_ref[...], preferred_element_type=_F32)
    nt = (((1,), (1,)), ((), ()))
    heads = []
    for h in range(_N_XHEADS):
        sl = slice(h * xd, (h + 1) * xd)
        qn = _rms(q[:, sl], gq_ref[...]).astype(_BF16)
        logits = lax.dot_general(qn, k_ref[0, :, sl], nt, preferred_element_type=_F32)
        p = jnp.exp(logits - jnp.max(logits, axis=-1, keepdims=True))
        p = p / jnp.sum(p, axis=-1, keepdims=True)
        heads.append(jnp.dot(p.astype(_BF16), v_ref[0, :, sl], preferred_element_type=_F32).astype(_BF16))
    o = jnp.concatenate(heads, axis=-1)
    o_ref[0] = h1 + jnp.dot(o, wxo_ref[...], preferred_element_type=_F32)


def _mix_xattn(x, ret, fox, w_out, g_xattn, w_xq, g_xq, k, v, w_xo, *, tm):
    B, T, D = x.shape
    W = ret.shape[-1]
    M = k.shape[1]
    xd = D // _N_XHEADS
    tok = lambda b, i: (b, i, 0)
    return pl.pallas_call(
        functools.partial(_mix_xattn_body, width=W, xd=xd),
        grid=(B, T // tm),
        in_specs=[
            pl.BlockSpec((1, tm, D), tok),
            pl.BlockSpec((1, tm, W), tok),
            pl.BlockSpec((1, tm, W), tok),
            _const_spec(w_out.shape),
            _const_spec((1, D)),
            _const_spec(w_xq.shape),
            _const_spec((1, xd)),
            pl.BlockSpec((1, M, D), lambda b, i: (b, 0, 0)),
            pl.BlockSpec((1, M, D), lambda b, i: (b, 0, 0)),
            _const_spec(w_xo.shape),
        ],
        out_specs=pl.BlockSpec((1, tm, D), tok),
        out_shape=jax.ShapeDtypeStruct((B, T, D), _F32),
        compiler_params=_params(("arbitrary", "arbitrary")),
        name="mix_xattn",
    )(x, ret, fox, w_out, g_xattn, w_xq, g_xq, k, v, w_xo)


def _ffn_body(h_ref, g_ref, wg_ref, wu_ref, wd_ref, o_ref, *, n_split, fs):
    h = h_ref[0]
    hn = _rms(h, g_ref[...]).astype(_BF16)
    acc = h
    for c in range(n_split):
        sl = slice(c * fs, (c + 1) * fs)
        gate = jnp.dot(hn, wg_ref[:, sl], preferred_element_type=_F32)
        up = jnp.dot(hn, wu_ref[:, sl], preferred_element_type=_F32)
        a = (gate * jax.nn.sigmoid(gate) * up).astype(_BF16)
        acc = acc + jnp.dot(a, wd_ref[sl, :], preferred_element_type=_F32)
    o_ref[0] = acc


def _ffn(h, g_ffn, w_gate, w_up, w_down, *, tm, n_split):
    B, T, D = h.shape
    F = w_gate.shape[1]
    tok = pl.BlockSpec((1, tm, D), lambda b, i: (b, i, 0))
    return pl.pallas_call(
        functools.partial(_ffn_body, n_split=n_split, fs=F // n_split),
        grid=(B, T // tm),
        in_specs=[tok, _const_spec((1, D)), _const_spec(w_gate.shape), _const_spec(w_up.shape),
                  _const_spec(w_down.shape)],
        out_specs=tok,
        out_shape=jax.ShapeDtypeStruct((B, T, D), _F32),
        compiler_params=_params(("arbitrary", "arbitrary")),
        name="ffn",
    )(h, g_ffn, w_gate, w_up, w_down)


def _rope_tables(T):
    half = _HEAD_DIM // 2
    inv_freq = (_ROPE_BASE ** (-np.arange(0, _HEAD_DIM, 2, dtype=np.float32) / _HEAD_DIM)).astype(np.float32)
    ang = (np.arange(T, dtype=np.float32)[:, None] * inv_freq[None, :]).astype(np.float32).astype(np.float64)
    cos, sin = np.cos(ang), np.sin(ang)
    reps = _LANES // _HEAD_DIM
    cos_t = np.tile(np.concatenate([cos, cos], axis=1), (1, reps))
    sin_t = np.tile(np.concatenate([-sin, sin], axis=1), (1, reps))
    return jnp.asarray(cos_t, _F32), jnp.asarray(sin_t, _F32)


def _retention_tables(tb):
    log_g = np.log(1.0 - 2.0 ** (-5.0 - np.arange(_N_HEADS, dtype=np.float64)))
    idx = np.arange(tb, dtype=np.float64)
    dist = np.abs(idx[:, None] - idx[None, :])
    chunk = np.arange(tb) // _RET_CHUNK
    visible = chunk[None, :] <= chunk[:, None]
    dmask = np.where(visible[None], np.exp(log_g[:, None, None] * dist[None]), 0.0)
    qdec = np.repeat(np.exp(log_g[None, :] * (idx[:, None] + 1.0)), _HEAD_DIM, axis=1)
    kdec = np.repeat(np.exp(log_g[None, :] * (tb - 1.0 - idx[:, None])), _HEAD_DIM, axis=1)
    bdec = np.broadcast_to(np.exp(log_g * tb)[:, None, None], (_N_HEADS, _HEAD_DIM, _HEAD_DIM))
    f = lambda a: jnp.asarray(a, _F32)
    return f(dmask), f(qdec), f(kdec), f(bdec)


def _pad_lanes(a):
    return jnp.pad(a, [(0, 0)] * (a.ndim - 1) + [(0, _LANES - a.shape[-1])])


def kernel(x, mem, g_mix, w_in, b_forget, g_ret_out, g_fox_q, g_fox_k, w_out, g_xattn, w_xq, w_xkv,
           g_mem, g_xq, g_xk, w_xo, g_ffn, w_gate, w_up, w_down):
    B, T, D = x.shape
    width = _N_HEADS * _HEAD_DIM
    tm_in, tb, tq, tm_mix, tm_ffn = 512, 256, 512, 512, 512
    cos_t, sin_t = _rope_tables(T)
    dmask, qdec, kdec, bdec = _retention_tables(tb)
    tri = jnp.asarray(np.tril(np.ones((tm_in, tm_in), np.float32)), _BF16)
    reps = _LANES // _HEAD_DIM
    row = lambda a: a.reshape(1, -1).astype(_F32)

    h = x
    for l in range(w_in.shape[0]):
        w_main = w_in[l, :, :7 * width].astype(_BF16)
        w_ff = _pad_lanes(w_in[l, :, 7 * width:]).astype(_BF16)
        b_ff = _pad_lanes(row(b_forget[l]))
        gq = jnp.tile(row(g_fox_q[l]), (1, reps)) * (_LOG2E * _HEAD_DIM ** -0.5)
        gk = jnp.tile(row(g_fox_k[l]), (1, reps))
        bound = (_HEAD_DIM ** 0.5 * _NORM_ROUNDING_SLACK) * jnp.max(jnp.abs(g_fox_q[l])) * jnp.max(
            jnp.abs(g_fox_k[l]))
        use_shift = bound <= _MAX_FIXED_SHIFT
        shift = jnp.where(use_shift, jnp.ceil(bound * (4.0 * _LOG2E)) * 0.25, 0.0).astype(_F32)
        rq, rk, rv, gate, qa, ka, va = _in_proj(h, row(g_mix[l]), w_main, w_ff, b_ff, cos_t, sin_t,
                                                gq, gk, jnp.full((1, _LANES), shift), tri, tm=tm_in)
        ret = _retention(rq, rk, rv, gate, dmask, qdec, kdec, bdec, row(g_ret_out[l]), tb=tb)
        fox = lax.cond(use_shift,
                       functools.partial(_fox, tq=tq, hg=4, shifted=True),
                       functools.partial(_fox, tq=tq, hg=4, shifted=False), qa, ka, va)
        k, v = _mem_kv(mem, row(g_mem[l]), w_xkv[l].astype(_BF16), row(g_xk[l]))
        h = _mix_xattn(h, ret, fox, w_out[l].astype(_BF16), row(g_xattn[l]), w_xq[l].astype(_BF16),
                       row(g_xq[l]), k, v, w_xo[l].astype(_BF16), tm=tm_mix)
        h = _ffn(h, row(g_ffn[l]), w_gate[l].astype(_BF16), w_up[l].astype(_BF16),
                 w_down[l].astype(_BF16), tm=tm_ffn, n_split=2)
    return h
```

```python
import functools

import numpy as np
import jax
import jax.numpy as jnp
from jax import lax
from jax.experimental import pallas as pl
from jax.experimental.pallas import tpu as pltpu

_BF16 = jnp.bfloat16
_F32 = jnp.float32

_EPS = 1e-6
_NEG_INF = -1e30
_ROPE_BASE = 10000.0
_HEAD_DIM = 64
_N_HEADS = 8
_RET_CHUNK = 64
_RET_GROUP = 256
_N_XHEADS = 4
_LANES = 128
_VMEM_LIMIT = 56 * 1024 * 1024

_AUG_HI, _AUG_MID, _AUG_LO = 64, 65, 66
_AUG_KHI, _AUG_KMID, _AUG_KLO = 67, 68, 69
_AUG_SHIFT = 70
_AUG_ONE = 64

_LOG2E = 1.4426950408889634
_MAX_FIXED_SHIFT = 32.0
_NORM_ROUNDING_SLACK = 1.01


def _params(sem):
    return pltpu.CompilerParams(dimension_semantics=sem, vmem_limit_bytes=_VMEM_LIMIT)


def _const_spec(shape):
    nd = len(shape)
    return pl.BlockSpec(shape, lambda *_: (0,) * nd, pipeline_mode=pl.Buffered(1))


def _rms(x, g):
    return x * lax.rsqrt(jnp.mean(x * x, axis=-1, keepdims=True) + _EPS) * g


def _split3(v):
    hi = v.astype(_BF16).astype(_F32)
    r = v - hi
    mid = r.astype(_BF16).astype(_F32)
    return hi, mid, r - mid


def _in_proj_body(x_ref, g_ref, w_ref, wff_ref, bf_ref, cos_ref, sin_ref, gq_ref, gk_ref, shift_ref,
                  tri_ref, avg_ref, place_ref, rq_ref, rk_ref, rv_ref, gate_ref, qa_ref, ka_ref, va_ref,
                  carry_ref, *, tm, width):
    i = pl.program_id(1)
    hb = _rms(x_ref[0], g_ref[...]).astype(_BF16)

    def proj(j):
        return jnp.dot(hb, w_ref[:, j * width:(j + 1) * width], preferred_element_type=_F32)

    lane = lax.broadcasted_iota(jnp.int32, (tm, _LANES), 1)
    low = lane < _HEAD_DIM
    first_half = (lane & (_HEAD_DIM // 2)) == 0
    n_pairs = width // _LANES

    cos = cos_ref[...]
    sin = sin_ref[...]
    for j, out_ref, scale in ((0, rq_ref, _HEAD_DIM ** -0.5), (1, rk_ref, None)):
        y = proj(j)
        for c in range(n_pairs):
            blk = y[:, c * _LANES:(c + 1) * _LANES]
            swapped = jnp.where(first_half, pltpu.roll(blk, _LANES - _HEAD_DIM // 2, 1),
                                pltpu.roll(blk, _HEAD_DIM // 2, 1))
            r = blk * cos + swapped * sin
            if scale is not None:
                r = r * scale
            out_ref[0, :, c * _LANES:(c + 1) * _LANES] = r.astype(_BF16)
    rv_ref[0] = proj(2).astype(_BF16)
    gate = proj(3)
    gate_ref[0] = (gate * jax.nn.sigmoid(gate)).astype(_BF16)

    z = jnp.dot(hb, wff_ref[...], preferred_element_type=_F32) + bf_ref[...]
    logf = (jnp.minimum(z, 0.0) - jnp.log(1.0 + jnp.exp(-jnp.abs(z)))) * _LOG2E
    hi, mid, lo = _split3(logf)
    tri = tri_ref[...]
    csum = (jnp.dot(tri, hi.astype(_BF16), preferred_element_type=_F32)
            + jnp.dot(tri, mid.astype(_BF16), preferred_element_type=_F32)
            + jnp.dot(tri, lo.astype(_BF16), preferred_element_type=_F32))

    @pl.when(i == 0)
    def _():
        carry_ref[...] = jnp.zeros_like(carry_ref)

    fcum = csum + carry_ref[0:1, :]
    carry_ref[...] = jnp.broadcast_to(fcum[tm - 1:tm, :], carry_ref.shape)

    def head_rms(y, g_row):
        out = []
        for g0 in range(0, width, _RET_GROUP):
            blk = y[:, g0:g0 + _RET_GROUP]
            ms = jnp.dot((blk * blk).astype(_BF16), avg_ref[...], preferred_element_type=_F32)
            out.append(blk * lax.rsqrt(ms + _EPS) * g_row[:, g0:g0 + _RET_GROUP])
        return out

    qn = head_rms(proj(4), gq_ref[...])
    kn = head_rms(proj(5), gk_ref[...])
    fv = proj(6)

    fh, fm, fl = _split3(fcum)
    packed = jnp.where(lane < _N_HEADS, fh, jnp.where(
        lane < 2 * _N_HEADS, pltpu.roll(fm, _N_HEADS, 1), pltpu.roll(fl, 2 * _N_HEADS, 1)))
    placed = jnp.dot(packed.astype(_BF16), place_ref[...], preferred_element_type=_F32)
    q_const = jnp.where((lane >= _AUG_KHI) & (lane <= _AUG_KLO), 1.0,
                        jnp.where(lane == _AUG_SHIFT, -shift_ref[...], 0.0))
    k_const = jnp.where(((lane >= _AUG_HI) & (lane <= _AUG_LO)) | (lane == _AUG_SHIFT), 1.0, 0.0)
    v_const = jnp.where(lane == _AUG_ONE, 1.0, 0.0)
    q_bias = (lane >= _AUG_HI) & (lane <= _AUG_LO)
    k_bias = (lane >= _AUG_KHI) & (lane <= _AUG_KLO)
    for h in range(_N_HEADS):
        g0, c0 = divmod(h * _HEAD_DIM, _RET_GROUP)
        c0 = (c0 // _LANES) * _LANES
        qb = qn[g0][:, c0:c0 + _LANES]
        kb = kn[g0][:, c0:c0 + _LANES]
        vb = fv[:, (h // 2) * _LANES:(h // 2 + 1) * _LANES]
        if h % 2 == 1:
            qb = pltpu.roll(qb, _HEAD_DIM, 1)
            kb = pltpu.roll(kb, _HEAD_DIM, 1)
            vb = pltpu.roll(vb, _HEAD_DIM, 1)
        bias = placed[:, h * _LANES:(h + 1) * _LANES]
        qa_ref[0, h] = jnp.where(low, qb, jnp.where(q_bias, bias, q_const)).astype(_BF16)
        ka_ref[0, h] = jnp.where(low, kb, jnp.where(k_bias, bias, k_const)).astype(_BF16)
        va_ref[0, h] = jnp.where(low, vb, v_const).astype(_BF16)


def _in_proj(x, g_mix, w_main, w_ff, b_ff, cos_t, sin_t, gq, gk, shift, tri, avg, place, *, tm):
    B, T, D = x.shape
    width = _N_HEADS * _HEAD_DIM
    tok = lambda b, i: (b, i, 0)
    head = lambda b, i: (b, 0, i, 0)
    bf_tok = jax.ShapeDtypeStruct((B, T, width), _BF16)
    bf_head = jax.ShapeDtypeStruct((B, _N_HEADS, T, _LANES), _BF16)
    return pl.pallas_call(
        functools.partial(_in_proj_body, tm=tm, width=width),
        grid=(B, T // tm),
        in_specs=[
            pl.BlockSpec((1, tm, D), tok),
            _const_spec((1, D)),
            _const_spec(w_main.shape),
            _const_spec(w_ff.shape),
            _const_spec((1, _LANES)),
            pl.BlockSpec((tm, _LANES), lambda b, i: (i, 0)),
            pl.BlockSpec((tm, _LANES), lambda b, i: (i, 0)),
            _const_spec((1, width)),
            _const_spec((1, width)),
            _const_spec((1, _LANES)),
            _const_spec((tm, tm)),
            _const_spec(avg.shape),
            _const_spec(place.shape),
        ],
        out_specs=[pl.BlockSpec((1, tm, width), tok)] * 4
        + [pl.BlockSpec((1, _N_HEADS, tm, _LANES), head)] * 3,
        out_shape=[bf_tok] * 4 + [bf_head] * 3,
        scratch_shapes=[pltpu.VMEM((8, _LANES), _F32)],
        compiler_params=_params(("arbitrary", "arbitrary")),
        name="in_proj",
    )(x, g_mix, w_main, w_ff, b_ff, cos_t, sin_t, gq, gk, shift, tri, avg, place)


def _retention_body(rq_ref, rk_ref, rv_ref, gate_ref, dmask_ref, qdec_ref, kdec_ref, sdec_ref, bd_ref,
                    avg_ref, g_ref, o_ref, state_ref, *, tb):
    i = pl.program_id(1)

    @pl.when(i == 0)
    def _():
        state_ref[...] = jnp.zeros_like(state_ref)

    nt = (((1,), (1,)), ((), ()))
    tn = (((0,), (0,)), ((), ()))
    gw = state_ref.shape[1]
    lane = lax.broadcasted_iota(jnp.int32, (1, _LANES), 1)
    low = lax.broadcasted_iota(jnp.int32, (tb, _LANES), 1) < _HEAD_DIM
    head_lanes = [jnp.where(lane < _HEAD_DIM, 1.0, 0.0).astype(_BF16),
                  jnp.where(lane < _HEAD_DIM, 0.0, 1.0).astype(_BF16)]
    for g in range(state_ref.shape[0]):
        gs = slice(g * gw, (g + 1) * gw)
        intra = []
        for c in range(gw // _LANES):
            ps = slice(g * gw + c * _LANES, g * gw + (c + 1) * _LANES)
            qp, kp, vp = rq_ref[0, :, ps], rk_ref[0, :, ps], rv_ref[0, :, ps]
            both = []
            for hh in range(2):
                h = (g * gw + c * _LANES) // _HEAD_DIM + hh
                scores = lax.dot_general(qp * head_lanes[hh], kp, nt,
                                         preferred_element_type=_F32) * dmask_ref[h]
                both.append(jnp.dot(scores.astype(_BF16), vp, preferred_element_type=_F32))
            intra.append(jnp.where(low, both[0], both[1]))
        state = state_ref[g]
        inter = jnp.dot(rq_ref[0, :, gs], state.astype(_BF16),
                        preferred_element_type=_F32) * qdec_ref[:, gs]
        kd = (rk_ref[0, :, gs].astype(_F32) * kdec_ref[:, gs]).astype(_BF16)
        state_ref[g] = state * sdec_ref[g] + lax.dot_general(
            kd, rv_ref[0, :, gs], tn, preferred_element_type=_F32) * bd_ref[...]
        o = jnp.concatenate(intra, axis=1) + inter
        mu = jnp.dot(o.astype(_BF16), avg_ref[...], preferred_element_type=_F32)
        oc = o - mu
        var = jnp.dot((oc * oc).astype(_BF16), avg_ref[...], preferred_element_type=_F32)
        y = oc * lax.rsqrt(var + _EPS) * g_ref[:, gs]
        o_ref[0, :, gs] = (y * gate_ref[0, :, gs].astype(_F32)).astype(_BF16)


def _retention(rq, rk, rv, gate, dmask, qdec, kdec, sdec, bd, avg, g_ret, *, tb):
    B, T, W = rq.shape
    tok = pl.BlockSpec((1, tb, W), lambda b, i: (b, i, 0))
    consts = (dmask, qdec, kdec, sdec, bd, avg, g_ret)
    return pl.pallas_call(
        functools.partial(_retention_body, tb=tb),
        grid=(B, T // tb),
        in_specs=[tok, tok, tok, tok] + [_const_spec(c.shape) for c in consts],
        out_specs=tok,
        out_shape=jax.ShapeDtypeStruct((B, T, W), _BF16),
        scratch_shapes=[pltpu.VMEM(sdec.shape, _F32)],
        compiler_params=_params(("arbitrary", "arbitrary")),
        name="retention",
    )(rq, rk, rv, gate, *consts)


def _causal_mask(tq):
    row = lax.broadcasted_iota(jnp.int32, (tq, tq), 0)
    col = lax.broadcasted_iota(jnp.int32, (tq, tq), 1)
    return row >= col


def _fox_finish(acc_refs, o_ref, tq):
    lane = lax.broadcasted_iota(jnp.int32, (tq, _LANES), 1)
    for c in range(len(acc_refs) // 2):
        pair = []
        for hh in (2 * c, 2 * c + 1):
            acc = acc_refs[hh][...]
            pair.append(acc / acc[:, _AUG_ONE:_AUG_ONE + 1])
        o_ref[0, :, c * _LANES:(c + 1) * _LANES] = jnp.where(
            lane < _HEAD_DIM, pair[0], pltpu.roll(pair[1], _HEAD_DIM, 1)).astype(_BF16)


def _fox_shifted_body(qa_ref, ka_ref, va_ref, o_ref, *acc_refs, tq):
    qi = pl.program_id(2)
    nt = (((1,), (1,)), ((), ()))
    n_heads = qa_ref.shape[1]
    for acc_ref in acc_refs:
        acc_ref[...] = jnp.zeros_like(acc_ref)

    def step(j, masked):
        start = pl.multiple_of(j * tq, tq)

        def qk(hh):
            return lax.dot_general(qa_ref[0, hh], ka_ref[0, hh, pl.ds(start, tq), :], nt,
                                   preferred_element_type=_F32)

        def pv(hh, s):
            if masked:
                s = jnp.where(_causal_mask(tq), s, _NEG_INF)
            acc_refs[hh][...] += jnp.dot(jnp.exp2(s).astype(_BF16), va_ref[0, hh, pl.ds(start, tq), :],
                                         preferred_element_type=_F32)

        s_prev = qk(0)
        for hh in range(1, n_heads):
            s_next = qk(hh)
            pv(hh - 1, s_prev)
            s_prev = s_next
        pv(n_heads - 1, s_prev)

    def off_diag(j, carry):
        step(j, False)
        return carry

    lax.fori_loop(0, qi, off_diag, 0)
    step(qi, True)
    _fox_finish(acc_refs, o_ref, tq)


def _fox_online_body(qa_ref, ka_ref, va_ref, o_ref, *scratch, tq):
    qi = pl.program_id(2)
    nt = (((1,), (1,)), ((), ()))
    n_heads = qa_ref.shape[1]
    m_refs, acc_refs = scratch[:n_heads], scratch[n_heads:]
    for hh in range(n_heads):
        m_refs[hh][...] = jnp.full_like(m_refs[hh], _NEG_INF)
        acc_refs[hh][...] = jnp.zeros_like(acc_refs[hh])

    def step(j, masked):
        start = pl.multiple_of(j * tq, tq)
        logits = [lax.dot_general(qa_ref[0, hh], ka_ref[0, hh, pl.ds(start, tq), :], nt,
                                  preferred_element_type=_F32) for hh in range(n_heads)]
        for hh in range(n_heads):
            m_ref, acc_ref = m_refs[hh], acc_refs[hh]
            s = logits[hh]
            if masked:
                s = jnp.where(_causal_mask(tq), s, _NEG_INF)
            m_old = m_ref[...]
            m_new = jnp.maximum(m_old, jnp.max(s, axis=-1, keepdims=True))
            p = jnp.exp2(s - m_new[:, 0:1])
            acc_ref[...] = jnp.exp2(m_old - m_new) * acc_ref[...] + jnp.dot(
                p.astype(_BF16), va_ref[0, hh, pl.ds(start, tq), :], preferred_element_type=_F32)
            m_ref[...] = m_new

    def off_diag(j, carry):
        step(j, False)
        return carry

    lax.fori_loop(0, qi, off_diag, 0)
    step(qi, True)
    _fox_finish(acc_refs, o_ref, tq)


def _fox(qa, ka, va, *, tq, hg, shifted):
    B, H, T, L = qa.shape
    body = _fox_shifted_body if shifted else _fox_online_body
    return pl.pallas_call(
        functools.partial(body, tq=tq),
        grid=(B, H // hg, T // tq),
        in_specs=[
            pl.BlockSpec((1, hg, tq, L), lambda b, p, i: (b, p, i, 0)),
            pl.BlockSpec((1, hg, T, L), lambda b, p, i: (b, p, 0, 0)),
            pl.BlockSpec((1, hg, T, L), lambda b, p, i: (b, p, 0, 0)),
        ],
        out_specs=pl.BlockSpec((1, tq, hg * _HEAD_DIM), lambda b, p, i: (b, i, p)),
        out_shape=jax.ShapeDtypeStruct((B, T, H * _HEAD_DIM), _BF16),
        scratch_shapes=[pltpu.VMEM((tq, L), _F32)] * (hg if shifted else 2 * hg),
        compiler_params=_params(("arbitrary", "arbitrary", "arbitrary")),
        name="fox_shifted" if shifted else "fox_online",
    )(qa, ka, va)


def _mem_kv_body(mem_ref, gm_ref, w_ref, gk_ref, k_ref, v_ref, *, d_model, xd):
    mn = _rms(mem_ref[0], gm_ref[...]).astype(_BF16)
    kv = jnp.dot(mn, w_ref[...], preferred_element_type=_F32)
    for h in range(_N_XHEADS):
        sl = slice(h * xd, (h + 1) * xd)
        k_ref[0, :, sl] = (_rms(kv[:, sl], gk_ref[...]) * (xd ** -0.5)).astype(_BF16)
    v_ref[0] = kv[:, d_model:].astype(_BF16)


def _mem_kv(mem, g_mem, w_xkv, g_xk):
    B, M, D = mem.shape
    xd = D // _N_XHEADS
    blk = pl.BlockSpec((1, M, D), lambda b: (b, 0, 0))
    out = jax.ShapeDtypeStruct((B, M, D), _BF16)
    return pl.pallas_call(
        functools.partial(_mem_kv_body, d_model=D, xd=xd),
        grid=(B,),
        in_specs=[blk, _const_spec((1, D)), _const_spec(w_xkv.shape), _const_spec((1, xd))],
        out_specs=[blk, blk],
        out_shape=[out, out],
        compiler_params=_params(("arbitrary",)),
        name="mem_kv",
    )(mem, g_mem, w_xkv, g_xk)


def _mix_xattn_body(x_ref, ret_ref, fox_ref, wo_ref, gx_ref, wq_ref, gq_ref, k_ref, v_ref, wxo_ref,
                    o_ref, *, width, xd):
    h1 = (x_ref[0]
          + jnp.dot(ret_ref[0], wo_ref[:width, :], preferred_element_type=_F32)
          + jnp.dot(fox_ref[0], wo_ref[width:, :], preferred_element_type=_F32))
    hn = _rms(h1, gx_ref[...]).astype(_BF16)
    q = jnp.dot(hn, wq_ref[...], preferred_element_type=_F32)
    nt = (((1,), (1,)), ((), ()))
    heads = []
    for h in range(_N_XHEADS):
        sl = slice(h * xd, (h + 1) * xd)
        qn = _rms(q[:, sl], gq_ref[...]).astype(_BF16)
        logits = lax.dot_general(qn, k_ref[0, :, sl], nt, preferred_element_type=_F32)
        p = jnp.exp(logits - jnp.max(logits, axis=-1, keepdims=True))
        p = p / jnp.sum(p, axis=-1, keepdims=True)
        heads.append(jnp.dot(p.astype(_BF16), v_ref[0, :, sl], preferred_element_type=_F32).astype(_BF16))
    o = jnp.concatenate(heads, axis=-1)
    o_ref[0] = h1 + jnp.dot(o, wxo_ref[...], preferred_element_type=_F32)


def _mix_xattn(x, ret, fox, w_out, g_xattn, w_xq, g_xq, k, v, w_xo, *, tm):
    B, T, D = x.shape
    W = ret.shape[-1]
    M = k.shape[1]
    xd = D // _N_XHEADS
    tok = lambda b, i: (b, i, 0)
    return pl.pallas_call(
        functools.partial(_mix_xattn_body, width=W, xd=xd),
        grid=(B, T // tm),
        in_specs=[
            pl.BlockSpec((1, tm, D), tok),
            pl.BlockSpec((1, tm, W), tok),
            pl.BlockSpec((1, tm, W), tok),
            _const_spec(w_out.shape),
            _const_spec((1, D)),
            _const_spec(w_xq.shape),
            _const_spec((1, xd)),
            pl.BlockSpec((1, M, D), lambda b, i: (b, 0, 0)),
            pl.BlockSpec((1, M, D), lambda b, i: (b, 0, 0)),
            _const_spec(w_xo.shape),
        ],
        out_specs=pl.BlockSpec((1, tm, D), tok),
        out_shape=jax.ShapeDtypeStruct((B, T, D), _F32),
        compiler_params=_params(("arbitrary", "arbitrary")),
        name="mix_xattn",
    )(x, ret, fox, w_out, g_xattn, w_xq, g_xq, k, v, w_xo)


def _ffn_body(h_ref, g_ref, wg_ref, wu_ref, wd_ref, o_ref, *, n_split, fs):
    h = h_ref[0]
    hn = _rms(h, g_ref[...]).astype(_BF16)
    acc = h
    for c in range(n_split):
        sl = slice(c * fs, (c + 1) * fs)
        gate = jnp.dot(hn, wg_ref[:, sl], preferred_element_type=_F32)
        up = jnp.dot(hn, wu_ref[:, sl], preferred_element_type=_F32)
        a = (gate * jax.nn.sigmoid(gate) * up).astype(_BF16)
        acc = acc + jnp.dot(a, wd_ref[sl, :], preferred_element_type=_F32)
    o_ref[0] = acc


def _ffn(h, g_ffn, w_gate, w_up, w_down, *, tm, n_split):
    B, T, D = h.shape
    F = w_gate.shape[1]
    tok = pl.BlockSpec((1, tm, D), lambda b, i: (b, i, 0))
    return pl.pallas_call(
        functools.partial(_ffn_body, n_split=n_split, fs=F // n_split),
        grid=(B, T // tm),
        in_specs=[tok, _const_spec((1, D)), _const_spec(w_gate.shape), _const_spec(w_up.shape),
                  _const_spec(w_down.shape)],
        out_specs=tok,
        out_shape=jax.ShapeDtypeStruct((B, T, D), _F32),
        compiler_params=_params(("arbitrary", "arbitrary")),
        name="ffn",
    )(h, g_ffn, w_gate, w_up, w_down)


def _rope_tables(T):
    half = _HEAD_DIM // 2
    inv_freq = (_ROPE_BASE ** (-np.arange(0, _HEAD_DIM, 2, dtype=np.float32) / _HEAD_DIM)).astype(np.float32)
    ang = (np.arange(T, dtype=np.float32)[:, None] * inv_freq[None, :]).astype(np.float32).astype(np.float64)
    cos, sin = np.cos(ang), np.sin(ang)
    reps = _LANES // _HEAD_DIM
    cos_t = np.tile(np.concatenate([cos, cos], axis=1), (1, reps))
    sin_t = np.tile(np.concatenate([-sin, sin], axis=1), (1, reps))
    return jnp.asarray(cos_t, _F32), jnp.asarray(sin_t, _F32)


def _retention_tables(tb):
    log_g = np.log(1.0 - 2.0 ** (-5.0 - np.arange(_N_HEADS, dtype=np.float64)))
    idx = np.arange(tb, dtype=np.float64)
    dist = np.abs(idx[:, None] - idx[None, :])
    chunk = np.arange(tb) // _RET_CHUNK
    visible = chunk[None, :] <= chunk[:, None]
    dmask = np.where(visible[None], np.exp(log_g[:, None, None] * dist[None]), 0.0)
    qdec = np.repeat(np.exp(log_g[None, :] * (idx[:, None] + 1.0)), _HEAD_DIM, axis=1)
    kdec = np.repeat(np.exp(log_g[None, :] * (tb - 1.0 - idx[:, None])), _HEAD_DIM, axis=1)
    heads_per_group = _RET_GROUP // _HEAD_DIM
    head_of = np.arange(_RET_GROUP) // _HEAD_DIM
    bd = (head_of[:, None] == head_of[None, :]).astype(np.float64)
    step_decay = np.exp(log_g * tb).reshape(-1, heads_per_group)
    sdec = bd[None] * np.repeat(step_decay, _HEAD_DIM, axis=1)[:, None, :]
    f = lambda a: jnp.asarray(a, _F32)
    return f(dmask), f(qdec), f(kdec), f(sdec), f(bd), jnp.asarray(bd / _HEAD_DIM, _BF16)


def _bias_placement():
    place = np.zeros((_LANES, _N_HEADS * _LANES), np.float32)
    for h in range(_N_HEADS):
        for part, (q_lane, k_lane) in enumerate(((_AUG_HI, _AUG_KHI), (_AUG_MID, _AUG_KMID),
                                                 (_AUG_LO, _AUG_KLO))):
            place[part * _N_HEADS + h, h * _LANES + q_lane] = 1.0
            place[part * _N_HEADS + h, h * _LANES + k_lane] = -1.0
    return jnp.asarray(place, _BF16)


def _pad_lanes(a):
    return jnp.pad(a, [(0, 0)] * (a.ndim - 1) + [(0, _LANES - a.shape[-1])])


def kernel(x, mem, g_mix, w_in, b_forget, g_ret_out, g_fox_q, g_fox_k, w_out, g_xattn, w_xq, w_xkv,
           g_mem, g_xq, g_xk, w_xo, g_ffn, w_gate, w_up, w_down):
    B, T, D = x.shape
    width = _N_HEADS * _HEAD_DIM
    tm_in, tb, tq, tm_mix, tm_ffn = 512, 256, 512, 512, 512
    cos_t, sin_t = _rope_tables(T)
    ret_tables = _retention_tables(tb)
    tri = jnp.asarray(np.tril(np.ones((tm_in, tm_in), np.float32)), _BF16)
    place = _bias_placement()
    row = lambda a: a.reshape(1, -1).astype(_F32)

    h = x
    for l in range(w_in.shape[0]):
        w_main = w_in[l, :, :7 * width].astype(_BF16)
        w_ff = _pad_lanes(w_in[l, :, 7 * width:]).astype(_BF16)
        b_ff = _pad_lanes(row(b_forget[l]))
        gq = jnp.tile(row(g_fox_q[l]), (1, _N_HEADS)) * (_LOG2E * _HEAD_DIM ** -0.5)
        gk = jnp.tile(row(g_fox_k[l]), (1, _N_HEADS))
        bound = (_HEAD_DIM ** 0.5 * _NORM_ROUNDING_SLACK) * jnp.max(jnp.abs(g_fox_q[l])) * jnp.max(
            jnp.abs(g_fox_k[l]))
        use_shift = bound <= _MAX_FIXED_SHIFT
        shift = jnp.where(use_shift, jnp.ceil(bound * (4.0 * _LOG2E)) * 0.25, 0.0).astype(_F32)
        rq, rk, rv, gate, qa, ka, va = _in_proj(h, row(g_mix[l]), w_main, w_ff, b_ff, cos_t, sin_t,
                                                gq, gk, jnp.full((1, _LANES), shift), tri,
                                                ret_tables[-1], place, tm=tm_in)
        ret = _retention(rq, rk, rv, gate, *ret_tables, row(g_ret_out[l]), tb=tb)
        fox = lax.cond(use_shift,
                       functools.partial(_fox, tq=tq, hg=4, shifted=True),
                       functools.partial(_fox, tq=tq, hg=4, shifted=False), qa, ka, va)
        k, v = _mem_kv(mem, row(g_mem[l]), w_xkv[l].astype(_BF16), row(g_xk[l]))
        h = _mix_xattn(h, ret, fox, w_out[l].astype(_BF16), row(g_xattn[l]), w_xq[l].astype(_BF16),
                       row(g_xq[l]), k, v, w_xo[l].astype(_BF16), tm=tm_mix)
        h = _ffn(h, row(g_ffn[l]), w_gate[l].astype(_BF16), w_up[l].astype(_BF16),
                 w_down[l].astype(_BF16), tm=tm_ffn, n_split=2)
    return h
```

```python
import functools

import numpy as np
import jax
import jax.numpy as jnp
from jax import lax
from jax.experimental import pallas as pl
from jax.experimental.pallas import tpu as pltpu

_BF16 = jnp.bfloat16
_F32 = jnp.float32

_EPS = 1e-6
_NEG_INF = -1e30
_ROPE_BASE = 10000.0
_HEAD_DIM = 64
_N_HEADS = 8
_RET_CHUNK = 64
_RET_GROUP = 256
_N_XHEADS = 4
_LANES = 128
_VMEM_LIMIT = 56 * 1024 * 1024

_AUG_HI, _AUG_MID, _AUG_LO = 64, 65, 66
_AUG_KHI, _AUG_KMID, _AUG_KLO = 67, 68, 69
_AUG_SHIFT = 70
_AUG_ONE = 64

_LOG2E = 1.4426950408889634
_MAX_FIXED_SHIFT = 32.0
_UNDERFLOW_LOG2 = -152.0
_TINY = 1e-30
_NORM_ROUNDING_SLACK = 1.01


def _params(sem):
    return pltpu.CompilerParams(dimension_semantics=sem, vmem_limit_bytes=_VMEM_LIMIT)


def _const_spec(shape):
    nd = len(shape)
    return pl.BlockSpec(shape, lambda *_: (0,) * nd, pipeline_mode=pl.Buffered(1))


def _rms(x, g):
    return x * lax.rsqrt(jnp.mean(x * x, axis=-1, keepdims=True) + _EPS) * g


def _split3(v):
    hi = v.astype(_BF16).astype(_F32)
    r = v - hi
    mid = r.astype(_BF16).astype(_F32)
    return hi, mid, r - mid


def _in_proj_body(x_ref, g_ref, w_ref, wff_ref, bf_ref, cos_ref, sin_ref, gq_ref, gk_ref, shift_ref,
                  tri_ref, avg_ref, place_ref, rq_ref, rk_ref, rv_ref, gate_ref, qa_ref, ka_ref, va_ref,
                  fend_ref, carry_ref, *, tm, width):
    i = pl.program_id(1)
    hb = _rms(x_ref[0], g_ref[...]).astype(_BF16)

    def proj(j):
        return jnp.dot(hb, w_ref[:, j * width:(j + 1) * width], preferred_element_type=_F32)

    lane = lax.broadcasted_iota(jnp.int32, (tm, _LANES), 1)
    low = lane < _HEAD_DIM
    first_half = (lane & (_HEAD_DIM // 2)) == 0
    n_pairs = width // _LANES

    cos = cos_ref[...]
    sin = sin_ref[...]
    for j, out_ref, scale in ((0, rq_ref, _HEAD_DIM ** -0.5), (1, rk_ref, None)):
        y = proj(j)
        for c in range(n_pairs):
            blk = y[:, c * _LANES:(c + 1) * _LANES]
            swapped = jnp.where(first_half, pltpu.roll(blk, _LANES - _HEAD_DIM // 2, 1),
                                pltpu.roll(blk, _HEAD_DIM // 2, 1))
            r = blk * cos + swapped * sin
            if scale is not None:
                r = r * scale
            out_ref[0, :, c * _LANES:(c + 1) * _LANES] = r.astype(_BF16)
    rv_ref[0] = proj(2).astype(_BF16)
    gate = proj(3)
    gate_ref[0] = (gate * jax.nn.sigmoid(gate)).astype(_BF16)

    z = jnp.dot(hb, wff_ref[...], preferred_element_type=_F32) + bf_ref[...]
    logf = (jnp.minimum(z, 0.0) - jnp.log(1.0 + jnp.exp(-jnp.abs(z)))) * _LOG2E
    hi, mid, lo = _split3(logf)
    tri = tri_ref[...]
    csum = (jnp.dot(tri, hi.astype(_BF16), preferred_element_type=_F32)
            + jnp.dot(tri, mid.astype(_BF16), preferred_element_type=_F32)
            + jnp.dot(tri, lo.astype(_BF16), preferred_element_type=_F32))

    @pl.when(i == 0)
    def _():
        carry_ref[...] = jnp.zeros_like(carry_ref)

    fcum = csum + carry_ref[0:1, :]
    carry_ref[...] = jnp.broadcast_to(fcum[tm - 1:tm, :], carry_ref.shape)
    fend_ref[0, 0] = carry_ref[...]

    def head_rms(y, g_row):
        out = []
        for g0 in range(0, width, _RET_GROUP):
            blk = y[:, g0:g0 + _RET_GROUP]
            ms = jnp.dot((blk * blk).astype(_BF16), avg_ref[...], preferred_element_type=_F32)
            out.append(blk * lax.rsqrt(ms + _EPS) * g_row[:, g0:g0 + _RET_GROUP])
        return out

    qn = head_rms(proj(4), gq_ref[...])
    kn = head_rms(proj(5), gk_ref[...])
    fv = proj(6)

    fh, fm, fl = _split3(fcum)
    packed = jnp.where(lane < _N_HEADS, fh, jnp.where(
        lane < 2 * _N_HEADS, pltpu.roll(fm, _N_HEADS, 1), pltpu.roll(fl, 2 * _N_HEADS, 1)))
    placed = jnp.dot(packed.astype(_BF16), place_ref[...], preferred_element_type=_F32)
    q_const = jnp.where((lane >= _AUG_KHI) & (lane <= _AUG_KLO), 1.0,
                        jnp.where(lane == _AUG_SHIFT, -shift_ref[...], 0.0))
    k_const = jnp.where(((lane >= _AUG_HI) & (lane <= _AUG_LO)) | (lane == _AUG_SHIFT), 1.0, 0.0)
    v_const = jnp.where(lane == _AUG_ONE, 1.0, 0.0)
    q_bias = (lane >= _AUG_HI) & (lane <= _AUG_LO)
    k_bias = (lane >= _AUG_KHI) & (lane <= _AUG_KLO)
    for h in range(_N_HEADS):
        g0, c0 = divmod(h * _HEAD_DIM, _RET_GROUP)
        c0 = (c0 // _LANES) * _LANES
        qb = qn[g0][:, c0:c0 + _LANES]
        kb = kn[g0][:, c0:c0 + _LANES]
        vb = fv[:, (h // 2) * _LANES:(h // 2 + 1) * _LANES]
        if h % 2 == 1:
            qb = pltpu.roll(qb, _HEAD_DIM, 1)
            kb = pltpu.roll(kb, _HEAD_DIM, 1)
            vb = pltpu.roll(vb, _HEAD_DIM, 1)
        bias = placed[:, h * _LANES:(h + 1) * _LANES]
        qa_ref[0, h] = jnp.where(low, qb, jnp.where(q_bias, bias, q_const)).astype(_BF16)
        ka_ref[0, h] = jnp.where(low, kb, jnp.where(k_bias, bias, k_const)).astype(_BF16)
        va_ref[0, h] = jnp.where(low, vb, v_const).astype(_BF16)


def _in_proj(x, g_mix, w_main, w_ff, b_ff, cos_t, sin_t, gq, gk, shift, tri, avg, place, *, tm):
    B, T, D = x.shape
    width = _N_HEADS * _HEAD_DIM
    tok = lambda b, i: (b, i, 0)
    head = lambda b, i: (b, 0, i, 0)
    bf_tok = jax.ShapeDtypeStruct((B, T, width), _BF16)
    bf_head = jax.ShapeDtypeStruct((B, _N_HEADS, T, _LANES), _BF16)
    return pl.pallas_call(
        functools.partial(_in_proj_body, tm=tm, width=width),
        grid=(B, T // tm),
        in_specs=[
            pl.BlockSpec((1, tm, D), tok),
            _const_spec((1, D)),
            _const_spec(w_main.shape),
            _const_spec(w_ff.shape),
            _const_spec((1, _LANES)),
            pl.BlockSpec((tm, _LANES), lambda b, i: (i, 0)),
            pl.BlockSpec((tm, _LANES), lambda b, i: (i, 0)),
            _const_spec((1, width)),
            _const_spec((1, width)),
            _const_spec((1, _LANES)),
            _const_spec((tm, tm)),
            _const_spec(avg.shape),
            _const_spec(place.shape),
        ],
        out_specs=[pl.BlockSpec((1, tm, width), tok)] * 4
        + [pl.BlockSpec((1, _N_HEADS, tm, _LANES), head)] * 3
        + [pl.BlockSpec((1, 1, 8, _LANES), lambda b, i: (b, i, 0, 0))],
        out_shape=[bf_tok] * 4 + [bf_head] * 3 + [jax.ShapeDtypeStruct((B, T // tm, 8, _LANES), _F32)],
        scratch_shapes=[pltpu.VMEM((8, _LANES), _F32)],
        compiler_params=_params(("arbitrary", "arbitrary")),
        name="in_proj",
    )(x, g_mix, w_main, w_ff, b_ff, cos_t, sin_t, gq, gk, shift, tri, avg, place)


def _retention_body(rq_ref, rk_ref, rv_ref, gate_ref, dmask_ref, qdec_ref, kdec_ref, sdec_ref, bd_ref,
                    avg_ref, g_ref, o_ref, state_ref, *, tb):
    i = pl.program_id(1)

    @pl.when(i == 0)
    def _():
        state_ref[...] = jnp.zeros_like(state_ref)

    nt = (((1,), (1,)), ((), ()))
    tn = (((0,), (0,)), ((), ()))
    gw = state_ref.shape[1]
    lane = lax.broadcasted_iota(jnp.int32, (1, _LANES), 1)
    low = lax.broadcasted_iota(jnp.int32, (tb, _LANES), 1) < _HEAD_DIM
    head_lanes = [jnp.where(lane < _HEAD_DIM, 1.0, 0.0).astype(_BF16),
                  jnp.where(lane < _HEAD_DIM, 0.0, 1.0).astype(_BF16)]
    for g in range(state_ref.shape[0]):
        gs = slice(g * gw, (g + 1) * gw)
        intra = []
        for c in range(gw // _LANES):
            ps = slice(g * gw + c * _LANES, g * gw + (c + 1) * _LANES)
            qp, kp, vp = rq_ref[0, :, ps], rk_ref[0, :, ps], rv_ref[0, :, ps]
            both = []
            for hh in range(2):
                h = (g * gw + c * _LANES) // _HEAD_DIM + hh
                scores = lax.dot_general(qp * head_lanes[hh], kp, nt,
                                         preferred_element_type=_F32) * dmask_ref[h]
                both.append(jnp.dot(scores.astype(_BF16), vp, preferred_element_type=_F32))
            intra.append(jnp.where(low, both[0], both[1]))
        state = state_ref[g]
        inter = jnp.dot(rq_ref[0, :, gs], state.astype(_BF16),
                        preferred_element_type=_F32) * qdec_ref[:, gs]
        kd = (rk_ref[0, :, gs].astype(_F32) * kdec_ref[:, gs]).astype(_BF16)
        state_ref[g] = state * sdec_ref[g] + lax.dot_general(
            kd, rv_ref[0, :, gs], tn, preferred_element_type=_F32) * bd_ref[...]
        o = jnp.concatenate(intra, axis=1) + inter
        mu = jnp.dot(o.astype(_BF16), avg_ref[...], preferred_element_type=_F32)
        oc = o - mu
        var = jnp.dot((oc * oc).astype(_BF16), avg_ref[...], preferred_element_type=_F32)
        y = oc * lax.rsqrt(var + _EPS) * g_ref[:, gs]
        o_ref[0, :, gs] = (y * gate_ref[0, :, gs].astype(_F32)).astype(_BF16)


def _retention(rq, rk, rv, gate, dmask, qdec, kdec, sdec, bd, avg, g_ret, *, tb):
    B, T, W = rq.shape
    tok = pl.BlockSpec((1, tb, W), lambda b, i: (b, i, 0))
    consts = (dmask, qdec, kdec, sdec, bd, avg, g_ret)
    return pl.pallas_call(
        functools.partial(_retention_body, tb=tb),
        grid=(B, T // tb),
        in_specs=[tok, tok, tok, tok] + [_const_spec(c.shape) for c in consts],
        out_specs=tok,
        out_shape=jax.ShapeDtypeStruct((B, T, W), _BF16),
        scratch_shapes=[pltpu.VMEM(sdec.shape, _F32)],
        compiler_params=_params(("arbitrary", "arbitrary")),
        name="retention",
    )(rq, rk, rv, gate, *consts)


def _causal_mask(tq):
    row = lax.broadcasted_iota(jnp.int32, (tq, tq), 0)
    col = lax.broadcasted_iota(jnp.int32, (tq, tq), 1)
    return row >= col


def _fox_finish(acc_refs, o_ref, tq):
    lane = lax.broadcasted_iota(jnp.int32, (tq, _LANES), 1)
    for c in range(len(acc_refs) // 2):
        pair = []
        for hh in (2 * c, 2 * c + 1):
            acc = acc_refs[hh][...]
            pair.append(acc / acc[:, _AUG_ONE:_AUG_ONE + 1])
        o_ref[0, :, c * _LANES:(c + 1) * _LANES] = jnp.where(
            lane < _HEAD_DIM, pair[0], pltpu.roll(pair[1], _HEAD_DIM, 1)).astype(_BF16)


def _fox_shifted_body(jlo_ref, qa_ref, ka_ref, va_ref, o_ref, acc_ref, it_i, it_j, it_off, it_keep,
                      *, tq, nq, unroll):
    b, pair = pl.program_id(0), pl.program_id(1)
    n_pair = qa_ref.shape[1]
    nt = (((1,), (1,)), ((), ()))
    diff = (lax.broadcasted_iota(jnp.int32, (tq, tq), 1)
            - lax.broadcasted_iota(jnp.int32, (tq, tq), 0))
    for hh in range(n_pair):
        head = pair * n_pair + hh
        n = jnp.int32(0)
        for i in range(nq):
            lo = jlo_ref[(b * (pl.num_programs(1) * n_pair) + head) * nq + i]

            def add(j, n, i=i, lo=lo):
                it_i[n] = jnp.int32(i)
                it_j[n] = j
                it_off[n] = jnp.where(j == i, 0, tq)
                it_keep[n] = jnp.where(j == lo, 0, 1)
                return n + 1

            n = lax.fori_loop(lo, i + 1, add, n)
        n_trips = (n + (unroll - 1)) // unroll

        def pad(m, carry):
            it_i[m] = jnp.int32(nq - 1)
            it_j[m] = jnp.int32(nq - 1)
            it_off[m] = jnp.int32(-tq)
            it_keep[m] = jnp.int32(1)
            return carry

        lax.fori_loop(n, n_trips * unroll, pad, 0)

        def trip(t, carry, hh=hh):
            items = [(it_i[t * unroll + u], it_j[t * unroll + u], it_off[t * unroll + u],
                      it_keep[t * unroll + u]) for u in range(unroll)]

            def qk(u):
                i, j = items[u][0], items[u][1]
                return lax.dot_general(qa_ref[0, hh, pl.ds(pl.multiple_of(i * tq, tq), tq), :],
                                       ka_ref[0, hh, pl.ds(pl.multiple_of(j * tq, tq), tq), :], nt,
                                       preferred_element_type=_F32)

            def finish(u, s):
                i, j, off, keep = items[u]
                p = jnp.exp2(jnp.where(diff <= off, s, _NEG_INF)).astype(_BF16)
                acc = jnp.where(keep > 0, acc_ref[...], 0.0) + jnp.dot(
                    p, va_ref[0, hh, pl.ds(pl.multiple_of(j * tq, tq), tq), :], preferred_element_type=_F32)
                acc_ref[...] = acc
                o = acc / jnp.maximum(acc[:, _AUG_ONE:_AUG_ONE + 1], _TINY)
                if hh % 2 == 1:
                    o = pltpu.roll(o, _HEAD_DIM, 1)
                sl = slice(hh * _HEAD_DIM, (hh + 1) * _HEAD_DIM)
                o_ref[0, pl.ds(pl.multiple_of(i * tq, tq), tq), sl] = o[:, sl].astype(_BF16)

            s_prev = qk(0)
            for u in range(1, unroll):
                s_next = qk(u)
                finish(u - 1, s_prev)
                s_prev = s_next
            finish(unroll - 1, s_prev)
            return carry

        lax.fori_loop(0, n_trips, trip, 0)


def _fox_online_body(qa_ref, ka_ref, va_ref, o_ref, *scratch, tq):
    qi = pl.program_id(2)
    nt = (((1,), (1,)), ((), ()))
    n_heads = qa_ref.shape[1]
    m_refs, acc_refs = scratch[:n_heads], scratch[n_heads:]
    for hh in range(n_heads):
        m_refs[hh][...] = jnp.full_like(m_refs[hh], _NEG_INF)
        acc_refs[hh][...] = jnp.zeros_like(acc_refs[hh])

    def step(j, masked):
        start = pl.multiple_of(j * tq, tq)
        logits = [lax.dot_general(qa_ref[0, hh], ka_ref[0, hh, pl.ds(start, tq), :], nt,
                                  preferred_element_type=_F32) for hh in range(n_heads)]
        for hh in range(n_heads):
            m_ref, acc_ref = m_refs[hh], acc_refs[hh]
            s = logits[hh]
            if masked:
                s = jnp.where(_causal_mask(tq), s, _NEG_INF)
            m_old = m_ref[...]
            m_new = jnp.maximum(m_old, jnp.max(s, axis=-1, keepdims=True))
            p = jnp.exp2(s - m_new[:, 0:1])
            acc_ref[...] = jnp.exp2(m_old - m_new) * acc_ref[...] + jnp.dot(
                p.astype(_BF16), va_ref[0, hh, pl.ds(start, tq), :], preferred_element_type=_F32)
            m_ref[...] = m_new

    def off_diag(j, carry):
        step(j, False)
        return carry

    lax.fori_loop(0, qi, off_diag, 0)
    step(qi, True)
    _fox_finish(acc_refs, o_ref, tq)


def _fox_shifted(jlo, qa, ka, va, *, tq, unroll):
    B, H, T, L = qa.shape
    nq = T // tq
    max_items = nq * (nq + 1) // 2 + unroll
    blk = pl.BlockSpec((1, 2, T, L), lambda b, p, jlo_ref: (b, p, 0, 0))
    return pl.pallas_call(
        functools.partial(_fox_shifted_body, tq=tq, nq=nq, unroll=unroll),
        grid_spec=pltpu.PrefetchScalarGridSpec(
            num_scalar_prefetch=1,
            grid=(B, H // 2),
            in_specs=[blk, blk, blk],
            out_specs=pl.BlockSpec((1, T, L), lambda b, p, jlo_ref: (b, 0, p)),
            scratch_shapes=[pltpu.VMEM((tq, L), _F32)] + [pltpu.SMEM((max_items,), jnp.int32)] * 4,
        ),
        out_shape=jax.ShapeDtypeStruct((B, T, H * _HEAD_DIM), _BF16),
        compiler_params=_params(("arbitrary", "arbitrary")),
        name="fox_shifted",
    )(jlo, qa, ka, va)


def _fox_online(qa, ka, va, *, tq, hg):
    B, H, T, L = qa.shape
    return pl.pallas_call(
        functools.partial(_fox_online_body, tq=tq),
        grid=(B, H // hg, T // tq),
        in_specs=[
            pl.BlockSpec((1, hg, tq, L), lambda b, p, i: (b, p, i, 0)),
            pl.BlockSpec((1, hg, T, L), lambda b, p, i: (b, p, 0, 0)),
            pl.BlockSpec((1, hg, T, L), lambda b, p, i: (b, p, 0, 0)),
        ],
        out_specs=pl.BlockSpec((1, tq, hg * _HEAD_DIM), lambda b, p, i: (b, i, p)),
        out_shape=jax.ShapeDtypeStruct((B, T, H * _HEAD_DIM), _BF16),
        scratch_shapes=[pltpu.VMEM((tq, L), _F32)] * (2 * hg),
        compiler_params=_params(("arbitrary", "arbitrary", "arbitrary")),
        name="fox_online",
    )(qa, ka, va)


def _first_live_block(fend):
    B, nq, H = fend.shape
    f = jnp.transpose(fend, (0, 2, 1))
    top = jnp.concatenate([jnp.zeros((B, H, 1), _F32), f[:, :, :-1]], axis=-1)
    dead = (top[:, :, :, None] - f[:, :, None, :]) <= _UNDERFLOW_LOG2
    before = jnp.arange(nq)[None, :] < jnp.arange(nq)[:, None]
    leading = jnp.cumprod((dead & before).astype(jnp.int32), axis=-1)
    return jnp.sum(leading, axis=-1).astype(jnp.int32).reshape(-1)


def _mem_kv_body(mem_ref, gm_ref, w_ref, gk_ref, k_ref, v_ref, *, d_model, xd):
    mn = _rms(mem_ref[0], gm_ref[...]).astype(_BF16)
    kv = jnp.dot(mn, w_ref[...], preferred_element_type=_F32)
    for h in range(_N_XHEADS):
        sl = slice(h * xd, (h + 1) * xd)
        k_ref[0, :, sl] = (_rms(kv[:, sl], gk_ref[...]) * (xd ** -0.5)).astype(_BF16)
    v_ref[0] = kv[:, d_model:].astype(_BF16)


def _mem_kv(mem, g_mem, w_xkv, g_xk):
    B, M, D = mem.shape
    xd = D // _N_XHEADS
    blk = pl.BlockSpec((1, M, D), lambda b: (b, 0, 0))
    out = jax.ShapeDtypeStruct((B, M, D), _BF16)
    return pl.pallas_call(
        functools.partial(_mem_kv_body, d_model=D, xd=xd),
        grid=(B,),
        in_specs=[blk, _const_spec((1, D)), _const_spec(w_xkv.shape), _const_spec((1, xd))],
        out_specs=[blk, blk],
        out_shape=[out, out],
        compiler_params=_params(("arbitrary",)),
        name="mem_kv",
    )(mem, g_mem, w_xkv, g_xk)


def _mix_xattn_body(x_ref, ret_ref, fox_ref, wo_ref, gx_ref, wq_ref, gq_ref, k_ref, v_ref, wxo_ref,
                    o_ref, *, width, xd):
    h1 = (x_ref[0]
          + jnp.dot(ret_ref[0], wo_ref[:width, :], preferred_element_type=_F32)
          + jnp.dot(fox_ref[0], wo_ref[width:, :], preferred_element_type=_F32))
    hn = _rms(h1, gx_ref[...]).astype(_BF16)
    q = jnp.dot(hn, wq_ref[...], preferred_element_type=_F32)
    nt = (((1,), (1,)), ((), ()))
    heads = []
    for h in range(_N_XHEADS):
        sl = slice(h * xd, (h + 1) * xd)
        qn = _rms(q[:, sl], gq_ref[...]).astype(_BF16)
        logits = lax.dot_general(qn, k_ref[0, :, sl], nt, preferred_element_type=_F32)
        p = jnp.exp(logits - jnp.max(logits, axis=-1, keepdims=True))
        p = p / jnp.sum(p, axis=-1, keepdims=True)
        heads.append(jnp.dot(p.astype(_BF16), v_ref[0, :, sl], preferred_element_type=_F32).astype(_BF16))
    o = jnp.concatenate(heads, axis=-1)
    o_ref[0] = h1 + jnp.dot(o, wxo_ref[...], preferred_element_type=_F32)


def _mix_xattn(x, ret, fox, w_out, g_xattn, w_xq, g_xq, k, v, w_xo, *, tm):
    B, T, D = x.shape
    W = ret.shape[-1]
    M = k.shape[1]
    xd = D // _N_XHEADS
    tok = lambda b, i: (b, i, 0)
    return pl.pallas_call(
        functools.partial(_mix_xattn_body, width=W, xd=xd),
        grid=(B, T // tm),
        in_specs=[
            pl.BlockSpec((1, tm, D), tok),
            pl.BlockSpec((1, tm, W), tok),
            pl.BlockSpec((1, tm, W), tok),
            _const_spec(w_out.shape),
            _const_spec((1, D)),
            _const_spec(w_xq.shape),
            _const_spec((1, xd)),
            pl.BlockSpec((1, M, D), lambda b, i: (b, 0, 0)),
            pl.BlockSpec((1, M, D), lambda b, i: (b, 0, 0)),
            _const_spec(w_xo.shape),
        ],
        out_specs=pl.BlockSpec((1, tm, D), tok),
        out_shape=jax.ShapeDtypeStruct((B, T, D), _F32),
        compiler_params=_params(("arbitrary", "arbitrary")),
        name="mix_xattn",
    )(x, ret, fox, w_out, g_xattn, w_xq, g_xq, k, v, w_xo)


def _ffn_body(h_ref, g_ref, wg_ref, wu_ref, wd_ref, o_ref, *, n_split, fs):
    h = h_ref[0]
    hn = _rms(h, g_ref[...]).astype(_BF16)
    acc = h
    for c in range(n_split):
        sl = slice(c * fs, (c + 1) * fs)
        gate = jnp.dot(hn, wg_ref[:, sl], preferred_element_type=_F32)
        up = jnp.dot(hn, wu_ref[:, sl], preferred_element_type=_F32)
        a = (gate * jax.nn.sigmoid(gate) * up).astype(_BF16)
        acc = acc + jnp.dot(a, wd_ref[sl, :], preferred_element_type=_F32)
    o_ref[0] = acc


def _ffn(h, g_ffn, w_gate, w_up, w_down, *, tm, n_split):
    B, T, D = h.shape
    F = w_gate.shape[1]
    tok = pl.BlockSpec((1, tm, D), lambda b, i: (b, i, 0))
    return pl.pallas_call(
        functools.partial(_ffn_body, n_split=n_split, fs=F // n_split),
        grid=(B, T // tm),
        in_specs=[tok, _const_spec((1, D)), _const_spec(w_gate.shape), _const_spec(w_up.shape),
                  _const_spec(w_down.shape)],
        out_specs=tok,
        out_shape=jax.ShapeDtypeStruct((B, T, D), _F32),
        compiler_params=_params(("arbitrary", "arbitrary")),
        name="ffn",
    )(h, g_ffn, w_gate, w_up, w_down)


def _rope_tables(T):
    half = _HEAD_DIM // 2
    inv_freq = (_ROPE_BASE ** (-np.arange(0, _HEAD_DIM, 2, dtype=np.float32) / _HEAD_DIM)).astype(np.float32)
    ang = (np.arange(T, dtype=np.float32)[:, None] * inv_freq[None, :]).astype(np.float32).astype(np.float64)
    cos, sin = np.cos(ang), np.sin(ang)
    reps = _LANES // _HEAD_DIM
    cos_t = np.tile(np.concatenate([cos, cos], axis=1), (1, reps))
    sin_t = np.tile(np.concatenate([-sin, sin], axis=1), (1, reps))
    return jnp.asarray(cos_t, _F32), jnp.asarray(sin_t, _F32)


def _retention_tables(tb):
    log_g = np.log(1.0 - 2.0 ** (-5.0 - np.arange(_N_HEADS, dtype=np.float64)))
    idx = np.arange(tb, dtype=np.float64)
    dist = np.abs(idx[:, None] - idx[None, :])
    chunk = np.arange(tb) // _RET_CHUNK
    visible = chunk[None, :] <= chunk[:, None]
    dmask = np.where(visible[None], np.exp(log_g[:, None, None] * dist[None]), 0.0)
    qdec = np.repeat(np.exp(log_g[None, :] * (idx[:, None] + 1.0)), _HEAD_DIM, axis=1)
    kdec = np.repeat(np.exp(log_g[None, :] * (tb - 1.0 - idx[:, None])), _HEAD_DIM, axis=1)
    heads_per_group = _RET_GROUP // _HEAD_DIM
    head_of = np.arange(_RET_GROUP) // _HEAD_DIM
    bd = (head_of[:, None] == head_of[None, :]).astype(np.float64)
    step_decay = np.exp(log_g * tb).reshape(-1, heads_per_group)
    sdec = bd[None] * np.repeat(step_decay, _HEAD_DIM, axis=1)[:, None, :]
    f = lambda a: jnp.asarray(a, _F32)
    return f(dmask), f(qdec), f(kdec), f(sdec), f(bd), jnp.asarray(bd / _HEAD_DIM, _BF16)


def _bias_placement():
    place = np.zeros((_LANES, _N_HEADS * _LANES), np.float32)
    for h in range(_N_HEADS):
        for part, (q_lane, k_lane) in enumerate(((_AUG_HI, _AUG_KHI), (_AUG_MID, _AUG_KMID),
                                                 (_AUG_LO, _AUG_KLO))):
            place[part * _N_HEADS + h, h * _LANES + q_lane] = 1.0
            place[part * _N_HEADS + h, h * _LANES + k_lane] = -1.0
    return jnp.asarray(place, _BF16)


def _pad_lanes(a):
    return jnp.pad(a, [(0, 0)] * (a.ndim - 1) + [(0, _LANES - a.shape[-1])])


def kernel(x, mem, g_mix, w_in, b_forget, g_ret_out, g_fox_q, g_fox_k, w_out, g_xattn, w_xq, w_xkv,
           g_mem, g_xq, g_xk, w_xo, g_ffn, w_gate, w_up, w_down):
    B, T, D = x.shape
    width = _N_HEADS * _HEAD_DIM
    tb, tq, tm_mix, tm_ffn = 256, 512, 512, 512
    cos_t, sin_t = _rope_tables(T)
    ret_tables = _retention_tables(tb)
    tri = jnp.asarray(np.tril(np.ones((tq, tq), np.float32)), _BF16)
    place = _bias_placement()
    row = lambda a: a.reshape(1, -1).astype(_F32)

    h = x
    for l in range(w_in.shape[0]):
        w_main = w_in[l, :, :7 * width].astype(_BF16)
        w_ff = _pad_lanes(w_in[l, :, 7 * width:]).astype(_BF16)
        b_ff = _pad_lanes(row(b_forget[l]))
        gq = jnp.tile(row(g_fox_q[l]), (1, _N_HEADS)) * (_LOG2E * _HEAD_DIM ** -0.5)
        gk = jnp.tile(row(g_fox_k[l]), (1, _N_HEADS))
        bound = (_HEAD_DIM ** 0.5 * _NORM_ROUNDING_SLACK) * jnp.max(jnp.abs(g_fox_q[l])) * jnp.max(
            jnp.abs(g_fox_k[l]))
        use_shift = bound <= _MAX_FIXED_SHIFT
        shift = jnp.where(use_shift, jnp.ceil(bound * (4.0 * _LOG2E)) * 0.25, 0.0).astype(_F32)
        rq, rk, rv, gate, qa, ka, va, fend = _in_proj(h, row(g_mix[l]), w_main, w_ff, b_ff, cos_t, sin_t,
                                                      gq, gk, jnp.full((1, _LANES), shift), tri,
                                                      ret_tables[-1], place, tm=tq)
        ret = _retention(rq, rk, rv, gate, *ret_tables, row(g_ret_out[l]), tb=tb)
        jlo = _first_live_block(fend[:, :, 0, :_N_HEADS])
        fox = lax.cond(use_shift,
                       lambda jlo, qa, ka, va: _fox_shifted(jlo, qa, ka, va, tq=tq, unroll=4),
                       lambda jlo, qa, ka, va: _fox_online(qa, ka, va, tq=tq, hg=4), jlo, qa, ka, va)
        k, v = _mem_kv(mem, row(g_mem[l]), w_xkv[l].astype(_BF16), row(g_xk[l]))
        h = _mix_xattn(h, ret, fox, w_out[l].astype(_BF16), row(g_xattn[l]), w_xq[l].astype(_BF16),
                       row(g_xq[l]), k, v, w_xo[l].astype(_BF16), tm=tm_mix)
        h = _ffn(h, row(g_ffn[l]), w_gate[l].astype(_BF16), w_up[l].astype(_BF16),
                 w_down[l].astype(_BF16), tm=tm_ffn, n_split=2)
    return h
```

```python
import functools

import numpy as np
import jax
import jax.numpy as jnp
from jax import lax
from jax.experimental import pallas as pl
from jax.experimental.pallas import tpu as pltpu

_BF16 = jnp.bfloat16
_F32 = jnp.float32

_EPS = 1e-6
_NEG_INF = -1e30
_ROPE_BASE = 10000.0
_HEAD_DIM = 64
_N_HEADS = 8
_RET_CHUNK = 64
_RET_GROUP = 256
_N_XHEADS = 4
_LANES = 128
_MXU_TILE = 256
_VMEM_LIMIT = 56 * 1024 * 1024

_AUG_HI, _AUG_MID, _AUG_LO = 64, 65, 66
_AUG_KHI, _AUG_KMID, _AUG_KLO = 67, 68, 69
_AUG_SHIFT = 70
_AUG_ONE = 64

_LOG2E = 1.4426950408889634
_MAX_FIXED_SHIFT = 32.0
_UNDERFLOW_LOG2 = -152.0
_TINY = 1e-30
_NORM_ROUNDING_SLACK = 1.01


def _params(sem):
    return pltpu.CompilerParams(dimension_semantics=sem, vmem_limit_bytes=_VMEM_LIMIT)


def _const_spec(shape):
    nd = len(shape)
    return pl.BlockSpec(shape, lambda *_: (0,) * nd, pipeline_mode=pl.Buffered(1))


def _rms(x, g):
    return x * lax.rsqrt(jnp.mean(x * x, axis=-1, keepdims=True) + _EPS) * g


def _rms_split(x, g):
    return (x * g).astype(_BF16), lax.rsqrt(jnp.mean(x * x, axis=-1, keepdims=True) + _EPS)


def _split3(v):
    hi = v.astype(_BF16).astype(_F32)
    r = v - hi
    mid = r.astype(_BF16).astype(_F32)
    return hi, mid, r - mid


def _in_proj_body(x_ref, g_ref, w_ref, wff_ref, bf_ref, cos_ref, sin_ref, gq_ref, gk_ref, shift_ref,
                  tri_ref, avg_ref, place_ref, rq_ref, rk_ref, rv_ref, gate_ref, qa_ref, ka_ref, va_ref,
                  fend_ref, carry_ref, *, tm, width):
    i = pl.program_id(1)
    hb, inv_rms = _rms_split(x_ref[0], g_ref[...])

    def proj(j):
        return jnp.dot(hb, w_ref[:, j * width:(j + 1) * width], preferred_element_type=_F32) * inv_rms

    lane = lax.broadcasted_iota(jnp.int32, (tm, _LANES), 1)
    low = lane < _HEAD_DIM
    first_half = (lane & (_HEAD_DIM // 2)) == 0
    n_pairs = width // _LANES

    cos = cos_ref[...]
    sin = sin_ref[...]
    for j, out_ref, scale in ((0, rq_ref, _HEAD_DIM ** -0.5), (1, rk_ref, None)):
        y = proj(j)
        for c in range(n_pairs):
            blk = y[:, c * _LANES:(c + 1) * _LANES]
            swapped = jnp.where(first_half, pltpu.roll(blk, _LANES - _HEAD_DIM // 2, 1),
                                pltpu.roll(blk, _HEAD_DIM // 2, 1))
            r = blk * cos + swapped * sin
            if scale is not None:
                r = r * scale
            out_ref[0, :, c * _LANES:(c + 1) * _LANES] = r.astype(_BF16)
    rv_ref[0] = proj(2).astype(_BF16)
    gate = proj(3)
    gate_ref[0] = (gate * jax.nn.sigmoid(gate)).astype(_BF16)

    half = tm // 2
    z = jnp.concatenate([jnp.dot(hb[:half], wff_ref[...], preferred_element_type=_F32),
                         jnp.dot(hb[half:], wff_ref[...], preferred_element_type=_F32)],
                        axis=0) * inv_rms + bf_ref[...]
    logf = (jnp.minimum(z, 0.0) - jnp.log(1.0 + jnp.exp(-jnp.abs(z)))) * _LOG2E
    hi, mid, lo = _split3(logf)
    tri = tri_ref[...]
    split = jnp.concatenate([hi, mid, lo], axis=1).astype(_BF16)
    parts = jnp.concatenate(
        [jnp.dot(tri[:half, :half], split[:half], preferred_element_type=_F32),
         jnp.dot(tri[half:], split, preferred_element_type=_F32)], axis=0)
    csum = parts[:, :_LANES] + parts[:, _LANES:2 * _LANES] + parts[:, 2 * _LANES:]

    @pl.when(i == 0)
    def _():
        carry_ref[...] = jnp.zeros_like(carry_ref)

    fcum = csum + carry_ref[0:1, :]
    carry_ref[...] = jnp.broadcast_to(fcum[tm - 1:tm, :], carry_ref.shape)
    fend_ref[0, 0] = carry_ref[...]

    def head_rms(y, g_row):
        out = []
        for g0 in range(0, width, _RET_GROUP):
            blk = y[:, g0:g0 + _RET_GROUP]
            ms = jnp.dot((blk * blk).astype(_BF16), avg_ref[...], preferred_element_type=_F32)
            out.append(blk * lax.rsqrt(ms + _EPS) * g_row[:, g0:g0 + _RET_GROUP])
        return out

    qn = head_rms(proj(4), gq_ref[...])
    kn = head_rms(proj(5), gk_ref[...])
    fv = proj(6)

    fh, fm, fl = _split3(fcum)
    packed = jnp.where(lane < _N_HEADS, fh, jnp.where(
        lane < 2 * _N_HEADS, pltpu.roll(fm, _N_HEADS, 1), pltpu.roll(fl, 2 * _N_HEADS, 1)))
    placed = jnp.dot(packed.astype(_BF16), place_ref[...], preferred_element_type=_F32)
    q_const = jnp.where((lane >= _AUG_KHI) & (lane <= _AUG_KLO), 1.0,
                        jnp.where(lane == _AUG_SHIFT, -shift_ref[...], 0.0))
    k_const = jnp.where(((lane >= _AUG_HI) & (lane <= _AUG_LO)) | (lane == _AUG_SHIFT), 1.0, 0.0)
    v_const = jnp.where(lane == _AUG_ONE, 1.0, 0.0)
    q_bias = (lane >= _AUG_HI) & (lane <= _AUG_LO)
    k_bias = (lane >= _AUG_KHI) & (lane <= _AUG_KLO)
    for h in range(_N_HEADS):
        g0, c0 = divmod(h * _HEAD_DIM, _RET_GROUP)
        c0 = (c0 // _LANES) * _LANES
        qb = qn[g0][:, c0:c0 + _LANES]
        kb = kn[g0][:, c0:c0 + _LANES]
        vb = fv[:, (h // 2) * _LANES:(h // 2 + 1) * _LANES]
        if h % 2 == 1:
            qb = pltpu.roll(qb, _HEAD_DIM, 1)
            kb = pltpu.roll(kb, _HEAD_DIM, 1)
            vb = pltpu.roll(vb, _HEAD_DIM, 1)
        bias = placed[:, h * _LANES:(h + 1) * _LANES]
        qa_ref[0, h] = jnp.where(low, qb, jnp.where(q_bias, bias, q_const)).astype(_BF16)
        ka_ref[0, h] = jnp.where(low, kb, jnp.where(k_bias, bias, k_const)).astype(_BF16)
        va_ref[0, h] = jnp.where(low, vb, v_const).astype(_BF16)


def _in_proj(x, g_mix, w_main, w_ff, b_ff, cos_t, sin_t, gq, gk, shift, tri, avg, place, *, tm):
    B, T, D = x.shape
    width = _N_HEADS * _HEAD_DIM
    tok = lambda b, i: (b, i, 0)
    head = lambda b, i: (b, 0, i, 0)
    bf_tok = jax.ShapeDtypeStruct((B, T, width), _BF16)
    bf_head = jax.ShapeDtypeStruct((B, _N_HEADS, T, _LANES), _BF16)
    return pl.pallas_call(
        functools.partial(_in_proj_body, tm=tm, width=width),
        grid=(B, T // tm),
        in_specs=[
            pl.BlockSpec((1, tm, D), tok),
            _const_spec((1, D)),
            _const_spec(w_main.shape),
            _const_spec(w_ff.shape),
            _const_spec((1, _LANES)),
            pl.BlockSpec((tm, _LANES), lambda b, i: (i, 0)),
            pl.BlockSpec((tm, _LANES), lambda b, i: (i, 0)),
            _const_spec((1, width)),
            _const_spec((1, width)),
            _const_spec((1, _LANES)),
            _const_spec((tm, tm)),
            _const_spec(avg.shape),
            _const_spec(place.shape),
        ],
        out_specs=[pl.BlockSpec((1, tm, width), tok)] * 4
        + [pl.BlockSpec((1, _N_HEADS, tm, _LANES), head)] * 3
        + [pl.BlockSpec((1, 1, 8, _LANES), lambda b, i: (b, i, 0, 0))],
        out_shape=[bf_tok] * 4 + [bf_head] * 3 + [jax.ShapeDtypeStruct((B, T // tm, 8, _LANES), _F32)],
        scratch_shapes=[pltpu.VMEM((8, _LANES), _F32)],
        compiler_params=_params(("arbitrary", "arbitrary")),
        name="in_proj",
    )(x, g_mix, w_main, w_ff, b_ff, cos_t, sin_t, gq, gk, shift, tri, avg, place)


def _retention_body(rq_ref, rk_ref, rv_ref, gate_ref, dmask_ref, qdec_ref, kdec_ref, sdec_ref, bd_ref,
                    avg_ref, g_ref, o_ref, state_ref, *, tb):
    i = pl.program_id(1)

    @pl.when(i == 0)
    def _():
        state_ref[...] = jnp.zeros_like(state_ref)

    nt = (((1,), (1,)), ((), ()))
    tn = (((0,), (0,)), ((), ()))
    gw = state_ref.shape[1]
    lane = lax.broadcasted_iota(jnp.int32, (1, _LANES), 1)
    low = lax.broadcasted_iota(jnp.int32, (tb, _LANES), 1) < _HEAD_DIM
    head_lanes = [jnp.where(lane < _HEAD_DIM, 1.0, 0.0).astype(_BF16),
                  jnp.where(lane < _HEAD_DIM, 0.0, 1.0).astype(_BF16)]
    n_groups = state_ref.shape[0]
    groups = [slice(g * gw, (g + 1) * gw) for g in range(n_groups)]
    scores = []
    for h in range(_N_HEADS):
        ps = slice((h // 2) * _LANES, (h // 2 + 1) * _LANES)
        scores.append(lax.dot_general(rq_ref[0, :, ps] * head_lanes[h % 2], rk_ref[0, :, ps], nt,
                                      preferred_element_type=_F32))
    inter = []
    for g, gs in enumerate(groups):
        state = state_ref[g]
        inter.append(jnp.dot(rq_ref[0, :, gs], state.astype(_BF16),
                             preferred_element_type=_F32) * qdec_ref[:, gs])
        kd = (rk_ref[0, :, gs].astype(_F32) * kdec_ref[:, gs]).astype(_BF16)
        state_ref[g] = state * sdec_ref[g] + lax.dot_general(
            kd, rv_ref[0, :, gs], tn, preferred_element_type=_F32) * bd_ref[...]
    intra = []
    for h in range(_N_HEADS):
        ps = slice((h // 2) * _LANES, (h // 2 + 1) * _LANES)
        intra.append(jnp.dot((scores[h] * dmask_ref[h]).astype(_BF16), rv_ref[0, :, ps],
                             preferred_element_type=_F32))
    outs = []
    for g, gs in enumerate(groups):
        pairs = [jnp.where(low, intra[2 * c], intra[2 * c + 1])
                 for c in range(g * gw // _LANES, (g + 1) * gw // _LANES)]
        outs.append(jnp.concatenate(pairs, axis=1) + inter[g])
    mus = [jnp.dot(o.astype(_BF16), avg_ref[...], preferred_element_type=_F32) for o in outs]
    cents = [o - mu for o, mu in zip(outs, mus)]
    vars_ = [jnp.dot((oc * oc).astype(_BF16), avg_ref[...], preferred_element_type=_F32) for oc in cents]
    for gs, oc, var in zip(groups, cents, vars_):
        y = oc * lax.rsqrt(var + _EPS) * g_ref[:, gs]
        o_ref[0, :, gs] = (y * gate_ref[0, :, gs].astype(_F32)).astype(_BF16)


def _retention(rq, rk, rv, gate, dmask, qdec, kdec, sdec, bd, avg, g_ret, *, tb):
    B, T, W = rq.shape
    tok = pl.BlockSpec((1, tb, W), lambda b, i: (b, i, 0))
    consts = (dmask, qdec, kdec, sdec, bd, avg, g_ret)
    return pl.pallas_call(
        functools.partial(_retention_body, tb=tb),
        grid=(B, T // tb),
        in_specs=[tok, tok, tok, tok] + [_const_spec(c.shape) for c in consts],
        out_specs=tok,
        out_shape=jax.ShapeDtypeStruct((B, T, W), _BF16),
        scratch_shapes=[pltpu.VMEM(sdec.shape, _F32)],
        compiler_params=_params(("arbitrary", "arbitrary")),
        name="retention",
    )(rq, rk, rv, gate, *consts)


def _causal_mask(tq):
    row = lax.broadcasted_iota(jnp.int32, (tq, tq), 0)
    col = lax.broadcasted_iota(jnp.int32, (tq, tq), 1)
    return row >= col


def _fox_finish(acc_refs, o_ref, tq):
    lane = lax.broadcasted_iota(jnp.int32, (tq, _LANES), 1)
    for c in range(len(acc_refs) // 2):
        pair = []
        for hh in (2 * c, 2 * c + 1):
            acc = acc_refs[hh][...]
            pair.append(acc / acc[:, _AUG_ONE:_AUG_ONE + 1])
        o_ref[0, :, c * _LANES:(c + 1) * _LANES] = jnp.where(
            lane < _HEAD_DIM, pair[0], pltpu.roll(pair[1], _HEAD_DIM, 1)).astype(_BF16)


def _fox_shifted_body(jlo_ref, qa_ref, ka_ref, va_ref, o_ref, acc_ref, it_i, it_j, it_off, it_keep,
                      *, tq, nq, unroll):
    b, pair = pl.program_id(0), pl.program_id(1)
    n_pair = qa_ref.shape[1]
    nt = (((1,), (1,)), ((), ()))
    diff = (lax.broadcasted_iota(jnp.int32, (tq, tq), 1)
            - lax.broadcasted_iota(jnp.int32, (tq, tq), 0))
    for hh in range(n_pair):
        head = pair * n_pair + hh
        n = jnp.int32(0)
        for i in range(nq):
            lo = jlo_ref[(b * (pl.num_programs(1) * n_pair) + head) * nq + i]

            def add(j, n, i=i, lo=lo):
                it_i[n] = jnp.int32(i)
                it_j[n] = j
                it_off[n] = jnp.where(j == i, 0, tq)
                it_keep[n] = jnp.where(j == lo, 0, 1)
                return n + 1

            n = lax.fori_loop(lo, i + 1, add, n)
        n_trips = (n + (unroll - 1)) // unroll

        def pad(m, carry):
            it_i[m] = jnp.int32(nq - 1)
            it_j[m] = jnp.int32(nq - 1)
            it_off[m] = jnp.int32(-tq)
            it_keep[m] = jnp.int32(1)
            return carry

        lax.fori_loop(n, n_trips * unroll, pad, 0)

        def trip(t, carry, hh=hh):
            items = [(it_i[t * unroll + u], it_j[t * unroll + u], it_off[t * unroll + u],
                      it_keep[t * unroll + u]) for u in range(unroll)]

            def qk(u):
                i, j = items[u][0], items[u][1]
                return lax.dot_general(qa_ref[0, hh, pl.ds(pl.multiple_of(i * tq, tq), tq), :],
                                       ka_ref[0, hh, pl.ds(pl.multiple_of(j * tq, tq), tq), :], nt,
                                       preferred_element_type=_F32)

            def finish(u, s):
                i, j, off, keep = items[u]
                p = jnp.exp2(jnp.where(diff <= off, s, _NEG_INF)).astype(_BF16)
                acc = jnp.where(keep > 0, acc_ref[...], 0.0) + jnp.dot(
                    p, va_ref[0, hh, pl.ds(pl.multiple_of(j * tq, tq), tq), :], preferred_element_type=_F32)
                acc_ref[...] = acc
                o = acc / jnp.maximum(acc[:, _AUG_ONE:_AUG_ONE + 1], _TINY)
                if hh % 2 == 1:
                    o = pltpu.roll(o, _HEAD_DIM, 1)
                sl = slice(hh * _HEAD_DIM, (hh + 1) * _HEAD_DIM)
                o_ref[0, pl.ds(pl.multiple_of(i * tq, tq), tq), sl] = o[:, sl].astype(_BF16)

            s_prev = qk(0)
            for u in range(1, unroll):
                s_next = qk(u)
                finish(u - 1, s_prev)
                s_prev = s_next
            finish(unroll - 1, s_prev)
            return carry

        lax.fori_loop(0, n_trips, trip, 0)


def _fox_online_body(qa_ref, ka_ref, va_ref, o_ref, *scratch, tq):
    qi = pl.program_id(2)
    nt = (((1,), (1,)), ((), ()))
    n_heads = qa_ref.shape[1]
    m_refs, acc_refs = scratch[:n_heads], scratch[n_heads:]
    for hh in range(n_heads):
        m_refs[hh][...] = jnp.full_like(m_refs[hh], _NEG_INF)
        acc_refs[hh][...] = jnp.zeros_like(acc_refs[hh])

    def step(j, masked):
        start = pl.multiple_of(j * tq, tq)
        logits = [lax.dot_general(qa_ref[0, hh], ka_ref[0, hh, pl.ds(start, tq), :], nt,
                                  preferred_element_type=_F32) for hh in range(n_heads)]
        for hh in range(n_heads):
            m_ref, acc_ref = m_refs[hh], acc_refs[hh]
            s = logits[hh]
            if masked:
                s = jnp.where(_causal_mask(tq), s, _NEG_INF)
            m_old = m_ref[...]
            m_new = jnp.maximum(m_old, jnp.max(s, axis=-1, keepdims=True))
            p = jnp.exp2(s - m_new[:, 0:1])
            acc_ref[...] = jnp.exp2(m_old - m_new) * acc_ref[...] + jnp.dot(
                p.astype(_BF16), va_ref[0, hh, pl.ds(start, tq), :], preferred_element_type=_F32)
            m_ref[...] = m_new

    def off_diag(j, carry):
        step(j, False)
        return carry

    lax.fori_loop(0, qi, off_diag, 0)
    step(qi, True)
    _fox_finish(acc_refs, o_ref, tq)


def _fox_shifted(jlo, qa, ka, va, *, tq, unroll):
    B, H, T, L = qa.shape
    nq = T // tq
    max_items = nq * (nq + 1) // 2 + unroll
    blk = pl.BlockSpec((1, 2, T, L), lambda b, p, jlo_ref: (b, p, 0, 0))
    return pl.pallas_call(
        functools.partial(_fox_shifted_body, tq=tq, nq=nq, unroll=unroll),
        grid_spec=pltpu.PrefetchScalarGridSpec(
            num_scalar_prefetch=1,
            grid=(B, H // 2),
            in_specs=[blk, blk, blk],
            out_specs=pl.BlockSpec((1, T, L), lambda b, p, jlo_ref: (b, 0, p)),
            scratch_shapes=[pltpu.VMEM((tq, L), _F32)] + [pltpu.SMEM((max_items,), jnp.int32)] * 4,
        ),
        out_shape=jax.ShapeDtypeStruct((B, T, H * _HEAD_DIM), _BF16),
        compiler_params=_params(("arbitrary", "arbitrary")),
        name="fox_shifted",
    )(jlo, qa, ka, va)


def _fox_online(qa, ka, va, *, tq, hg):
    B, H, T, L = qa.shape
    return pl.pallas_call(
        functools.partial(_fox_online_body, tq=tq),
        grid=(B, H // hg, T // tq),
        in_specs=[
            pl.BlockSpec((1, hg, tq, L), lambda b, p, i: (b, p, i, 0)),
            pl.BlockSpec((1, hg, T, L), lambda b, p, i: (b, p, 0, 0)),
            pl.BlockSpec((1, hg, T, L), lambda b, p, i: (b, p, 0, 0)),
        ],
        out_specs=pl.BlockSpec((1, tq, hg * _HEAD_DIM), lambda b, p, i: (b, i, p)),
        out_shape=jax.ShapeDtypeStruct((B, T, H * _HEAD_DIM), _BF16),
        scratch_shapes=[pltpu.VMEM((tq, L), _F32)] * (2 * hg),
        compiler_params=_params(("arbitrary", "arbitrary", "arbitrary")),
        name="fox_online",
    )(qa, ka, va)


def _first_live_block(fend):
    B, nq, H = fend.shape
    f = jnp.transpose(fend, (0, 2, 1))
    top = jnp.concatenate([jnp.zeros((B, H, 1), _F32), f[:, :, :-1]], axis=-1)
    dead = (top[:, :, :, None] - f[:, :, None, :]) <= _UNDERFLOW_LOG2
    j = jnp.arange(nq, dtype=jnp.int32)
    before = j[None, :] < j[:, None]
    first_live = jnp.min(jnp.where(dead & before, nq, j), axis=-1)
    return first_live.astype(jnp.int32).reshape(-1)


def _mem_kv_body(mem_ref, gm_ref, w_ref, gk_ref, k_ref, v_ref, *, d_model, xd):
    mn = _rms(mem_ref[0], gm_ref[...]).astype(_BF16)
    kv = jnp.dot(mn, w_ref[...], preferred_element_type=_F32)
    for h in range(_N_XHEADS):
        sl = slice(h * xd, (h + 1) * xd)
        k_ref[0, :, sl] = (_rms(kv[:, sl], gk_ref[...]) * (xd ** -0.5)).astype(_BF16)
    v_ref[0] = kv[:, d_model:].astype(_BF16)


def _mem_kv(mem, g_mem, w_xkv, g_xk):
    B, M, D = mem.shape
    xd = D // _N_XHEADS
    blk = pl.BlockSpec((1, M, D), lambda b: (b, 0, 0))
    out = jax.ShapeDtypeStruct((B, M, D), _BF16)
    return pl.pallas_call(
        functools.partial(_mem_kv_body, d_model=D, xd=xd),
        grid=(B,),
        in_specs=[blk, _const_spec((1, D)), _const_spec(w_xkv.shape), _const_spec((1, xd))],
        out_specs=[blk, blk],
        out_shape=[out, out],
        compiler_params=_params(("arbitrary",)),
        name="mem_kv",
    )(mem, g_mem, w_xkv, g_xk)


def _mix_xattn_body(x_ref, ret_ref, fox_ref, wo_ref, gx_ref, wq_ref, gq_ref, k_ref, v_ref, wxo_ref,
                    o_ref, *, width, xd):
    h1 = (x_ref[0]
          + jnp.dot(ret_ref[0], wo_ref[:width, :], preferred_element_type=_F32)
          + jnp.dot(fox_ref[0], wo_ref[width:, :], preferred_element_type=_F32))
    hn, inv_h = _rms_split(h1, gx_ref[...])
    q = jnp.dot(hn, wq_ref[...], preferred_element_type=_F32) * inv_h
    nt = (((1,), (1,)), ((), ()))
    heads = []
    for h in range(_N_XHEADS):
        sl = slice(h * xd, (h + 1) * xd)
        qn, inv_q = _rms_split(q[:, sl], gq_ref[...])
        logits = lax.dot_general(qn, k_ref[0, :, sl], nt, preferred_element_type=_F32) * inv_q
        p = jnp.exp(logits - jnp.max(logits, axis=-1, keepdims=True))
        p = p / jnp.sum(p, axis=-1, keepdims=True)
        heads.append(jnp.dot(p.astype(_BF16), v_ref[0, :, sl], preferred_element_type=_F32).astype(_BF16))
    o = jnp.concatenate(heads, axis=-1)
    o_ref[0] = h1 + jnp.dot(o, wxo_ref[...], preferred_element_type=_F32)


def _mix_xattn(x, ret, fox, w_out, g_xattn, w_xq, g_xq, k, v, w_xo, *, tm):
    B, T, D = x.shape
    W = ret.shape[-1]
    M = k.shape[1]
    xd = D // _N_XHEADS
    tok = lambda b, i: (b, i, 0)
    return pl.pallas_call(
        functools.partial(_mix_xattn_body, width=W, xd=xd),
        grid=(B, T // tm),
        in_specs=[
            pl.BlockSpec((1, tm, D), tok),
            pl.BlockSpec((1, tm, W), tok),
            pl.BlockSpec((1, tm, W), tok),
            _const_spec(w_out.shape),
            _const_spec((1, D)),
            _const_spec(w_xq.shape),
            _const_spec((1, xd)),
            pl.BlockSpec((1, M, D), lambda b, i: (b, 0, 0)),
            pl.BlockSpec((1, M, D), lambda b, i: (b, 0, 0)),
            _const_spec(w_xo.shape),
        ],
        out_specs=pl.BlockSpec((1, tm, D), tok),
        out_shape=jax.ShapeDtypeStruct((B, T, D), _F32),
        compiler_params=_params(("arbitrary", "arbitrary")),
        name="mix_xattn",
    )(x, ret, fox, w_out, g_xattn, w_xq, g_xq, k, v, w_xo)


def _ffn_chunks(d_ff, n_chunks):
    tiles = -(-d_ff // _MXU_TILE)
    bounds = [min(d_ff, _MXU_TILE * ((tiles * c) // n_chunks)) for c in range(n_chunks)] + [d_ff]
    return [(lo, hi) for lo, hi in zip(bounds[:-1], bounds[1:]) if hi > lo]


def _ffn_body(h_ref, g_ref, wg_ref, wu_ref, wd_ref, o_ref, *, chunks):
    h = h_ref[0]
    hn, inv_rms = _rms_split(h, g_ref[...])
    acc = h
    for lo, hi in chunks:
        sl = slice(lo, hi)
        gate = jnp.dot(hn, wg_ref[:, sl], preferred_element_type=_F32) * inv_rms
        up = jnp.dot(hn, wu_ref[:, sl], preferred_element_type=_F32) * inv_rms
        a = (gate * jax.nn.sigmoid(gate) * up).astype(_BF16)
        acc = acc + jnp.dot(a, wd_ref[sl, :], preferred_element_type=_F32)
    o_ref[0] = acc


def _ffn(h, g_ffn, w_gate, w_up, w_down, *, tm, n_split):
    B, T, D = h.shape
    F = w_gate.shape[1]
    tok = pl.BlockSpec((1, tm, D), lambda b, i: (b, i, 0))
    return pl.pallas_call(
        functools.partial(_ffn_body, chunks=_ffn_chunks(F, n_split)),
        grid=(B, T // tm),
        in_specs=[tok, _const_spec((1, D)), _const_spec(w_gate.shape), _const_spec(w_up.shape),
                  _const_spec(w_down.shape)],
        out_specs=tok,
        out_shape=jax.ShapeDtypeStruct((B, T, D), _F32),
        compiler_params=_params(("arbitrary", "arbitrary")),
        name="ffn",
    )(h, g_ffn, w_gate, w_up, w_down)


def _rope_tables(T):
    half = _HEAD_DIM // 2
    inv_freq = (_ROPE_BASE ** (-np.arange(0, _HEAD_DIM, 2, dtype=np.float32) / _HEAD_DIM)).astype(np.float32)
    ang = (np.arange(T, dtype=np.float32)[:, None] * inv_freq[None, :]).astype(np.float32).astype(np.float64)
    cos, sin = np.cos(ang), np.sin(ang)
    reps = _LANES // _HEAD_DIM
    cos_t = np.tile(np.concatenate([cos, cos], axis=1), (1, reps))
    sin_t = np.tile(np.concatenate([-sin, sin], axis=1), (1, reps))
    return jnp.asarray(cos_t, _F32), jnp.asarray(sin_t, _F32)


def _retention_tables(tb):
    log_g = np.log(1.0 - 2.0 ** (-5.0 - np.arange(_N_HEADS, dtype=np.float64)))
    idx = np.arange(tb, dtype=np.float64)
    dist = np.abs(idx[:, None] - idx[None, :])
    chunk = np.arange(tb) // _RET_CHUNK
    visible = chunk[None, :] <= chunk[:, None]
    dmask = np.where(visible[None], np.exp(log_g[:, None, None] * dist[None]), 0.0)
    qdec = np.repeat(np.exp(log_g[None, :] * (idx[:, None] + 1.0)), _HEAD_DIM, axis=1)
    kdec = np.repeat(np.exp(log_g[None, :] * (tb - 1.0 - idx[:, None])), _HEAD_DIM, axis=1)
    heads_per_group = _RET_GROUP // _HEAD_DIM
    head_of = np.arange(_RET_GROUP) // _HEAD_DIM
    bd = (head_of[:, None] == head_of[None, :]).astype(np.float64)
    step_decay = np.exp(log_g * tb).reshape(-1, heads_per_group)
    sdec = bd[None] * np.repeat(step_decay, _HEAD_DIM, axis=1)[:, None, :]
    f = lambda a: jnp.asarray(a, _F32)
    return f(dmask), f(qdec), f(kdec), f(sdec), f(bd), jnp.asarray(bd / _HEAD_DIM, _BF16)


def _bias_placement():
    place = np.zeros((_LANES, _N_HEADS * _LANES), np.float32)
    for h in range(_N_HEADS):
        for part, (q_lane, k_lane) in enumerate(((_AUG_HI, _AUG_KHI), (_AUG_MID, _AUG_KMID),
                                                 (_AUG_LO, _AUG_KLO))):
            place[part * _N_HEADS + h, h * _LANES + q_lane] = 1.0
            place[part * _N_HEADS + h, h * _LANES + k_lane] = -1.0
    return jnp.asarray(place, _BF16)


def _pad_lanes(a):
    return jnp.pad(a, [(0, 0)] * (a.ndim - 1) + [(0, _LANES - a.shape[-1])])


def kernel(x, mem, g_mix, w_in, b_forget, g_ret_out, g_fox_q, g_fox_k, w_out, g_xattn, w_xq, w_xkv,
           g_mem, g_xq, g_xk, w_xo, g_ffn, w_gate, w_up, w_down):
    B, T, D = x.shape
    width = _N_HEADS * _HEAD_DIM
    tb, tq, tm_mix, tm_ffn = 256, 512, 512, 512
    cos_t, sin_t = _rope_tables(T)
    ret_tables = _retention_tables(tb)
    tri = jnp.asarray(np.tril(np.ones((tq, tq), np.float32)), _BF16)
    place = _bias_placement()
    row = lambda a: a.reshape(1, -1).astype(_F32)

    h = x
    for l in range(w_in.shape[0]):
        w_main = w_in[l, :, :7 * width].astype(_BF16)
        w_ff = _pad_lanes(w_in[l, :, 7 * width:]).astype(_BF16)
        b_ff = _pad_lanes(row(b_forget[l]))
        gq = jnp.tile(row(g_fox_q[l]), (1, _N_HEADS)) * (_LOG2E * _HEAD_DIM ** -0.5)
        gk = jnp.tile(row(g_fox_k[l]), (1, _N_HEADS))
        bound = (_HEAD_DIM ** 0.5 * _NORM_ROUNDING_SLACK) * jnp.max(jnp.abs(g_fox_q[l])) * jnp.max(
            jnp.abs(g_fox_k[l]))
        use_shift = bound <= _MAX_FIXED_SHIFT
        shift = jnp.where(use_shift, jnp.ceil(bound * (4.0 * _LOG2E)) * 0.25, 0.0).astype(_F32)
        rq, rk, rv, gate, qa, ka, va, fend = _in_proj(h, row(g_mix[l]), w_main, w_ff, b_ff, cos_t, sin_t,
                                                      gq, gk, jnp.full((1, _LANES), shift), tri,
                                                      ret_tables[-1], place, tm=tq)
        ret = _retention(rq, rk, rv, gate, *ret_tables, row(g_ret_out[l]), tb=tb)
        jlo = _first_live_block(fend[:, :, 0, :_N_HEADS])
        fox = lax.cond(use_shift,
                       lambda jlo, qa, ka, va: _fox_shifted(jlo, qa, ka, va, tq=tq, unroll=4),
                       lambda jlo, qa, ka, va: _fox_online(qa, ka, va, tq=tq, hg=4), jlo, qa, ka, va)
        k, v = _mem_kv(mem, row(g_mem[l]), w_xkv[l].astype(_BF16), row(g_xk[l]))
        h = _mix_xattn(h, ret, fox, w_out[l].astype(_BF16), row(g_xattn[l]), w_xq[l].astype(_BF16),
                       row(g_xq[l]), k, v, w_xo[l].astype(_BF16), tm=tm_mix)
        h = _ffn(h, row(g_ffn[l]), w_gate[l].astype(_BF16), w_up[l].astype(_BF16),
                 w_down[l].astype(_BF16), tm=tm_ffn, n_split=2)
    return h
```

```python
import functools

import numpy as np
import jax
import jax.numpy as jnp
from jax import lax
from jax.experimental import pallas as pl
from jax.experimental.pallas import tpu as pltpu

_BF16 = jnp.bfloat16
_F32 = jnp.float32

_EPS = 1e-6
_NEG_INF = -1e30
_ROPE_BASE = 10000.0
_HEAD_DIM = 64
_N_HEADS = 8
_RET_CHUNK = 64
_RET_GROUP = 256
_N_XHEADS = 4
_LANES = 128
_MXU_TILE = 256
_VMEM_LIMIT = 56 * 1024 * 1024

_AUG_HI, _AUG_MID, _AUG_LO = 64, 65, 66
_AUG_KHI, _AUG_KMID, _AUG_KLO = 67, 68, 69
_AUG_SHIFT = 70
_AUG_ONE = 64

_LOG2E = 1.4426950408889634
_MAX_FIXED_SHIFT = 32.0
_DIAG_PER_TRIP = 4
_UNDERFLOW_LOG2 = -152.0
_NORM_ROUNDING_SLACK = 1.01


def _params(sem):
    return pltpu.CompilerParams(dimension_semantics=sem, vmem_limit_bytes=_VMEM_LIMIT)


def _const_spec(shape):
    nd = len(shape)
    return pl.BlockSpec(shape, lambda *_: (0,) * nd, pipeline_mode=pl.Buffered(1))


def _rms(x, g):
    return x * lax.rsqrt(jnp.mean(x * x, axis=-1, keepdims=True) + _EPS) * g


def _rms_split(x, g):
    return (x * g).astype(_BF16), lax.rsqrt(jnp.mean(x * x, axis=-1, keepdims=True) + _EPS)


def _split3(v):
    hi = v.astype(_BF16).astype(_F32)
    r = v - hi
    mid = r.astype(_BF16).astype(_F32)
    return hi, mid, r - mid


def _in_proj_body(x_ref, g_ref, w_ref, wff_ref, bf_ref, cos_ref, sin_ref, gq_ref, gk_ref, shift_ref,
                  tri_ref, avg_ref, place_ref, rq_ref, rk_ref, rv_ref, gate_ref, qa_ref, ka_ref, va_ref,
                  fend_ref, carry_ref, *, tm, width):
    i = pl.program_id(1)
    hb, inv_rms = _rms_split(x_ref[0], g_ref[...])

    def proj(j):
        return jnp.dot(hb, w_ref[:, j * width:(j + 1) * width], preferred_element_type=_F32) * inv_rms

    lane = lax.broadcasted_iota(jnp.int32, (tm, _LANES), 1)
    low = lane < _HEAD_DIM
    first_half = (lane & (_HEAD_DIM // 2)) == 0
    n_pairs = width // _LANES

    cos = cos_ref[...]
    sin = sin_ref[...]
    for j, out_ref, scale in ((0, rq_ref, _HEAD_DIM ** -0.5), (1, rk_ref, None)):
        y = proj(j)
        for c in range(n_pairs):
            blk = y[:, c * _LANES:(c + 1) * _LANES]
            swapped = jnp.where(first_half, pltpu.roll(blk, _LANES - _HEAD_DIM // 2, 1),
                                pltpu.roll(blk, _HEAD_DIM // 2, 1))
            r = blk * cos + swapped * sin
            if scale is not None:
                r = r * scale
            out_ref[0, :, c * _LANES:(c + 1) * _LANES] = r.astype(_BF16)
    rv_ref[0] = proj(2).astype(_BF16)
    gate = proj(3)
    gate_ref[0] = (gate * jax.nn.sigmoid(gate)).astype(_BF16)

    half = tm // 2
    z = jnp.concatenate([jnp.dot(hb[:half], wff_ref[...], preferred_element_type=_F32),
                         jnp.dot(hb[half:], wff_ref[...], preferred_element_type=_F32)],
                        axis=0) * inv_rms + bf_ref[...]
    logf = (jnp.minimum(z, 0.0) - jnp.log(1.0 + jnp.exp(-jnp.abs(z)))) * _LOG2E
    hi, mid, lo = _split3(logf)
    tri = tri_ref[...]
    split = jnp.concatenate([hi, mid, lo], axis=1).astype(_BF16)
    parts = jnp.concatenate(
        [jnp.dot(tri[:half, :half], split[:half], preferred_element_type=_F32),
         jnp.dot(tri[half:], split, preferred_element_type=_F32)], axis=0)
    csum = parts[:, :_LANES] + parts[:, _LANES:2 * _LANES] + parts[:, 2 * _LANES:]

    @pl.when(i == 0)
    def _():
        carry_ref[...] = jnp.zeros_like(carry_ref)

    fcum = csum + carry_ref[0:1, :]
    carry_ref[...] = jnp.broadcast_to(fcum[tm - 1:tm, :], carry_ref.shape)
    fend_ref[0, 0] = carry_ref[...]

    def head_rms(y, g_row):
        out = []
        for g0 in range(0, width, _RET_GROUP):
            blk = y[:, g0:g0 + _RET_GROUP]
            ms = jnp.dot((blk * blk).astype(_BF16), avg_ref[...], preferred_element_type=_F32)
            out.append(blk * lax.rsqrt(ms + _EPS) * g_row[:, g0:g0 + _RET_GROUP])
        return out

    qn = head_rms(proj(4), gq_ref[...])
    kn = head_rms(proj(5), gk_ref[...])
    fv = proj(6)

    fh, fm, fl = _split3(fcum)
    packed = jnp.where(lane < _N_HEADS, fh, jnp.where(
        lane < 2 * _N_HEADS, pltpu.roll(fm, _N_HEADS, 1), pltpu.roll(fl, 2 * _N_HEADS, 1)))
    placed = jnp.dot(packed.astype(_BF16), place_ref[...], preferred_element_type=_F32)
    q_const = jnp.where((lane >= _AUG_KHI) & (lane <= _AUG_KLO), 1.0,
                        jnp.where(lane == _AUG_SHIFT, -shift_ref[...], 0.0))
    k_const = jnp.where(((lane >= _AUG_HI) & (lane <= _AUG_LO)) | (lane == _AUG_SHIFT), 1.0, 0.0)
    v_const = jnp.where(lane == _AUG_ONE, 1.0, 0.0)
    q_bias = (lane >= _AUG_HI) & (lane <= _AUG_LO)
    k_bias = (lane >= _AUG_KHI) & (lane <= _AUG_KLO)
    for h in range(_N_HEADS):
        g0, c0 = divmod(h * _HEAD_DIM, _RET_GROUP)
        c0 = (c0 // _LANES) * _LANES
        qb = qn[g0][:, c0:c0 + _LANES]
        kb = kn[g0][:, c0:c0 + _LANES]
        vb = fv[:, (h // 2) * _LANES:(h // 2 + 1) * _LANES]
        if h % 2 == 1:
            qb = pltpu.roll(qb, _HEAD_DIM, 1)
            kb = pltpu.roll(kb, _HEAD_DIM, 1)
            vb = pltpu.roll(vb, _HEAD_DIM, 1)
        bias = placed[:, h * _LANES:(h + 1) * _LANES]
        qa_ref[0, h] = jnp.where(low, qb, jnp.where(q_bias, bias, q_const)).astype(_BF16)
        ka_ref[0, h] = jnp.where(low, kb, jnp.where(k_bias, bias, k_const)).astype(_BF16)
        va_ref[0, h] = jnp.where(low, vb, v_const).astype(_BF16)


def _in_proj(x, g_mix, w_main, w_ff, b_ff, cos_t, sin_t, gq, gk, shift, tri, avg, place, *, tm):
    B, T, D = x.shape
    width = _N_HEADS * _HEAD_DIM
    tok = lambda b, i: (b, i, 0)
    head = lambda b, i: (b, 0, i, 0)
    bf_tok = jax.ShapeDtypeStruct((B, T, width), _BF16)
    bf_head = jax.ShapeDtypeStruct((B, _N_HEADS, T, _LANES), _BF16)
    return pl.pallas_call(
        functools.partial(_in_proj_body, tm=tm, width=width),
        grid=(B, T // tm),
        in_specs=[
            pl.BlockSpec((1, tm, D), tok),
            _const_spec((1, D)),
            _const_spec(w_main.shape),
            _const_spec(w_ff.shape),
            _const_spec((1, _LANES)),
            pl.BlockSpec((tm, _LANES), lambda b, i: (i, 0)),
            pl.BlockSpec((tm, _LANES), lambda b, i: (i, 0)),
            _const_spec((1, width)),
            _const_spec((1, width)),
            _const_spec((1, _LANES)),
            _const_spec((tm, tm)),
            _const_spec(avg.shape),
            _const_spec(place.shape),
        ],
        out_specs=[pl.BlockSpec((1, tm, width), tok)] * 4
        + [pl.BlockSpec((1, _N_HEADS, tm, _LANES), head)] * 3
        + [pl.BlockSpec((1, 1, 8, _LANES), lambda b, i: (b, i, 0, 0))],
        out_shape=[bf_tok] * 4 + [bf_head] * 3 + [jax.ShapeDtypeStruct((B, T // tm, 8, _LANES), _F32)],
        scratch_shapes=[pltpu.VMEM((8, _LANES), _F32)],
        compiler_params=_params(("arbitrary", "arbitrary")),
        name="in_proj",
    )(x, g_mix, w_main, w_ff, b_ff, cos_t, sin_t, gq, gk, shift, tri, avg, place)


def _retention_body(rq_ref, rk_ref, rv_ref, gate_ref, dmask_ref, qdec_ref, kdec_ref, sdec_ref, bd_ref,
                    avg_ref, g_ref, o_ref, state_ref, *, tb):
    i = pl.program_id(1)

    @pl.when(i == 0)
    def _():
        state_ref[...] = jnp.zeros_like(state_ref)

    nt = (((1,), (1,)), ((), ()))
    tn = (((0,), (0,)), ((), ()))
    gw = state_ref.shape[1]
    lane = lax.broadcasted_iota(jnp.int32, (1, _LANES), 1)
    low = lax.broadcasted_iota(jnp.int32, (tb, _LANES), 1) < _HEAD_DIM
    head_lanes = [jnp.where(lane < _HEAD_DIM, 1.0, 0.0).astype(_BF16),
                  jnp.where(lane < _HEAD_DIM, 0.0, 1.0).astype(_BF16)]
    n_groups = state_ref.shape[0]
    groups = [slice(g * gw, (g + 1) * gw) for g in range(n_groups)]
    scores = []
    for h in range(_N_HEADS):
        ps = slice((h // 2) * _LANES, (h // 2 + 1) * _LANES)
        scores.append(lax.dot_general(rq_ref[0, :, ps] * head_lanes[h % 2], rk_ref[0, :, ps], nt,
                                      preferred_element_type=_F32))
    inter = []
    for g, gs in enumerate(groups):
        state = state_ref[g]
        inter.append(jnp.dot(rq_ref[0, :, gs], state.astype(_BF16),
                             preferred_element_type=_F32) * qdec_ref[:, gs])
        kd = (rk_ref[0, :, gs].astype(_F32) * kdec_ref[:, gs]).astype(_BF16)
        state_ref[g] = state * sdec_ref[g] + lax.dot_general(
            kd, rv_ref[0, :, gs], tn, preferred_element_type=_F32) * bd_ref[...]
    intra = []
    for h in range(_N_HEADS):
        ps = slice((h // 2) * _LANES, (h // 2 + 1) * _LANES)
        intra.append(jnp.dot((scores[h] * dmask_ref[h]).astype(_BF16), rv_ref[0, :, ps],
                             preferred_element_type=_F32))
    outs = []
    for g, gs in enumerate(groups):
        pairs = [jnp.where(low, intra[2 * c], intra[2 * c + 1])
                 for c in range(g * gw // _LANES, (g + 1) * gw // _LANES)]
        outs.append(jnp.concatenate(pairs, axis=1) + inter[g])
    mus = [jnp.dot(o.astype(_BF16), avg_ref[...], preferred_element_type=_F32) for o in outs]
    cents = [o - mu for o, mu in zip(outs, mus)]
    vars_ = [jnp.dot((oc * oc).astype(_BF16), avg_ref[...], preferred_element_type=_F32) for oc in cents]
    for gs, oc, var in zip(groups, cents, vars_):
        y = oc * lax.rsqrt(var + _EPS) * g_ref[:, gs]
        o_ref[0, :, gs] = (y * gate_ref[0, :, gs].astype(_F32)).astype(_BF16)


def _retention(rq, rk, rv, gate, dmask, qdec, kdec, sdec, bd, avg, g_ret, *, tb):
    B, T, W = rq.shape
    tok = pl.BlockSpec((1, tb, W), lambda b, i: (b, i, 0))
    consts = (dmask, qdec, kdec, sdec, bd, avg, g_ret)
    return pl.pallas_call(
        functools.partial(_retention_body, tb=tb),
        grid=(B, T // tb),
        in_specs=[tok, tok, tok, tok] + [_const_spec(c.shape) for c in consts],
        out_specs=tok,
        out_shape=jax.ShapeDtypeStruct((B, T, W), _BF16),
        scratch_shapes=[pltpu.VMEM(sdec.shape, _F32)],
        compiler_params=_params(("arbitrary", "arbitrary")),
        name="retention",
    )(rq, rk, rv, gate, *consts)


def _causal_mask(tq):
    row = lax.broadcasted_iota(jnp.int32, (tq, tq), 0)
    col = lax.broadcasted_iota(jnp.int32, (tq, tq), 1)
    return row >= col


def _fox_finish(acc_refs, o_ref, tq):
    lane = lax.broadcasted_iota(jnp.int32, (tq, _LANES), 1)
    for c in range(len(acc_refs) // 2):
        pair = []
        for hh in (2 * c, 2 * c + 1):
            acc = acc_refs[hh][...]
            pair.append(acc / acc[:, _AUG_ONE:_AUG_ONE + 1])
        o_ref[0, :, c * _LANES:(c + 1) * _LANES] = jnp.where(
            lane < _HEAD_DIM, pair[0], pltpu.roll(pair[1], _HEAD_DIM, 1)).astype(_BF16)


def _fox_shifted_body(jlo_ref, qa_ref, ka_ref, va_ref, o_ref, acc_ref, it_i, it_j, it_slot,
                      *, tq, nq, unroll):
    b, pair = pl.program_id(0), pl.program_id(1)
    n_pair = qa_ref.shape[1]
    nt = (((1,), (1,)), ((), ()))
    half = tq // 2
    diff_top = (lax.broadcasted_iota(jnp.int32, (half, half), 1)
                - lax.broadcasted_iota(jnp.int32, (half, half), 0))
    diff_bot = (lax.broadcasted_iota(jnp.int32, (half, tq), 1)
                - lax.broadcasted_iota(jnp.int32, (half, tq), 0))

    def interleave(n_chains, qk, finish):
        s_prev = qk(0)
        for u in range(1, n_chains):
            s_next = qk(u)
            finish(u - 1, s_prev)
            s_prev = s_next
        finish(n_chains - 1, s_prev)

    for hh in range(n_pair):
        base = (b * (pl.num_programs(1) * n_pair) + pair * n_pair + hh) * nq
        acc_ref[...] = jnp.zeros_like(acc_ref)

        n = jnp.int32(0)
        for i in range(1, nq):
            def add(j, n, i=i):
                it_i[n] = jnp.int32(i)
                it_j[n] = j
                it_slot[n] = jnp.int32(i)
                return n + 1

            n = lax.fori_loop(jlo_ref[base + i], i, add, n)
        n_trips = (n + (unroll - 1)) // unroll

        def pad(m, carry):
            it_i[m] = jnp.int32(0)
            it_j[m] = jnp.int32(0)
            it_slot[m] = jnp.int32(nq)
            return carry

        lax.fori_loop(n, n_trips * unroll, pad, 0)

        def trip(t, carry, hh=hh):
            items = [(it_i[t * unroll + u], it_j[t * unroll + u], it_slot[t * unroll + u])
                     for u in range(unroll)]

            def qk(u):
                i, j, _ = items[u]
                return lax.dot_general(qa_ref[0, hh, pl.ds(pl.multiple_of(i * tq, tq), tq), :],
                                       ka_ref[0, hh, pl.ds(pl.multiple_of(j * tq, tq), tq), :], nt,
                                       preferred_element_type=_F32)

            def finish(u, s):
                _, j, slot = items[u]
                acc_ref[slot] += jnp.dot(jnp.exp2(s).astype(_BF16),
                                         va_ref[0, hh, pl.ds(pl.multiple_of(j * tq, tq), tq), :],
                                         preferred_element_type=_F32)

            interleave(unroll, qk, finish)
            return carry

        lax.fori_loop(0, n_trips, trip, 0)

        def diag_trip(t, carry, hh=hh):
            def rows(u):
                i, bottom = _DIAG_PER_TRIP * t + u // 2, u % 2
                return i, bottom, pl.multiple_of(i * tq + bottom * half, half)

            def qk(u):
                i, bottom, r0 = rows(u)
                k0 = pl.multiple_of(i * tq, tq)
                width = tq if bottom else half
                return lax.dot_general(qa_ref[0, hh, pl.ds(r0, half), :],
                                       ka_ref[0, hh, pl.ds(k0, width), :], nt, preferred_element_type=_F32)

            def finish(u, s):
                i, bottom, r0 = rows(u)
                k0 = pl.multiple_of(i * tq, tq)
                width = tq if bottom else half
                live = (diff_bot <= half) if bottom else (diff_top <= 0)
                p = jnp.exp2(jnp.where(live, s, _NEG_INF)).astype(_BF16)
                acc = acc_ref[i, bottom * half:(bottom + 1) * half, :] + jnp.dot(
                    p, va_ref[0, hh, pl.ds(k0, width), :], preferred_element_type=_F32)
                o = acc / acc[:, _AUG_ONE:_AUG_ONE + 1]
                if hh % 2 == 1:
                    o = pltpu.roll(o, _HEAD_DIM, 1)
                sl = slice(hh * _HEAD_DIM, (hh + 1) * _HEAD_DIM)
                o_ref[0, pl.ds(r0, half), sl] = o[:, sl].astype(_BF16)

            interleave(2 * _DIAG_PER_TRIP, qk, finish)
            return carry

        lax.fori_loop(0, nq // _DIAG_PER_TRIP, diag_trip, 0)


def _fox_online_body(qa_ref, ka_ref, va_ref, o_ref, *scratch, tq):
    qi = pl.program_id(2)
    nt = (((1,), (1,)), ((), ()))
    n_heads = qa_ref.shape[1]
    m_refs, acc_refs = scratch[:n_heads], scratch[n_heads:]
    for hh in range(n_heads):
        m_refs[hh][...] = jnp.full_like(m_refs[hh], _NEG_INF)
        acc_refs[hh][...] = jnp.zeros_like(acc_refs[hh])

    def step(j, masked):
        start = pl.multiple_of(j * tq, tq)
        logits = [lax.dot_general(qa_ref[0, hh], ka_ref[0, hh, pl.ds(start, tq), :], nt,
                                  preferred_element_type=_F32) for hh in range(n_heads)]
        for hh in range(n_heads):
            m_ref, acc_ref = m_refs[hh], acc_refs[hh]
            s = logits[hh]
            if masked:
                s = jnp.where(_causal_mask(tq), s, _NEG_INF)
            m_old = m_ref[...]
            m_new = jnp.maximum(m_old, jnp.max(s, axis=-1, keepdims=True))
            p = jnp.exp2(s - m_new[:, 0:1])
            acc_ref[...] = jnp.exp2(m_old - m_new) * acc_ref[...] + jnp.dot(
                p.astype(_BF16), va_ref[0, hh, pl.ds(start, tq), :], preferred_element_type=_F32)
            m_ref[...] = m_new

    def off_diag(j, carry):
        step(j, False)
        return carry

    lax.fori_loop(0, qi, off_diag, 0)
    step(qi, True)
    _fox_finish(acc_refs, o_ref, tq)


def _fox_shifted(jlo, qa, ka, va, *, tq, unroll):
    B, H, T, L = qa.shape
    nq = T // tq
    assert nq % _DIAG_PER_TRIP == 0 and tq % (2 * _MXU_TILE) == 0
    max_items = nq * (nq - 1) // 2 + unroll
    blk = pl.BlockSpec((1, 2, T, L), lambda b, p, jlo_ref: (b, p, 0, 0))
    return pl.pallas_call(
        functools.partial(_fox_shifted_body, tq=tq, nq=nq, unroll=unroll),
        grid_spec=pltpu.PrefetchScalarGridSpec(
            num_scalar_prefetch=1,
            grid=(B, H // 2),
            in_specs=[blk, blk, blk],
            out_specs=pl.BlockSpec((1, T, L), lambda b, p, jlo_ref: (b, 0, p)),
            scratch_shapes=[pltpu.VMEM((nq + 1, tq, L), _F32)] + [pltpu.SMEM((max_items,), jnp.int32)] * 3,
        ),
        out_shape=jax.ShapeDtypeStruct((B, T, H * _HEAD_DIM), _BF16),
        compiler_params=_params(("arbitrary", "arbitrary")),
        name="fox_shifted",
    )(jlo, qa, ka, va)


def _fox_online(qa, ka, va, *, tq, hg):
    B, H, T, L = qa.shape
    return pl.pallas_call(
        functools.partial(_fox_online_body, tq=tq),
        grid=(B, H // hg, T // tq),
        in_specs=[
            pl.BlockSpec((1, hg, tq, L), lambda b, p, i: (b, p, i, 0)),
            pl.BlockSpec((1, hg, T, L), lambda b, p, i: (b, p, 0, 0)),
            pl.BlockSpec((1, hg, T, L), lambda b, p, i: (b, p, 0, 0)),
        ],
        out_specs=pl.BlockSpec((1, tq, hg * _HEAD_DIM), lambda b, p, i: (b, i, p)),
        out_shape=jax.ShapeDtypeStruct((B, T, H * _HEAD_DIM), _BF16),
        scratch_shapes=[pltpu.VMEM((tq, L), _F32)] * (2 * hg),
        compiler_params=_params(("arbitrary", "arbitrary", "arbitrary")),
        name="fox_online",
    )(qa, ka, va)


def _first_live_block(fend):
    B, nq, H = fend.shape
    f = jnp.transpose(fend, (0, 2, 1))
    top = jnp.concatenate([jnp.zeros((B, H, 1), _F32), f[:, :, :-1]], axis=-1)
    dead = (top[:, :, :, None] - f[:, :, None, :]) <= _UNDERFLOW_LOG2
    j = jnp.arange(nq, dtype=jnp.int32)
    before = j[None, :] < j[:, None]
    first_live = jnp.min(jnp.where(dead & before, nq, j), axis=-1)
    return first_live.astype(jnp.int32).reshape(-1)


def _mem_kv_body(mem_ref, gm_ref, w_ref, gk_ref, k_ref, v_ref, *, d_model, xd):
    mn = _rms(mem_ref[0], gm_ref[...]).astype(_BF16)
    kv = jnp.dot(mn, w_ref[...], preferred_element_type=_F32)
    for h in range(_N_XHEADS):
        sl = slice(h * xd, (h + 1) * xd)
        k_ref[0, :, sl] = (_rms(kv[:, sl], gk_ref[...]) * (xd ** -0.5)).astype(_BF16)
    v_ref[0] = kv[:, d_model:].astype(_BF16)


def _mem_kv(mem, g_mem, w_xkv, g_xk):
    B, M, D = mem.shape
    xd = D // _N_XHEADS
    blk = pl.BlockSpec((1, M, D), lambda b: (b, 0, 0))
    out = jax.ShapeDtypeStruct((B, M, D), _BF16)
    return pl.pallas_call(
        functools.partial(_mem_kv_body, d_model=D, xd=xd),
        grid=(B,),
        in_specs=[blk, _const_spec((1, D)), _const_spec(w_xkv.shape), _const_spec((1, xd))],
        out_specs=[blk, blk],
        out_shape=[out, out],
        compiler_params=_params(("arbitrary",)),
        name="mem_kv",
    )(mem, g_mem, w_xkv, g_xk)


def _mix_xattn_body(x_ref, ret_ref, fox_ref, wo_ref, gx_ref, wq_ref, gq_ref, k_ref, v_ref, wxo_ref,
                    o_ref, *, width, xd):
    h1 = (x_ref[0]
          + jnp.dot(ret_ref[0], wo_ref[:width, :], preferred_element_type=_F32)
          + jnp.dot(fox_ref[0], wo_ref[width:, :], preferred_element_type=_F32))
    hn, inv_h = _rms_split(h1, gx_ref[...])
    q = jnp.dot(hn, wq_ref[...], preferred_element_type=_F32) * inv_h
    nt = (((1,), (1,)), ((), ()))
    heads = []
    for h in range(_N_XHEADS):
        sl = slice(h * xd, (h + 1) * xd)
        qn, inv_q = _rms_split(q[:, sl], gq_ref[...])
        logits = lax.dot_general(qn, k_ref[0, :, sl], nt, preferred_element_type=_F32) * inv_q
        p = jnp.exp(logits - jnp.max(logits, axis=-1, keepdims=True))
        p = p / jnp.sum(p, axis=-1, keepdims=True)
        heads.append(jnp.dot(p.astype(_BF16), v_ref[0, :, sl], preferred_element_type=_F32).astype(_BF16))
    o = jnp.concatenate(heads, axis=-1)
    o_ref[0] = h1 + jnp.dot(o, wxo_ref[...], preferred_element_type=_F32)


def _mix_xattn(x, ret, fox, w_out, g_xattn, w_xq, g_xq, k, v, w_xo, *, tm):
    B, T, D = x.shape
    W = ret.shape[-1]
    M = k.shape[1]
    xd = D // _N_XHEADS
    tok = lambda b, i: (b, i, 0)
    return pl.pallas_call(
        functools.partial(_mix_xattn_body, width=W, xd=xd),
        grid=(B, T // tm),
        in_specs=[
            pl.BlockSpec((1, tm, D), tok),
            pl.BlockSpec((1, tm, W), tok),
            pl.BlockSpec((1, tm, W), tok),
            _const_spec(w_out.shape),
            _const_spec((1, D)),
            _const_spec(w_xq.shape),
            _const_spec((1, xd)),
            pl.BlockSpec((1, M, D), lambda b, i: (b, 0, 0)),
            pl.BlockSpec((1, M, D), lambda b, i: (b, 0, 0)),
            _const_spec(w_xo.shape),
        ],
        out_specs=pl.BlockSpec((1, tm, D), tok),
        out_shape=jax.ShapeDtypeStruct((B, T, D), _F32),
        compiler_params=_params(("arbitrary", "arbitrary")),
        name="mix_xattn",
    )(x, ret, fox, w_out, g_xattn, w_xq, g_xq, k, v, w_xo)


def _ffn_chunks(d_ff, n_chunks):
    tiles = -(-d_ff // _MXU_TILE)
    bounds = [min(d_ff, _MXU_TILE * ((tiles * c) // n_chunks)) for c in range(n_chunks)] + [d_ff]
    return [(lo, hi) for lo, hi in zip(bounds[:-1], bounds[1:]) if hi > lo]


def _ffn_body(h_ref, g_ref, wg_ref, wu_ref, wd_ref, o_ref, *, chunks):
    h = h_ref[0]
    hn, inv_rms = _rms_split(h, g_ref[...])
    acc = h
    for lo, hi in chunks:
        sl = slice(lo, hi)
        gate = jnp.dot(hn, wg_ref[:, sl], preferred_element_type=_F32) * inv_rms
        up = jnp.dot(hn, wu_ref[:, sl], preferred_element_type=_F32) * inv_rms
        a = (gate * jax.nn.sigmoid(gate) * up).astype(_BF16)
        acc = acc + jnp.dot(a, wd_ref[sl, :], preferred_element_type=_F32)
    o_ref[0] = acc


def _ffn(h, g_ffn, w_gate, w_up, w_down, *, tm, n_split):
    B, T, D = h.shape
    F = w_gate.shape[1]
    tok = pl.BlockSpec((1, tm, D), lambda b, i: (b, i, 0))
    return pl.pallas_call(
        functools.partial(_ffn_body, chunks=_ffn_chunks(F, n_split)),
        grid=(B, T // tm),
        in_specs=[tok, _const_spec((1, D)), _const_spec(w_gate.shape), _const_spec(w_up.shape),
                  _const_spec(w_down.shape)],
        out_specs=tok,
        out_shape=jax.ShapeDtypeStruct((B, T, D), _F32),
        compiler_params=_params(("arbitrary", "arbitrary")),
        name="ffn",
    )(h, g_ffn, w_gate, w_up, w_down)


def _rope_tables(T):
    half = _HEAD_DIM // 2
    inv_freq = (_ROPE_BASE ** (-np.arange(0, _HEAD_DIM, 2, dtype=np.float32) / _HEAD_DIM)).astype(np.float32)
    ang = (np.arange(T, dtype=np.float32)[:, None] * inv_freq[None, :]).astype(np.float32).astype(np.float64)
    cos, sin = np.cos(ang), np.sin(ang)
    reps = _LANES // _HEAD_DIM
    cos_t = np.tile(np.concatenate([cos, cos], axis=1), (1, reps))
    sin_t = np.tile(np.concatenate([-sin, sin], axis=1), (1, reps))
    return jnp.asarray(cos_t, _F32), jnp.asarray(sin_t, _F32)


def _retention_tables(tb):
    log_g = np.log(1.0 - 2.0 ** (-5.0 - np.arange(_N_HEADS, dtype=np.float64)))
    idx = np.arange(tb, dtype=np.float64)
    dist = np.abs(idx[:, None] - idx[None, :])
    chunk = np.arange(tb) // _RET_CHUNK
    visible = chunk[None, :] <= chunk[:, None]
    dmask = np.where(visible[None], np.exp(log_g[:, None, None] * dist[None]), 0.0)
    qdec = np.repeat(np.exp(log_g[None, :] * (idx[:, None] + 1.0)), _HEAD_DIM, axis=1)
    kdec = np.repeat(np.exp(log_g[None, :] * (tb - 1.0 - idx[:, None])), _HEAD_DIM, axis=1)
    heads_per_group = _RET_GROUP // _HEAD_DIM
    head_of = np.arange(_RET_GROUP) // _HEAD_DIM
    bd = (head_of[:, None] == head_of[None, :]).astype(np.float64)
    step_decay = np.exp(log_g * tb).reshape(-1, heads_per_group)
    sdec = bd[None] * np.repeat(step_decay, _HEAD_DIM, axis=1)[:, None, :]
    f = lambda a: jnp.asarray(a, _F32)
    return f(dmask), f(qdec), f(kdec), f(sdec), f(bd), jnp.asarray(bd / _HEAD_DIM, _BF16)


def _bias_placement():
    place = np.zeros((_LANES, _N_HEADS * _LANES), np.float32)
    for h in range(_N_HEADS):
        for part, (q_lane, k_lane) in enumerate(((_AUG_HI, _AUG_KHI), (_AUG_MID, _AUG_KMID),
                                                 (_AUG_LO, _AUG_KLO))):
            place[part * _N_HEADS + h, h * _LANES + q_lane] = 1.0
            place[part * _N_HEADS + h, h * _LANES + k_lane] = -1.0
    return jnp.asarray(place, _BF16)


def _pad_lanes(a):
    return jnp.pad(a, [(0, 0)] * (a.ndim - 1) + [(0, _LANES - a.shape[-1])])


def kernel(x, mem, g_mix, w_in, b_forget, g_ret_out, g_fox_q, g_fox_k, w_out, g_xattn, w_xq, w_xkv,
           g_mem, g_xq, g_xk, w_xo, g_ffn, w_gate, w_up, w_down):
    B, T, D = x.shape
    width = _N_HEADS * _HEAD_DIM
    tb, tq, tm_mix, tm_ffn = 256, 512, 512, 512
    cos_t, sin_t = _rope_tables(T)
    ret_tables = _retention_tables(tb)
    tri = jnp.asarray(np.tril(np.ones((tq, tq), np.float32)), _BF16)
    place = _bias_placement()
    row = lambda a: a.reshape(1, -1).astype(_F32)

    h = x
    for l in range(w_in.shape[0]):
        w_main = w_in[l, :, :7 * width].astype(_BF16)
        w_ff = _pad_lanes(w_in[l, :, 7 * width:]).astype(_BF16)
        b_ff = _pad_lanes(row(b_forget[l]))
        gq = jnp.tile(row(g_fox_q[l]), (1, _N_HEADS)) * (_LOG2E * _HEAD_DIM ** -0.5)
        gk = jnp.tile(row(g_fox_k[l]), (1, _N_HEADS))
        bound = (_HEAD_DIM ** 0.5 * _NORM_ROUNDING_SLACK) * jnp.max(jnp.abs(g_fox_q[l])) * jnp.max(
            jnp.abs(g_fox_k[l]))
        use_shift = bound <= _MAX_FIXED_SHIFT
        shift = jnp.where(use_shift, jnp.ceil(bound * (4.0 * _LOG2E)) * 0.25, 0.0).astype(_F32)
        rq, rk, rv, gate, qa, ka, va, fend = _in_proj(h, row(g_mix[l]), w_main, w_ff, b_ff, cos_t, sin_t,
                                                      gq, gk, jnp.full((1, _LANES), shift), tri,
                                                      ret_tables[-1], place, tm=tq)
        ret = _retention(rq, rk, rv, gate, *ret_tables, row(g_ret_out[l]), tb=tb)
        jlo = _first_live_block(fend[:, :, 0, :_N_HEADS])
        fox = lax.cond(use_shift,
                       lambda jlo, qa, ka, va: _fox_shifted(jlo, qa, ka, va, tq=tq, unroll=4),
                       lambda jlo, qa, ka, va: _fox_online(qa, ka, va, tq=tq, hg=4), jlo, qa, ka, va)
        k, v = _mem_kv(mem, row(g_mem[l]), w_xkv[l].astype(_BF16), row(g_xk[l]))
        h = _mix_xattn(h, ret, fox, w_out[l].astype(_BF16), row(g_xattn[l]), w_xq[l].astype(_BF16),
                       row(g_xq[l]), k, v, w_xo[l].astype(_BF16), tm=tm_mix)
        h = _ffn(h, row(g_ffn[l]), w_gate[l].astype(_BF16), w_up[l].astype(_BF16),
                 w_down[l].astype(_BF16), tm=tm_ffn, n_split=2)
    return h
```

```python
import functools

import numpy as np
import jax
import jax.numpy as jnp
from jax import lax
from jax.experimental import pallas as pl
from jax.experimental.pallas import tpu as pltpu

_BF16 = jnp.bfloat16
_F32 = jnp.float32

_EPS = 1e-6
_NEG_INF = -1e30
_ROPE_BASE = 10000.0
_HEAD_DIM = 64
_N_HEADS = 8
_RET_CHUNK = 64
_RET_GROUP = 256
_N_XHEADS = 4
_LANES = 128
_MXU_TILE = 256
_VMEM_LIMIT = 56 * 1024 * 1024

_AUG_HI, _AUG_MID, _AUG_LO = 64, 65, 66
_AUG_KHI, _AUG_KMID, _AUG_KLO = 67, 68, 69
_AUG_SHIFT = 70
_AUG_ONE = 64

_LOG2E = 1.4426950408889634
_MAX_FIXED_SHIFT = 32.0
_DIAG_PER_TRIP = 4
_UNDERFLOW_LOG2 = -152.0
_NORM_ROUNDING_SLACK = 1.01


def _params(sem):
    return pltpu.CompilerParams(dimension_semantics=sem, vmem_limit_bytes=_VMEM_LIMIT)


def _const_spec(shape):
    nd = len(shape)
    return pl.BlockSpec(shape, lambda *_: (0,) * nd, pipeline_mode=pl.Buffered(1))


def _rms(x, g):
    return x * lax.rsqrt(jnp.mean(x * x, axis=-1, keepdims=True) + _EPS) * g


def _rms_split(x, g):
    return (x * g).astype(_BF16), lax.rsqrt(jnp.mean(x * x, axis=-1, keepdims=True) + _EPS)


def _split3(v):
    hi = v.astype(_BF16).astype(_F32)
    r = v - hi
    mid = r.astype(_BF16).astype(_F32)
    return hi, mid, r - mid


def _in_proj_body(x_ref, g_ref, w32_ref, bf_ref, cos_ref, sin_ref, gq_ref, gk_ref, shift_ref,
                  tri_ref, avg_ref, place_ref, rq_ref, rk_ref, rv_ref, gate_ref, qa_ref, ka_ref, va_ref,
                  fend_ref, w_ref, wff_ref, carry_ref, *, tm, width):
    i = pl.program_id(1)
    n_main = 7 * width

    @pl.when((pl.program_id(0) == 0) & (i == 0))
    def _():
        for j in range(n_main // width):
            w_ref[:, j * width:(j + 1) * width] = jnp.transpose(
                w32_ref[j * width:(j + 1) * width, :]).astype(_BF16)
        tail = jnp.concatenate([w32_ref[n_main:, :],
                                jnp.zeros((_LANES - _N_HEADS, w32_ref.shape[1]), _F32)], axis=0)
        wff_ref[...] = jnp.transpose(tail).astype(_BF16)

    hb, inv_rms = _rms_split(x_ref[0], g_ref[...])

    def proj(j):
        return jnp.dot(hb, w_ref[:, j * width:(j + 1) * width], preferred_element_type=_F32) * inv_rms

    lane = lax.broadcasted_iota(jnp.int32, (tm, _LANES), 1)
    low = lane < _HEAD_DIM
    first_half = (lane & (_HEAD_DIM // 2)) == 0
    n_pairs = width // _LANES

    cos = cos_ref[...]
    sin = sin_ref[...]
    for j, out_ref, scale in ((0, rq_ref, _HEAD_DIM ** -0.5), (1, rk_ref, None)):
        y = proj(j)
        for c in range(n_pairs):
            blk = y[:, c * _LANES:(c + 1) * _LANES]
            swapped = jnp.where(first_half, pltpu.roll(blk, _LANES - _HEAD_DIM // 2, 1),
                                pltpu.roll(blk, _HEAD_DIM // 2, 1))
            r = blk * cos + swapped * sin
            if scale is not None:
                r = r * scale
            out_ref[0, :, c * _LANES:(c + 1) * _LANES] = r.astype(_BF16)
    rv_ref[0] = proj(2).astype(_BF16)
    gate = proj(3)
    gate_ref[0] = (gate * jax.nn.sigmoid(gate)).astype(_BF16)

    half = tm // 2
    z = jnp.concatenate([jnp.dot(hb[:half], wff_ref[...], preferred_element_type=_F32),
                         jnp.dot(hb[half:], wff_ref[...], preferred_element_type=_F32)],
                        axis=0) * inv_rms + bf_ref[...]
    logf = (jnp.minimum(z, 0.0) - jnp.log(1.0 + jnp.exp(-jnp.abs(z)))) * _LOG2E
    hi, mid, lo = _split3(logf)
    tri = tri_ref[...]
    split = jnp.concatenate([hi, mid, lo], axis=1).astype(_BF16)
    parts = jnp.concatenate(
        [jnp.dot(tri[:half, :half], split[:half], preferred_element_type=_F32),
         jnp.dot(tri[half:], split, preferred_element_type=_F32)], axis=0)
    csum = parts[:, :_LANES] + parts[:, _LANES:2 * _LANES] + parts[:, 2 * _LANES:]

    @pl.when(i == 0)
    def _():
        carry_ref[...] = jnp.zeros_like(carry_ref)

    fcum = csum + carry_ref[0:1, :]
    carry_ref[...] = jnp.broadcast_to(fcum[tm - 1:tm, :], carry_ref.shape)
    fend_ref[0, 0] = carry_ref[...]

    def head_rms(y, g_row):
        out = []
        for g0 in range(0, width, _RET_GROUP):
            blk = y[:, g0:g0 + _RET_GROUP]
            ms = jnp.dot((blk * blk).astype(_BF16), avg_ref[...], preferred_element_type=_F32)
            out.append(blk * lax.rsqrt(ms + _EPS) * g_row[:, g0:g0 + _RET_GROUP])
        return out

    qn = head_rms(proj(4), gq_ref[...])
    kn = head_rms(proj(5), gk_ref[...])
    fv = proj(6)

    fh, fm, fl = _split3(fcum)
    packed = jnp.where(lane < _N_HEADS, fh, jnp.where(
        lane < 2 * _N_HEADS, pltpu.roll(fm, _N_HEADS, 1), pltpu.roll(fl, 2 * _N_HEADS, 1)))
    placed = jnp.dot(packed.astype(_BF16), place_ref[...], preferred_element_type=_F32)
    q_const = jnp.where((lane >= _AUG_KHI) & (lane <= _AUG_KLO), 1.0,
                        jnp.where(lane == _AUG_SHIFT, -shift_ref[...], 0.0))
    k_const = jnp.where(((lane >= _AUG_HI) & (lane <= _AUG_LO)) | (lane == _AUG_SHIFT), 1.0, 0.0)
    v_const = jnp.where(lane == _AUG_ONE, 1.0, 0.0)
    q_bias = (lane >= _AUG_HI) & (lane <= _AUG_LO)
    k_bias = (lane >= _AUG_KHI) & (lane <= _AUG_KLO)
    for h in range(_N_HEADS):
        g0, c0 = divmod(h * _HEAD_DIM, _RET_GROUP)
        c0 = (c0 // _LANES) * _LANES
        qb = qn[g0][:, c0:c0 + _LANES]
        kb = kn[g0][:, c0:c0 + _LANES]
        vb = fv[:, (h // 2) * _LANES:(h // 2 + 1) * _LANES]
        if h % 2 == 1:
            qb = pltpu.roll(qb, _HEAD_DIM, 1)
            kb = pltpu.roll(kb, _HEAD_DIM, 1)
            vb = pltpu.roll(vb, _HEAD_DIM, 1)
        bias = placed[:, h * _LANES:(h + 1) * _LANES]
        qa_ref[0, h] = jnp.where(low, qb, jnp.where(q_bias, bias, q_const)).astype(_BF16)
        ka_ref[0, h] = jnp.where(low, kb, jnp.where(k_bias, bias, k_const)).astype(_BF16)
        va_ref[0, h] = jnp.where(low, vb, v_const).astype(_BF16)


def _in_proj(x, g_mix, w_in_t, b_ff, cos_t, sin_t, gq, gk, shift, tri, avg, place, *, tm):
    B, T, D = x.shape
    width = _N_HEADS * _HEAD_DIM
    assert w_in_t.shape == (7 * width + _N_HEADS, D)
    tok = lambda b, i: (b, i, 0)
    head = lambda b, i: (b, 0, i, 0)
    bf_tok = jax.ShapeDtypeStruct((B, T, width), _BF16)
    bf_head = jax.ShapeDtypeStruct((B, _N_HEADS, T, _LANES), _BF16)
    return pl.pallas_call(
        functools.partial(_in_proj_body, tm=tm, width=width),
        grid=(B, T // tm),
        in_specs=[
            pl.BlockSpec((1, tm, D), tok),
            _const_spec((1, D)),
            _const_spec(w_in_t.shape),
            _const_spec((1, _LANES)),
            pl.BlockSpec((tm, _LANES), lambda b, i: (i, 0)),
            pl.BlockSpec((tm, _LANES), lambda b, i: (i, 0)),
            _const_spec((1, width)),
            _const_spec((1, width)),
            _const_spec((1, _LANES)),
            _const_spec((tm, tm)),
            _const_spec(avg.shape),
            _const_spec(place.shape),
        ],
        out_specs=[pl.BlockSpec((1, tm, width), tok)] * 4
        + [pl.BlockSpec((1, _N_HEADS, tm, _LANES), head)] * 3
        + [pl.BlockSpec((1, 1, 8, _LANES), lambda b, i: (b, i, 0, 0))],
        out_shape=[bf_tok] * 4 + [bf_head] * 3 + [jax.ShapeDtypeStruct((B, T // tm, 8, _LANES), _F32)],
        scratch_shapes=[pltpu.VMEM((D, 7 * width), _BF16), pltpu.VMEM((D, _LANES), _BF16),
                        pltpu.VMEM((8, _LANES), _F32)],
        compiler_params=_params(("arbitrary", "arbitrary")),
        name="in_proj",
    )(x, g_mix, w_in_t, b_ff, cos_t, sin_t, gq, gk, shift, tri, avg, place)


def _retention_body(rq_ref, rk_ref, rv_ref, gate_ref, dmask_ref, qdec_ref, kdec_ref, sdec_ref, bd_ref,
                    avg_ref, g_ref, o_ref, state_ref, *, tb):
    i = pl.program_id(1)

    @pl.when(i == 0)
    def _():
        state_ref[...] = jnp.zeros_like(state_ref)

    nt = (((1,), (1,)), ((), ()))
    tn = (((0,), (0,)), ((), ()))
    gw = state_ref.shape[1]
    lane = lax.broadcasted_iota(jnp.int32, (1, _LANES), 1)
    low = lax.broadcasted_iota(jnp.int32, (tb, _LANES), 1) < _HEAD_DIM
    head_lanes = [jnp.where(lane < _HEAD_DIM, 1.0, 0.0).astype(_BF16),
                  jnp.where(lane < _HEAD_DIM, 0.0, 1.0).astype(_BF16)]
    n_groups = state_ref.shape[0]
    groups = [slice(g * gw, (g + 1) * gw) for g in range(n_groups)]
    scores = []
    for h in range(_N_HEADS):
        ps = slice((h // 2) * _LANES, (h // 2 + 1) * _LANES)
        scores.append(lax.dot_general(rq_ref[0, :, ps] * head_lanes[h % 2], rk_ref[0, :, ps], nt,
                                      preferred_element_type=_F32))
    inter = []
    for g, gs in enumerate(groups):
        state = state_ref[g]
        inter.append(jnp.dot(rq_ref[0, :, gs], state.astype(_BF16),
                             preferred_element_type=_F32) * qdec_ref[:, gs])
        kd = (rk_ref[0, :, gs].astype(_F32) * kdec_ref[:, gs]).astype(_BF16)
        state_ref[g] = state * sdec_ref[g] + lax.dot_general(
            kd, rv_ref[0, :, gs], tn, preferred_element_type=_F32) * bd_ref[...]
    intra = []
    for h in range(_N_HEADS):
        ps = slice((h // 2) * _LANES, (h // 2 + 1) * _LANES)
        intra.append(jnp.dot((scores[h] * dmask_ref[h]).astype(_BF16), rv_ref[0, :, ps],
                             preferred_element_type=_F32))
    outs = []
    for g, gs in enumerate(groups):
        pairs = [jnp.where(low, intra[2 * c], intra[2 * c + 1])
                 for c in range(g * gw // _LANES, (g + 1) * gw // _LANES)]
        outs.append(jnp.concatenate(pairs, axis=1) + inter[g])
    mus = [jnp.dot(o.astype(_BF16), avg_ref[...], preferred_element_type=_F32) for o in outs]
    cents = [o - mu for o, mu in zip(outs, mus)]
    vars_ = [jnp.dot((oc * oc).astype(_BF16), avg_ref[...], preferred_element_type=_F32) for oc in cents]
    for gs, oc, var in zip(groups, cents, vars_):
        y = oc * lax.rsqrt(var + _EPS) * g_ref[:, gs]
        o_ref[0, :, gs] = (y * gate_ref[0, :, gs].astype(_F32)).astype(_BF16)


def _retention(rq, rk, rv, gate, dmask, qdec, kdec, sdec, bd, avg, g_ret, *, tb):
    B, T, W = rq.shape
    tok = pl.BlockSpec((1, tb, W), lambda b, i: (b, i, 0))
    consts = (dmask, qdec, kdec, sdec, bd, avg, g_ret)
    return pl.pallas_call(
        functools.partial(_retention_body, tb=tb),
        grid=(B, T // tb),
        in_specs=[tok, tok, tok, tok] + [_const_spec(c.shape) for c in consts],
        out_specs=tok,
        out_shape=jax.ShapeDtypeStruct((B, T, W), _BF16),
        scratch_shapes=[pltpu.VMEM(sdec.shape, _F32)],
        compiler_params=_params(("arbitrary", "arbitrary")),
        name="retention",
    )(rq, rk, rv, gate, *consts)


def _causal_mask(tq):
    row = lax.broadcasted_iota(jnp.int32, (tq, tq), 0)
    col = lax.broadcasted_iota(jnp.int32, (tq, tq), 1)
    return row >= col


def _fox_finish(acc_refs, o_ref, tq):
    lane = lax.broadcasted_iota(jnp.int32, (tq, _LANES), 1)
    for c in range(len(acc_refs) // 2):
        pair = []
        for hh in (2 * c, 2 * c + 1):
            acc = acc_refs[hh][...]
            pair.append(acc / acc[:, _AUG_ONE:_AUG_ONE + 1])
        o_ref[0, :, c * _LANES:(c + 1) * _LANES] = jnp.where(
            lane < _HEAD_DIM, pair[0], pltpu.roll(pair[1], _HEAD_DIM, 1)).astype(_BF16)


def _fox_shifted_body(jlo_ref, qa_ref, ka_ref, va_ref, o_ref, acc_ref, it_i, it_j, it_slot,
                      *, tq, nq, unroll):
    b, pair = pl.program_id(0), pl.program_id(1)
    n_pair = qa_ref.shape[1]
    nt = (((1,), (1,)), ((), ()))
    half = tq // 2
    diff_top = (lax.broadcasted_iota(jnp.int32, (half, half), 1)
                - lax.broadcasted_iota(jnp.int32, (half, half), 0))
    diff_bot = (lax.broadcasted_iota(jnp.int32, (half, tq), 1)
                - lax.broadcasted_iota(jnp.int32, (half, tq), 0))

    def interleave(n_chains, qk, finish):
        s_prev = qk(0)
        for u in range(1, n_chains):
            s_next = qk(u)
            finish(u - 1, s_prev)
            s_prev = s_next
        finish(n_chains - 1, s_prev)

    for hh in range(n_pair):
        base = (b * (pl.num_programs(1) * n_pair) + pair * n_pair + hh) * nq
        acc_ref[...] = jnp.zeros_like(acc_ref)

        n = jnp.int32(0)
        for i in range(1, nq):
            def add(j, n, i=i):
                it_i[n] = jnp.int32(i)
                it_j[n] = j
                it_slot[n] = jnp.int32(i)
                return n + 1

            n = lax.fori_loop(jlo_ref[base + i], i, add, n)
        n_trips = (n + (unroll - 1)) // unroll

        def pad(m, carry):
            it_i[m] = jnp.int32(0)
            it_j[m] = jnp.int32(0)
            it_slot[m] = jnp.int32(nq)
            return carry

        lax.fori_loop(n, n_trips * unroll, pad, 0)

        def trip(t, carry, hh=hh):
            items = [(it_i[t * unroll + u], it_j[t * unroll + u], it_slot[t * unroll + u])
                     for u in range(unroll)]

            def qk(u):
                i, j, _ = items[u]
                return lax.dot_general(qa_ref[0, hh, pl.ds(pl.multiple_of(i * tq, tq), tq), :],
                                       ka_ref[0, hh, pl.ds(pl.multiple_of(j * tq, tq), tq), :], nt,
                                       preferred_element_type=_F32)

            def finish(u, s):
                _, j, slot = items[u]
                acc_ref[slot] += jnp.dot(jnp.exp2(s).astype(_BF16),
                                         va_ref[0, hh, pl.ds(pl.multiple_of(j * tq, tq), tq), :],
                                         preferred_element_type=_F32)

            interleave(unroll, qk, finish)
            return carry

        lax.fori_loop(0, n_trips, trip, 0)

        def diag_trip(t, carry, hh=hh):
            def rows(u):
                i, bottom = _DIAG_PER_TRIP * t + u // 2, u % 2
                return i, bottom, pl.multiple_of(i * tq + bottom * half, half)

            def qk(u):
                i, bottom, r0 = rows(u)
                k0 = pl.multiple_of(i * tq, tq)
                width = tq if bottom else half
                return lax.dot_general(qa_ref[0, hh, pl.ds(r0, half), :],
                                       ka_ref[0, hh, pl.ds(k0, width), :], nt, preferred_element_type=_F32)

            def finish(u, s):
                i, bottom, r0 = rows(u)
                k0 = pl.multiple_of(i * tq, tq)
                width = tq if bottom else half
                live = (diff_bot <= half) if bottom else (diff_top <= 0)
                p = jnp.exp2(jnp.where(live, s, _NEG_INF)).astype(_BF16)
                acc = acc_ref[i, bottom * half:(bottom + 1) * half, :] + jnp.dot(
                    p, va_ref[0, hh, pl.ds(k0, width), :], preferred_element_type=_F32)
                o = acc / acc[:, _AUG_ONE:_AUG_ONE + 1]
                if hh % 2 == 1:
                    o = pltpu.roll(o, _HEAD_DIM, 1)
                sl = slice(hh * _HEAD_DIM, (hh + 1) * _HEAD_DIM)
                o_ref[0, pl.ds(r0, half), sl] = o[:, sl].astype(_BF16)

            interleave(2 * _DIAG_PER_TRIP, qk, finish)
            return carry

        lax.fori_loop(0, nq // _DIAG_PER_TRIP, diag_trip, 0)


def _fox_online_body(qa_ref, ka_ref, va_ref, o_ref, *scratch, tq):
    qi = pl.program_id(2)
    nt = (((1,), (1,)), ((), ()))
    n_heads = qa_ref.shape[1]
    m_refs, acc_refs = scratch[:n_heads], scratch[n_heads:]
    for hh in range(n_heads):
        m_refs[hh][...] = jnp.full_like(m_refs[hh], _NEG_INF)
        acc_refs[hh][...] = jnp.zeros_like(acc_refs[hh])

    def step(j, masked):
        start = pl.multiple_of(j * tq, tq)
        logits = [lax.dot_general(qa_ref[0, hh], ka_ref[0, hh, pl.ds(start, tq), :], nt,
                                  preferred_element_type=_F32) for hh in range(n_heads)]
        for hh in range(n_heads):
            m_ref, acc_ref = m_refs[hh], acc_refs[hh]
            s = logits[hh]
            if masked:
                s = jnp.where(_causal_mask(tq), s, _NEG_INF)
            m_old = m_ref[...]
            m_new = jnp.maximum(m_old, jnp.max(s, axis=-1, keepdims=True))
            p = jnp.exp2(s - m_new[:, 0:1])
            acc_ref[...] = jnp.exp2(m_old - m_new) * acc_ref[...] + jnp.dot(
                p.astype(_BF16), va_ref[0, hh, pl.ds(start, tq), :], preferred_element_type=_F32)
            m_ref[...] = m_new

    def off_diag(j, carry):
        step(j, False)
        return carry

    lax.fori_loop(0, qi, off_diag, 0)
    step(qi, True)
    _fox_finish(acc_refs, o_ref, tq)


def _fox_shifted(jlo, qa, ka, va, *, tq, unroll):
    B, H, T, L = qa.shape
    nq = T // tq
    assert nq % _DIAG_PER_TRIP == 0 and tq % (2 * _MXU_TILE) == 0
    max_items = nq * (nq - 1) // 2 + unroll
    blk = pl.BlockSpec((1, 2, T, L), lambda b, p, jlo_ref: (b, p, 0, 0))
    return pl.pallas_call(
        functools.partial(_fox_shifted_body, tq=tq, nq=nq, unroll=unroll),
        grid_spec=pltpu.PrefetchScalarGridSpec(
            num_scalar_prefetch=1,
            grid=(B, H // 2),
            in_specs=[blk, blk, blk],
            out_specs=pl.BlockSpec((1, T, L), lambda b, p, jlo_ref: (b, 0, p)),
            scratch_shapes=[pltpu.VMEM((nq + 1, tq, L), _F32)] + [pltpu.SMEM((max_items,), jnp.int32)] * 3,
        ),
        out_shape=jax.ShapeDtypeStruct((B, T, H * _HEAD_DIM), _BF16),
        compiler_params=_params(("arbitrary", "arbitrary")),
        name="fox_shifted",
    )(jlo, qa, ka, va)


def _fox_online(qa, ka, va, *, tq, hg):
    B, H, T, L = qa.shape
    return pl.pallas_call(
        functools.partial(_fox_online_body, tq=tq),
        grid=(B, H // hg, T // tq),
        in_specs=[
            pl.BlockSpec((1, hg, tq, L), lambda b, p, i: (b, p, i, 0)),
            pl.BlockSpec((1, hg, T, L), lambda b, p, i: (b, p, 0, 0)),
            pl.BlockSpec((1, hg, T, L), lambda b, p, i: (b, p, 0, 0)),
        ],
        out_specs=pl.BlockSpec((1, tq, hg * _HEAD_DIM), lambda b, p, i: (b, i, p)),
        out_shape=jax.ShapeDtypeStruct((B, T, H * _HEAD_DIM), _BF16),
        scratch_shapes=[pltpu.VMEM((tq, L), _F32)] * (2 * hg),
        compiler_params=_params(("arbitrary", "arbitrary", "arbitrary")),
        name="fox_online",
    )(qa, ka, va)


def _first_live_block(fend):
    B, nq, H = fend.shape
    f = jnp.transpose(fend, (0, 2, 1))
    top = jnp.concatenate([jnp.zeros((B, H, 1), _F32), f[:, :, :-1]], axis=-1)
    dead = (top[:, :, :, None] - f[:, :, None, :]) <= _UNDERFLOW_LOG2
    j = jnp.arange(nq, dtype=jnp.int32)
    before = j[None, :] < j[:, None]
    first_live = jnp.min(jnp.where(dead & before, nq, j), axis=-1)
    return first_live.astype(jnp.int32).reshape(-1)


def _mem_kv_body(mem_ref, gm_ref, w32_ref, gk_ref, k_ref, v_ref, w_ref, *, d_model, xd):
    @pl.when(pl.program_id(0) == 0)
    def _():
        w_ref[...] = w32_ref[...].astype(_BF16)

    mn = _rms(mem_ref[0], gm_ref[...]).astype(_BF16)
    kv = jnp.dot(mn, w_ref[...], preferred_element_type=_F32)
    for h in range(_N_XHEADS):
        sl = slice(h * xd, (h + 1) * xd)
        k_ref[0, :, sl] = (_rms(kv[:, sl], gk_ref[...]) * (xd ** -0.5)).astype(_BF16)
    v_ref[0] = kv[:, d_model:].astype(_BF16)


def _mem_kv(mem, g_mem, w_xkv, g_xk):
    B, M, D = mem.shape
    xd = D // _N_XHEADS
    blk = pl.BlockSpec((1, M, D), lambda b: (b, 0, 0))
    out = jax.ShapeDtypeStruct((B, M, D), _BF16)
    return pl.pallas_call(
        functools.partial(_mem_kv_body, d_model=D, xd=xd),
        grid=(B,),
        in_specs=[blk, _const_spec((1, D)), _const_spec(w_xkv.shape), _const_spec((1, xd))],
        out_specs=[blk, blk],
        out_shape=[out, out],
        scratch_shapes=[pltpu.VMEM(w_xkv.shape, _BF16)],
        compiler_params=_params(("arbitrary",)),
        name="mem_kv",
    )(mem, g_mem, w_xkv, g_xk)


def _mix_xattn_body(x_ref, ret_ref, fox_ref, wo32_ref, gx_ref, wq32_ref, gq_ref, k_ref, v_ref, wxo32_ref,
                    o_ref, wo_ref, wq_ref, wxo_ref, *, width, xd):
    @pl.when((pl.program_id(0) == 0) & (pl.program_id(1) == 0))
    def _():
        wo_ref[...] = wo32_ref[...].astype(_BF16)
        wq_ref[...] = wq32_ref[...].astype(_BF16)
        wxo_ref[...] = wxo32_ref[...].astype(_BF16)

    h1 = (x_ref[0]
          + jnp.dot(ret_ref[0], wo_ref[:width, :], preferred_element_type=_F32)
          + jnp.dot(fox_ref[0], wo_ref[width:, :], preferred_element_type=_F32))
    hn, inv_h = _rms_split(h1, gx_ref[...])
    y = jnp.dot(hn, wq_ref[...], preferred_element_type=_F32)
    nt = (((1,), (1,)), ((), ()))
    head_slices = [slice(h * xd, (h + 1) * xd) for h in range(_N_XHEADS)]
    logits = []
    for sl in head_slices:
        yh = y[:, sl]
        inv_q = lax.rsqrt(inv_h * inv_h * jnp.mean(yh * yh, axis=-1, keepdims=True) + _EPS)
        logits.append(lax.dot_general((yh * gq_ref[...]).astype(_BF16), k_ref[0, :, sl], nt,
                                      preferred_element_type=_F32) * (inv_q * inv_h * _LOG2E))
    probs = []
    for lg in logits:
        p = jnp.exp2(lg - jnp.max(lg, axis=-1, keepdims=True))
        probs.append((p / jnp.sum(p, axis=-1, keepdims=True)).astype(_BF16))
    heads = [jnp.dot(p, v_ref[0, :, sl], preferred_element_type=_F32).astype(_BF16)
             for p, sl in zip(probs, head_slices)]
    o = jnp.concatenate(heads, axis=-1)
    o_ref[0] = h1 + jnp.dot(o, wxo_ref[...], preferred_element_type=_F32)


def _mix_xattn(x, ret, fox, w_out, g_xattn, w_xq, g_xq, k, v, w_xo, *, tm):
    B, T, D = x.shape
    W = ret.shape[-1]
    M = k.shape[1]
    xd = D // _N_XHEADS
    tok = lambda b, i: (b, i, 0)
    return pl.pallas_call(
        functools.partial(_mix_xattn_body, width=W, xd=xd),
        grid=(B, T // tm),
        in_specs=[
            pl.BlockSpec((1, tm, D), tok),
            pl.BlockSpec((1, tm, W), tok),
            pl.BlockSpec((1, tm, W), tok),
            _const_spec(w_out.shape),
            _const_spec((1, D)),
            _const_spec(w_xq.shape),
            _const_spec((1, xd)),
            pl.BlockSpec((1, M, D), lambda b, i: (b, 0, 0)),
            pl.BlockSpec((1, M, D), lambda b, i: (b, 0, 0)),
            _const_spec(w_xo.shape),
        ],
        out_specs=pl.BlockSpec((1, tm, D), tok),
        out_shape=jax.ShapeDtypeStruct((B, T, D), _F32),
        scratch_shapes=[pltpu.VMEM(w.shape, _BF16) for w in (w_out, w_xq, w_xo)],
        compiler_params=_params(("arbitrary", "arbitrary")),
        name="mix_xattn",
    )(x, ret, fox, w_out, g_xattn, w_xq, g_xq, k, v, w_xo)


def _ffn_chunks(d_ff, n_chunks):
    tiles = -(-d_ff // _MXU_TILE)
    bounds = [min(d_ff, _MXU_TILE * ((tiles * c) // n_chunks)) for c in range(n_chunks)] + [d_ff]
    return [(lo, hi) for lo, hi in zip(bounds[:-1], bounds[1:]) if hi > lo]


def _ffn_body(h_ref, g_ref, wg_ref, wu_ref, wd_ref, o_ref, *, chunks):
    h = h_ref[0]
    hn, inv_rms = _rms_split(h, g_ref[...])
    acc = h
    for lo, hi in chunks:
        sl = slice(lo, hi)
        gate = jnp.dot(hn, wg_ref[:, sl], preferred_element_type=_F32) * inv_rms
        up = jnp.dot(hn, wu_ref[:, sl], preferred_element_type=_F32) * inv_rms
        a = (gate * jax.nn.sigmoid(gate) * up).astype(_BF16)
        acc = acc + jnp.dot(a, wd_ref[sl, :], preferred_element_type=_F32)
    o_ref[0] = acc


def _ffn(h, g_ffn, w_gate, w_up, w_down, *, tm, n_split):
    B, T, D = h.shape
    F = w_gate.shape[1]
    tok = pl.BlockSpec((1, tm, D), lambda b, i: (b, i, 0))
    return pl.pallas_call(
        functools.partial(_ffn_body, chunks=_ffn_chunks(F, n_split)),
        grid=(B, T // tm),
        in_specs=[tok, _const_spec((1, D)), _const_spec(w_gate.shape), _const_spec(w_up.shape),
                  _const_spec(w_down.shape)],
        out_specs=tok,
        out_shape=jax.ShapeDtypeStruct((B, T, D), _F32),
        compiler_params=_params(("arbitrary", "arbitrary")),
        name="ffn",
    )(h, g_ffn, w_gate, w_up, w_down)


def _rope_tables(T):
    half = _HEAD_DIM // 2
    inv_freq = (_ROPE_BASE ** (-np.arange(0, _HEAD_DIM, 2, dtype=np.float32) / _HEAD_DIM)).astype(np.float32)
    ang = (np.arange(T, dtype=np.float32)[:, None] * inv_freq[None, :]).astype(np.float32).astype(np.float64)
    cos, sin = np.cos(ang), np.sin(ang)
    reps = _LANES // _HEAD_DIM
    cos_t = np.tile(np.concatenate([cos, cos], axis=1), (1, reps))
    sin_t = np.tile(np.concatenate([-sin, sin], axis=1), (1, reps))
    return jnp.asarray(cos_t, _F32), jnp.asarray(sin_t, _F32)


def _retention_tables(tb):
    log_g = np.log(1.0 - 2.0 ** (-5.0 - np.arange(_N_HEADS, dtype=np.float64)))
    idx = np.arange(tb, dtype=np.float64)
    dist = np.abs(idx[:, None] - idx[None, :])
    chunk = np.arange(tb) // _RET_CHUNK
    visible = chunk[None, :] <= chunk[:, None]
    dmask = np.where(visible[None], np.exp(log_g[:, None, None] * dist[None]), 0.0)
    qdec = np.repeat(np.exp(log_g[None, :] * (idx[:, None] + 1.0)), _HEAD_DIM, axis=1)
    kdec = np.repeat(np.exp(log_g[None, :] * (tb - 1.0 - idx[:, None])), _HEAD_DIM, axis=1)
    heads_per_group = _RET_GROUP // _HEAD_DIM
    head_of = np.arange(_RET_GROUP) // _HEAD_DIM
    bd = (head_of[:, None] == head_of[None, :]).astype(np.float64)
    step_decay = np.exp(log_g * tb).reshape(-1, heads_per_group)
    sdec = bd[None] * np.repeat(step_decay, _HEAD_DIM, axis=1)[:, None, :]
    f = lambda a: jnp.asarray(a, _F32)
    return f(dmask), f(qdec), f(kdec), f(sdec), f(bd), jnp.asarray(bd / _HEAD_DIM, _BF16)


def _bias_placement():
    place = np.zeros((_LANES, _N_HEADS * _LANES), np.float32)
    for h in range(_N_HEADS):
        for part, (q_lane, k_lane) in enumerate(((_AUG_HI, _AUG_KHI), (_AUG_MID, _AUG_KMID),
                                                 (_AUG_LO, _AUG_KLO))):
            place[part * _N_HEADS + h, h * _LANES + q_lane] = 1.0
            place[part * _N_HEADS + h, h * _LANES + k_lane] = -1.0
    return jnp.asarray(place, _BF16)


def _pad_lanes(a):
    return jnp.pad(a, [(0, 0)] * (a.ndim - 1) + [(0, _LANES - a.shape[-1])])


def kernel(x, mem, g_mix, w_in, b_forget, g_ret_out, g_fox_q, g_fox_k, w_out, g_xattn, w_xq, w_xkv,
           g_mem, g_xq, g_xk, w_xo, g_ffn, w_gate, w_up, w_down):
    B, T, D = x.shape
    width = _N_HEADS * _HEAD_DIM
    tb, tq, tm_mix, tm_ffn = 256, 512, 512, 512
    cos_t, sin_t = _rope_tables(T)
    ret_tables = _retention_tables(tb)
    tri = jnp.asarray(np.tril(np.ones((tq, tq), np.float32)), _BF16)
    place = _bias_placement()
    row = lambda a: a.reshape(1, -1).astype(_F32)

    h = x
    for l in range(w_in.shape[0]):
        b_ff = _pad_lanes(row(b_forget[l]))
        gq = jnp.tile(row(g_fox_q[l]), (1, _N_HEADS)) * (_LOG2E * _HEAD_DIM ** -0.5)
        gk = jnp.tile(row(g_fox_k[l]), (1, _N_HEADS))
        bound = (_HEAD_DIM ** 0.5 * _NORM_ROUNDING_SLACK) * jnp.max(jnp.abs(g_fox_q[l])) * jnp.max(
            jnp.abs(g_fox_k[l]))
        use_shift = bound <= _MAX_FIXED_SHIFT
        shift = jnp.where(use_shift, jnp.ceil(bound * (4.0 * _LOG2E)) * 0.25, 0.0).astype(_F32)
        rq, rk, rv, gate, qa, ka, va, fend = _in_proj(h, row(g_mix[l]), w_in[l].T, b_ff, cos_t, sin_t,
                                                      gq, gk, jnp.full((1, _LANES), shift), tri,
                                                      ret_tables[-1], place, tm=tq)
        ret = _retention(rq, rk, rv, gate, *ret_tables, row(g_ret_out[l]), tb=tb)
        jlo = _first_live_block(fend[:, :, 0, :_N_HEADS])
        fox = lax.cond(use_shift,
                       lambda jlo, qa, ka, va: _fox_shifted(jlo, qa, ka, va, tq=tq, unroll=4),
                       lambda jlo, qa, ka, va: _fox_online(qa, ka, va, tq=tq, hg=4), jlo, qa, ka, va)
        k, v = _mem_kv(mem, row(g_mem[l]), w_xkv[l], row(g_xk[l]))
        h = _mix_xattn(h, ret, fox, w_out[l], row(g_xattn[l]), w_xq[l], row(g_xq[l]), k, v, w_xo[l],
                       tm=tm_mix)
        h = _ffn(h, row(g_ffn[l]), w_gate[l].astype(_BF16), w_up[l].astype(_BF16),
                 w_down[l].astype(_BF16), tm=tm_ffn, n_split=2)
    return h
```

```python
import functools

import numpy as np
import jax
import jax.numpy as jnp
from jax import lax
from jax.experimental import pallas as pl
from jax.experimental.pallas import tpu as pltpu

_BF16 = jnp.bfloat16
_F32 = jnp.float32

_EPS = 1e-6
_NEG_INF = -1e30
_ROPE_BASE = 10000.0
_HEAD_DIM = 64
_N_HEADS = 8
_RET_CHUNK = 64
_RET_GROUP = 256
_N_XHEADS = 4
_LANES = 128
_MXU_TILE = 256
_VMEM_LIMIT = 56 * 1024 * 1024

_AUG_QF = 64
_AUG_KF = 88
_AUG_SHIFT = 112
_AUG_ONE = 64

_LOG2E = 1.4426950408889634
_MAX_FIXED_SHIFT = 32.0
_DIAG_PER_TRIP = 4
_UNDERFLOW_LOG2 = -152.0
_NORM_ROUNDING_SLACK = 1.01


def _params(sem):
    return pltpu.CompilerParams(dimension_semantics=sem, vmem_limit_bytes=_VMEM_LIMIT)


def _const_spec(shape):
    nd = len(shape)
    return pl.BlockSpec(shape, lambda *_: (0,) * nd, pipeline_mode=pl.Buffered(1))


def _rms(x, g):
    return x * lax.rsqrt(jnp.mean(x * x, axis=-1, keepdims=True) + _EPS) * g


def _rms_split(x, g):
    return (x * g).astype(_BF16), lax.rsqrt(jnp.mean(x * x, axis=-1, keepdims=True) + _EPS)


def _split3(v):
    hi = v.astype(_BF16).astype(_F32)
    r = v - hi
    mid = r.astype(_BF16).astype(_F32)
    return hi, mid, r - mid


def _in_proj_body(x_ref, g_ref, w32_ref, bf_ref, cos_ref, sin_ref, gq_ref, gk_ref, shift_ref,
                  tri_ref, avg_ref, rq_ref, rk_ref, rv_ref, gate_ref, qa_ref, ka_ref, va_ref,
                  fend_ref, w_ref, wff_ref, carry_ref, *, tm, width):
    i = pl.program_id(1)
    n_main = 7 * width

    @pl.when((pl.program_id(0) == 0) & (i == 0))
    def _():
        for j in range(n_main // width):
            w_ref[:, j * width:(j + 1) * width] = jnp.transpose(
                w32_ref[j * width:(j + 1) * width, :]).astype(_BF16)
        tail = jnp.concatenate([w32_ref[n_main:, :],
                                jnp.zeros((_LANES - _N_HEADS, w32_ref.shape[1]), _F32)], axis=0)
        wff_ref[...] = jnp.transpose(tail).astype(_BF16)

    hb, inv_rms = _rms_split(x_ref[0], g_ref[...])

    def proj(j):
        return jnp.dot(hb, w_ref[:, j * width:(j + 1) * width], preferred_element_type=_F32) * inv_rms

    lane = lax.broadcasted_iota(jnp.int32, (tm, _LANES), 1)
    low = lane < _HEAD_DIM
    first_half = (lane & (_HEAD_DIM // 2)) == 0
    n_pairs = width // _LANES

    cos = cos_ref[...]
    sin = sin_ref[...]
    for j, out_ref, scale in ((0, rq_ref, _HEAD_DIM ** -0.5), (1, rk_ref, None)):
        y = proj(j)
        for c in range(n_pairs):
            blk = y[:, c * _LANES:(c + 1) * _LANES]
            swapped = jnp.where(first_half, pltpu.roll(blk, _LANES - _HEAD_DIM // 2, 1),
                                pltpu.roll(blk, _HEAD_DIM // 2, 1))
            r = blk * cos + swapped * sin
            if scale is not None:
                r = r * scale
            out_ref[0, :, c * _LANES:(c + 1) * _LANES] = r.astype(_BF16)
    rv_ref[0] = proj(2).astype(_BF16)
    gate = proj(3)
    gate_ref[0] = (gate * jax.nn.sigmoid(gate)).astype(_BF16)

    half = tm // 2
    z = jnp.concatenate([jnp.dot(hb[:half], wff_ref[...], preferred_element_type=_F32),
                         jnp.dot(hb[half:], wff_ref[...], preferred_element_type=_F32)],
                        axis=0) * inv_rms + bf_ref[...]
    logf = (jnp.minimum(z, 0.0) - jnp.log(1.0 + jnp.exp(-jnp.abs(z)))) * _LOG2E
    hi, mid, lo = _split3(logf)
    tri = tri_ref[...]
    split = jnp.concatenate([hi, mid, lo], axis=1).astype(_BF16)
    parts = jnp.concatenate(
        [jnp.dot(tri[:half, :half], split[:half], preferred_element_type=_F32),
         jnp.dot(tri[half:], split, preferred_element_type=_F32)], axis=0)
    csum = parts[:, :_LANES] + parts[:, _LANES:2 * _LANES] + parts[:, 2 * _LANES:]

    @pl.when(i == 0)
    def _():
        carry_ref[...] = jnp.zeros_like(carry_ref)

    fcum = csum + carry_ref[0:1, :]
    carry_ref[...] = jnp.broadcast_to(fcum[tm - 1:tm, :], carry_ref.shape)
    fend_ref[0, 0] = carry_ref[...]

    def head_rms(y, g_row):
        out = []
        for g0 in range(0, width, _RET_GROUP):
            blk = y[:, g0:g0 + _RET_GROUP]
            ms = jnp.dot((blk * blk).astype(_BF16), avg_ref[...], preferred_element_type=_F32)
            out.append(blk * lax.rsqrt(ms + _EPS) * g_row[:, g0:g0 + _RET_GROUP])
        return out

    qn = head_rms(proj(4), gq_ref[...])
    kn = head_rms(proj(5), gk_ref[...])
    fv = proj(6)

    fh, fm, fl = _split3(fcum)
    n_parts = 3 * _N_HEADS
    packed = jnp.where(lane < _N_HEADS, fh, jnp.where(
        lane < 2 * _N_HEADS, pltpu.roll(fm, _N_HEADS, 1), pltpu.roll(fl, 2 * _N_HEADS, 1)))
    in_qf = (lane >= _AUG_QF) & (lane < _AUG_QF + n_parts)
    in_kf = (lane >= _AUG_KF) & (lane < _AUG_KF + n_parts)
    shared = jnp.where(in_qf, pltpu.roll(packed, _AUG_QF, 1), -pltpu.roll(packed, _AUG_KF, 1))
    q_bias = jnp.where(in_qf, shared, jnp.where(in_kf, 1.0, jnp.where(
        lane == _AUG_SHIFT, -shift_ref[...], 0.0)))
    v_const = jnp.where(lane == _AUG_ONE, 1.0, 0.0)
    for h in range(_N_HEADS):
        own = (lane & (_N_HEADS - 1)) == h
        k_bias = jnp.where(in_kf & own, shared,
                           jnp.where((in_qf & own) | (lane == _AUG_SHIFT), 1.0, 0.0))
        g0, c0 = divmod(h * _HEAD_DIM, _RET_GROUP)
        c0 = (c0 // _LANES) * _LANES
        qb = qn[g0][:, c0:c0 + _LANES]
        kb = kn[g0][:, c0:c0 + _LANES]
        vb = fv[:, (h // 2) * _LANES:(h // 2 + 1) * _LANES]
        if h % 2 == 1:
            qb = pltpu.roll(qb, _HEAD_DIM, 1)
            kb = pltpu.roll(kb, _HEAD_DIM, 1)
            vb = pltpu.roll(vb, _HEAD_DIM, 1)
        qa_ref[0, h] = jnp.where(low, qb, q_bias).astype(_BF16)
        ka_ref[0, h] = jnp.where(low, kb, k_bias).astype(_BF16)
        va_ref[0, h] = jnp.where(low, vb, v_const).astype(_BF16)


def _in_proj(x, g_mix, w_in_t, b_ff, cos_t, sin_t, gq, gk, shift, tri, avg, *, tm):
    B, T, D = x.shape
    width = _N_HEADS * _HEAD_DIM
    assert w_in_t.shape == (7 * width + _N_HEADS, D)
    tok = lambda b, i: (b, i, 0)
    head = lambda b, i: (b, 0, i, 0)
    bf_tok = jax.ShapeDtypeStruct((B, T, width), _BF16)
    bf_head = jax.ShapeDtypeStruct((B, _N_HEADS, T, _LANES), _BF16)
    return pl.pallas_call(
        functools.partial(_in_proj_body, tm=tm, width=width),
        grid=(B, T // tm),
        in_specs=[
            pl.BlockSpec((1, tm, D), tok),
            _const_spec((1, D)),
            _const_spec(w_in_t.shape),
            _const_spec((1, _LANES)),
            pl.BlockSpec((tm, _LANES), lambda b, i: (i, 0)),
            pl.BlockSpec((tm, _LANES), lambda b, i: (i, 0)),
            _const_spec((1, width)),
            _const_spec((1, width)),
            _const_spec((1, _LANES)),
            _const_spec((tm, tm)),
            _const_spec(avg.shape),
        ],
        out_specs=[pl.BlockSpec((1, tm, width), tok)] * 4
        + [pl.BlockSpec((1, _N_HEADS, tm, _LANES), head)] * 3
        + [pl.BlockSpec((1, 1, 8, _LANES), lambda b, i: (b, i, 0, 0))],
        out_shape=[bf_tok] * 4 + [bf_head] * 3 + [jax.ShapeDtypeStruct((B, T // tm, 8, _LANES), _F32)],
        scratch_shapes=[pltpu.VMEM((D, 7 * width), _BF16), pltpu.VMEM((D, _LANES), _BF16),
                        pltpu.VMEM((8, _LANES), _F32)],
        compiler_params=_params(("arbitrary", "arbitrary")),
        name="in_proj",
    )(x, g_mix, w_in_t, b_ff, cos_t, sin_t, gq, gk, shift, tri, avg)


def _retention_body(rq_ref, rk_ref, rv_ref, gate_ref, dmask_ref, qdec_ref, kdec_ref, sdec_ref, bd_ref,
                    avg_ref, g_ref, o_ref, state_ref, *, tb):
    i = pl.program_id(1)

    @pl.when(i == 0)
    def _():
        state_ref[...] = jnp.zeros_like(state_ref)

    nt = (((1,), (1,)), ((), ()))
    tn = (((0,), (0,)), ((), ()))
    gw = state_ref.shape[1]
    lane = lax.broadcasted_iota(jnp.int32, (1, _LANES), 1)
    low = lax.broadcasted_iota(jnp.int32, (tb, _LANES), 1) < _HEAD_DIM
    head_lanes = [jnp.where(lane < _HEAD_DIM, 1.0, 0.0).astype(_BF16),
                  jnp.where(lane < _HEAD_DIM, 0.0, 1.0).astype(_BF16)]
    n_groups = state_ref.shape[0]
    groups = [slice(g * gw, (g + 1) * gw) for g in range(n_groups)]
    scores = []
    for h in range(_N_HEADS):
        ps = slice((h // 2) * _LANES, (h // 2 + 1) * _LANES)
        scores.append(lax.dot_general(rq_ref[0, :, ps] * head_lanes[h % 2], rk_ref[0, :, ps], nt,
                                      preferred_element_type=_F32))
    inter = []
    for g, gs in enumerate(groups):
        state = state_ref[g]
        inter.append(jnp.dot(rq_ref[0, :, gs], state.astype(_BF16),
                             preferred_element_type=_F32) * qdec_ref[:, gs])
        kd = (rk_ref[0, :, gs].astype(_F32) * kdec_ref[:, gs]).astype(_BF16)
        state_ref[g] = state * sdec_ref[g] + lax.dot_general(
            kd, rv_ref[0, :, gs], tn, preferred_element_type=_F32) * bd_ref[...]
    intra = []
    for h in range(_N_HEADS):
        ps = slice((h // 2) * _LANES, (h // 2 + 1) * _LANES)
        intra.append(jnp.dot((scores[h] * dmask_ref[h]).astype(_BF16), rv_ref[0, :, ps],
                             preferred_element_type=_F32))
    outs = []
    for g, gs in enumerate(groups):
        pairs = [jnp.where(low, intra[2 * c], intra[2 * c + 1])
                 for c in range(g * gw // _LANES, (g + 1) * gw // _LANES)]
        outs.append(jnp.concatenate(pairs, axis=1) + inter[g])
    mus = [jnp.dot(o.astype(_BF16), avg_ref[...], preferred_element_type=_F32) for o in outs]
    cents = [o - mu for o, mu in zip(outs, mus)]
    vars_ = [jnp.dot((oc * oc).astype(_BF16), avg_ref[...], preferred_element_type=_F32) for oc in cents]
    for gs, oc, var in zip(groups, cents, vars_):
        y = oc * lax.rsqrt(var + _EPS) * g_ref[:, gs]
        o_ref[0, :, gs] = (y * gate_ref[0, :, gs].astype(_F32)).astype(_BF16)


def _retention(rq, rk, rv, gate, dmask, qdec, kdec, sdec, bd, avg, g_ret, *, tb):
    B, T, W = rq.shape
    tok = pl.BlockSpec((1, tb, W), lambda b, i: (b, i, 0))
    consts = (dmask, qdec, kdec, sdec, bd, avg, g_ret)
    return pl.pallas_call(
        functools.partial(_retention_body, tb=tb),
        grid=(B, T // tb),
        in_specs=[tok, tok, tok, tok] + [_const_spec(c.shape) for c in consts],
        out_specs=tok,
        out_shape=jax.ShapeDtypeStruct((B, T, W), _BF16),
        scratch_shapes=[pltpu.VMEM(sdec.shape, _F32)],
        compiler_params=_params(("arbitrary", "arbitrary")),
        name="retention",
    )(rq, rk, rv, gate, *consts)


def _causal_mask(tq):
    row = lax.broadcasted_iota(jnp.int32, (tq, tq), 0)
    col = lax.broadcasted_iota(jnp.int32, (tq, tq), 1)
    return row >= col


def _fox_finish(acc_refs, o_ref, tq):
    lane = lax.broadcasted_iota(jnp.int32, (tq, _LANES), 1)
    for c in range(len(acc_refs) // 2):
        pair = []
        for hh in (2 * c, 2 * c + 1):
            acc = acc_refs[hh][...]
            pair.append(acc / acc[:, _AUG_ONE:_AUG_ONE + 1])
        o_ref[0, :, c * _LANES:(c + 1) * _LANES] = jnp.where(
            lane < _HEAD_DIM, pair[0], pltpu.roll(pair[1], _HEAD_DIM, 1)).astype(_BF16)


def _fox_shifted_body(jlo_ref, qa_ref, ka_ref, va_ref, o_ref, acc_ref, it_i, it_j, it_slot,
                      *, tq, nq, unroll):
    b, pair = pl.program_id(0), pl.program_id(1)
    n_pair = qa_ref.shape[1]
    nt = (((1,), (1,)), ((), ()))
    half = tq // 2
    diff_top = (lax.broadcasted_iota(jnp.int32, (half, half), 1)
                - lax.broadcasted_iota(jnp.int32, (half, half), 0))
    diff_bot = (lax.broadcasted_iota(jnp.int32, (half, tq), 1)
                - lax.broadcasted_iota(jnp.int32, (half, tq), 0))

    def interleave(n_chains, qk, finish):
        s_prev = qk(0)
        for u in range(1, n_chains):
            s_next = qk(u)
            finish(u - 1, s_prev)
            s_prev = s_next
        finish(n_chains - 1, s_prev)

    for hh in range(n_pair):
        base = (b * (pl.num_programs(1) * n_pair) + pair * n_pair + hh) * nq
        acc_ref[...] = jnp.zeros_like(acc_ref)

        n = jnp.int32(0)
        for i in range(1, nq):
            def add(j, n, i=i):
                it_i[n] = jnp.int32(i)
                it_j[n] = j
                it_slot[n] = jnp.int32(i)
                return n + 1

            n = lax.fori_loop(jlo_ref[base + i], i, add, n)
        n_trips = (n + (unroll - 1)) // unroll

        def pad(m, carry):
            it_i[m] = jnp.int32(0)
            it_j[m] = jnp.int32(0)
            it_slot[m] = jnp.int32(nq)
            return carry

        lax.fori_loop(n, n_trips * unroll, pad, 0)

        def trip(t, carry, hh=hh):
            items = [(it_i[t * unroll + u], it_j[t * unroll + u], it_slot[t * unroll + u])
                     for u in range(unroll)]

            def qk(u):
                i, j, _ = items[u]
                return lax.dot_general(qa_ref[0, hh, pl.ds(pl.multiple_of(i * tq, tq), tq), :],
                                       ka_ref[0, hh, pl.ds(pl.multiple_of(j * tq, tq), tq), :], nt,
                                       preferred_element_type=_F32)

            def finish(u, s):
                _, j, slot = items[u]
                acc_ref[slot] += jnp.dot(jnp.exp2(s).astype(_BF16),
                                         va_ref[0, hh, pl.ds(pl.multiple_of(j * tq, tq), tq), :],
                                         preferred_element_type=_F32)

            interleave(unroll, qk, finish)
            return carry

        lax.fori_loop(0, n_trips, trip, 0)

        def diag_trip(t, carry, hh=hh):
            def rows(u):
                i, bottom = _DIAG_PER_TRIP * t + u // 2, u % 2
                return i, bottom, pl.multiple_of(i * tq + bottom * half, half)

            def qk(u):
                i, bottom, r0 = rows(u)
                k0 = pl.multiple_of(i * tq, tq)
                width = tq if bottom else half
                return lax.dot_general(qa_ref[0, hh, pl.ds(r0, half), :],
                                       ka_ref[0, hh, pl.ds(k0, width), :], nt, preferred_element_type=_F32)

            def finish(u, s):
                i, bottom, r0 = rows(u)
                k0 = pl.multiple_of(i * tq, tq)
                width = tq if bottom else half
                live = (diff_bot <= half) if bottom else (diff_top <= 0)
                p = jnp.exp2(jnp.where(live, s, _NEG_INF)).astype(_BF16)
                acc = acc_ref[i, bottom * half:(bottom + 1) * half, :] + jnp.dot(
                    p, va_ref[0, hh, pl.ds(k0, width), :], preferred_element_type=_F32)
                o = acc / acc[:, _AUG_ONE:_AUG_ONE + 1]
                if hh % 2 == 1:
                    o = pltpu.roll(o, _HEAD_DIM, 1)
                sl = slice(hh * _HEAD_DIM, (hh + 1) * _HEAD_DIM)
                o_ref[0, pl.ds(r0, half), sl] = o[:, sl].astype(_BF16)

            interleave(2 * _DIAG_PER_TRIP, qk, finish)
            return carry

        lax.fori_loop(0, nq // _DIAG_PER_TRIP, diag_trip, 0)


def _fox_online_body(qa_ref, ka_ref, va_ref, o_ref, *scratch, tq):
    qi = pl.program_id(2)
    nt = (((1,), (1,)), ((), ()))
    n_heads = qa_ref.shape[1]
    m_refs, acc_refs = scratch[:n_heads], scratch[n_heads:]
    for hh in range(n_heads):
        m_refs[hh][...] = jnp.full_like(m_refs[hh], _NEG_INF)
        acc_refs[hh][...] = jnp.zeros_like(acc_refs[hh])

    def step(j, masked):
        start = pl.multiple_of(j * tq, tq)
        logits = [lax.dot_general(qa_ref[0, hh], ka_ref[0, hh, pl.ds(start, tq), :], nt,
                                  preferred_element_type=_F32) for hh in range(n_heads)]
        for hh in range(n_heads):
            m_ref, acc_ref = m_refs[hh], acc_refs[hh]
            s = logits[hh]
            if masked:
                s = jnp.where(_causal_mask(tq), s, _NEG_INF)
            m_old = m_ref[...]
            m_new = jnp.maximum(m_old, jnp.max(s, axis=-1, keepdims=True))
            p = jnp.exp2(s - m_new[:, 0:1])
            acc_ref[...] = jnp.exp2(m_old - m_new) * acc_ref[...] + jnp.dot(
                p.astype(_BF16), va_ref[0, hh, pl.ds(start, tq), :], preferred_element_type=_F32)
            m_ref[...] = m_new

    def off_diag(j, carry):
        step(j, False)
        return carry

    lax.fori_loop(0, qi, off_diag, 0)
    step(qi, True)
    _fox_finish(acc_refs, o_ref, tq)


def _fox_shifted(jlo, qa, ka, va, *, tq, unroll):
    B, H, T, L = qa.shape
    nq = T // tq
    assert nq % _DIAG_PER_TRIP == 0 and tq % (2 * _MXU_TILE) == 0
    max_items = nq * (nq - 1) // 2 + unroll
    blk = pl.BlockSpec((1, 2, T, L), lambda b, p, jlo_ref: (b, p, 0, 0))
    return pl.pallas_call(
        functools.partial(_fox_shifted_body, tq=tq, nq=nq, unroll=unroll),
        grid_spec=pltpu.PrefetchScalarGridSpec(
            num_scalar_prefetch=1,
            grid=(B, H // 2),
            in_specs=[blk, blk, blk],
            out_specs=pl.BlockSpec((1, T, L), lambda b, p, jlo_ref: (b, 0, p)),
            scratch_shapes=[pltpu.VMEM((nq + 1, tq, L), _F32)] + [pltpu.SMEM((max_items,), jnp.int32)] * 3,
        ),
        out_shape=jax.ShapeDtypeStruct((B, T, H * _HEAD_DIM), _BF16),
        compiler_params=_params(("arbitrary", "arbitrary")),
        name="fox_shifted",
    )(jlo, qa, ka, va)


def _fox_online(qa, ka, va, *, tq, hg):
    B, H, T, L = qa.shape
    return pl.pallas_call(
        functools.partial(_fox_online_body, tq=tq),
        grid=(B, H // hg, T // tq),
        in_specs=[
            pl.BlockSpec((1, hg, tq, L), lambda b, p, i: (b, p, i, 0)),
            pl.BlockSpec((1, hg, T, L), lambda b, p, i: (b, p, 0, 0)),
            pl.BlockSpec((1, hg, T, L), lambda b, p, i: (b, p, 0, 0)),
        ],
        out_specs=pl.BlockSpec((1, tq, hg * _HEAD_DIM), lambda b, p, i: (b, i, p)),
        out_shape=jax.ShapeDtypeStruct((B, T, H * _HEAD_DIM), _BF16),
        scratch_shapes=[pltpu.VMEM((tq, L), _F32)] * (2 * hg),
        compiler_params=_params(("arbitrary", "arbitrary", "arbitrary")),
        name="fox_online",
    )(qa, ka, va)


def _first_live_block(fend):
    B, nq, H = fend.shape
    f = jnp.transpose(fend, (0, 2, 1))
    top = jnp.concatenate([jnp.zeros((B, H, 1), _F32), f[:, :, :-1]], axis=-1)
    dead = (top[:, :, :, None] - f[:, :, None, :]) <= _UNDERFLOW_LOG2
    j = jnp.arange(nq, dtype=jnp.int32)
    before = j[None, :] < j[:, None]
    first_live = jnp.min(jnp.where(dead & before, nq, j), axis=-1)
    return first_live.astype(jnp.int32).reshape(-1)


def _mem_kv_body(mem_ref, gm_ref, w32_ref, gk_ref, k_ref, v_ref, w_ref, *, d_model, xd):
    @pl.when(pl.program_id(0) == 0)
    def _():
        w_ref[...] = w32_ref[...].astype(_BF16)

    mn = _rms(mem_ref[0], gm_ref[...]).astype(_BF16)
    kv = jnp.dot(mn, w_ref[...], preferred_element_type=_F32)
    for h in range(_N_XHEADS):
        sl = slice(h * xd, (h + 1) * xd)
        k_ref[0, :, sl] = (_rms(kv[:, sl], gk_ref[...]) * (xd ** -0.5)).astype(_BF16)
    v_ref[0] = kv[:, d_model:].astype(_BF16)


def _mem_kv(mem, g_mem, w_xkv, g_xk):
    B, M, D = mem.shape
    xd = D // _N_XHEADS
    blk = pl.BlockSpec((1, M, D), lambda b: (b, 0, 0))
    out = jax.ShapeDtypeStruct((B, M, D), _BF16)
    return pl.pallas_call(
        functools.partial(_mem_kv_body, d_model=D, xd=xd),
        grid=(B,),
        in_specs=[blk, _const_spec((1, D)), _const_spec(w_xkv.shape), _const_spec((1, xd))],
        out_specs=[blk, blk],
        out_shape=[out, out],
        scratch_shapes=[pltpu.VMEM(w_xkv.shape, _BF16)],
        compiler_params=_params(("arbitrary",)),
        name="mem_kv",
    )(mem, g_mem, w_xkv, g_xk)


def _mix_xattn_body(x_ref, ret_ref, fox_ref, wo32_ref, gx_ref, wq32_ref, gq_ref, k_ref, v_ref, wxo32_ref,
                    o_ref, wo_ref, wq_ref, wxo_ref, *, width, xd):
    @pl.when((pl.program_id(0) == 0) & (pl.program_id(1) == 0))
    def _():
        wo_ref[...] = wo32_ref[...].astype(_BF16)
        wq_ref[...] = wq32_ref[...].astype(_BF16)
        wxo_ref[...] = wxo32_ref[...].astype(_BF16)

    h1 = (x_ref[0]
          + jnp.dot(ret_ref[0], wo_ref[:width, :], preferred_element_type=_F32)
          + jnp.dot(fox_ref[0], wo_ref[width:, :], preferred_element_type=_F32))
    hn, inv_h = _rms_split(h1, gx_ref[...])
    y = jnp.dot(hn, wq_ref[...], preferred_element_type=_F32)
    nt = (((1,), (1,)), ((), ()))
    head_slices = [slice(h * xd, (h + 1) * xd) for h in range(_N_XHEADS)]
    logits = []
    for sl in head_slices:
        yh = y[:, sl]
        inv_q = lax.rsqrt(inv_h * inv_h * jnp.mean(yh * yh, axis=-1, keepdims=True) + _EPS)
        logits.append(lax.dot_general((yh * gq_ref[...]).astype(_BF16), k_ref[0, :, sl], nt,
                                      preferred_element_type=_F32) * (inv_q * inv_h * _LOG2E))
    probs = []
    for lg in logits:
        p = jnp.exp2(lg - jnp.max(lg, axis=-1, keepdims=True))
        probs.append((p / jnp.sum(p, axis=-1, keepdims=True)).astype(_BF16))
    heads = [jnp.dot(p, v_ref[0, :, sl], preferred_element_type=_F32).astype(_BF16)
             for p, sl in zip(probs, head_slices)]
    o = jnp.concatenate(heads, axis=-1)
    o_ref[0] = h1 + jnp.dot(o, wxo_ref[...], preferred_element_type=_F32)


def _mix_xattn(x, ret, fox, w_out, g_xattn, w_xq, g_xq, k, v, w_xo, *, tm):
    B, T, D = x.shape
    W = ret.shape[-1]
    M = k.shape[1]
    xd = D // _N_XHEADS
    tok = lambda b, i: (b, i, 0)
    return pl.pallas_call(
        functools.partial(_mix_xattn_body, width=W, xd=xd),
        grid=(B, T // tm),
        in_specs=[
            pl.BlockSpec((1, tm, D), tok),
            pl.BlockSpec((1, tm, W), tok),
            pl.BlockSpec((1, tm, W), tok),
            _const_spec(w_out.shape),
            _const_spec((1, D)),
            _const_spec(w_xq.shape),
            _const_spec((1, xd)),
            pl.BlockSpec((1, M, D), lambda b, i: (b, 0, 0)),
            pl.BlockSpec((1, M, D), lambda b, i: (b, 0, 0)),
            _const_spec(w_xo.shape),
        ],
        out_specs=pl.BlockSpec((1, tm, D), tok),
        out_shape=jax.ShapeDtypeStruct((B, T, D), _F32),
        scratch_shapes=[pltpu.VMEM(w.shape, _BF16) for w in (w_out, w_xq, w_xo)],
        compiler_params=_params(("arbitrary", "arbitrary")),
        name="mix_xattn",
    )(x, ret, fox, w_out, g_xattn, w_xq, g_xq, k, v, w_xo)


def _ffn_chunks(d_ff, n_chunks):
    tiles = -(-d_ff // _MXU_TILE)
    bounds = [min(d_ff, _MXU_TILE * ((tiles * c) // n_chunks)) for c in range(n_chunks)] + [d_ff]
    return [(lo, hi) for lo, hi in zip(bounds[:-1], bounds[1:]) if hi > lo]


def _ffn_body(h_ref, g_ref, wg32_ref, wu32_ref, wd32_ref, o_ref, wg_ref, wu_ref, wd_ref,
              *, n_cast, ck, chunks):
    step = pl.program_id(0)
    for c in range(n_cast):
        @pl.when(step == c)
        def _(c=c):
            wg_ref[:, c * ck:(c + 1) * ck] = wg32_ref[...].astype(_BF16)
            wu_ref[:, c * ck:(c + 1) * ck] = wu32_ref[...].astype(_BF16)
            wd_ref[c * ck:(c + 1) * ck, :] = wd32_ref[...].astype(_BF16)

    @pl.when(step >= n_cast)
    def _():
        h = h_ref[0]
        hn, inv_rms = _rms_split(h, g_ref[...])
        acc = h
        for lo, hi in chunks:
            sl = slice(lo, hi)
            gate = jnp.dot(hn, wg_ref[:, sl], preferred_element_type=_F32) * inv_rms
            up = jnp.dot(hn, wu_ref[:, sl], preferred_element_type=_F32) * inv_rms
            a = (gate * jax.nn.sigmoid(gate) * up).astype(_BF16)
            acc = acc + jnp.dot(a, wd_ref[sl, :], preferred_element_type=_F32)
        o_ref[0] = acc


def _ffn(h, g_ffn, w_gate, w_up, w_down, *, tm, n_split):
    B, T, D = h.shape
    F = w_gate.shape[1]
    ck = _MXU_TILE
    assert F % ck == 0 and T % tm == 0
    n_cast = F // ck
    n_tiles = B * T // tm
    tile = lambda s: (jnp.maximum(s - n_cast, 0), 0, 0)
    chunk = lambda s: jnp.minimum(s, n_cast - 1)
    tok = pl.BlockSpec((1, tm, D), tile)
    out = pl.pallas_call(
        functools.partial(_ffn_body, n_cast=n_cast, ck=ck, chunks=_ffn_chunks(F, n_split)),
        grid=(n_cast + n_tiles,),
        in_specs=[tok, _const_spec((1, D)),
                  pl.BlockSpec((D, ck), lambda s: (0, chunk(s))),
                  pl.BlockSpec((D, ck), lambda s: (0, chunk(s))),
                  pl.BlockSpec((ck, D), lambda s: (chunk(s), 0))],
        out_specs=tok,
        out_shape=jax.ShapeDtypeStruct((n_tiles, tm, D), _F32),
        scratch_shapes=[pltpu.VMEM((D, F), _BF16), pltpu.VMEM((D, F), _BF16), pltpu.VMEM((F, D), _BF16)],
        compiler_params=_params(("arbitrary",)),
        name="ffn",
    )(h.reshape(n_tiles, tm, D), g_ffn, w_gate, w_up, w_down)
    return out.reshape(B, T, D)


def _rope_tables(T):
    half = _HEAD_DIM // 2
    inv_freq = (_ROPE_BASE ** (-np.arange(0, _HEAD_DIM, 2, dtype=np.float32) / _HEAD_DIM)).astype(np.float32)
    ang = (np.arange(T, dtype=np.float32)[:, None] * inv_freq[None, :]).astype(np.float32).astype(np.float64)
    cos, sin = np.cos(ang), np.sin(ang)
    reps = _LANES // _HEAD_DIM
    cos_t = np.tile(np.concatenate([cos, cos], axis=1), (1, reps))
    sin_t = np.tile(np.concatenate([-sin, sin], axis=1), (1, reps))
    return jnp.asarray(cos_t, _F32), jnp.asarray(sin_t, _F32)


def _retention_tables(tb):
    log_g = np.log(1.0 - 2.0 ** (-5.0 - np.arange(_N_HEADS, dtype=np.float64)))
    idx = np.arange(tb, dtype=np.float64)
    dist = np.abs(idx[:, None] - idx[None, :])
    chunk = np.arange(tb) // _RET_CHUNK
    visible = chunk[None, :] <= chunk[:, None]
    dmask = np.where(visible[None], np.exp(log_g[:, None, None] * dist[None]), 0.0)
    qdec = np.repeat(np.exp(log_g[None, :] * (idx[:, None] + 1.0)), _HEAD_DIM, axis=1)
    kdec = np.repeat(np.exp(log_g[None, :] * (tb - 1.0 - idx[:, None])), _HEAD_DIM, axis=1)
    heads_per_group = _RET_GROUP // _HEAD_DIM
    head_of = np.arange(_RET_GROUP) // _HEAD_DIM
    bd = (head_of[:, None] == head_of[None, :]).astype(np.float64)
    step_decay = np.exp(log_g * tb).reshape(-1, heads_per_group)
    sdec = bd[None] * np.repeat(step_decay, _HEAD_DIM, axis=1)[:, None, :]
    f = lambda a: jnp.asarray(a, _F32)
    return f(dmask), f(qdec), f(kdec), f(sdec), f(bd), jnp.asarray(bd / _HEAD_DIM, _BF16)


def _pad_lanes(a):
    return jnp.pad(a, [(0, 0)] * (a.ndim - 1) + [(0, _LANES - a.shape[-1])])


def kernel(x, mem, g_mix, w_in, b_forget, g_ret_out, g_fox_q, g_fox_k, w_out, g_xattn, w_xq, w_xkv,
           g_mem, g_xq, g_xk, w_xo, g_ffn, w_gate, w_up, w_down):
    B, T, D = x.shape
    width = _N_HEADS * _HEAD_DIM
    tb, tq, tm_mix, tm_ffn = 256, 512, 512, 512
    cos_t, sin_t = _rope_tables(T)
    ret_tables = _retention_tables(tb)
    tri = jnp.asarray(np.tril(np.ones((tq, tq), np.float32)), _BF16)
    row = lambda a: a.reshape(1, -1).astype(_F32)

    h = x
    for l in range(w_in.shape[0]):
        b_ff = _pad_lanes(row(b_forget[l]))
        gq = jnp.tile(row(g_fox_q[l]), (1, _N_HEADS)) * (_LOG2E * _HEAD_DIM ** -0.5)
        gk = jnp.tile(row(g_fox_k[l]), (1, _N_HEADS))
        bound = (_HEAD_DIM ** 0.5 * _NORM_ROUNDING_SLACK) * jnp.max(jnp.abs(g_fox_q[l])) * jnp.max(
            jnp.abs(g_fox_k[l]))
        use_shift = bound <= _MAX_FIXED_SHIFT
        shift = jnp.where(use_shift, jnp.ceil(bound * (4.0 * _LOG2E)) * 0.25, 0.0).astype(_F32)
        rq, rk, rv, gate, qa, ka, va, fend = _in_proj(h, row(g_mix[l]), w_in[l].T, b_ff, cos_t, sin_t,
                                                      gq, gk, jnp.full((1, _LANES), shift), tri,
                                                      ret_tables[-1], tm=tq)
        ret = _retention(rq, rk, rv, gate, *ret_tables, row(g_ret_out[l]), tb=tb)
        jlo = _first_live_block(fend[:, :, 0, :_N_HEADS])
        fox = lax.cond(use_shift,
                       lambda jlo, qa, ka, va: _fox_shifted(jlo, qa, ka, va, tq=tq, unroll=4),
                       lambda jlo, qa, ka, va: _fox_online(qa, ka, va, tq=tq, hg=4), jlo, qa, ka, va)
        k, v = _mem_kv(mem, row(g_mem[l]), w_xkv[l], row(g_xk[l]))
        h = _mix_xattn(h, ret, fox, w_out[l], row(g_xattn[l]), w_xq[l], row(g_xq[l]), k, v, w_xo[l],
                       tm=tm_mix)
        h = _ffn(h, row(g_ffn[l]), w_gate[l], w_up[l], w_down[l], tm=tm_ffn, n_split=2)
    return h
```

```python
import functools

import numpy as np
import jax
import jax.numpy as jnp
from jax import lax
from jax.experimental import pallas as pl
from jax.experimental.pallas import tpu as pltpu

_BF16 = jnp.bfloat16
_F32 = jnp.float32

_EPS = 1e-6
_NEG_INF = -1e30
_ROPE_BASE = 10000.0
_HEAD_DIM = 64
_N_HEADS = 8
_RET_CHUNK = 64
_RET_GROUP = 256
_N_XHEADS = 4
_LANES = 128
_MXU_TILE = 256
_VMEM_LIMIT = 56 * 1024 * 1024

_AUG_QF = 64
_AUG_KF = 88
_AUG_SHIFT = 112
_AUG_ONE = 64

_LOG2E = 1.4426950408889634
_MAX_FIXED_SHIFT = 32.0
_DIAG_PER_TRIP = 4
_UNDERFLOW_LOG2 = -152.0
_NORM_ROUNDING_SLACK = 1.01


def _params(sem):
    return pltpu.CompilerParams(dimension_semantics=sem, vmem_limit_bytes=_VMEM_LIMIT)


def _const_spec(shape):
    nd = len(shape)
    return pl.BlockSpec(shape, lambda *_: (0,) * nd, pipeline_mode=pl.Buffered(1))


def _rms(x, g):
    return x * lax.rsqrt(jnp.mean(x * x, axis=-1, keepdims=True) + _EPS) * g


def _rms_split(x, g):
    return (x * g).astype(_BF16), lax.rsqrt(jnp.mean(x * x, axis=-1, keepdims=True) + _EPS)


def _split3(v):
    hi = v.astype(_BF16).astype(_F32)
    r = v - hi
    mid = r.astype(_BF16).astype(_F32)
    return hi, mid, r - mid


def _in_proj_body(x_ref, g_ref, w32_ref, bf_ref, cos_ref, sin_ref, gq_ref, gk_ref, shift_ref,
                  tri_ref, avg_ref, dmask_ref, qdec_ref, kdec_ref, sdec_ref, bd_ref, gret_ref,
                  ret_ref, qa_ref, ka_ref, va_ref, fend_ref,
                  w_ref, wff_ref, carry_ref, rq_ref, rk_ref, rv_ref, gate_ref, state_ref, *, tm, tb, width):
    i = pl.program_id(1)
    n_main = 7 * width

    @pl.when((pl.program_id(0) == 0) & (i == 0))
    def _():
        for j in range(n_main // width):
            w_ref[:, j * width:(j + 1) * width] = jnp.transpose(
                w32_ref[j * width:(j + 1) * width, :]).astype(_BF16)
        tail = jnp.concatenate([w32_ref[n_main:, :],
                                jnp.zeros((_LANES - _N_HEADS, w32_ref.shape[1]), _F32)], axis=0)
        wff_ref[...] = jnp.transpose(tail).astype(_BF16)

    hb, inv_rms = _rms_split(x_ref[0], g_ref[...])

    def proj(j):
        return jnp.dot(hb, w_ref[:, j * width:(j + 1) * width], preferred_element_type=_F32) * inv_rms

    lane = lax.broadcasted_iota(jnp.int32, (tm, _LANES), 1)
    low = lane < _HEAD_DIM
    first_half = (lane & (_HEAD_DIM // 2)) == 0
    n_pairs = width // _LANES

    cos = cos_ref[...]
    sin = sin_ref[...]
    for j, out_ref, scale in ((0, rq_ref, _HEAD_DIM ** -0.5), (1, rk_ref, None)):
        y = proj(j)
        for c in range(n_pairs):
            blk = y[:, c * _LANES:(c + 1) * _LANES]
            swapped = jnp.where(first_half, pltpu.roll(blk, _LANES - _HEAD_DIM // 2, 1),
                                pltpu.roll(blk, _HEAD_DIM // 2, 1))
            r = blk * cos + swapped * sin
            if scale is not None:
                r = r * scale
            out_ref[:, c * _LANES:(c + 1) * _LANES] = r.astype(_BF16)
    rv_ref[...] = proj(2).astype(_BF16)
    gate = proj(3)
    gate_ref[...] = (gate * jax.nn.sigmoid(gate)).astype(_BF16)

    @pl.when(i == 0)
    def _():
        state_ref[...] = jnp.zeros_like(state_ref)

    for r0 in range(0, tm, tb):
        _retention_block(rq_ref, rk_ref, rv_ref, gate_ref, slice(r0, r0 + tb), dmask_ref, qdec_ref, kdec_ref,
                         sdec_ref, bd_ref, avg_ref, gret_ref, state_ref, ret_ref)

    half = tm // 2
    z = jnp.concatenate([jnp.dot(hb[:half], wff_ref[...], preferred_element_type=_F32),
                         jnp.dot(hb[half:], wff_ref[...], preferred_element_type=_F32)],
                        axis=0) * inv_rms + bf_ref[...]
    logf = (jnp.minimum(z, 0.0) - jnp.log(1.0 + jnp.exp(-jnp.abs(z)))) * _LOG2E
    hi, mid, lo = _split3(logf)
    tri = tri_ref[...]
    split = jnp.concatenate([hi, mid, lo], axis=1).astype(_BF16)
    parts = jnp.concatenate(
        [jnp.dot(tri[:half, :half], split[:half], preferred_element_type=_F32),
         jnp.dot(tri[half:], split, preferred_element_type=_F32)], axis=0)
    csum = parts[:, :_LANES] + parts[:, _LANES:2 * _LANES] + parts[:, 2 * _LANES:]

    @pl.when(i == 0)
    def _():
        carry_ref[...] = jnp.zeros_like(carry_ref)

    fcum = csum + carry_ref[0:1, :]
    carry_ref[...] = jnp.broadcast_to(fcum[tm - 1:tm, :], carry_ref.shape)
    fend_ref[0, 0] = carry_ref[...]

    def head_rms(y, g_row):
        out = []
        for g0 in range(0, width, _RET_GROUP):
            blk = y[:, g0:g0 + _RET_GROUP]
            ms = jnp.dot((blk * blk).astype(_BF16), avg_ref[...], preferred_element_type=_F32)
            out.append(blk * lax.rsqrt(ms + _EPS) * g_row[:, g0:g0 + _RET_GROUP])
        return out

    qn = head_rms(proj(4), gq_ref[...])
    kn = head_rms(proj(5), gk_ref[...])
    fv = proj(6)

    fh, fm, fl = _split3(fcum)
    n_parts = 3 * _N_HEADS
    packed = jnp.where(lane < _N_HEADS, fh, jnp.where(
        lane < 2 * _N_HEADS, pltpu.roll(fm, _N_HEADS, 1), pltpu.roll(fl, 2 * _N_HEADS, 1)))
    in_qf = (lane >= _AUG_QF) & (lane < _AUG_QF + n_parts)
    in_kf = (lane >= _AUG_KF) & (lane < _AUG_KF + n_parts)
    shared = jnp.where(in_qf, pltpu.roll(packed, _AUG_QF, 1), -pltpu.roll(packed, _AUG_KF, 1))
    q_bias = jnp.where(in_qf, shared, jnp.where(in_kf, 1.0, jnp.where(
        lane == _AUG_SHIFT, -shift_ref[...], 0.0)))
    v_const = jnp.where(lane == _AUG_ONE, 1.0, 0.0)
    for h in range(_N_HEADS):
        own = (lane & (_N_HEADS - 1)) == h
        k_bias = jnp.where(in_kf & own, shared,
                           jnp.where((in_qf & own) | (lane == _AUG_SHIFT), 1.0, 0.0))
        g0, c0 = divmod(h * _HEAD_DIM, _RET_GROUP)
        c0 = (c0 // _LANES) * _LANES
        qb = qn[g0][:, c0:c0 + _LANES]
        kb = kn[g0][:, c0:c0 + _LANES]
        vb = fv[:, (h // 2) * _LANES:(h // 2 + 1) * _LANES]
        if h % 2 == 1:
            qb = pltpu.roll(qb, _HEAD_DIM, 1)
            kb = pltpu.roll(kb, _HEAD_DIM, 1)
            vb = pltpu.roll(vb, _HEAD_DIM, 1)
        qa_ref[0, h] = jnp.where(low, qb, q_bias).astype(_BF16)
        ka_ref[0, h] = jnp.where(low, kb, k_bias).astype(_BF16)
        va_ref[0, h] = jnp.where(low, vb, v_const).astype(_BF16)


def _in_proj(x, g_mix, w_in_t, b_ff, cos_t, sin_t, gq, gk, shift, tri, ret_tables, g_ret, *, tm, tb):
    B, T, D = x.shape
    width = _N_HEADS * _HEAD_DIM
    assert w_in_t.shape == (7 * width + _N_HEADS, D) and tm % tb == 0
    dmask, qdec, kdec, sdec, bd, avg = ret_tables
    tok = lambda b, i: (b, i, 0)
    head = lambda b, i: (b, 0, i, 0)
    bf_tok = jax.ShapeDtypeStruct((B, T, width), _BF16)
    bf_head = jax.ShapeDtypeStruct((B, _N_HEADS, T, _LANES), _BF16)
    return pl.pallas_call(
        functools.partial(_in_proj_body, tm=tm, tb=tb, width=width),
        grid=(B, T // tm),
        in_specs=[
            pl.BlockSpec((1, tm, D), tok),
            _const_spec((1, D)),
            _const_spec(w_in_t.shape),
            _const_spec((1, _LANES)),
            pl.BlockSpec((tm, _LANES), lambda b, i: (i, 0)),
            pl.BlockSpec((tm, _LANES), lambda b, i: (i, 0)),
            _const_spec((1, width)),
            _const_spec((1, width)),
            _const_spec((1, _LANES)),
            _const_spec((tm, tm)),
            _const_spec(avg.shape),
            _const_spec(dmask.shape),
            _const_spec(qdec.shape),
            _const_spec(kdec.shape),
            _const_spec(sdec.shape),
            _const_spec(bd.shape),
            _const_spec(g_ret.shape),
        ],
        out_specs=[pl.BlockSpec((1, tm, width), tok)]
        + [pl.BlockSpec((1, _N_HEADS, tm, _LANES), head)] * 3
        + [pl.BlockSpec((1, 1, 8, _LANES), lambda b, i: (b, i, 0, 0))],
        out_shape=[bf_tok] + [bf_head] * 3 + [jax.ShapeDtypeStruct((B, T // tm, 8, _LANES), _F32)],
        scratch_shapes=[pltpu.VMEM((D, 7 * width), _BF16), pltpu.VMEM((D, _LANES), _BF16),
                        pltpu.VMEM((8, _LANES), _F32)]
        + [pltpu.VMEM((tm, width), _BF16)] * 4 + [pltpu.VMEM(sdec.shape, _F32)],
        compiler_params=_params(("arbitrary", "arbitrary")),
        name="in_proj",
    )(x, g_mix, w_in_t, b_ff, cos_t, sin_t, gq, gk, shift, tri, avg, dmask, qdec, kdec, sdec, bd, g_ret)


def _retention_block(rq_ref, rk_ref, rv_ref, gate_ref, rows, dmask_ref, qdec_ref, kdec_ref, sdec_ref, bd_ref,
                     avg_ref, g_ref, state_ref, o_ref):
    tb = rows.stop - rows.start
    nt = (((1,), (1,)), ((), ()))
    tn = (((0,), (0,)), ((), ()))
    gw = state_ref.shape[1]
    lane = lax.broadcasted_iota(jnp.int32, (1, _LANES), 1)
    low = lax.broadcasted_iota(jnp.int32, (tb, _LANES), 1) < _HEAD_DIM
    head_lanes = [jnp.where(lane < _HEAD_DIM, 1.0, 0.0).astype(_BF16),
                  jnp.where(lane < _HEAD_DIM, 0.0, 1.0).astype(_BF16)]
    n_groups = state_ref.shape[0]
    groups = [slice(g * gw, (g + 1) * gw) for g in range(n_groups)]
    scores = []
    for h in range(_N_HEADS):
        ps = slice((h // 2) * _LANES, (h // 2 + 1) * _LANES)
        scores.append(lax.dot_general(rq_ref[rows, ps] * head_lanes[h % 2], rk_ref[rows, ps], nt,
                                      preferred_element_type=_F32))
    inter = []
    for g, gs in enumerate(groups):
        state = state_ref[g]
        inter.append(jnp.dot(rq_ref[rows, gs], state.astype(_BF16),
                             preferred_element_type=_F32) * qdec_ref[:, gs])
        kd = (rk_ref[rows, gs].astype(_F32) * kdec_ref[:, gs]).astype(_BF16)
        state_ref[g] = state * sdec_ref[g] + lax.dot_general(
            kd, rv_ref[rows, gs], tn, preferred_element_type=_F32) * bd_ref[...]
    intra = []
    for h in range(_N_HEADS):
        ps = slice((h // 2) * _LANES, (h // 2 + 1) * _LANES)
        intra.append(jnp.dot((scores[h] * dmask_ref[h]).astype(_BF16), rv_ref[rows, ps],
                             preferred_element_type=_F32))
    outs = []
    for g, gs in enumerate(groups):
        pairs = [jnp.where(low, intra[2 * c], intra[2 * c + 1])
                 for c in range(g * gw // _LANES, (g + 1) * gw // _LANES)]
        outs.append(jnp.concatenate(pairs, axis=1) + inter[g])
    mus = [jnp.dot(o.astype(_BF16), avg_ref[...], preferred_element_type=_F32) for o in outs]
    cents = [o - mu for o, mu in zip(outs, mus)]
    vars_ = [jnp.dot((oc * oc).astype(_BF16), avg_ref[...], preferred_element_type=_F32) for oc in cents]
    for gs, oc, var in zip(groups, cents, vars_):
        y = oc * lax.rsqrt(var + _EPS) * g_ref[:, gs]
        o_ref[0, rows, gs] = (y * gate_ref[rows, gs].astype(_F32)).astype(_BF16)


def _causal_mask(tq):
    row = lax.broadcasted_iota(jnp.int32, (tq, tq), 0)
    col = lax.broadcasted_iota(jnp.int32, (tq, tq), 1)
    return row >= col


def _fox_finish(acc_refs, o_ref, tq):
    lane = lax.broadcasted_iota(jnp.int32, (tq, _LANES), 1)
    for c in range(len(acc_refs) // 2):
        pair = []
        for hh in (2 * c, 2 * c + 1):
            acc = acc_refs[hh][...]
            pair.append(acc / acc[:, _AUG_ONE:_AUG_ONE + 1])
        o_ref[0, :, c * _LANES:(c + 1) * _LANES] = jnp.where(
            lane < _HEAD_DIM, pair[0], pltpu.roll(pair[1], _HEAD_DIM, 1)).astype(_BF16)


def _fox_shifted_body(jlo_ref, qa_ref, ka_ref, va_ref, o_ref, acc_ref, it_i, it_j, it_slot,
                      *, tq, nq, unroll):
    b, pair = pl.program_id(0), pl.program_id(1)
    n_pair = qa_ref.shape[1]
    nt = (((1,), (1,)), ((), ()))
    half = tq // 2
    diff_top = (lax.broadcasted_iota(jnp.int32, (half, half), 1)
                - lax.broadcasted_iota(jnp.int32, (half, half), 0))
    diff_bot = (lax.broadcasted_iota(jnp.int32, (half, tq), 1)
                - lax.broadcasted_iota(jnp.int32, (half, tq), 0))

    def interleave(n_chains, qk, finish):
        s_prev = qk(0)
        for u in range(1, n_chains):
            s_next = qk(u)
            finish(u - 1, s_prev)
            s_prev = s_next
        finish(n_chains - 1, s_prev)

    for hh in range(n_pair):
        base = (b * (pl.num_programs(1) * n_pair) + pair * n_pair + hh) * nq
        acc_ref[...] = jnp.zeros_like(acc_ref)

        n = jnp.int32(0)
        for i in range(1, nq):
            def add(j, n, i=i):
                it_i[n] = jnp.int32(i)
                it_j[n] = j
                it_slot[n] = jnp.int32(i)
                return n + 1

            n = lax.fori_loop(jlo_ref[base + i], i, add, n)
        n_trips = (n + (unroll - 1)) // unroll

        def pad(m, carry):
            it_i[m] = jnp.int32(0)
            it_j[m] = jnp.int32(0)
            it_slot[m] = jnp.int32(nq)
            return carry

        lax.fori_loop(n, n_trips * unroll, pad, 0)

        def trip(t, carry, hh=hh):
            items = [(it_i[t * unroll + u], it_j[t * unroll + u], it_slot[t * unroll + u])
                     for u in range(unroll)]

            def qk(u):
                i, j, _ = items[u]
                return lax.dot_general(qa_ref[0, hh, pl.ds(pl.multiple_of(i * tq, tq), tq), :],
                                       ka_ref[0, hh, pl.ds(pl.multiple_of(j * tq, tq), tq), :], nt,
                                       preferred_element_type=_F32)

            def finish(u, s):
                _, j, slot = items[u]
                acc_ref[slot] += jnp.dot(jnp.exp2(s).astype(_BF16),
                                         va_ref[0, hh, pl.ds(pl.multiple_of(j * tq, tq), tq), :],
                                         preferred_element_type=_F32)

            interleave(unroll, qk, finish)
            return carry

        lax.fori_loop(0, n_trips, trip, 0)

        def diag_trip(t, carry, hh=hh):
            def rows(u):
                i, bottom = _DIAG_PER_TRIP * t + u // 2, u % 2
                return i, bottom, pl.multiple_of(i * tq + bottom * half, half)

            def qk(u):
                i, bottom, r0 = rows(u)
                k0 = pl.multiple_of(i * tq, tq)
                width = tq if bottom else half
                return lax.dot_general(qa_ref[0, hh, pl.ds(r0, half), :],
                                       ka_ref[0, hh, pl.ds(k0, width), :], nt, preferred_element_type=_F32)

            def finish(u, s):
                i, bottom, r0 = rows(u)
                k0 = pl.multiple_of(i * tq, tq)
                width = tq if bottom else half
                live = (diff_bot <= half) if bottom else (diff_top <= 0)
                p = jnp.exp2(jnp.where(live, s, _NEG_INF)).astype(_BF16)
                acc = acc_ref[i, bottom * half:(bottom + 1) * half, :] + jnp.dot(
                    p, va_ref[0, hh, pl.ds(k0, width), :], preferred_element_type=_F32)
                o = acc / acc[:, _AUG_ONE:_AUG_ONE + 1]
                if hh % 2 == 1:
                    o = pltpu.roll(o, _HEAD_DIM, 1)
                sl = slice(hh * _HEAD_DIM, (hh + 1) * _HEAD_DIM)
                o_ref[0, pl.ds(r0, half), sl] = o[:, sl].astype(_BF16)

            interleave(2 * _DIAG_PER_TRIP, qk, finish)
            return carry

        lax.fori_loop(0, nq // _DIAG_PER_TRIP, diag_trip, 0)


def _fox_online_body(qa_ref, ka_ref, va_ref, o_ref, *scratch, tq):
    qi = pl.program_id(2)
    nt = (((1,), (1,)), ((), ()))
    n_heads = qa_ref.shape[1]
    m_refs, acc_refs = scratch[:n_heads], scratch[n_heads:]
    for hh in range(n_heads):
        m_refs[hh][...] = jnp.full_like(m_refs[hh], _NEG_INF)
        acc_refs[hh][...] = jnp.zeros_like(acc_refs[hh])

    def step(j, masked):
        start = pl.multiple_of(j * tq, tq)
        logits = [lax.dot_general(qa_ref[0, hh], ka_ref[0, hh, pl.ds(start, tq), :], nt,
                                  preferred_element_type=_F32) for hh in range(n_heads)]
        for hh in range(n_heads):
            m_ref, acc_ref = m_refs[hh], acc_refs[hh]
            s = logits[hh]
            if masked:
                s = jnp.where(_causal_mask(tq), s, _NEG_INF)
            m_old = m_ref[...]
            m_new = jnp.maximum(m_old, jnp.max(s, axis=-1, keepdims=True))
            p = jnp.exp2(s - m_new[:, 0:1])
            acc_ref[...] = jnp.exp2(m_old - m_new) * acc_ref[...] + jnp.dot(
                p.astype(_BF16), va_ref[0, hh, pl.ds(start, tq), :], preferred_element_type=_F32)
            m_ref[...] = m_new

    def off_diag(j, carry):
        step(j, False)
        return carry

    lax.fori_loop(0, qi, off_diag, 0)
    step(qi, True)
    _fox_finish(acc_refs, o_ref, tq)


def _fox_shifted(jlo, qa, ka, va, *, tq, unroll):
    B, H, T, L = qa.shape
    nq = T // tq
    assert nq % _DIAG_PER_TRIP == 0 and tq % (2 * _MXU_TILE) == 0
    max_items = nq * (nq - 1) // 2 + unroll
    blk = pl.BlockSpec((1, 2, T, L), lambda b, p, jlo_ref: (b, p, 0, 0))
    return pl.pallas_call(
        functools.partial(_fox_shifted_body, tq=tq, nq=nq, unroll=unroll),
        grid_spec=pltpu.PrefetchScalarGridSpec(
            num_scalar_prefetch=1,
            grid=(B, H // 2),
            in_specs=[blk, blk, blk],
            out_specs=pl.BlockSpec((1, T, L), lambda b, p, jlo_ref: (b, 0, p)),
            scratch_shapes=[pltpu.VMEM((nq + 1, tq, L), _F32)] + [pltpu.SMEM((max_items,), jnp.int32)] * 3,
        ),
        out_shape=jax.ShapeDtypeStruct((B, T, H * _HEAD_DIM), _BF16),
        compiler_params=_params(("arbitrary", "arbitrary")),
        name="fox_shifted",
    )(jlo, qa, ka, va)


def _fox_online(qa, ka, va, *, tq, hg):
    B, H, T, L = qa.shape
    return pl.pallas_call(
        functools.partial(_fox_online_body, tq=tq),
        grid=(B, H // hg, T // tq),
        in_specs=[
            pl.BlockSpec((1, hg, tq, L), lambda b, p, i: (b, p, i, 0)),
            pl.BlockSpec((1, hg, T, L), lambda b, p, i: (b, p, 0, 0)),
            pl.BlockSpec((1, hg, T, L), lambda b, p, i: (b, p, 0, 0)),
        ],
        out_specs=pl.BlockSpec((1, tq, hg * _HEAD_DIM), lambda b, p, i: (b, i, p)),
        out_shape=jax.ShapeDtypeStruct((B, T, H * _HEAD_DIM), _BF16),
        scratch_shapes=[pltpu.VMEM((tq, L), _F32)] * (2 * hg),
        compiler_params=_params(("arbitrary", "arbitrary", "arbitrary")),
        name="fox_online",
    )(qa, ka, va)


def _first_live_block(fend):
    B, nq, H = fend.shape
    f = jnp.transpose(fend, (0, 2, 1))
    top = jnp.concatenate([jnp.zeros((B, H, 1), _F32), f[:, :, :-1]], axis=-1)
    dead = (top[:, :, :, None] - f[:, :, None, :]) <= _UNDERFLOW_LOG2
    j = jnp.arange(nq, dtype=jnp.int32)
    before = j[None, :] < j[:, None]
    first_live = jnp.min(jnp.where(dead & before, nq, j), axis=-1)
    return first_live.astype(jnp.int32).reshape(-1)


def _mem_kv_body(mem_ref, gm_ref, w32_ref, gk_ref, k_ref, v_ref, w_ref, *, d_model, xd):
    @pl.when(pl.program_id(0) == 0)
    def _():
        w_ref[...] = w32_ref[...].astype(_BF16)

    mn = _rms(mem_ref[0], gm_ref[...]).astype(_BF16)
    kv = jnp.dot(mn, w_ref[...], preferred_element_type=_F32)
    for h in range(_N_XHEADS):
        sl = slice(h * xd, (h + 1) * xd)
        k_ref[0, :, sl] = (_rms(kv[:, sl], gk_ref[...]) * (xd ** -0.5)).astype(_BF16)
    v_ref[0] = kv[:, d_model:].astype(_BF16)


def _mem_kv(mem, g_mem, w_xkv, g_xk):
    B, M, D = mem.shape
    xd = D // _N_XHEADS
    blk = pl.BlockSpec((1, M, D), lambda b: (b, 0, 0))
    out = jax.ShapeDtypeStruct((B, M, D), _BF16)
    return pl.pallas_call(
        functools.partial(_mem_kv_body, d_model=D, xd=xd),
        grid=(B,),
        in_specs=[blk, _const_spec((1, D)), _const_spec(w_xkv.shape), _const_spec((1, xd))],
        out_specs=[blk, blk],
        out_shape=[out, out],
        scratch_shapes=[pltpu.VMEM(w_xkv.shape, _BF16)],
        compiler_params=_params(("arbitrary",)),
        name="mem_kv",
    )(mem, g_mem, w_xkv, g_xk)


def _mix_xattn_body(x_ref, ret_ref, fox_ref, wo32_ref, gx_ref, wq32_ref, gq_ref, k_ref, v_ref, wxo32_ref,
                    o_ref, wo_ref, wq_ref, wxo_ref, *, width, xd):
    @pl.when((pl.program_id(0) == 0) & (pl.program_id(1) == 0))
    def _():
        wo_ref[...] = wo32_ref[...].astype(_BF16)
        wq_ref[...] = wq32_ref[...].astype(_BF16)
        wxo_ref[...] = wxo32_ref[...].astype(_BF16)

    h1 = (x_ref[0]
          + jnp.dot(ret_ref[0], wo_ref[:width, :], preferred_element_type=_F32)
          + jnp.dot(fox_ref[0], wo_ref[width:, :], preferred_element_type=_F32))
    hn, inv_h = _rms_split(h1, gx_ref[...])
    y = jnp.dot(hn, wq_ref[...], preferred_element_type=_F32)
    nt = (((1,), (1,)), ((), ()))
    head_slices = [slice(h * xd, (h + 1) * xd) for h in range(_N_XHEADS)]
    logits = []
    for sl in head_slices:
        yh = y[:, sl]
        inv_q = lax.rsqrt(inv_h * inv_h * jnp.mean(yh * yh, axis=-1, keepdims=True) + _EPS)
        logits.append(lax.dot_general((yh * gq_ref[...]).astype(_BF16), k_ref[0, :, sl], nt,
                                      preferred_element_type=_F32) * (inv_q * inv_h * _LOG2E))
    probs = []
    for lg in logits:
        p = jnp.exp2(lg - jnp.max(lg, axis=-1, keepdims=True))
        probs.append((p / jnp.sum(p, axis=-1, keepdims=True)).astype(_BF16))
    heads = [jnp.dot(p, v_ref[0, :, sl], preferred_element_type=_F32).astype(_BF16)
             for p, sl in zip(probs, head_slices)]
    o = jnp.concatenate(heads, axis=-1)
    o_ref[0] = h1 + jnp.dot(o, wxo_ref[...], preferred_element_type=_F32)


def _mix_xattn(x, ret, fox, w_out, g_xattn, w_xq, g_xq, k, v, w_xo, *, tm):
    B, T, D = x.shape
    W = ret.shape[-1]
    M = k.shape[1]
    xd = D // _N_XHEADS
    tok = lambda b, i: (b, i, 0)
    return pl.pallas_call(
        functools.partial(_mix_xattn_body, width=W, xd=xd),
        grid=(B, T // tm),
        in_specs=[
            pl.BlockSpec((1, tm, D), tok),
            pl.BlockSpec((1, tm, W), tok),
            pl.BlockSpec((1, tm, W), tok),
            _const_spec(w_out.shape),
            _const_spec((1, D)),
            _const_spec(w_xq.shape),
            _const_spec((1, xd)),
            pl.BlockSpec((1, M, D), lambda b, i: (b, 0, 0)),
            pl.BlockSpec((1, M, D), lambda b, i: (b, 0, 0)),
            _const_spec(w_xo.shape),
        ],
        out_specs=pl.BlockSpec((1, tm, D), tok),
        out_shape=jax.ShapeDtypeStruct((B, T, D), _F32),
        scratch_shapes=[pltpu.VMEM(w.shape, _BF16) for w in (w_out, w_xq, w_xo)],
        compiler_params=_params(("arbitrary", "arbitrary")),
        name="mix_xattn",
    )(x, ret, fox, w_out, g_xattn, w_xq, g_xq, k, v, w_xo)


def _ffn_chunks(d_ff, n_chunks):
    tiles = -(-d_ff // _MXU_TILE)
    bounds = [min(d_ff, _MXU_TILE * ((tiles * c) // n_chunks)) for c in range(n_chunks)] + [d_ff]
    return [(lo, hi) for lo, hi in zip(bounds[:-1], bounds[1:]) if hi > lo]


def _ffn_body(h_ref, g_ref, wg32_ref, wu32_ref, wd32_ref, o_ref, wg_ref, wu_ref, wd_ref,
              *, n_cast, ck, chunks):
    step = pl.program_id(0)
    for c in range(n_cast):
        @pl.when(step == c)
        def _(c=c):
            wg_ref[:, c * ck:(c + 1) * ck] = wg32_ref[...].astype(_BF16)
            wu_ref[:, c * ck:(c + 1) * ck] = wu32_ref[...].astype(_BF16)
            wd_ref[c * ck:(c + 1) * ck, :] = wd32_ref[...].astype(_BF16)

    @pl.when(step >= n_cast)
    def _():
        h = h_ref[0]
        hn, inv_rms = _rms_split(h, g_ref[...])
        acc = h
        for lo, hi in chunks:
            sl = slice(lo, hi)
            gate = jnp.dot(hn, wg_ref[:, sl], preferred_element_type=_F32) * inv_rms
            up = jnp.dot(hn, wu_ref[:, sl], preferred_element_type=_F32) * inv_rms
            a = (gate * jax.nn.sigmoid(gate) * up).astype(_BF16)
            acc = acc + jnp.dot(a, wd_ref[sl, :], preferred_element_type=_F32)
        o_ref[0] = acc


def _ffn(h, g_ffn, w_gate, w_up, w_down, *, tm, n_split):
    B, T, D = h.shape
    F = w_gate.shape[1]
    ck = _MXU_TILE
    assert F % ck == 0 and T % tm == 0
    n_cast = F // ck
    n_tiles = B * T // tm
    tile = lambda s: (jnp.maximum(s - n_cast, 0), 0, 0)
    chunk = lambda s: jnp.minimum(s, n_cast - 1)
    tok = pl.BlockSpec((1, tm, D), tile)
    out = pl.pallas_call(
        functools.partial(_ffn_body, n_cast=n_cast, ck=ck, chunks=_ffn_chunks(F, n_split)),
        grid=(n_cast + n_tiles,),
        in_specs=[tok, _const_spec((1, D)),
                  pl.BlockSpec((D, ck), lambda s: (0, chunk(s))),
                  pl.BlockSpec((D, ck), lambda s: (0, chunk(s))),
                  pl.BlockSpec((ck, D), lambda s: (chunk(s), 0))],
        out_specs=tok,
        out_shape=jax.ShapeDtypeStruct((n_tiles, tm, D), _F32),
        scratch_shapes=[pltpu.VMEM((D, F), _BF16), pltpu.VMEM((D, F), _BF16), pltpu.VMEM((F, D), _BF16)],
        compiler_params=_params(("arbitrary",)),
        name="ffn",
    )(h.reshape(n_tiles, tm, D), g_ffn, w_gate, w_up, w_down)
    return out.reshape(B, T, D)


def _rope_tables(T):
    half = _HEAD_DIM // 2
    inv_freq = (_ROPE_BASE ** (-np.arange(0, _HEAD_DIM, 2, dtype=np.float32) / _HEAD_DIM)).astype(np.float32)
    ang = (np.arange(T, dtype=np.float32)[:, None] * inv_freq[None, :]).astype(np.float32).astype(np.float64)
    cos, sin = np.cos(ang), np.sin(ang)
    reps = _LANES // _HEAD_DIM
    cos_t = np.tile(np.concatenate([cos, cos], axis=1), (1, reps))
    sin_t = np.tile(np.concatenate([-sin, sin], axis=1), (1, reps))
    return jnp.asarray(cos_t, _F32), jnp.asarray(sin_t, _F32)


def _retention_tables(tb):
    log_g = np.log(1.0 - 2.0 ** (-5.0 - np.arange(_N_HEADS, dtype=np.float64)))
    idx = np.arange(tb, dtype=np.float64)
    dist = np.abs(idx[:, None] - idx[None, :])
    chunk = np.arange(tb) // _RET_CHUNK
    visible = chunk[None, :] <= chunk[:, None]
    dmask = np.where(visible[None], np.exp(log_g[:, None, None] * dist[None]), 0.0)
    qdec = np.repeat(np.exp(log_g[None, :] * (idx[:, None] + 1.0)), _HEAD_DIM, axis=1)
    kdec = np.repeat(np.exp(log_g[None, :] * (tb - 1.0 - idx[:, None])), _HEAD_DIM, axis=1)
    heads_per_group = _RET_GROUP // _HEAD_DIM
    head_of = np.arange(_RET_GROUP) // _HEAD_DIM
    bd = (head_of[:, None] == head_of[None, :]).astype(np.float64)
    step_decay = np.exp(log_g * tb).reshape(-1, heads_per_group)
    sdec = bd[None] * np.repeat(step_decay, _HEAD_DIM, axis=1)[:, None, :]
    f = lambda a: jnp.asarray(a, _F32)
    return f(dmask), f(qdec), f(kdec), f(sdec), f(bd), jnp.asarray(bd / _HEAD_DIM, _BF16)


def _pad_lanes(a):
    return jnp.pad(a, [(0, 0)] * (a.ndim - 1) + [(0, _LANES - a.shape[-1])])


def kernel(x, mem, g_mix, w_in, b_forget, g_ret_out, g_fox_q, g_fox_k, w_out, g_xattn, w_xq, w_xkv,
           g_mem, g_xq, g_xk, w_xo, g_ffn, w_gate, w_up, w_down):
    B, T, D = x.shape
    width = _N_HEADS * _HEAD_DIM
    tb, tq, tm_mix, tm_ffn = 256, 512, 512, 512
    cos_t, sin_t = _rope_tables(T)
    ret_tables = _retention_tables(tb)
    tri = jnp.asarray(np.tril(np.ones((tq, tq), np.float32)), _BF16)
    row = lambda a: a.reshape(1, -1).astype(_F32)

    h = x
    for l in range(w_in.shape[0]):
        b_ff = _pad_lanes(row(b_forget[l]))
        gq = jnp.tile(row(g_fox_q[l]), (1, _N_HEADS)) * (_LOG2E * _HEAD_DIM ** -0.5)
        gk = jnp.tile(row(g_fox_k[l]), (1, _N_HEADS))
        bound = (_HEAD_DIM ** 0.5 * _NORM_ROUNDING_SLACK) * jnp.max(jnp.abs(g_fox_q[l])) * jnp.max(
            jnp.abs(g_fox_k[l]))
        use_shift = bound <= _MAX_FIXED_SHIFT
        shift = jnp.where(use_shift, jnp.ceil(bound * (4.0 * _LOG2E)) * 0.25, 0.0).astype(_F32)
        ret, qa, ka, va, fend = _in_proj(h, row(g_mix[l]), w_in[l].T, b_ff, cos_t, sin_t, gq, gk,
                                         jnp.full((1, _LANES), shift), tri, ret_tables,
                                         row(g_ret_out[l]), tm=tq, tb=tb)
        jlo = _first_live_block(fend[:, :, 0, :_N_HEADS])
        fox = lax.cond(use_shift,
                       lambda jlo, qa, ka, va: _fox_shifted(jlo, qa, ka, va, tq=tq, unroll=4),
                       lambda jlo, qa, ka, va: _fox_online(qa, ka, va, tq=tq, hg=4), jlo, qa, ka, va)
        k, v = _mem_kv(mem, row(g_mem[l]), w_xkv[l], row(g_xk[l]))
        h = _mix_xattn(h, ret, fox, w_out[l], row(g_xattn[l]), w_xq[l], row(g_xq[l]), k, v, w_xo[l],
                       tm=tm_mix)
        h = _ffn(h, row(g_ffn[l]), w_gate[l], w_up[l], w_down[l], tm=tm_ffn, n_split=2)
    return h
```

```python
import functools

import numpy as np
import jax
import jax.numpy as jnp
from jax import lax
from jax.experimental import pallas as pl
from jax.experimental.pallas import tpu as pltpu

_BF16 = jnp.bfloat16
_F32 = jnp.float32

_EPS = 1e-6
_NEG_INF = -1e30
_ROPE_BASE = 10000.0
_HEAD_DIM = 64
_N_HEADS = 8
_RET_CHUNK = 64
_RET_GROUP = 256
_N_XHEADS = 4
_LANES = 128
_MXU_TILE = 256
_VMEM_LIMIT = 56 * 1024 * 1024

_AUG_QF = 64
_AUG_KF = 88
_AUG_SHIFT = 112
_AUG_ONE = 64

_LOG2E = 1.4426950408889634
_MAX_FIXED_SHIFT = 32.0
_DIAG_PER_TRIP = 4
_UNDERFLOW_LOG2 = -152.0
_NORM_ROUNDING_SLACK = 1.01


def _params(sem):
    return pltpu.CompilerParams(dimension_semantics=sem, vmem_limit_bytes=_VMEM_LIMIT)


def _const_spec(shape):
    nd = len(shape)
    return pl.BlockSpec(shape, lambda *_: (0,) * nd, pipeline_mode=pl.Buffered(1))


def _rms(x, g):
    return x * lax.rsqrt(jnp.mean(x * x, axis=-1, keepdims=True) + _EPS) * g


def _rms_split(x, g):
    return (x * g).astype(_BF16), lax.rsqrt(jnp.mean(x * x, axis=-1, keepdims=True) + _EPS)


def _split3(v):
    hi = v.astype(_BF16).astype(_F32)
    r = v - hi
    mid = r.astype(_BF16).astype(_F32)
    return hi, mid, r - mid


def _in_proj_body(x_ref, g_ref, w32_ref, bf_ref, cos_ref, sin_ref, gq_ref, gk_ref, shift_ref,
                  tri_ref, avg_ref, dmask_ref, qdec_ref, kdec_ref, sdec_ref, bd_ref, gret_ref,
                  ret_ref, qa_ref, ka_ref, va_ref, fend_ref,
                  w_ref, wff_ref, carry_ref, rq_ref, rk_ref, rv_ref, gate_ref, state_ref, *, tm, tb, width):
    i = pl.program_id(1)
    n_main = 7 * width

    @pl.when((pl.program_id(0) == 0) & (i == 0))
    def _():
        for j in range(n_main // width):
            w_ref[:, j * width:(j + 1) * width] = jnp.transpose(
                w32_ref[j * width:(j + 1) * width, :]).astype(_BF16)
        tail = jnp.concatenate([w32_ref[n_main:, :],
                                jnp.zeros((_LANES - _N_HEADS, w32_ref.shape[1]), _F32)], axis=0)
        wff_ref[...] = jnp.transpose(tail).astype(_BF16)

    @pl.when(i == 0)
    def _():
        state_ref[...] = jnp.zeros_like(state_ref)
        carry_ref[...] = jnp.zeros_like(carry_ref)
    hb, inv_rms = _rms_split(x_ref[0], g_ref[...])

    def proj(j):
        return jnp.dot(hb, w_ref[:, j * width:(j + 1) * width], preferred_element_type=_F32) * inv_rms

    lane = lax.broadcasted_iota(jnp.int32, (tm, _LANES), 1)
    low = lane < _HEAD_DIM
    first_half = (lane & (_HEAD_DIM // 2)) == 0
    n_pairs = width // _LANES

    cos = cos_ref[...]
    sin = sin_ref[...]
    for j, out_ref, scale in ((0, rq_ref, _HEAD_DIM ** -0.5), (1, rk_ref, None)):
        y = proj(j)
        for c in range(n_pairs):
            blk = y[:, c * _LANES:(c + 1) * _LANES]
            swapped = jnp.where(first_half, pltpu.roll(blk, _LANES - _HEAD_DIM // 2, 1),
                                pltpu.roll(blk, _HEAD_DIM // 2, 1))
            r = blk * cos + swapped * sin
            if scale is not None:
                r = r * scale
            out_ref[:, c * _LANES:(c + 1) * _LANES] = r.astype(_BF16)
    rv_ref[...] = proj(2).astype(_BF16)
    gate = proj(3)
    gate_ref[...] = (gate * jax.nn.sigmoid(gate)).astype(_BF16)

    for r0 in range(0, tm, tb):
        _retention_block(rq_ref, rk_ref, rv_ref, gate_ref, slice(r0, r0 + tb), dmask_ref, qdec_ref, kdec_ref,
                         sdec_ref, bd_ref, avg_ref, gret_ref, state_ref, ret_ref)

    half = tm // 2
    z = jnp.concatenate([jnp.dot(hb[:half], wff_ref[...], preferred_element_type=_F32),
                         jnp.dot(hb[half:], wff_ref[...], preferred_element_type=_F32)],
                        axis=0) * inv_rms + bf_ref[...]
    logf = (jnp.minimum(z, 0.0) - jnp.log(1.0 + jnp.exp(-jnp.abs(z)))) * _LOG2E
    hi, mid, lo = _split3(logf)
    tri = tri_ref[...]
    split = jnp.concatenate([hi, mid, lo], axis=1).astype(_BF16)
    parts = jnp.concatenate(
        [jnp.dot(tri[:half, :half], split[:half], preferred_element_type=_F32),
         jnp.dot(tri[half:], split, preferred_element_type=_F32)], axis=0)
    csum = parts[:, :_LANES] + parts[:, _LANES:2 * _LANES] + parts[:, 2 * _LANES:]

    fcum = csum + carry_ref[0:1, :]
    carry_ref[...] = jnp.broadcast_to(fcum[tm - 1:tm, :], carry_ref.shape)
    fend_ref[0, 0] = carry_ref[...]

    def head_rms(y, g_row):
        out = []
        for g0 in range(0, width, _RET_GROUP):
            blk = y[:, g0:g0 + _RET_GROUP]
            ms = jnp.dot((blk * blk).astype(_BF16), avg_ref[...], preferred_element_type=_F32)
            out.append(blk * lax.rsqrt(ms + _EPS) * g_row[:, g0:g0 + _RET_GROUP])
        return out

    qn = head_rms(proj(4), gq_ref[...])
    kn = head_rms(proj(5), gk_ref[...])
    fv = proj(6)

    fh, fm, fl = _split3(fcum)
    n_parts = 3 * _N_HEADS
    packed = jnp.where(lane < _N_HEADS, fh, jnp.where(
        lane < 2 * _N_HEADS, pltpu.roll(fm, _N_HEADS, 1), pltpu.roll(fl, 2 * _N_HEADS, 1)))
    in_qf = (lane >= _AUG_QF) & (lane < _AUG_QF + n_parts)
    in_kf = (lane >= _AUG_KF) & (lane < _AUG_KF + n_parts)
    shared = jnp.where(in_qf, pltpu.roll(packed, _AUG_QF, 1), -pltpu.roll(packed, _AUG_KF, 1))
    q_bias = jnp.where(in_qf, shared, jnp.where(in_kf, 1.0, jnp.where(
        lane == _AUG_SHIFT, -shift_ref[...], 0.0)))
    v_const = jnp.where(lane == _AUG_ONE, 1.0, 0.0)
    for h in range(_N_HEADS):
        own = (lane & (_N_HEADS - 1)) == h
        k_bias = jnp.where(in_kf & own, shared,
                           jnp.where((in_qf & own) | (lane == _AUG_SHIFT), 1.0, 0.0))
        g0, c0 = divmod(h * _HEAD_DIM, _RET_GROUP)
        c0 = (c0 // _LANES) * _LANES
        qb = qn[g0][:, c0:c0 + _LANES]
        kb = kn[g0][:, c0:c0 + _LANES]
        vb = fv[:, (h // 2) * _LANES:(h // 2 + 1) * _LANES]
        if h % 2 == 1:
            qb = pltpu.roll(qb, _HEAD_DIM, 1)
            kb = pltpu.roll(kb, _HEAD_DIM, 1)
            vb = pltpu.roll(vb, _HEAD_DIM, 1)
        qa_ref[0, h] = jnp.where(low, qb, q_bias).astype(_BF16)
        ka_ref[0, h] = jnp.where(low, kb, k_bias).astype(_BF16)
        va_ref[0, h] = jnp.where(low, vb, v_const).astype(_BF16)


def _in_proj(x, g_mix, w_in_t, b_ff, cos_t, sin_t, gq, gk, shift, tri, ret_tables, g_ret, *, tm, tb):
    B, T, D = x.shape
    width = _N_HEADS * _HEAD_DIM
    assert w_in_t.shape == (7 * width + _N_HEADS, D) and tm % tb == 0
    dmask, qdec, kdec, sdec, bd, avg = ret_tables
    tok = lambda b, i: (b, i, 0)
    head = lambda b, i: (b, 0, i, 0)
    bf_tok = jax.ShapeDtypeStruct((B, T, width), _BF16)
    bf_head = jax.ShapeDtypeStruct((B, _N_HEADS, T, _LANES), _BF16)
    return pl.pallas_call(
        functools.partial(_in_proj_body, tm=tm, tb=tb, width=width),
        grid=(B, T // tm),
        in_specs=[
            pl.BlockSpec((1, tm, D), tok),
            _const_spec((1, D)),
            _const_spec(w_in_t.shape),
            _const_spec((1, _LANES)),
            pl.BlockSpec((tm, _LANES), lambda b, i: (i, 0)),
            pl.BlockSpec((tm, _LANES), lambda b, i: (i, 0)),
            _const_spec((1, width)),
            _const_spec((1, width)),
            _const_spec((1, _LANES)),
            _const_spec((tm, tm)),
            _const_spec(avg.shape),
            _const_spec(dmask.shape),
            _const_spec(qdec.shape),
            _const_spec(kdec.shape),
            _const_spec(sdec.shape),
            _const_spec(bd.shape),
            _const_spec(g_ret.shape),
        ],
        out_specs=[pl.BlockSpec((1, tm, width), tok)]
        + [pl.BlockSpec((1, _N_HEADS, tm, _LANES), head)] * 3
        + [pl.BlockSpec((1, 1, 8, _LANES), lambda b, i: (b, i, 0, 0))],
        out_shape=[bf_tok] + [bf_head] * 3 + [jax.ShapeDtypeStruct((B, T // tm, 8, _LANES), _F32)],
        scratch_shapes=[pltpu.VMEM((D, 7 * width), _BF16), pltpu.VMEM((D, _LANES), _BF16),
                        pltpu.VMEM((8, _LANES), _F32)]
        + [pltpu.VMEM((tm, width), _BF16)] * 4 + [pltpu.VMEM(sdec.shape, _F32)],
        compiler_params=_params(("arbitrary", "arbitrary")),
        name="in_proj",
    )(x, g_mix, w_in_t, b_ff, cos_t, sin_t, gq, gk, shift, tri, avg, dmask, qdec, kdec, sdec, bd, g_ret)


def _retention_block(rq_ref, rk_ref, rv_ref, gate_ref, rows, dmask_ref, qdec_ref, kdec_ref, sdec_ref, bd_ref,
                     avg_ref, g_ref, state_ref, o_ref):
    tb = rows.stop - rows.start
    nt = (((1,), (1,)), ((), ()))
    tn = (((0,), (0,)), ((), ()))
    gw = state_ref.shape[1]
    lane = lax.broadcasted_iota(jnp.int32, (1, _LANES), 1)
    low = lax.broadcasted_iota(jnp.int32, (tb, _LANES), 1) < _HEAD_DIM
    head_lanes = [jnp.where(lane < _HEAD_DIM, 1.0, 0.0).astype(_BF16),
                  jnp.where(lane < _HEAD_DIM, 0.0, 1.0).astype(_BF16)]
    n_groups = state_ref.shape[0]
    groups = [slice(g * gw, (g + 1) * gw) for g in range(n_groups)]
    scores = []
    for h in range(_N_HEADS):
        ps = slice((h // 2) * _LANES, (h // 2 + 1) * _LANES)
        scores.append(lax.dot_general(rq_ref[rows, ps] * head_lanes[h % 2], rk_ref[rows, ps], nt,
                                      preferred_element_type=_F32))
    inter = []
    for g, gs in enumerate(groups):
        state = state_ref[g]
        inter.append(jnp.dot(rq_ref[rows, gs], state.astype(_BF16),
                             preferred_element_type=_F32) * qdec_ref[:, gs])
        kd = (rk_ref[rows, gs].astype(_F32) * kdec_ref[:, gs]).astype(_BF16)
        state_ref[g] = state * sdec_ref[g] + lax.dot_general(
            kd, rv_ref[rows, gs], tn, preferred_element_type=_F32) * bd_ref[...]
    intra = []
    for h in range(_N_HEADS):
        ps = slice((h // 2) * _LANES, (h // 2 + 1) * _LANES)
        intra.append(jnp.dot((scores[h] * dmask_ref[h]).astype(_BF16), rv_ref[rows, ps],
                             preferred_element_type=_F32))
    outs = []
    for g, gs in enumerate(groups):
        pairs = [jnp.where(low, intra[2 * c], intra[2 * c + 1])
                 for c in range(g * gw // _LANES, (g + 1) * gw // _LANES)]
        outs.append(jnp.concatenate(pairs, axis=1) + inter[g])
    mus = [jnp.dot(o.astype(_BF16), avg_ref[...], preferred_element_type=_F32) for o in outs]
    cents = [o - mu for o, mu in zip(outs, mus)]
    vars_ = [jnp.dot((oc * oc).astype(_BF16), avg_ref[...], preferred_element_type=_F32) for oc in cents]
    for gs, oc, var in zip(groups, cents, vars_):
        y = oc * lax.rsqrt(var + _EPS) * g_ref[:, gs]
        o_ref[0, rows, gs] = (y * gate_ref[rows, gs].astype(_F32)).astype(_BF16)


def _causal_mask(tq):
    row = lax.broadcasted_iota(jnp.int32, (tq, tq), 0)
    col = lax.broadcasted_iota(jnp.int32, (tq, tq), 1)
    return row >= col


def _fox_finish(acc_refs, o_ref, tq):
    lane = lax.broadcasted_iota(jnp.int32, (tq, _LANES), 1)
    for c in range(len(acc_refs) // 2):
        pair = []
        for hh in (2 * c, 2 * c + 1):
            acc = acc_refs[hh][...]
            pair.append(acc / acc[:, _AUG_ONE:_AUG_ONE + 1])
        o_ref[0, :, c * _LANES:(c + 1) * _LANES] = jnp.where(
            lane < _HEAD_DIM, pair[0], pltpu.roll(pair[1], _HEAD_DIM, 1)).astype(_BF16)


def _fox_shifted_body(jlo_ref, qa_ref, ka_ref, va_ref, o_ref, acc_ref, it_i, it_j, it_slot,
                      *, tq, nq, unroll):
    b, pair = pl.program_id(0), pl.program_id(1)
    n_pair = qa_ref.shape[1]
    nt = (((1,), (1,)), ((), ()))
    half = tq // 2
    diff_top = (lax.broadcasted_iota(jnp.int32, (half, half), 1)
                - lax.broadcasted_iota(jnp.int32, (half, half), 0))
    diff_bot = (lax.broadcasted_iota(jnp.int32, (half, tq), 1)
                - lax.broadcasted_iota(jnp.int32, (half, tq), 0))

    def interleave(n_chains, qk, finish):
        s_prev = qk(0)
        for u in range(1, n_chains):
            s_next = qk(u)
            finish(u - 1, s_prev)
            s_prev = s_next
        finish(n_chains - 1, s_prev)

    for hh in range(n_pair):
        base = (b * (pl.num_programs(1) * n_pair) + pair * n_pair + hh) * nq
        acc_ref[...] = jnp.zeros_like(acc_ref)

        n = jnp.int32(0)
        for i in range(1, nq):
            def add(j, n, i=i):
                it_i[n] = jnp.int32(i)
                it_j[n] = j
                it_slot[n] = jnp.int32(i)
                return n + 1

            n = lax.fori_loop(jlo_ref[base + i], i, add, n)
        n_trips = (n + (unroll - 1)) // unroll

        def pad(m, carry):
            it_i[m] = jnp.int32(0)
            it_j[m] = jnp.int32(0)
            it_slot[m] = jnp.int32(nq)
            return carry

        lax.fori_loop(n, n_trips * unroll, pad, 0)

        def trip(t, carry, hh=hh):
            items = [(it_i[t * unroll + u], it_j[t * unroll + u], it_slot[t * unroll + u])
                     for u in range(unroll)]

            def qk(u):
                i, j, _ = items[u]
                return lax.dot_general(qa_ref[0, hh, pl.ds(pl.multiple_of(i * tq, tq), tq), :],
                                       ka_ref[0, hh, pl.ds(pl.multiple_of(j * tq, tq), tq), :], nt,
                                       preferred_element_type=_F32)

            def finish(u, s):
                _, j, slot = items[u]
                acc_ref[slot] += jnp.dot(jnp.exp2(s).astype(_BF16),
                                         va_ref[0, hh, pl.ds(pl.multiple_of(j * tq, tq), tq), :],
                                         preferred_element_type=_F32)

            interleave(unroll, qk, finish)
            return carry

        lax.fori_loop(0, n_trips, trip, 0)

        def diag_trip(t, carry, hh=hh):
            def rows(u):
                i, bottom = _DIAG_PER_TRIP * t + u // 2, u % 2
                return i, bottom, pl.multiple_of(i * tq + bottom * half, half)

            def qk(u):
                i, bottom, r0 = rows(u)
                k0 = pl.multiple_of(i * tq, tq)
                width = tq if bottom else half
                return lax.dot_general(qa_ref[0, hh, pl.ds(r0, half), :],
                                       ka_ref[0, hh, pl.ds(k0, width), :], nt, preferred_element_type=_F32)

            def finish(u, s):
                i, bottom, r0 = rows(u)
                k0 = pl.multiple_of(i * tq, tq)
                width = tq if bottom else half
                live = (diff_bot <= half) if bottom else (diff_top <= 0)
                p = jnp.exp2(jnp.where(live, s, _NEG_INF)).astype(_BF16)
                acc = acc_ref[i, bottom * half:(bottom + 1) * half, :] + jnp.dot(
                    p, va_ref[0, hh, pl.ds(k0, width), :], preferred_element_type=_F32)
                o = acc / acc[:, _AUG_ONE:_AUG_ONE + 1]
                if hh % 2 == 1:
                    o = pltpu.roll(o, _HEAD_DIM, 1)
                sl = slice(hh * _HEAD_DIM, (hh + 1) * _HEAD_DIM)
                o_ref[0, pl.ds(r0, half), sl] = o[:, sl].astype(_BF16)

            interleave(2 * _DIAG_PER_TRIP, qk, finish)
            return carry

        lax.fori_loop(0, nq // _DIAG_PER_TRIP, diag_trip, 0)


def _fox_online_body(qa_ref, ka_ref, va_ref, o_ref, *scratch, tq):
    qi = pl.program_id(2)
    nt = (((1,), (1,)), ((), ()))
    n_heads = qa_ref.shape[1]
    m_refs, acc_refs = scratch[:n_heads], scratch[n_heads:]
    for hh in range(n_heads):
        m_refs[hh][...] = jnp.full_like(m_refs[hh], _NEG_INF)
        acc_refs[hh][...] = jnp.zeros_like(acc_refs[hh])

    def step(j, masked):
        start = pl.multiple_of(j * tq, tq)
        logits = [lax.dot_general(qa_ref[0, hh], ka_ref[0, hh, pl.ds(start, tq), :], nt,
                                  preferred_element_type=_F32) for hh in range(n_heads)]
        for hh in range(n_heads):
            m_ref, acc_ref = m_refs[hh], acc_refs[hh]
            s = logits[hh]
            if masked:
                s = jnp.where(_causal_mask(tq), s, _NEG_INF)
            m_old = m_ref[...]
            m_new = jnp.maximum(m_old, jnp.max(s, axis=-1, keepdims=True))
            p = jnp.exp2(s - m_new[:, 0:1])
            acc_ref[...] = jnp.exp2(m_old - m_new) * acc_ref[...] + jnp.dot(
                p.astype(_BF16), va_ref[0, hh, pl.ds(start, tq), :], preferred_element_type=_F32)
            m_ref[...] = m_new

    def off_diag(j, carry):
        step(j, False)
        return carry

    lax.fori_loop(0, qi, off_diag, 0)
    step(qi, True)
    _fox_finish(acc_refs, o_ref, tq)


def _fox_shifted(jlo, qa, ka, va, *, tq, unroll):
    B, H, T, L = qa.shape
    nq = T // tq
    assert nq % _DIAG_PER_TRIP == 0 and tq % (2 * _MXU_TILE) == 0
    max_items = nq * (nq - 1) // 2 + unroll
    blk = pl.BlockSpec((1, 2, T, L), lambda b, p, jlo_ref: (b, p, 0, 0))
    return pl.pallas_call(
        functools.partial(_fox_shifted_body, tq=tq, nq=nq, unroll=unroll),
        grid_spec=pltpu.PrefetchScalarGridSpec(
            num_scalar_prefetch=1,
            grid=(B, H // 2),
            in_specs=[blk, blk, blk],
            out_specs=pl.BlockSpec((1, T, L), lambda b, p, jlo_ref: (b, 0, p)),
            scratch_shapes=[pltpu.VMEM((nq + 1, tq, L), _F32)] + [pltpu.SMEM((max_items,), jnp.int32)] * 3,
        ),
        out_shape=jax.ShapeDtypeStruct((B, T, H * _HEAD_DIM), _BF16),
        compiler_params=_params(("arbitrary", "arbitrary")),
        name="fox_shifted",
    )(jlo, qa, ka, va)


def _fox_online(qa, ka, va, *, tq, hg):
    B, H, T, L = qa.shape
    return pl.pallas_call(
        functools.partial(_fox_online_body, tq=tq),
        grid=(B, H // hg, T // tq),
        in_specs=[
            pl.BlockSpec((1, hg, tq, L), lambda b, p, i: (b, p, i, 0)),
            pl.BlockSpec((1, hg, T, L), lambda b, p, i: (b, p, 0, 0)),
            pl.BlockSpec((1, hg, T, L), lambda b, p, i: (b, p, 0, 0)),
        ],
        out_specs=pl.BlockSpec((1, tq, hg * _HEAD_DIM), lambda b, p, i: (b, i, p)),
        out_shape=jax.ShapeDtypeStruct((B, T, H * _HEAD_DIM), _BF16),
        scratch_shapes=[pltpu.VMEM((tq, L), _F32)] * (2 * hg),
        compiler_params=_params(("arbitrary", "arbitrary", "arbitrary")),
        name="fox_online",
    )(qa, ka, va)


def _first_live_block(fend):
    B, nq, H = fend.shape
    f = jnp.transpose(fend, (0, 2, 1))
    top = jnp.concatenate([jnp.zeros((B, H, 1), _F32), f[:, :, :-1]], axis=-1)
    dead = (top[:, :, :, None] - f[:, :, None, :]) <= _UNDERFLOW_LOG2
    j = jnp.arange(nq, dtype=jnp.int32)
    before = j[None, :] < j[:, None]
    first_live = jnp.min(jnp.where(dead & before, nq, j), axis=-1)
    return first_live.astype(jnp.int32).reshape(-1)


def _mem_kv_body(mem_ref, gm_ref, w32_ref, gk_ref, k_ref, v_ref, w_ref, *, d_model, xd):
    @pl.when(pl.program_id(0) == 0)
    def _():
        w_ref[...] = w32_ref[...].astype(_BF16)

    mn = _rms(mem_ref[0], gm_ref[...]).astype(_BF16)
    kv = jnp.dot(mn, w_ref[...], preferred_element_type=_F32)
    for h in range(_N_XHEADS):
        sl = slice(h * xd, (h + 1) * xd)
        k_ref[0, :, sl] = (_rms(kv[:, sl], gk_ref[...]) * (xd ** -0.5)).astype(_BF16)
    v_ref[0] = kv[:, d_model:].astype(_BF16)


def _mem_kv(mem, g_mem, w_xkv, g_xk):
    B, M, D = mem.shape
    xd = D // _N_XHEADS
    blk = pl.BlockSpec((1, M, D), lambda b: (b, 0, 0))
    out = jax.ShapeDtypeStruct((B, M, D), _BF16)
    return pl.pallas_call(
        functools.partial(_mem_kv_body, d_model=D, xd=xd),
        grid=(B,),
        in_specs=[blk, _const_spec((1, D)), _const_spec(w_xkv.shape), _const_spec((1, xd))],
        out_specs=[blk, blk],
        out_shape=[out, out],
        scratch_shapes=[pltpu.VMEM(w_xkv.shape, _BF16)],
        compiler_params=_params(("arbitrary",)),
        name="mem_kv",
    )(mem, g_mem, w_xkv, g_xk)


def _mix_xattn_body(x_ref, ret_ref, fox_ref, wo32_ref, gx_ref, wq32_ref, gq_ref, k_ref, v_ref, wxo32_ref,
                    o_ref, wo_ref, wq_ref, wxo_ref, *, width, xd, sub):
    @pl.when((pl.program_id(0) == 0) & (pl.program_id(1) == 0))
    def _():
        wo_ref[...] = wo32_ref[...].astype(_BF16)
        wq_ref[...] = wq32_ref[...].astype(_BF16)
        wxo_ref[...] = wxo32_ref[...].astype(_BF16)

    nt = (((1,), (1,)), ((), ()))
    head_slices = [slice(h * xd, (h + 1) * xd) for h in range(_N_XHEADS)]

    def front(rows):
        h1 = (x_ref[0, rows, :]
              + jnp.dot(ret_ref[0, rows, :], wo_ref[:width, :], preferred_element_type=_F32)
              + jnp.dot(fox_ref[0, rows, :], wo_ref[width:, :], preferred_element_type=_F32))
        hn, inv_h = _rms_split(h1, gx_ref[...])
        y = jnp.dot(hn, wq_ref[...], preferred_element_type=_F32)
        logits = []
        for sl in head_slices:
            yh = y[:, sl]
            inv_q = lax.rsqrt(inv_h * inv_h * jnp.mean(yh * yh, axis=-1, keepdims=True) + _EPS)
            logits.append(lax.dot_general((yh * gq_ref[...]).astype(_BF16), k_ref[0, :, sl], nt,
                                          preferred_element_type=_F32) * (inv_q * inv_h * _LOG2E))
        return h1, logits

    def back(rows, h1, logits):
        probs = []
        for lg in logits:
            p = jnp.exp2(lg - jnp.max(lg, axis=-1, keepdims=True))
            probs.append((p / jnp.sum(p, axis=-1, keepdims=True)).astype(_BF16))
        heads = [jnp.dot(p, v_ref[0, :, sl], preferred_element_type=_F32).astype(_BF16)
                 for p, sl in zip(probs, head_slices)]
        o = jnp.concatenate(heads, axis=-1)
        o_ref[0, rows, :] = h1 + jnp.dot(o, wxo_ref[...], preferred_element_type=_F32)

    tm = x_ref.shape[1]
    subs = [slice(r0, r0 + sub) for r0 in range(0, tm, sub)]
    fronts = [front(rows) for rows in subs]
    for rows, (h1, logits) in zip(subs, fronts):
        back(rows, h1, logits)


def _mix_xattn(x, ret, fox, w_out, g_xattn, w_xq, g_xq, k, v, w_xo, *, tm, sub):
    B, T, D = x.shape
    W = ret.shape[-1]
    M = k.shape[1]
    xd = D // _N_XHEADS
    assert tm % sub == 0 and T % tm == 0
    tok = lambda b, i: (b, i, 0)
    return pl.pallas_call(
        functools.partial(_mix_xattn_body, width=W, xd=xd, sub=sub),
        grid=(B, T // tm),
        in_specs=[
            pl.BlockSpec((1, tm, D), tok),
            pl.BlockSpec((1, tm, W), tok),
            pl.BlockSpec((1, tm, W), tok),
            _const_spec(w_out.shape),
            _const_spec((1, D)),
            _const_spec(w_xq.shape),
            _const_spec((1, xd)),
            pl.BlockSpec((1, M, D), lambda b, i: (b, 0, 0)),
            pl.BlockSpec((1, M, D), lambda b, i: (b, 0, 0)),
            _const_spec(w_xo.shape),
        ],
        out_specs=pl.BlockSpec((1, tm, D), tok),
        out_shape=jax.ShapeDtypeStruct((B, T, D), _F32),
        scratch_shapes=[pltpu.VMEM(w.shape, _BF16) for w in (w_out, w_xq, w_xo)],
        compiler_params=_params(("arbitrary", "arbitrary")),
        name="mix_xattn",
    )(x, ret, fox, w_out, g_xattn, w_xq, g_xq, k, v, w_xo)


def _ffn_chunks(d_ff, n_chunks):
    tiles = -(-d_ff // _MXU_TILE)
    bounds = [min(d_ff, _MXU_TILE * ((tiles * c) // n_chunks)) for c in range(n_chunks)] + [d_ff]
    return [(lo, hi) for lo, hi in zip(bounds[:-1], bounds[1:]) if hi > lo]


def _ffn_body(h_ref, g_ref, wg32_ref, wu32_ref, wd32_ref, o_ref, wg_ref, wu_ref, wd_ref,
              *, n_cast, ck, chunks):
    step = pl.program_id(0)
    for c in range(n_cast):
        @pl.when(step == c)
        def _(c=c):
            wg_ref[:, c * ck:(c + 1) * ck] = wg32_ref[...].astype(_BF16)
            wu_ref[:, c * ck:(c + 1) * ck] = wu32_ref[...].astype(_BF16)
            wd_ref[c * ck:(c + 1) * ck, :] = wd32_ref[...].astype(_BF16)

    @pl.when(step >= n_cast)
    def _():
        h = h_ref[0]
        hn, inv_rms = _rms_split(h, g_ref[...])
        acc = h
        for lo, hi in chunks:
            sl = slice(lo, hi)
            gate = jnp.dot(hn, wg_ref[:, sl], preferred_element_type=_F32) * inv_rms
            up = jnp.dot(hn, wu_ref[:, sl], preferred_element_type=_F32) * inv_rms
            a = (gate * jax.nn.sigmoid(gate) * up).astype(_BF16)
            acc = acc + jnp.dot(a, wd_ref[sl, :], preferred_element_type=_F32)
        o_ref[0] = acc


def _ffn(h, g_ffn, w_gate, w_up, w_down, *, tm, n_split):
    B, T, D = h.shape
    F = w_gate.shape[1]
    ck = _MXU_TILE
    assert F % ck == 0 and T % tm == 0
    n_cast = F // ck
    n_tiles = B * T // tm
    tile = lambda s: (jnp.maximum(s - n_cast, 0), 0, 0)
    chunk = lambda s: jnp.minimum(s, n_cast - 1)
    tok = pl.BlockSpec((1, tm, D), tile)
    out = pl.pallas_call(
        functools.partial(_ffn_body, n_cast=n_cast, ck=ck, chunks=_ffn_chunks(F, n_split)),
        grid=(n_cast + n_tiles,),
        in_specs=[tok, _const_spec((1, D)),
                  pl.BlockSpec((D, ck), lambda s: (0, chunk(s))),
                  pl.BlockSpec((D, ck), lambda s: (0, chunk(s))),
                  pl.BlockSpec((ck, D), lambda s: (chunk(s), 0))],
        out_specs=tok,
        out_shape=jax.ShapeDtypeStruct((n_tiles, tm, D), _F32),
        scratch_shapes=[pltpu.VMEM((D, F), _BF16), pltpu.VMEM((D, F), _BF16), pltpu.VMEM((F, D), _BF16)],
        compiler_params=_params(("arbitrary",)),
        name="ffn",
    )(h.reshape(n_tiles, tm, D), g_ffn, w_gate, w_up, w_down)
    return out.reshape(B, T, D)


def _rope_tables(T):
    half = _HEAD_DIM // 2
    inv_freq = (_ROPE_BASE ** (-np.arange(0, _HEAD_DIM, 2, dtype=np.float32) / _HEAD_DIM)).astype(np.float32)
    ang = (np.arange(T, dtype=np.float32)[:, None] * inv_freq[None, :]).astype(np.float32).astype(np.float64)
    cos, sin = np.cos(ang), np.sin(ang)
    reps = _LANES // _HEAD_DIM
    cos_t = np.tile(np.concatenate([cos, cos], axis=1), (1, reps))
    sin_t = np.tile(np.concatenate([-sin, sin], axis=1), (1, reps))
    return jnp.asarray(cos_t, _F32), jnp.asarray(sin_t, _F32)


def _retention_tables(tb):
    log_g = np.log(1.0 - 2.0 ** (-5.0 - np.arange(_N_HEADS, dtype=np.float64)))
    idx = np.arange(tb, dtype=np.float64)
    dist = np.abs(idx[:, None] - idx[None, :])
    chunk = np.arange(tb) // _RET_CHUNK
    visible = chunk[None, :] <= chunk[:, None]
    dmask = np.where(visible[None], np.exp(log_g[:, None, None] * dist[None]), 0.0)
    qdec = np.repeat(np.exp(log_g[None, :] * (idx[:, None] + 1.0)), _HEAD_DIM, axis=1)
    kdec = np.repeat(np.exp(log_g[None, :] * (tb - 1.0 - idx[:, None])), _HEAD_DIM, axis=1)
    heads_per_group = _RET_GROUP // _HEAD_DIM
    head_of = np.arange(_RET_GROUP) // _HEAD_DIM
    bd = (head_of[:, None] == head_of[None, :]).astype(np.float64)
    step_decay = np.exp(log_g * tb).reshape(-1, heads_per_group)
    sdec = bd[None] * np.repeat(step_decay, _HEAD_DIM, axis=1)[:, None, :]
    f = lambda a: jnp.asarray(a, _F32)
    return f(dmask), f(qdec), f(kdec), f(sdec), f(bd), jnp.asarray(bd / _HEAD_DIM, _BF16)


def _pad_lanes(a):
    return jnp.pad(a, [(0, 0)] * (a.ndim - 1) + [(0, _LANES - a.shape[-1])])


def kernel(x, mem, g_mix, w_in, b_forget, g_ret_out, g_fox_q, g_fox_k, w_out, g_xattn, w_xq, w_xkv,
           g_mem, g_xq, g_xk, w_xo, g_ffn, w_gate, w_up, w_down):
    B, T, D = x.shape
    width = _N_HEADS * _HEAD_DIM
    tb, tq, tm_mix, tm_ffn = 256, 512, 1024, 512
    cos_t, sin_t = _rope_tables(T)
    ret_tables = _retention_tables(tb)
    tri = jnp.asarray(np.tril(np.ones((tq, tq), np.float32)), _BF16)
    row = lambda a: a.reshape(1, -1).astype(_F32)

    h = x
    for l in range(w_in.shape[0]):
        b_ff = _pad_lanes(row(b_forget[l]))
        gq = jnp.tile(row(g_fox_q[l]), (1, _N_HEADS)) * (_LOG2E * _HEAD_DIM ** -0.5)
        gk = jnp.tile(row(g_fox_k[l]), (1, _N_HEADS))
        bound = (_HEAD_DIM ** 0.5 * _NORM_ROUNDING_SLACK) * jnp.max(jnp.abs(g_fox_q[l])) * jnp.max(
            jnp.abs(g_fox_k[l]))
        use_shift = bound <= _MAX_FIXED_SHIFT
        shift = jnp.where(use_shift, jnp.ceil(bound * (4.0 * _LOG2E)) * 0.25, 0.0).astype(_F32)
        ret, qa, ka, va, fend = _in_proj(h, row(g_mix[l]), w_in[l].T, b_ff, cos_t, sin_t, gq, gk,
                                         jnp.full((1, _LANES), shift), tri, ret_tables,
                                         row(g_ret_out[l]), tm=tq, tb=tb)
        jlo = _first_live_block(fend[:, :, 0, :_N_HEADS])
        fox = lax.cond(use_shift,
                       lambda jlo, qa, ka, va: _fox_shifted(jlo, qa, ka, va, tq=tq, unroll=4),
                       lambda jlo, qa, ka, va: _fox_online(qa, ka, va, tq=tq, hg=4), jlo, qa, ka, va)
        k, v = _mem_kv(mem, row(g_mem[l]), w_xkv[l], row(g_xk[l]))
        h = _mix_xattn(h, ret, fox, w_out[l], row(g_xattn[l]), w_xq[l], row(g_xq[l]), k, v, w_xo[l],
                       tm=tm_mix, sub=512)
        h = _ffn(h, row(g_ffn[l]), w_gate[l], w_up[l], w_down[l], tm=tm_ffn, n_split=2)
    return h
```

```python
import functools

import numpy as np
import jax
import jax.numpy as jnp
from jax import lax
from jax.experimental import pallas as pl
from jax.experimental.pallas import tpu as pltpu

_BF16 = jnp.bfloat16
_F32 = jnp.float32

_EPS = 1e-6
_NEG_INF = -1e30
_ROPE_BASE = 10000.0
_HEAD_DIM = 64
_N_HEADS = 8
_RET_CHUNK = 64
_RET_GROUP = 256
_N_XHEADS = 4
_LANES = 128
_MXU_TILE = 256
_VMEM_LIMIT = 56 * 1024 * 1024

_AUG_QF = 64
_AUG_KF = 88
_AUG_SHIFT = 112
_AUG_ONE = 64

_LOG2E = 1.4426950408889634
_MAX_FIXED_SHIFT = 32.0
_DIAG_PER_TRIP = 4
_UNDERFLOW_LOG2 = -152.0
_NORM_ROUNDING_SLACK = 1.01


def _params(sem):
    return pltpu.CompilerParams(dimension_semantics=sem, vmem_limit_bytes=_VMEM_LIMIT)


def _const_spec(shape):
    nd = len(shape)
    return pl.BlockSpec(shape, lambda *_: (0,) * nd, pipeline_mode=pl.Buffered(1))


def _rms(x, g):
    return x * lax.rsqrt(jnp.mean(x * x, axis=-1, keepdims=True) + _EPS) * g


def _rms_split(x, g):
    return (x * g).astype(_BF16), lax.rsqrt(jnp.mean(x * x, axis=-1, keepdims=True) + _EPS)


def _split3(v):
    hi = v.astype(_BF16).astype(_F32)
    r = v - hi
    mid = r.astype(_BF16).astype(_F32)
    return hi, mid, r - mid


def _in_proj_body(x_ref, g_ref, w32_ref, bf_ref, cos_ref, sin_ref, gq_ref, gk_ref, shift_ref,
                  tri_ref, avg_ref, dmask_ref, qdec_ref, kdec_ref, sdec_ref, bd_ref, gret_ref,
                  ret_ref, qa_ref, ka_ref, va_ref, fend_ref,
                  w_ref, wff_ref, carry_ref, rq_ref, rk_ref, rv_ref, gate_ref, state_ref, *, tm, tb, width):
    i = pl.program_id(1)
    n_main = 7 * width

    @pl.when((pl.program_id(0) == 0) & (i == 0))
    def _():
        for j in range(n_main // width):
            w_ref[:, j * width:(j + 1) * width] = jnp.transpose(
                w32_ref[j * width:(j + 1) * width, :]).astype(_BF16)
        tail = jnp.concatenate([w32_ref[n_main:, :],
                                jnp.zeros((_LANES - _N_HEADS, w32_ref.shape[1]), _F32)], axis=0)
        wff_ref[...] = jnp.transpose(tail).astype(_BF16)

    @pl.when(i == 0)
    def _():
        state_ref[...] = jnp.zeros_like(state_ref)
        carry_ref[...] = jnp.zeros_like(carry_ref)
    hb, inv_rms = _rms_split(x_ref[0], g_ref[...])

    def proj(j):
        return jnp.dot(hb, w_ref[:, j * width:(j + 1) * width], preferred_element_type=_F32) * inv_rms

    lane = lax.broadcasted_iota(jnp.int32, (tm, _LANES), 1)
    low = lane < _HEAD_DIM
    first_half = (lane & (_HEAD_DIM // 2)) == 0
    n_pairs = width // _LANES

    cos = cos_ref[...]
    sin = sin_ref[...]
    for j, out_ref, scale in ((0, rq_ref, _HEAD_DIM ** -0.5), (1, rk_ref, None)):
        y = proj(j)
        for c in range(n_pairs):
            blk = y[:, c * _LANES:(c + 1) * _LANES]
            swapped = jnp.where(first_half, pltpu.roll(blk, _LANES - _HEAD_DIM // 2, 1),
                                pltpu.roll(blk, _HEAD_DIM // 2, 1))
            r = blk * cos + swapped * sin
            if scale is not None:
                r = r * scale
            out_ref[:, c * _LANES:(c + 1) * _LANES] = r.astype(_BF16)
    rv_ref[...] = proj(2).astype(_BF16)
    gate = proj(3)
    gate_ref[...] = (gate * jax.nn.sigmoid(gate)).astype(_BF16)

    for r0 in range(0, tm, tb):
        _retention_block(rq_ref, rk_ref, rv_ref, gate_ref, slice(r0, r0 + tb), dmask_ref, qdec_ref, kdec_ref,
                         sdec_ref, bd_ref, avg_ref, gret_ref, state_ref, ret_ref)

    half = tm // 2
    z = jnp.concatenate([jnp.dot(hb[:half], wff_ref[...], preferred_element_type=_F32),
                         jnp.dot(hb[half:], wff_ref[...], preferred_element_type=_F32)],
                        axis=0) * inv_rms + bf_ref[...]
    logf = (jnp.minimum(z, 0.0) - jnp.log(1.0 + jnp.exp(-jnp.abs(z)))) * _LOG2E
    hi, mid, lo = _split3(logf)
    tri = tri_ref[...]
    split = jnp.concatenate([hi, mid, lo], axis=1).astype(_BF16)
    parts = jnp.concatenate(
        [jnp.dot(tri[:half, :half], split[:half], preferred_element_type=_F32),
         jnp.dot(tri[half:], split, preferred_element_type=_F32)], axis=0)
    csum = parts[:, :_LANES] + parts[:, _LANES:2 * _LANES] + parts[:, 2 * _LANES:]

    fcum = csum + carry_ref[0:1, :]
    carry_ref[...] = jnp.broadcast_to(fcum[tm - 1:tm, :], carry_ref.shape)
    fend_ref[0, 0] = carry_ref[...]

    def head_rms(y, g_row):
        out = []
        for g0 in range(0, width, _RET_GROUP):
            blk = y[:, g0:g0 + _RET_GROUP]
            ms = jnp.dot((blk * blk).astype(_BF16), avg_ref[...], preferred_element_type=_F32)
            out.append(blk * lax.rsqrt(ms + _EPS) * g_row[:, g0:g0 + _RET_GROUP])
        return out

    qn = head_rms(proj(4), gq_ref[...])
    kn = head_rms(proj(5), gk_ref[...])
    fv = proj(6)

    fh, fm, fl = _split3(fcum)
    n_parts = 3 * _N_HEADS
    packed = jnp.where(lane < _N_HEADS, fh, jnp.where(
        lane < 2 * _N_HEADS, pltpu.roll(fm, _N_HEADS, 1), pltpu.roll(fl, 2 * _N_HEADS, 1)))
    in_qf = (lane >= _AUG_QF) & (lane < _AUG_QF + n_parts)
    in_kf = (lane >= _AUG_KF) & (lane < _AUG_KF + n_parts)
    shared = jnp.where(in_qf, pltpu.roll(packed, _AUG_QF, 1), -pltpu.roll(packed, _AUG_KF, 1))
    q_bias = jnp.where(in_qf, shared, jnp.where(in_kf, 1.0, jnp.where(
        lane == _AUG_SHIFT, -shift_ref[...], 0.0)))
    v_const = jnp.where(lane == _AUG_ONE, 1.0, 0.0)
    for h in range(_N_HEADS):
        own = (lane & (_N_HEADS - 1)) == h
        k_bias = jnp.where(in_kf & own, shared,
                           jnp.where((in_qf & own) | (lane == _AUG_SHIFT), 1.0, 0.0))
        g0, c0 = divmod(h * _HEAD_DIM, _RET_GROUP)
        c0 = (c0 // _LANES) * _LANES
        qb = qn[g0][:, c0:c0 + _LANES]
        kb = kn[g0][:, c0:c0 + _LANES]
        vb = fv[:, (h // 2) * _LANES:(h // 2 + 1) * _LANES]
        if h % 2 == 1:
            qb = pltpu.roll(qb, _HEAD_DIM, 1)
            kb = pltpu.roll(kb, _HEAD_DIM, 1)
            vb = pltpu.roll(vb, _HEAD_DIM, 1)
        qa_ref[0, h] = jnp.where(low, qb, q_bias).astype(_BF16)
        ka_ref[0, h] = jnp.where(low, kb, k_bias).astype(_BF16)
        va_ref[0, h] = jnp.where(low, vb, v_const).astype(_BF16)


def _in_proj(x, g_mix, w_in_t, b_ff, cos_t, sin_t, gq, gk, shift, tri, ret_tables, g_ret, *, tm, tb):
    B, T, D = x.shape
    width = _N_HEADS * _HEAD_DIM
    assert w_in_t.shape == (7 * width + _N_HEADS, D) and tm % tb == 0
    dmask, qdec, kdec, sdec, bd, avg = ret_tables
    tok = lambda b, i: (b, i, 0)
    head = lambda b, i: (b, 0, i, 0)
    bf_tok = jax.ShapeDtypeStruct((B, T, width), _BF16)
    bf_head = jax.ShapeDtypeStruct((B, _N_HEADS, T, _LANES), _BF16)
    return pl.pallas_call(
        functools.partial(_in_proj_body, tm=tm, tb=tb, width=width),
        grid=(B, T // tm),
        in_specs=[
            pl.BlockSpec((1, tm, D), tok),
            _const_spec((1, D)),
            _const_spec(w_in_t.shape),
            _const_spec((1, _LANES)),
            pl.BlockSpec((tm, _LANES), lambda b, i: (i, 0)),
            pl.BlockSpec((tm, _LANES), lambda b, i: (i, 0)),
            _const_spec((1, width)),
            _const_spec((1, width)),
            _const_spec((1, _LANES)),
            _const_spec((tm, tm)),
            _const_spec(avg.shape),
            _const_spec(dmask.shape),
            _const_spec(qdec.shape),
            _const_spec(kdec.shape),
            _const_spec(sdec.shape),
            _const_spec(bd.shape),
            _const_spec(g_ret.shape),
        ],
        out_specs=[pl.BlockSpec((1, tm, width), tok)]
        + [pl.BlockSpec((1, _N_HEADS, tm, _LANES), head)] * 3
        + [pl.BlockSpec((1, 1, 8, _LANES), lambda b, i: (b, i, 0, 0))],
        out_shape=[bf_tok] + [bf_head] * 3 + [jax.ShapeDtypeStruct((B, T // tm, 8, _LANES), _F32)],
        scratch_shapes=[pltpu.VMEM((D, 7 * width), _BF16), pltpu.VMEM((D, _LANES), _BF16),
                        pltpu.VMEM((8, _LANES), _F32)]
        + [pltpu.VMEM((tm, width), _BF16)] * 4 + [pltpu.VMEM(sdec.shape, _F32)],
        compiler_params=_params(("arbitrary", "arbitrary")),
        name="in_proj",
    )(x, g_mix, w_in_t, b_ff, cos_t, sin_t, gq, gk, shift, tri, avg, dmask, qdec, kdec, sdec, bd, g_ret)


def _retention_block(rq_ref, rk_ref, rv_ref, gate_ref, rows, dmask_ref, qdec_ref, kdec_ref, sdec_ref, bd_ref,
                     avg_ref, g_ref, state_ref, o_ref):
    tb = rows.stop - rows.start
    nt = (((1,), (1,)), ((), ()))
    tn = (((0,), (0,)), ((), ()))
    gw = state_ref.shape[1]
    lane = lax.broadcasted_iota(jnp.int32, (1, _LANES), 1)
    low = lax.broadcasted_iota(jnp.int32, (tb, _LANES), 1) < _HEAD_DIM
    head_lanes = [jnp.where(lane < _HEAD_DIM, 1.0, 0.0).astype(_BF16),
                  jnp.where(lane < _HEAD_DIM, 0.0, 1.0).astype(_BF16)]
    n_groups = state_ref.shape[0]
    groups = [slice(g * gw, (g + 1) * gw) for g in range(n_groups)]
    scores = []
    for h in range(_N_HEADS):
        ps = slice((h // 2) * _LANES, (h // 2 + 1) * _LANES)
        scores.append(lax.dot_general(rq_ref[rows, ps] * head_lanes[h % 2], rk_ref[rows, ps], nt,
                                      preferred_element_type=_F32))
    inter = []
    for g, gs in enumerate(groups):
        state = state_ref[g]
        inter.append(jnp.dot(rq_ref[rows, gs], state.astype(_BF16),
                             preferred_element_type=_F32) * qdec_ref[:, gs])
        kd = (rk_ref[rows, gs].astype(_F32) * kdec_ref[:, gs]).astype(_BF16)
        state_ref[g] = state * sdec_ref[g] + lax.dot_general(
            kd, rv_ref[rows, gs], tn, preferred_element_type=_F32) * bd_ref[...]
    intra = []
    for h in range(_N_HEADS):
        ps = slice((h // 2) * _LANES, (h // 2 + 1) * _LANES)
        intra.append(jnp.dot((scores[h] * dmask_ref[h]).astype(_BF16), rv_ref[rows, ps],
                             preferred_element_type=_F32))
    outs = []
    for g, gs in enumerate(groups):
        pairs = [jnp.where(low, intra[2 * c], intra[2 * c + 1])
                 for c in range(g * gw // _LANES, (g + 1) * gw // _LANES)]
        outs.append(jnp.concatenate(pairs, axis=1) + inter[g])
    mus = [jnp.dot(o.astype(_BF16), avg_ref[...], preferred_element_type=_F32) for o in outs]
    cents = [o - mu for o, mu in zip(outs, mus)]
    vars_ = [jnp.dot((oc * oc).astype(_BF16), avg_ref[...], preferred_element_type=_F32) for oc in cents]
    for gs, oc, var in zip(groups, cents, vars_):
        y = oc * lax.rsqrt(var + _EPS) * g_ref[:, gs]
        o_ref[0, rows, gs] = (y * gate_ref[rows, gs].astype(_F32)).astype(_BF16)


def _causal_mask(tq):
    row = lax.broadcasted_iota(jnp.int32, (tq, tq), 0)
    col = lax.broadcasted_iota(jnp.int32, (tq, tq), 1)
    return row >= col


def _fox_finish(acc_refs, o_ref, tq):
    lane = lax.broadcasted_iota(jnp.int32, (tq, _LANES), 1)
    for c in range(len(acc_refs) // 2):
        pair = []
        for hh in (2 * c, 2 * c + 1):
            acc = acc_refs[hh][...]
            pair.append(acc / acc[:, _AUG_ONE:_AUG_ONE + 1])
        o_ref[0, :, c * _LANES:(c + 1) * _LANES] = jnp.where(
            lane < _HEAD_DIM, pair[0], pltpu.roll(pair[1], _HEAD_DIM, 1)).astype(_BF16)


def _fox_shifted_body(jlo_ref, qa_ref, ka_ref, va_ref, o_ref, acc_ref, it_h, it_i, it_j, it_slot, it_keep,
                      *, tq, nq, unroll):
    n_pair = qa_ref.shape[1]
    base = (pl.program_id(0) * pl.num_programs(1) + pl.program_id(1)) * (n_pair * nq)
    nt = (((1,), (1,)), ((), ()))
    half = tq // 2
    diff_top = (lax.broadcasted_iota(jnp.int32, (half, half), 1)
                - lax.broadcasted_iota(jnp.int32, (half, half), 0))
    diff_bot = (lax.broadcasted_iota(jnp.int32, (half, tq), 1)
                - lax.broadcasted_iota(jnp.int32, (half, tq), 0))

    def interleave(n_chains, qk, finish):
        s_prev = qk(0)
        for u in range(1, n_chains):
            s_next = qk(u)
            finish(u - 1, s_prev)
            s_prev = s_next
        finish(n_chains - 1, s_prev)

    n = jnp.int32(0)
    for hh in range(n_pair):
        for i in range(1, nq):
            lo = jlo_ref[base + hh * nq + i]

            def add(j, n, hh=hh, i=i, lo=lo):
                it_h[n] = jnp.int32(hh)
                it_i[n] = jnp.int32(i)
                it_j[n] = j
                it_slot[n] = jnp.int32(hh * nq + i)
                it_keep[n] = jnp.where(j == lo, 0, 1)
                return n + 1

            n = lax.fori_loop(lo, i, add, n)
    n_trips = (n + (unroll - 1)) // unroll

    def pad(m, carry):
        it_h[m] = jnp.int32(0)
        it_i[m] = jnp.int32(0)
        it_j[m] = jnp.int32(0)
        it_slot[m] = jnp.int32(n_pair * nq)
        it_keep[m] = jnp.int32(0)
        return carry

    lax.fori_loop(n, n_trips * unroll, pad, 0)

    def trip(t, carry):
        items = [tuple(ref[t * unroll + u] for ref in (it_h, it_i, it_j, it_slot, it_keep))
                 for u in range(unroll)]

        def qk(u):
            h, i, j = items[u][:3]
            return lax.dot_general(qa_ref[0, h, pl.ds(pl.multiple_of(i * tq, tq), tq), :],
                                   ka_ref[0, h, pl.ds(pl.multiple_of(j * tq, tq), tq), :], nt,
                                   preferred_element_type=_F32)

        def finish(u, s):
            h, _, j, slot, keep = items[u]
            pv = jnp.dot(jnp.exp2(s).astype(_BF16), va_ref[0, h, pl.ds(pl.multiple_of(j * tq, tq), tq), :],
                         preferred_element_type=_F32)
            acc_ref[slot] = jnp.where(keep > 0, acc_ref[slot], 0.0) + pv

        interleave(unroll, qk, finish)
        return carry

    lax.fori_loop(0, n_trips, trip, 0)

    for hh in range(n_pair):
        def diag_trip(t, carry, hh=hh):
            def rows(u):
                i, bottom = _DIAG_PER_TRIP * t + u // 2, u % 2
                return i, bottom, pl.multiple_of(i * tq + bottom * half, half)

            def qk(u):
                i, bottom, r0 = rows(u)
                k0 = pl.multiple_of(i * tq, tq)
                width = tq if bottom else half
                return lax.dot_general(qa_ref[0, hh, pl.ds(r0, half), :],
                                       ka_ref[0, hh, pl.ds(k0, width), :], nt, preferred_element_type=_F32)

            def finish(u, s):
                i, bottom, r0 = rows(u)
                k0 = pl.multiple_of(i * tq, tq)
                width = tq if bottom else half
                live = (diff_bot <= half) if bottom else (diff_top <= 0)
                p = jnp.exp2(jnp.where(live, s, _NEG_INF)).astype(_BF16)
                has_off_diagonal = jlo_ref[base + hh * nq + i] < i
                acc = jnp.where(has_off_diagonal,
                                acc_ref[hh * nq + i, bottom * half:(bottom + 1) * half, :], 0.0) + jnp.dot(
                    p, va_ref[0, hh, pl.ds(k0, width), :], preferred_element_type=_F32)
                o = acc / acc[:, _AUG_ONE:_AUG_ONE + 1]
                lanes = slice((hh % 2) * _HEAD_DIM, (hh % 2 + 1) * _HEAD_DIM)
                if hh % 2 == 1:
                    o = pltpu.roll(o, _HEAD_DIM, 1)
                o_ref[0, pl.ds(r0, half), hh * _HEAD_DIM:(hh + 1) * _HEAD_DIM] = o[:, lanes].astype(_BF16)

            interleave(2 * _DIAG_PER_TRIP, qk, finish)
            return carry

        lax.fori_loop(0, nq // _DIAG_PER_TRIP, diag_trip, 0)


def _fox_online_body(qa_ref, ka_ref, va_ref, o_ref, *scratch, tq):
    qi = pl.program_id(2)
    nt = (((1,), (1,)), ((), ()))
    n_heads = qa_ref.shape[1]
    m_refs, acc_refs = scratch[:n_heads], scratch[n_heads:]
    for hh in range(n_heads):
        m_refs[hh][...] = jnp.full_like(m_refs[hh], _NEG_INF)
        acc_refs[hh][...] = jnp.zeros_like(acc_refs[hh])

    def step(j, masked):
        start = pl.multiple_of(j * tq, tq)
        logits = [lax.dot_general(qa_ref[0, hh], ka_ref[0, hh, pl.ds(start, tq), :], nt,
                                  preferred_element_type=_F32) for hh in range(n_heads)]
        for hh in range(n_heads):
            m_ref, acc_ref = m_refs[hh], acc_refs[hh]
            s = logits[hh]
            if masked:
                s = jnp.where(_causal_mask(tq), s, _NEG_INF)
            m_old = m_ref[...]
            m_new = jnp.maximum(m_old, jnp.max(s, axis=-1, keepdims=True))
            p = jnp.exp2(s - m_new[:, 0:1])
            acc_ref[...] = jnp.exp2(m_old - m_new) * acc_ref[...] + jnp.dot(
                p.astype(_BF16), va_ref[0, hh, pl.ds(start, tq), :], preferred_element_type=_F32)
            m_ref[...] = m_new

    def off_diag(j, carry):
        step(j, False)
        return carry

    lax.fori_loop(0, qi, off_diag, 0)
    step(qi, True)
    _fox_finish(acc_refs, o_ref, tq)


def _fox_shifted(jlo, qa, ka, va, *, tq, hg, unroll):
    B, H, T, L = qa.shape
    nq = T // tq
    assert nq % _DIAG_PER_TRIP == 0 and tq % (2 * _MXU_TILE) == 0 and hg % 2 == 0 and H % hg == 0
    max_items = hg * nq * (nq - 1) // 2 + unroll
    blk = pl.BlockSpec((1, hg, T, L), lambda b, p, jlo_ref: (b, p, 0, 0))
    return pl.pallas_call(
        functools.partial(_fox_shifted_body, tq=tq, nq=nq, unroll=unroll),
        grid_spec=pltpu.PrefetchScalarGridSpec(
            num_scalar_prefetch=1,
            grid=(B, H // hg),
            in_specs=[blk, blk, blk],
            out_specs=pl.BlockSpec((1, T, hg * _HEAD_DIM), lambda b, p, jlo_ref: (b, 0, p)),
            scratch_shapes=[pltpu.VMEM((hg * nq + 1, tq, L), _F32)]
            + [pltpu.SMEM((max_items,), jnp.int32)] * 5,
        ),
        out_shape=jax.ShapeDtypeStruct((B, T, H * _HEAD_DIM), _BF16),
        compiler_params=_params(("arbitrary", "arbitrary")),
        name="fox_shifted",
    )(jlo, qa, ka, va)


def _fox_online(qa, ka, va, *, tq, hg):
    B, H, T, L = qa.shape
    return pl.pallas_call(
        functools.partial(_fox_online_body, tq=tq),
        grid=(B, H // hg, T // tq),
        in_specs=[
            pl.BlockSpec((1, hg, tq, L), lambda b, p, i: (b, p, i, 0)),
            pl.BlockSpec((1, hg, T, L), lambda b, p, i: (b, p, 0, 0)),
            pl.BlockSpec((1, hg, T, L), lambda b, p, i: (b, p, 0, 0)),
        ],
        out_specs=pl.BlockSpec((1, tq, hg * _HEAD_DIM), lambda b, p, i: (b, i, p)),
        out_shape=jax.ShapeDtypeStruct((B, T, H * _HEAD_DIM), _BF16),
        scratch_shapes=[pltpu.VMEM((tq, L), _F32)] * (2 * hg),
        compiler_params=_params(("arbitrary", "arbitrary", "arbitrary")),
        name="fox_online",
    )(qa, ka, va)


def _first_live_block(fend):
    B, nq, H = fend.shape
    f = jnp.transpose(fend, (0, 2, 1))
    top = jnp.concatenate([jnp.zeros((B, H, 1), _F32), f[:, :, :-1]], axis=-1)
    dead = (top[:, :, :, None] - f[:, :, None, :]) <= _UNDERFLOW_LOG2
    j = jnp.arange(nq, dtype=jnp.int32)
    before = j[None, :] < j[:, None]
    first_live = jnp.min(jnp.where(dead & before, nq, j), axis=-1)
    return first_live.astype(jnp.int32).reshape(-1)


def _mem_kv_body(mem_ref, gm_ref, w32_ref, gk_ref, k_ref, v_ref, w_ref, *, d_model, xd):
    @pl.when(pl.program_id(0) == 0)
    def _():
        w_ref[...] = w32_ref[...].astype(_BF16)

    mn = _rms(mem_ref[0], gm_ref[...]).astype(_BF16)
    kv = jnp.dot(mn, w_ref[...], preferred_element_type=_F32)
    for h in range(_N_XHEADS):
        sl = slice(h * xd, (h + 1) * xd)
        k_ref[0, :, sl] = (_rms(kv[:, sl], gk_ref[...]) * (xd ** -0.5)).astype(_BF16)
    v_ref[0] = kv[:, d_model:].astype(_BF16)


def _mem_kv(mem, g_mem, w_xkv, g_xk):
    B, M, D = mem.shape
    xd = D // _N_XHEADS
    blk = pl.BlockSpec((1, M, D), lambda b: (b, 0, 0))
    out = jax.ShapeDtypeStruct((B, M, D), _BF16)
    return pl.pallas_call(
        functools.partial(_mem_kv_body, d_model=D, xd=xd),
        grid=(B,),
        in_specs=[blk, _const_spec((1, D)), _const_spec(w_xkv.shape), _const_spec((1, xd))],
        out_specs=[blk, blk],
        out_shape=[out, out],
        scratch_shapes=[pltpu.VMEM(w_xkv.shape, _BF16)],
        compiler_params=_params(("arbitrary",)),
        name="mem_kv",
    )(mem, g_mem, w_xkv, g_xk)


def _mix_xattn_body(x_ref, ret_ref, fox_ref, wo32_ref, gx_ref, wq32_ref, gq_ref, k_ref, v_ref, wxo32_ref,
                    o_ref, wo_ref, wq_ref, wxo_ref, *, width, xd, sub):
    @pl.when((pl.program_id(0) == 0) & (pl.program_id(1) == 0))
    def _():
        wo_ref[...] = wo32_ref[...].astype(_BF16)
        wq_ref[...] = wq32_ref[...].astype(_BF16)
        wxo_ref[...] = wxo32_ref[...].astype(_BF16)

    nt = (((1,), (1,)), ((), ()))
    head_slices = [slice(h * xd, (h + 1) * xd) for h in range(_N_XHEADS)]

    def front(rows):
        h1 = (x_ref[0, rows, :]
              + jnp.dot(ret_ref[0, rows, :], wo_ref[:width, :], preferred_element_type=_F32)
              + jnp.dot(fox_ref[0, rows, :], wo_ref[width:, :], preferred_element_type=_F32))
        hn, inv_h = _rms_split(h1, gx_ref[...])
        y = jnp.dot(hn, wq_ref[...], preferred_element_type=_F32)
        logits = []
        for sl in head_slices:
            yh = y[:, sl]
            inv_q = lax.rsqrt(inv_h * inv_h * jnp.mean(yh * yh, axis=-1, keepdims=True) + _EPS)
            logits.append(lax.dot_general((yh * gq_ref[...]).astype(_BF16), k_ref[0, :, sl], nt,
                                          preferred_element_type=_F32) * (inv_q * inv_h * _LOG2E))
        return h1, logits

    def back(rows, h1, logits):
        probs = []
        for lg in logits:
            p = jnp.exp2(lg - jnp.max(lg, axis=-1, keepdims=True))
            probs.append((p / jnp.sum(p, axis=-1, keepdims=True)).astype(_BF16))
        heads = [jnp.dot(p, v_ref[0, :, sl], preferred_element_type=_F32).astype(_BF16)
                 for p, sl in zip(probs, head_slices)]
        o = jnp.concatenate(heads, axis=-1)
        o_ref[0, rows, :] = h1 + jnp.dot(o, wxo_ref[...], preferred_element_type=_F32)

    tm = x_ref.shape[1]
    subs = [slice(r0, r0 + sub) for r0 in range(0, tm, sub)]
    fronts = [front(rows) for rows in subs]
    for rows, (h1, logits) in zip(subs, fronts):
        back(rows, h1, logits)


def _mix_xattn(x, ret, fox, w_out, g_xattn, w_xq, g_xq, k, v, w_xo, *, tm, sub):
    B, T, D = x.shape
    W = ret.shape[-1]
    M = k.shape[1]
    xd = D // _N_XHEADS
    assert tm % sub == 0 and T % tm == 0
    tok = lambda b, i: (b, i, 0)
    return pl.pallas_call(
        functools.partial(_mix_xattn_body, width=W, xd=xd, sub=sub),
        grid=(B, T // tm),
        in_specs=[
            pl.BlockSpec((1, tm, D), tok),
            pl.BlockSpec((1, tm, W), tok),
            pl.BlockSpec((1, tm, W), tok),
            _const_spec(w_out.shape),
            _const_spec((1, D)),
            _const_spec(w_xq.shape),
            _const_spec((1, xd)),
            pl.BlockSpec((1, M, D), lambda b, i: (b, 0, 0)),
            pl.BlockSpec((1, M, D), lambda b, i: (b, 0, 0)),
            _const_spec(w_xo.shape),
        ],
        out_specs=pl.BlockSpec((1, tm, D), tok),
        out_shape=jax.ShapeDtypeStruct((B, T, D), _F32),
        scratch_shapes=[pltpu.VMEM(w.shape, _BF16) for w in (w_out, w_xq, w_xo)],
        compiler_params=_params(("arbitrary", "arbitrary")),
        name="mix_xattn",
    )(x, ret, fox, w_out, g_xattn, w_xq, g_xq, k, v, w_xo)


def _ffn_chunks(d_ff, n_chunks):
    tiles = -(-d_ff // _MXU_TILE)
    bounds = [min(d_ff, _MXU_TILE * ((tiles * c) // n_chunks)) for c in range(n_chunks)] + [d_ff]
    return [(lo, hi) for lo, hi in zip(bounds[:-1], bounds[1:]) if hi > lo]


def _ffn_body(h_ref, g_ref, wg32_ref, wu32_ref, wd32_ref, o_ref, wg_ref, wu_ref, wd_ref,
              *, n_cast, ck, chunks):
    step = pl.program_id(0)
    for c in range(n_cast):
        @pl.when(step == c)
        def _(c=c):
            wg_ref[:, c * ck:(c + 1) * ck] = wg32_ref[...].astype(_BF16)
            wu_ref[:, c * ck:(c + 1) * ck] = wu32_ref[...].astype(_BF16)
            wd_ref[c * ck:(c + 1) * ck, :] = wd32_ref[...].astype(_BF16)

    @pl.when(step >= n_cast)
    def _():
        h = h_ref[0]
        hn, inv_rms = _rms_split(h, g_ref[...])
        acc = h
        for lo, hi in chunks:
            sl = slice(lo, hi)
            gate = jnp.dot(hn, wg_ref[:, sl], preferred_element_type=_F32) * inv_rms
            up = jnp.dot(hn, wu_ref[:, sl], preferred_element_type=_F32) * inv_rms
            a = (gate * jax.nn.sigmoid(gate) * up).astype(_BF16)
            acc = acc + jnp.dot(a, wd_ref[sl, :], preferred_element_type=_F32)
        o_ref[0] = acc


def _ffn(h, g_ffn, w_gate, w_up, w_down, *, tm, n_split):
    B, T, D = h.shape
    F = w_gate.shape[1]
    ck = _MXU_TILE
    assert F % ck == 0 and T % tm == 0
    n_cast = F // ck
    n_tiles = B * T // tm
    tile = lambda s: (jnp.maximum(s - n_cast, 0), 0, 0)
    chunk = lambda s: jnp.minimum(s, n_cast - 1)
    tok = pl.BlockSpec((1, tm, D), tile)
    out = pl.pallas_call(
        functools.partial(_ffn_body, n_cast=n_cast, ck=ck, chunks=_ffn_chunks(F, n_split)),
        grid=(n_cast + n_tiles,),
        in_specs=[tok, _const_spec((1, D)),
                  pl.BlockSpec((D, ck), lambda s: (0, chunk(s))),
                  pl.BlockSpec((D, ck), lambda s: (0, chunk(s))),
                  pl.BlockSpec((ck, D), lambda s: (chunk(s), 0))],
        out_specs=tok,
        out_shape=jax.ShapeDtypeStruct((n_tiles, tm, D), _F32),
        scratch_shapes=[pltpu.VMEM((D, F), _BF16), pltpu.VMEM((D, F), _BF16), pltpu.VMEM((F, D), _BF16)],
        compiler_params=_params(("arbitrary",)),
        name="ffn",
    )(h.reshape(n_tiles, tm, D), g_ffn, w_gate, w_up, w_down)
    return out.reshape(B, T, D)


def _rope_tables(T):
    half = _HEAD_DIM // 2
    inv_freq = (_ROPE_BASE ** (-np.arange(0, _HEAD_DIM, 2, dtype=np.float32) / _HEAD_DIM)).astype(np.float32)
    ang = (np.arange(T, dtype=np.float32)[:, None] * inv_freq[None, :]).astype(np.float32).astype(np.float64)
    cos, sin = np.cos(ang), np.sin(ang)
    reps = _LANES // _HEAD_DIM
    cos_t = np.tile(np.concatenate([cos, cos], axis=1), (1, reps))
    sin_t = np.tile(np.concatenate([-sin, sin], axis=1), (1, reps))
    return jnp.asarray(cos_t, _F32), jnp.asarray(sin_t, _F32)


def _retention_tables(tb):
    log_g = np.log(1.0 - 2.0 ** (-5.0 - np.arange(_N_HEADS, dtype=np.float64)))
    idx = np.arange(tb, dtype=np.float64)
    dist = np.abs(idx[:, None] - idx[None, :])
    chunk = np.arange(tb) // _RET_CHUNK
    visible = chunk[None, :] <= chunk[:, None]
    dmask = np.where(visible[None], np.exp(log_g[:, None, None] * dist[None]), 0.0)
    qdec = np.repeat(np.exp(log_g[None, :] * (idx[:, None] + 1.0)), _HEAD_DIM, axis=1)
    kdec = np.repeat(np.exp(log_g[None, :] * (tb - 1.0 - idx[:, None])), _HEAD_DIM, axis=1)
    heads_per_group = _RET_GROUP // _HEAD_DIM
    head_of = np.arange(_RET_GROUP) // _HEAD_DIM
    bd = (head_of[:, None] == head_of[None, :]).astype(np.float64)
    step_decay = np.exp(log_g * tb).reshape(-1, heads_per_group)
    sdec = bd[None] * np.repeat(step_decay, _HEAD_DIM, axis=1)[:, None, :]
    f = lambda a: jnp.asarray(a, _F32)
    return f(dmask), f(qdec), f(kdec), f(sdec), f(bd), jnp.asarray(bd / _HEAD_DIM, _BF16)


def _pad_lanes(a):
    return jnp.pad(a, [(0, 0)] * (a.ndim - 1) + [(0, _LANES - a.shape[-1])])


def kernel(x, mem, g_mix, w_in, b_forget, g_ret_out, g_fox_q, g_fox_k, w_out, g_xattn, w_xq, w_xkv,
           g_mem, g_xq, g_xk, w_xo, g_ffn, w_gate, w_up, w_down):
    B, T, D = x.shape
    width = _N_HEADS * _HEAD_DIM
    tb, tq, tm_mix, tm_ffn = 256, 512, 1024, 512
    cos_t, sin_t = _rope_tables(T)
    ret_tables = _retention_tables(tb)
    tri = jnp.asarray(np.tril(np.ones((tq, tq), np.float32)), _BF16)
    row = lambda a: a.reshape(1, -1).astype(_F32)

    h = x
    for l in range(w_in.shape[0]):
        b_ff = _pad_lanes(row(b_forget[l]))
        gq = jnp.tile(row(g_fox_q[l]), (1, _N_HEADS)) * (_LOG2E * _HEAD_DIM ** -0.5)
        gk = jnp.tile(row(g_fox_k[l]), (1, _N_HEADS))
        bound = (_HEAD_DIM ** 0.5 * _NORM_ROUNDING_SLACK) * jnp.max(jnp.abs(g_fox_q[l])) * jnp.max(
            jnp.abs(g_fox_k[l]))
        use_shift = bound <= _MAX_FIXED_SHIFT
        shift = jnp.where(use_shift, jnp.ceil(bound * (4.0 * _LOG2E)) * 0.25, 0.0).astype(_F32)
        ret, qa, ka, va, fend = _in_proj(h, row(g_mix[l]), w_in[l].T, b_ff, cos_t, sin_t, gq, gk,
                                         jnp.full((1, _LANES), shift), tri, ret_tables,
                                         row(g_ret_out[l]), tm=tq, tb=tb)
        jlo = _first_live_block(fend[:, :, 0, :_N_HEADS])
        fox = lax.cond(use_shift,
                       lambda jlo, qa, ka, va: _fox_shifted(jlo, qa, ka, va, tq=tq, hg=4, unroll=4),
                       lambda jlo, qa, ka, va: _fox_online(qa, ka, va, tq=tq, hg=4), jlo, qa, ka, va)
        k, v = _mem_kv(mem, row(g_mem[l]), w_xkv[l], row(g_xk[l]))
        h = _mix_xattn(h, ret, fox, w_out[l], row(g_xattn[l]), w_xq[l], row(g_xq[l]), k, v, w_xo[l],
                       tm=tm_mix, sub=512)
        h = _ffn(h, row(g_ffn[l]), w_gate[l], w_up[l], w_down[l], tm=tm_ffn, n_split=2)
    return h
```

```python
import functools

import numpy as np
import jax
import jax.numpy as jnp
from jax import lax
from jax.experimental import pallas as pl
from jax.experimental.pallas import tpu as pltpu

_BF16 = jnp.bfloat16
_F32 = jnp.float32

_EPS = 1e-6
_NEG_INF = -1e30
_ROPE_BASE = 10000.0
_HEAD_DIM = 64
_N_HEADS = 8
_RET_CHUNK = 64
_RET_GROUP = 256
_N_XHEADS = 4
_LANES = 128
_MXU_TILE = 256
_VMEM_LIMIT = 56 * 1024 * 1024

_AUG_QF = 64
_AUG_KF = 88
_AUG_SHIFT = 112
_AUG_ONE = 64

_LOG2E = 1.4426950408889634
_MAX_FIXED_SHIFT = 32.0
_DIAG_PER_TRIP = 8
_UNDERFLOW_LOG2 = -152.0
_NORM_ROUNDING_SLACK = 1.01


def _params(sem):
    return pltpu.CompilerParams(dimension_semantics=sem, vmem_limit_bytes=_VMEM_LIMIT)


def _const_spec(shape):
    nd = len(shape)
    return pl.BlockSpec(shape, lambda *_: (0,) * nd, pipeline_mode=pl.Buffered(1))


def _rms(x, g):
    return x * lax.rsqrt(jnp.mean(x * x, axis=-1, keepdims=True) + _EPS) * g


def _rms_split(x, g):
    return (x * g).astype(_BF16), lax.rsqrt(jnp.mean(x * x, axis=-1, keepdims=True) + _EPS)


def _split3(v):
    hi = v.astype(_BF16).astype(_F32)
    r = v - hi
    mid = r.astype(_BF16).astype(_F32)
    return hi, mid, r - mid


def _in_proj_body(x_ref, g_ref, w32_ref, bf_ref, cos_ref, sin_ref, gq_ref, gk_ref, shift_ref,
                  tri_ref, avg_ref, dmask_ref, qdec_ref, kdec_ref, sdec_ref, bd_ref, gret_ref,
                  ret_ref, qa_ref, ka_ref, va_ref, fend_ref,
                  w_ref, wff_ref, carry_ref, rq_ref, rk_ref, rv_ref, gate_ref, state_ref, *, tm, tb, width):
    i = pl.program_id(1)
    n_main = 7 * width

    @pl.when((pl.program_id(0) == 0) & (i == 0))
    def _():
        for j in range(n_main // width):
            w_ref[:, j * width:(j + 1) * width] = jnp.transpose(
                w32_ref[j * width:(j + 1) * width, :]).astype(_BF16)
        tail = jnp.concatenate([w32_ref[n_main:, :],
                                jnp.zeros((_LANES - _N_HEADS, w32_ref.shape[1]), _F32)], axis=0)
        wff_ref[...] = jnp.transpose(tail).astype(_BF16)

    @pl.when(i == 0)
    def _():
        state_ref[...] = jnp.zeros_like(state_ref)
        carry_ref[...] = jnp.zeros_like(carry_ref)
    hb, inv_rms = _rms_split(x_ref[0], g_ref[...])

    def proj(j):
        return jnp.dot(hb, w_ref[:, j * width:(j + 1) * width], preferred_element_type=_F32) * inv_rms

    lane = lax.broadcasted_iota(jnp.int32, (tm, _LANES), 1)
    low = lane < _HEAD_DIM
    first_half = (lane & (_HEAD_DIM // 2)) == 0
    n_pairs = width // _LANES

    cos = cos_ref[...]
    sin = sin_ref[...]
    for j, out_ref, scale in ((0, rq_ref, _HEAD_DIM ** -0.5), (1, rk_ref, None)):
        y = proj(j)
        for c in range(n_pairs):
            blk = y[:, c * _LANES:(c + 1) * _LANES]
            swapped = jnp.where(first_half, pltpu.roll(blk, _LANES - _HEAD_DIM // 2, 1),
                                pltpu.roll(blk, _HEAD_DIM // 2, 1))
            r = blk * cos + swapped * sin
            if scale is not None:
                r = r * scale
            out_ref[:, c * _LANES:(c + 1) * _LANES] = r.astype(_BF16)
    rv_ref[...] = proj(2).astype(_BF16)
    gate = proj(3)
    gate_ref[...] = (gate * jax.nn.sigmoid(gate)).astype(_BF16)


    half = tm // 2
    z = jnp.concatenate([jnp.dot(hb[:half], wff_ref[...], preferred_element_type=_F32),
                         jnp.dot(hb[half:], wff_ref[...], preferred_element_type=_F32)],
                        axis=0) * inv_rms + bf_ref[...]
    logf = (jnp.minimum(z, 0.0) - jnp.log(1.0 + jnp.exp(-jnp.abs(z)))) * _LOG2E
    hi, mid, lo = _split3(logf)
    tri = tri_ref[...]
    split = jnp.concatenate([hi, mid, lo], axis=1).astype(_BF16)
    parts = jnp.concatenate(
        [jnp.dot(tri[:half, :half], split[:half], preferred_element_type=_F32),
         jnp.dot(tri[half:], split, preferred_element_type=_F32)], axis=0)
    csum = parts[:, :_LANES] + parts[:, _LANES:2 * _LANES] + parts[:, 2 * _LANES:]

    fcum = csum + carry_ref[0:1, :]
    carry_ref[...] = jnp.broadcast_to(fcum[tm - 1:tm, :], carry_ref.shape)
    fend_ref[0, 0] = carry_ref[...]

    def head_rms(y, g_row):
        out = []
        for g0 in range(0, width, _RET_GROUP):
            blk = y[:, g0:g0 + _RET_GROUP]
            ms = jnp.dot((blk * blk).astype(_BF16), avg_ref[...], preferred_element_type=_F32)
            out.append(blk * lax.rsqrt(ms + _EPS) * g_row[:, g0:g0 + _RET_GROUP])
        return out

    qn = head_rms(proj(4), gq_ref[...])
    kn = head_rms(proj(5), gk_ref[...])
    fv = proj(6)

    fh, fm, fl = _split3(fcum)
    n_parts = 3 * _N_HEADS
    packed = jnp.where(lane < _N_HEADS, fh, jnp.where(
        lane < 2 * _N_HEADS, pltpu.roll(fm, _N_HEADS, 1), pltpu.roll(fl, 2 * _N_HEADS, 1)))
    in_qf = (lane >= _AUG_QF) & (lane < _AUG_QF + n_parts)
    in_kf = (lane >= _AUG_KF) & (lane < _AUG_KF + n_parts)
    shared = jnp.where(in_qf, pltpu.roll(packed, _AUG_QF, 1), -pltpu.roll(packed, _AUG_KF, 1))
    q_bias = jnp.where(in_qf, shared, jnp.where(in_kf, 1.0, jnp.where(
        lane == _AUG_SHIFT, -shift_ref[...], 0.0)))
    v_const = jnp.where(lane == _AUG_ONE, 1.0, 0.0)
    for h in range(_N_HEADS):
        own = (lane & (_N_HEADS - 1)) == h
        k_bias = jnp.where(in_kf & own, shared,
                           jnp.where((in_qf & own) | (lane == _AUG_SHIFT), 1.0, 0.0))
        g0, c0 = divmod(h * _HEAD_DIM, _RET_GROUP)
        c0 = (c0 // _LANES) * _LANES
        qb = qn[g0][:, c0:c0 + _LANES]
        kb = kn[g0][:, c0:c0 + _LANES]
        vb = fv[:, (h // 2) * _LANES:(h // 2 + 1) * _LANES]
        if h % 2 == 1:
            qb = pltpu.roll(qb, _HEAD_DIM, 1)
            kb = pltpu.roll(kb, _HEAD_DIM, 1)
            vb = pltpu.roll(vb, _HEAD_DIM, 1)
        qa_ref[0, h] = jnp.where(low, qb, q_bias).astype(_BF16)
        ka_ref[0, h] = jnp.where(low, kb, k_bias).astype(_BF16)
        va_ref[0, h] = jnp.where(low, vb, v_const).astype(_BF16)

    for r0 in range(0, tm, tb):
        _retention_block(rq_ref, rk_ref, rv_ref, gate_ref, slice(r0, r0 + tb), dmask_ref, qdec_ref, kdec_ref,
                         sdec_ref, bd_ref, avg_ref, gret_ref, state_ref, ret_ref)


def _in_proj(x, g_mix, w_in_t, b_ff, cos_t, sin_t, gq, gk, shift, tri, ret_tables, g_ret, *, tm, tb):
    B, T, D = x.shape
    width = _N_HEADS * _HEAD_DIM
    assert w_in_t.shape == (7 * width + _N_HEADS, D) and tm % tb == 0
    dmask, qdec, kdec, sdec, bd, avg = ret_tables
    tok = lambda b, i: (b, i, 0)
    head = lambda b, i: (b, 0, i, 0)
    bf_tok = jax.ShapeDtypeStruct((B, T, width), _BF16)
    bf_head = jax.ShapeDtypeStruct((B, _N_HEADS, T, _LANES), _BF16)
    return pl.pallas_call(
        functools.partial(_in_proj_body, tm=tm, tb=tb, width=width),
        grid=(B, T // tm),
        in_specs=[
            pl.BlockSpec((1, tm, D), tok),
            _const_spec((1, D)),
            _const_spec(w_in_t.shape),
            _const_spec((1, _LANES)),
            pl.BlockSpec((tm, _LANES), lambda b, i: (i, 0)),
            pl.BlockSpec((tm, _LANES), lambda b, i: (i, 0)),
            _const_spec((1, width)),
            _const_spec((1, width)),
            _const_spec((1, _LANES)),
            _const_spec((tm, tm)),
            _const_spec(avg.shape),
            _const_spec(dmask.shape),
            _const_spec(qdec.shape),
            _const_spec(kdec.shape),
            _const_spec(sdec.shape),
            _const_spec(bd.shape),
            _const_spec(g_ret.shape),
        ],
        out_specs=[pl.BlockSpec((1, tm, width), tok)]
        + [pl.BlockSpec((1, _N_HEADS, tm, _LANES), head)] * 3
        + [pl.BlockSpec((1, 1, 8, _LANES), lambda b, i: (b, i, 0, 0))],
        out_shape=[bf_tok] + [bf_head] * 3 + [jax.ShapeDtypeStruct((B, T // tm, 8, _LANES), _F32)],
        scratch_shapes=[pltpu.VMEM((D, 7 * width), _BF16), pltpu.VMEM((D, _LANES), _BF16),
                        pltpu.VMEM((8, _LANES), _F32)]
        + [pltpu.VMEM((tm, width), _BF16)] * 4 + [pltpu.VMEM(sdec.shape, _F32)],
        compiler_params=_params(("arbitrary", "arbitrary")),
        name="in_proj",
    )(x, g_mix, w_in_t, b_ff, cos_t, sin_t, gq, gk, shift, tri, avg, dmask, qdec, kdec, sdec, bd, g_ret)


def _retention_block(rq_ref, rk_ref, rv_ref, gate_ref, rows, dmask_ref, qdec_ref, kdec_ref, sdec_ref, bd_ref,
                     avg_ref, g_ref, state_ref, o_ref):
    tb = rows.stop - rows.start
    nt = (((1,), (1,)), ((), ()))
    tn = (((0,), (0,)), ((), ()))
    gw = state_ref.shape[1]
    lane = lax.broadcasted_iota(jnp.int32, (1, _LANES), 1)
    low = lax.broadcasted_iota(jnp.int32, (tb, _LANES), 1) < _HEAD_DIM
    head_lanes = [jnp.where(lane < _HEAD_DIM, 1.0, 0.0).astype(_BF16),
                  jnp.where(lane < _HEAD_DIM, 0.0, 1.0).astype(_BF16)]
    n_groups = state_ref.shape[0]
    groups = [slice(g * gw, (g + 1) * gw) for g in range(n_groups)]
    scores = []
    for h in range(_N_HEADS):
        ps = slice((h // 2) * _LANES, (h // 2 + 1) * _LANES)
        scores.append(lax.dot_general(rq_ref[rows, ps] * head_lanes[h % 2], rk_ref[rows, ps], nt,
                                      preferred_element_type=_F32))
    inter = []
    for g, gs in enumerate(groups):
        state = state_ref[g]
        inter.append(jnp.dot(rq_ref[rows, gs], state.astype(_BF16),
                             preferred_element_type=_F32) * qdec_ref[:, gs])
        kd = (rk_ref[rows, gs].astype(_F32) * kdec_ref[:, gs]).astype(_BF16)
        state_ref[g] = state * sdec_ref[g] + lax.dot_general(
            kd, rv_ref[rows, gs], tn, preferred_element_type=_F32) * bd_ref[...]
    intra = []
    for h in range(_N_HEADS):
        ps = slice((h // 2) * _LANES, (h // 2 + 1) * _LANES)
        intra.append(jnp.dot((scores[h] * dmask_ref[h]).astype(_BF16), rv_ref[rows, ps],
                             preferred_element_type=_F32))
    outs = []
    for g, gs in enumerate(groups):
        pairs = [jnp.where(low, intra[2 * c], intra[2 * c + 1])
                 for c in range(g * gw // _LANES, (g + 1) * gw // _LANES)]
        outs.append(jnp.concatenate(pairs, axis=1) + inter[g])
    mus = [jnp.dot(o.astype(_BF16), avg_ref[...], preferred_element_type=_F32) for o in outs]
    cents = [o - mu for o, mu in zip(outs, mus)]
    vars_ = [jnp.dot((oc * oc).astype(_BF16), avg_ref[...], preferred_element_type=_F32) for oc in cents]
    for gs, oc, var in zip(groups, cents, vars_):
        y = oc * lax.rsqrt(var + _EPS) * g_ref[:, gs]
        o_ref[0, rows, gs] = (y * gate_ref[rows, gs].astype(_F32)).astype(_BF16)


def _causal_mask(tq):
    row = lax.broadcasted_iota(jnp.int32, (tq, tq), 0)
    col = lax.broadcasted_iota(jnp.int32, (tq, tq), 1)
    return row >= col


def _fox_finish(acc_refs, o_ref, tq):
    lane = lax.broadcasted_iota(jnp.int32, (tq, _LANES), 1)
    for c in range(len(acc_refs) // 2):
        pair = []
        for hh in (2 * c, 2 * c + 1):
            acc = acc_refs[hh][...]
            pair.append(acc / acc[:, _AUG_ONE:_AUG_ONE + 1])
        o_ref[0, :, c * _LANES:(c + 1) * _LANES] = jnp.where(
            lane < _HEAD_DIM, pair[0], pltpu.roll(pair[1], _HEAD_DIM, 1)).astype(_BF16)


def _fox_shifted_body(jlo_ref, qa_ref, ka_ref, va_ref, o_ref, acc_ref, it_h, it_i, it_j, it_slot, it_keep,
                      *, tq, nq, unroll):
    n_pair = qa_ref.shape[1]
    base = (pl.program_id(0) * pl.num_programs(1) + pl.program_id(1)) * (n_pair * nq)
    nt = (((1,), (1,)), ((), ()))
    half = tq // 2
    diff_top = (lax.broadcasted_iota(jnp.int32, (half, half), 1)
                - lax.broadcasted_iota(jnp.int32, (half, half), 0))
    diff_bot = (lax.broadcasted_iota(jnp.int32, (half, tq), 1)
                - lax.broadcasted_iota(jnp.int32, (half, tq), 0))

    def interleave(n_chains, qk, finish):
        s_prev = qk(0)
        for u in range(1, n_chains):
            s_next = qk(u)
            finish(u - 1, s_prev)
            s_prev = s_next
        finish(n_chains - 1, s_prev)

    n = jnp.int32(0)
    for hh in range(n_pair):
        for i in range(1, nq):
            lo = jlo_ref[base + hh * nq + i]

            def add(j, n, hh=hh, i=i, lo=lo):
                it_h[n] = jnp.int32(hh)
                it_i[n] = jnp.int32(i)
                it_j[n] = j
                it_slot[n] = jnp.int32(hh * nq + i)
                it_keep[n] = jnp.where(j == lo, 0, 1)
                return n + 1

            n = lax.fori_loop(lo, i, add, n)
    n_trips = (n + (unroll - 1)) // unroll

    def pad(m, carry):
        it_h[m] = jnp.int32(0)
        it_i[m] = jnp.int32(0)
        it_j[m] = jnp.int32(0)
        it_slot[m] = jnp.int32(n_pair * nq)
        it_keep[m] = jnp.int32(0)
        return carry

    lax.fori_loop(n, n_trips * unroll, pad, 0)

    def trip(t, carry):
        items = [tuple(ref[t * unroll + u] for ref in (it_h, it_i, it_j, it_slot, it_keep))
                 for u in range(unroll)]

        def qk(u):
            h, i, j = items[u][:3]
            return lax.dot_general(qa_ref[0, h, pl.ds(pl.multiple_of(i * tq, tq), tq), :],
                                   ka_ref[0, h, pl.ds(pl.multiple_of(j * tq, tq), tq), :], nt,
                                   preferred_element_type=_F32)

        def finish(u, s):
            h, _, j, slot, keep = items[u]
            pv = jnp.dot(jnp.exp2(s).astype(_BF16), va_ref[0, h, pl.ds(pl.multiple_of(j * tq, tq), tq), :],
                         preferred_element_type=_F32)
            acc_ref[slot] = jnp.where(keep > 0, acc_ref[slot], 0.0) + pv

        interleave(unroll, qk, finish)
        return carry

    lax.fori_loop(0, n_trips, trip, 0)

    for hh in range(n_pair):
        def diag_trip(t, carry, hh=hh):
            def rows(u):
                i, bottom = _DIAG_PER_TRIP * t + u // 2, u % 2
                return i, bottom, pl.multiple_of(i * tq + bottom * half, half)

            def qk(u):
                i, bottom, r0 = rows(u)
                k0 = pl.multiple_of(i * tq, tq)
                width = tq if bottom else half
                return lax.dot_general(qa_ref[0, hh, pl.ds(r0, half), :],
                                       ka_ref[0, hh, pl.ds(k0, width), :], nt, preferred_element_type=_F32)

            def finish(u, s):
                i, bottom, r0 = rows(u)
                k0 = pl.multiple_of(i * tq, tq)
                width = tq if bottom else half
                live = (diff_bot <= half) if bottom else (diff_top <= 0)
                p = jnp.exp2(jnp.where(live, s, _NEG_INF)).astype(_BF16)
                has_off_diagonal = jlo_ref[base + hh * nq + i] < i
                acc = jnp.where(has_off_diagonal,
                                acc_ref[hh * nq + i, bottom * half:(bottom + 1) * half, :], 0.0) + jnp.dot(
                    p, va_ref[0, hh, pl.ds(k0, width), :], preferred_element_type=_F32)
                o = acc / acc[:, _AUG_ONE:_AUG_ONE + 1]
                lanes = slice((hh % 2) * _HEAD_DIM, (hh % 2 + 1) * _HEAD_DIM)
                if hh % 2 == 1:
                    o = pltpu.roll(o, _HEAD_DIM, 1)
                o_ref[0, pl.ds(r0, half), hh * _HEAD_DIM:(hh + 1) * _HEAD_DIM] = o[:, lanes].astype(_BF16)

            interleave(2 * _DIAG_PER_TRIP, qk, finish)
            return carry

        lax.fori_loop(0, nq // _DIAG_PER_TRIP, diag_trip, 0)


def _fox_online_body(qa_ref, ka_ref, va_ref, o_ref, *scratch, tq):
    qi = pl.program_id(2)
    nt = (((1,), (1,)), ((), ()))
    n_heads = qa_ref.shape[1]
    m_refs, acc_refs = scratch[:n_heads], scratch[n_heads:]
    for hh in range(n_heads):
        m_refs[hh][...] = jnp.full_like(m_refs[hh], _NEG_INF)
        acc_refs[hh][...] = jnp.zeros_like(acc_refs[hh])

    def step(j, masked):
        start = pl.multiple_of(j * tq, tq)
        logits = [lax.dot_general(qa_ref[0, hh], ka_ref[0, hh, pl.ds(start, tq), :], nt,
                                  preferred_element_type=_F32) for hh in range(n_heads)]
        for hh in range(n_heads):
            m_ref, acc_ref = m_refs[hh], acc_refs[hh]
            s = logits[hh]
            if masked:
                s = jnp.where(_causal_mask(tq), s, _NEG_INF)
            m_old = m_ref[...]
            m_new = jnp.maximum(m_old, jnp.max(s, axis=-1, keepdims=True))
            p = jnp.exp2(s - m_new[:, 0:1])
            acc_ref[...] = jnp.exp2(m_old - m_new) * acc_ref[...] + jnp.dot(
                p.astype(_BF16), va_ref[0, hh, pl.ds(start, tq), :], preferred_element_type=_F32)
            m_ref[...] = m_new

    def off_diag(j, carry):
        step(j, False)
        return carry

    lax.fori_loop(0, qi, off_diag, 0)
    step(qi, True)
    _fox_finish(acc_refs, o_ref, tq)


def _fox_shifted(jlo, qa, ka, va, *, tq, hg, unroll):
    B, H, T, L = qa.shape
    nq = T // tq
    assert nq % _DIAG_PER_TRIP == 0 and tq % (2 * _MXU_TILE) == 0 and hg % 2 == 0 and H % hg == 0
    max_items = hg * nq * (nq - 1) // 2 + unroll
    blk = pl.BlockSpec((1, hg, T, L), lambda b, p, jlo_ref: (b, p, 0, 0))
    return pl.pallas_call(
        functools.partial(_fox_shifted_body, tq=tq, nq=nq, unroll=unroll),
        grid_spec=pltpu.PrefetchScalarGridSpec(
            num_scalar_prefetch=1,
            grid=(B, H // hg),
            in_specs=[blk, blk, blk],
            out_specs=pl.BlockSpec((1, T, hg * _HEAD_DIM), lambda b, p, jlo_ref: (b, 0, p)),
            scratch_shapes=[pltpu.VMEM((hg * nq + 1, tq, L), _F32)]
            + [pltpu.SMEM((max_items,), jnp.int32)] * 5,
        ),
        out_shape=jax.ShapeDtypeStruct((B, T, H * _HEAD_DIM), _BF16),
        compiler_params=_params(("arbitrary", "arbitrary")),
        name="fox_shifted",
    )(jlo, qa, ka, va)


def _fox_online(qa, ka, va, *, tq, hg):
    B, H, T, L = qa.shape
    return pl.pallas_call(
        functools.partial(_fox_online_body, tq=tq),
        grid=(B, H // hg, T // tq),
        in_specs=[
            pl.BlockSpec((1, hg, tq, L), lambda b, p, i: (b, p, i, 0)),
            pl.BlockSpec((1, hg, T, L), lambda b, p, i: (b, p, 0, 0)),
            pl.BlockSpec((1, hg, T, L), lambda b, p, i: (b, p, 0, 0)),
        ],
        out_specs=pl.BlockSpec((1, tq, hg * _HEAD_DIM), lambda b, p, i: (b, i, p)),
        out_shape=jax.ShapeDtypeStruct((B, T, H * _HEAD_DIM), _BF16),
        scratch_shapes=[pltpu.VMEM((tq, L), _F32)] * (2 * hg),
        compiler_params=_params(("arbitrary", "arbitrary", "arbitrary")),
        name="fox_online",
    )(qa, ka, va)


def _first_live_block(fend):
    B, nq, H = fend.shape
    f = jnp.transpose(fend, (0, 2, 1))
    top = jnp.concatenate([jnp.zeros((B, H, 1), _F32), f[:, :, :-1]], axis=-1)
    dead = (top[:, :, :, None] - f[:, :, None, :]) <= _UNDERFLOW_LOG2
    j = jnp.arange(nq, dtype=jnp.int32)
    before = j[None, :] < j[:, None]
    first_live = jnp.min(jnp.where(dead & before, nq, j), axis=-1)
    return first_live.astype(jnp.int32).reshape(-1)


def _mem_kv_body(mem_ref, gm_ref, w32_ref, gk_ref, k_ref, v_ref, w_ref, *, d_model, xd):
    @pl.when(pl.program_id(0) == 0)
    def _():
        w_ref[...] = w32_ref[...].astype(_BF16)

    mn = _rms(mem_ref[0], gm_ref[...]).astype(_BF16)
    kv = jnp.dot(mn, w_ref[...], preferred_element_type=_F32)
    for h in range(_N_XHEADS):
        sl = slice(h * xd, (h + 1) * xd)
        k_ref[0, :, sl] = (_rms(kv[:, sl], gk_ref[...]) * (xd ** -0.5)).astype(_BF16)
    v_ref[0] = kv[:, d_model:].astype(_BF16)


def _mem_kv(mem, g_mem, w_xkv, g_xk):
    B, M, D = mem.shape
    xd = D // _N_XHEADS
    blk = pl.BlockSpec((1, M, D), lambda b: (b, 0, 0))
    out = jax.ShapeDtypeStruct((B, M, D), _BF16)
    return pl.pallas_call(
        functools.partial(_mem_kv_body, d_model=D, xd=xd),
        grid=(B,),
        in_specs=[blk, _const_spec((1, D)), _const_spec(w_xkv.shape), _const_spec((1, xd))],
        out_specs=[blk, blk],
        out_shape=[out, out],
        scratch_shapes=[pltpu.VMEM(w_xkv.shape, _BF16)],
        compiler_params=_params(("arbitrary",)),
        name="mem_kv",
    )(mem, g_mem, w_xkv, g_xk)


def _mix_xattn_body(x_ref, ret_ref, fox_ref, wo32_ref, gx_ref, wq32_ref, gq_ref, k_ref, v_ref, wxo32_ref,
                    o_ref, wo_ref, wq_ref, wxo_ref, *, width, xd, sub):
    @pl.when((pl.program_id(0) == 0) & (pl.program_id(1) == 0))
    def _():
        wo_ref[...] = wo32_ref[...].astype(_BF16)
        wq_ref[...] = wq32_ref[...].astype(_BF16)
        wxo_ref[...] = wxo32_ref[...].astype(_BF16)

    nt = (((1,), (1,)), ((), ()))
    head_slices = [slice(h * xd, (h + 1) * xd) for h in range(_N_XHEADS)]

    def front(rows):
        h1 = (x_ref[0, rows, :]
              + jnp.dot(ret_ref[0, rows, :], wo_ref[:width, :], preferred_element_type=_F32)
              + jnp.dot(fox_ref[0, rows, :], wo_ref[width:, :], preferred_element_type=_F32))
        hn, inv_h = _rms_split(h1, gx_ref[...])
        y = jnp.dot(hn, wq_ref[...], preferred_element_type=_F32)
        logits = []
        for sl in head_slices:
            yh = y[:, sl]
            inv_q = lax.rsqrt(inv_h * inv_h * jnp.mean(yh * yh, axis=-1, keepdims=True) + _EPS)
            logits.append(lax.dot_general((yh * gq_ref[...]).astype(_BF16), k_ref[0, :, sl], nt,
                                          preferred_element_type=_F32) * (inv_q * inv_h * _LOG2E))
        return h1, logits

    def back(rows, h1, logits):
        probs = []
        for lg in logits:
            p = jnp.exp2(lg - jnp.max(lg, axis=-1, keepdims=True))
            probs.append((p / jnp.sum(p, axis=-1, keepdims=True)).astype(_BF16))
        heads = [jnp.dot(p, v_ref[0, :, sl], preferred_element_type=_F32).astype(_BF16)
                 for p, sl in zip(probs, head_slices)]
        o = jnp.concatenate(heads, axis=-1)
        o_ref[0, rows, :] = h1 + jnp.dot(o, wxo_ref[...], preferred_element_type=_F32)

    tm = x_ref.shape[1]
    subs = [slice(r0, r0 + sub) for r0 in range(0, tm, sub)]
    fronts = [front(rows) for rows in subs]
    for rows, (h1, logits) in zip(subs, fronts):
        back(rows, h1, logits)


def _mix_xattn(x, ret, fox, w_out, g_xattn, w_xq, g_xq, k, v, w_xo, *, tm, sub):
    B, T, D = x.shape
    W = ret.shape[-1]
    M = k.shape[1]
    xd = D // _N_XHEADS
    assert tm % sub == 0 and T % tm == 0
    tok = lambda b, i: (b, i, 0)
    return pl.pallas_call(
        functools.partial(_mix_xattn_body, width=W, xd=xd, sub=sub),
        grid=(B, T // tm),
        in_specs=[
            pl.BlockSpec((1, tm, D), tok),
            pl.BlockSpec((1, tm, W), tok),
            pl.BlockSpec((1, tm, W), tok),
            _const_spec(w_out.shape),
            _const_spec((1, D)),
            _const_spec(w_xq.shape),
            _const_spec((1, xd)),
            pl.BlockSpec((1, M, D), lambda b, i: (b, 0, 0)),
            pl.BlockSpec((1, M, D), lambda b, i: (b, 0, 0)),
            _const_spec(w_xo.shape),
        ],
        out_specs=pl.BlockSpec((1, tm, D), tok),
        out_shape=jax.ShapeDtypeStruct((B, T, D), _F32),
        scratch_shapes=[pltpu.VMEM(w.shape, _BF16) for w in (w_out, w_xq, w_xo)],
        compiler_params=_params(("arbitrary", "arbitrary")),
        name="mix_xattn",
    )(x, ret, fox, w_out, g_xattn, w_xq, g_xq, k, v, w_xo)


def _ffn_chunks(d_ff, n_chunks):
    tiles = -(-d_ff // _MXU_TILE)
    bounds = [min(d_ff, _MXU_TILE * ((tiles * c) // n_chunks)) for c in range(n_chunks)] + [d_ff]
    return [(lo, hi) for lo, hi in zip(bounds[:-1], bounds[1:]) if hi > lo]


def _ffn_body(h_ref, g_ref, wg32_ref, wu32_ref, wd32_ref, o_ref, wg_ref, wu_ref, wd_ref, acc0_ref,
              *, n_cast, ck, chunks):
    step = pl.program_id(0)

    def swiglu(hn, inv_rms, sl):
        gate = jnp.dot(hn, wg_ref[:, sl], preferred_element_type=_F32) * inv_rms
        up = jnp.dot(hn, wu_ref[:, sl], preferred_element_type=_F32) * inv_rms
        a = (gate * jax.nn.sigmoid(gate) * up).astype(_BF16)
        return jnp.dot(a, wd_ref[sl, :], preferred_element_type=_F32)

    for c in range(n_cast):
        @pl.when(step == c)
        def _(c=c):
            sl = slice(c * ck, (c + 1) * ck)
            wg_ref[:, sl] = wg32_ref[...].astype(_BF16)
            wu_ref[:, sl] = wu32_ref[...].astype(_BF16)
            wd_ref[sl, :] = wd32_ref[...].astype(_BF16)
            h = h_ref[0]
            part = swiglu(*_rms_split(h, g_ref[...]), sl)
            if c == 0:
                acc0_ref[...] = h + part
            elif c < n_cast - 1:
                acc0_ref[...] += part
            else:
                o_ref[0] = acc0_ref[...] + part

    @pl.when(step >= n_cast)
    def _():
        h = h_ref[0]
        hn, inv_rms = _rms_split(h, g_ref[...])
        acc = h
        for lo, hi in chunks:
            acc = acc + swiglu(hn, inv_rms, slice(lo, hi))
        o_ref[0] = acc


def _ffn(h, g_ffn, w_gate, w_up, w_down, *, tm, n_split):
    B, T, D = h.shape
    F = w_gate.shape[1]
    ck = _MXU_TILE
    assert F % ck == 0 and T % tm == 0
    n_cast = F // ck
    n_tiles = B * T // tm
    tile = lambda s: (jnp.maximum(s - (n_cast - 1), 0), 0, 0)
    chunk = lambda s: jnp.minimum(s, n_cast - 1)
    tok = pl.BlockSpec((1, tm, D), tile)
    out = pl.pallas_call(
        functools.partial(_ffn_body, n_cast=n_cast, ck=ck, chunks=_ffn_chunks(F, n_split)),
        grid=(n_cast - 1 + n_tiles,),
        in_specs=[tok, _const_spec((1, D)),
                  pl.BlockSpec((D, ck), lambda s: (0, chunk(s))),
                  pl.BlockSpec((D, ck), lambda s: (0, chunk(s))),
                  pl.BlockSpec((ck, D), lambda s: (chunk(s), 0))],
        out_specs=tok,
        out_shape=jax.ShapeDtypeStruct((n_tiles, tm, D), _F32),
        scratch_shapes=[pltpu.VMEM((D, F), _BF16), pltpu.VMEM((D, F), _BF16), pltpu.VMEM((F, D), _BF16),
                        pltpu.VMEM((tm, D), _F32)],
        compiler_params=_params(("arbitrary",)),
        name="ffn",
    )(h.reshape(n_tiles, tm, D), g_ffn, w_gate, w_up, w_down)
    return out.reshape(B, T, D)


def _rope_tables(T):
    half = _HEAD_DIM // 2
    inv_freq = (_ROPE_BASE ** (-np.arange(0, _HEAD_DIM, 2, dtype=np.float32) / _HEAD_DIM)).astype(np.float32)
    ang = (np.arange(T, dtype=np.float32)[:, None] * inv_freq[None, :]).astype(np.float32).astype(np.float64)
    cos, sin = np.cos(ang), np.sin(ang)
    reps = _LANES // _HEAD_DIM
    cos_t = np.tile(np.concatenate([cos, cos], axis=1), (1, reps))
    sin_t = np.tile(np.concatenate([-sin, sin], axis=1), (1, reps))
    return jnp.asarray(cos_t, _F32), jnp.asarray(sin_t, _F32)


def _retention_tables(tb):
    log_g = np.log(1.0 - 2.0 ** (-5.0 - np.arange(_N_HEADS, dtype=np.float64)))
    idx = np.arange(tb, dtype=np.float64)
    dist = np.abs(idx[:, None] - idx[None, :])
    chunk = np.arange(tb) // _RET_CHUNK
    visible = chunk[None, :] <= chunk[:, None]
    dmask = np.where(visible[None], np.exp(log_g[:, None, None] * dist[None]), 0.0)
    qdec = np.repeat(np.exp(log_g[None, :] * (idx[:, None] + 1.0)), _HEAD_DIM, axis=1)
    kdec = np.repeat(np.exp(log_g[None, :] * (tb - 1.0 - idx[:, None])), _HEAD_DIM, axis=1)
    heads_per_group = _RET_GROUP // _HEAD_DIM
    head_of = np.arange(_RET_GROUP) // _HEAD_DIM
    bd = (head_of[:, None] == head_of[None, :]).astype(np.float64)
    step_decay = np.exp(log_g * tb).reshape(-1, heads_per_group)
    sdec = bd[None] * np.repeat(step_decay, _HEAD_DIM, axis=1)[:, None, :]
    f = lambda a: jnp.asarray(a, _F32)
    return f(dmask), f(qdec), f(kdec), f(sdec), f(bd), jnp.asarray(bd / _HEAD_DIM, _BF16)


def _pad_lanes(a):
    return jnp.pad(a, [(0, 0)] * (a.ndim - 1) + [(0, _LANES - a.shape[-1])])


def kernel(x, mem, g_mix, w_in, b_forget, g_ret_out, g_fox_q, g_fox_k, w_out, g_xattn, w_xq, w_xkv,
           g_mem, g_xq, g_xk, w_xo, g_ffn, w_gate, w_up, w_down):
    B, T, D = x.shape
    width = _N_HEADS * _HEAD_DIM
    tb, tq, tm_mix, tm_ffn = 256, 512, 1024, 512
    cos_t, sin_t = _rope_tables(T)
    ret_tables = _retention_tables(tb)
    tri = jnp.asarray(np.tril(np.ones((tq, tq), np.float32)), _BF16)
    row = lambda a: a.reshape(1, -1).astype(_F32)

    h = x
    for l in range(w_in.shape[0]):
        b_ff = _pad_lanes(row(b_forget[l]))
        gq = jnp.tile(row(g_fox_q[l]), (1, _N_HEADS)) * (_LOG2E * _HEAD_DIM ** -0.5)
        gk = jnp.tile(row(g_fox_k[l]), (1, _N_HEADS))
        bound = (_HEAD_DIM ** 0.5 * _NORM_ROUNDING_SLACK) * jnp.max(jnp.abs(g_fox_q[l])) * jnp.max(
            jnp.abs(g_fox_k[l]))
        use_shift = bound <= _MAX_FIXED_SHIFT
        shift = jnp.where(use_shift, jnp.ceil(bound * (4.0 * _LOG2E)) * 0.25, 0.0).astype(_F32)
        ret, qa, ka, va, fend = _in_proj(h, row(g_mix[l]), w_in[l].T, b_ff, cos_t, sin_t, gq, gk,
                                         jnp.full((1, _LANES), shift), tri, ret_tables,
                                         row(g_ret_out[l]), tm=tq, tb=tb)
        jlo = _first_live_block(fend[:, :, 0, :_N_HEADS])
        fox = lax.cond(use_shift,
                       lambda jlo, qa, ka, va: _fox_shifted(jlo, qa, ka, va, tq=tq, hg=4, unroll=4),
                       lambda jlo, qa, ka, va: _fox_online(qa, ka, va, tq=tq, hg=4), jlo, qa, ka, va)
        k, v = _mem_kv(mem, row(g_mem[l]), w_xkv[l], row(g_xk[l]))
        h = _mix_xattn(h, ret, fox, w_out[l], row(g_xattn[l]), w_xq[l], row(g_xq[l]), k, v, w_xo[l],
                       tm=tm_mix, sub=512)
        h = _ffn(h, row(g_ffn[l]), w_gate[l], w_up[l], w_down[l], tm=tm_ffn, n_split=2)
    return h
```

```python
import functools

import numpy as np
import jax
import jax.numpy as jnp
from jax import lax
from jax.experimental import pallas as pl
from jax.experimental.pallas import tpu as pltpu

_BF16 = jnp.bfloat16
_F32 = jnp.float32

_EPS = 1e-6
_NEG_INF = -1e30
_ROPE_BASE = 10000.0
_HEAD_DIM = 64
_N_HEADS = 8
_RET_CHUNK = 64
_RET_GROUP = 256
_N_XHEADS = 4
_LANES = 128
_MXU_TILE = 256
_VMEM_LIMIT = 56 * 1024 * 1024

_AUG_QF = 64
_AUG_KF = 88
_AUG_SHIFT = 112
_AUG_ONE = 64

_LOG2E = 1.4426950408889634
_MAX_FIXED_SHIFT = 32.0
_UNDERFLOW_LOG2 = -152.0
_NORM_ROUNDING_SLACK = 1.01


def _params(sem):
    return pltpu.CompilerParams(dimension_semantics=sem, vmem_limit_bytes=_VMEM_LIMIT)


def _const_spec(shape):
    nd = len(shape)
    return pl.BlockSpec(shape, lambda *_: (0,) * nd, pipeline_mode=pl.Buffered(1))


def _rms(x, g):
    return x * lax.rsqrt(jnp.mean(x * x, axis=-1, keepdims=True) + _EPS) * g


def _rms_split(x, g):
    return (x * g).astype(_BF16), lax.rsqrt(jnp.mean(x * x, axis=-1, keepdims=True) + _EPS)


def _split3(v):
    hi = v.astype(_BF16).astype(_F32)
    r = v - hi
    mid = r.astype(_BF16).astype(_F32)
    return hi, mid, r - mid


def _in_proj_body(x_ref, g_ref, w32_ref, bf_ref, cos_ref, sin_ref, gq_ref, gk_ref, shift_ref,
                  tri_ref, avg_ref, dmask_ref, qdec_ref, kdec_ref, sdec_ref, bd_ref, gret_ref,
                  ret_ref, qa_ref, ka_ref, va_ref, fend_ref,
                  w_ref, wff_ref, carry_ref, rq_ref, rk_ref, rv_ref, gate_ref, state_ref, *, tm, tb, width):
    i = pl.program_id(1)
    n_main = 7 * width

    @pl.when((pl.program_id(0) == 0) & (i == 0))
    def _():
        for j in range(n_main // width):
            w_ref[:, j * width:(j + 1) * width] = jnp.transpose(
                w32_ref[j * width:(j + 1) * width, :]).astype(_BF16)
        tail = jnp.concatenate([w32_ref[n_main:, :],
                                jnp.zeros((_LANES - _N_HEADS, w32_ref.shape[1]), _F32)], axis=0)
        wff_ref[...] = jnp.transpose(tail).astype(_BF16)

    @pl.when(i == 0)
    def _():
        state_ref[...] = jnp.zeros_like(state_ref)
        carry_ref[...] = jnp.zeros_like(carry_ref)
    hb, inv_rms = _rms_split(x_ref[0], g_ref[...])

    def proj(j):
        return jnp.dot(hb, w_ref[:, j * width:(j + 1) * width], preferred_element_type=_F32) * inv_rms

    lane = lax.broadcasted_iota(jnp.int32, (tm, _LANES), 1)
    low = lane < _HEAD_DIM
    first_half = (lane & (_HEAD_DIM // 2)) == 0
    n_pairs = width // _LANES

    cos = cos_ref[...]
    sin = sin_ref[...]
    for j, out_ref, scale in ((0, rq_ref, _HEAD_DIM ** -0.5), (1, rk_ref, None)):
        y = proj(j)
        for c in range(n_pairs):
            blk = y[:, c * _LANES:(c + 1) * _LANES]
            swapped = jnp.where(first_half, pltpu.roll(blk, _LANES - _HEAD_DIM // 2, 1),
                                pltpu.roll(blk, _HEAD_DIM // 2, 1))
            r = blk * cos + swapped * sin
            if scale is not None:
                r = r * scale
            out_ref[:, c * _LANES:(c + 1) * _LANES] = r.astype(_BF16)
    rv_ref[...] = proj(2).astype(_BF16)
    gate = proj(3)
    gate_ref[...] = (gate * jax.nn.sigmoid(gate)).astype(_BF16)


    half = tm // 2
    z = jnp.concatenate([jnp.dot(hb[:half], wff_ref[...], preferred_element_type=_F32),
                         jnp.dot(hb[half:], wff_ref[...], preferred_element_type=_F32)],
                        axis=0) * inv_rms + bf_ref[...]
    logf = (jnp.minimum(z, 0.0) - jnp.log(1.0 + jnp.exp(-jnp.abs(z)))) * _LOG2E
    hi, mid, lo = _split3(logf)
    tri = tri_ref[...]
    split = jnp.concatenate([hi, mid, lo], axis=1).astype(_BF16)
    parts = jnp.concatenate(
        [jnp.dot(tri[:half, :half], split[:half], preferred_element_type=_F32),
         jnp.dot(tri[half:], split, preferred_element_type=_F32)], axis=0)
    csum = parts[:, :_LANES] + parts[:, _LANES:2 * _LANES] + parts[:, 2 * _LANES:]

    fcum = csum + carry_ref[0:1, :]
    carry_ref[...] = jnp.broadcast_to(fcum[tm - 1:tm, :], carry_ref.shape)
    fend_ref[0, 0] = carry_ref[...]

    def head_rms(y, g_row):
        out = []
        for g0 in range(0, width, _RET_GROUP):
            blk = y[:, g0:g0 + _RET_GROUP]
            ms = jnp.dot((blk * blk).astype(_BF16), avg_ref[...], preferred_element_type=_F32)
            out.append(blk * lax.rsqrt(ms + _EPS) * g_row[:, g0:g0 + _RET_GROUP])
        return out

    qn = head_rms(proj(4), gq_ref[...])
    kn = head_rms(proj(5), gk_ref[...])
    fv = proj(6)

    fh, fm, fl = _split3(fcum)
    n_parts = 3 * _N_HEADS
    packed = jnp.where(lane < _N_HEADS, fh, jnp.where(
        lane < 2 * _N_HEADS, pltpu.roll(fm, _N_HEADS, 1), pltpu.roll(fl, 2 * _N_HEADS, 1)))
    in_qf = (lane >= _AUG_QF) & (lane < _AUG_QF + n_parts)
    in_kf = (lane >= _AUG_KF) & (lane < _AUG_KF + n_parts)
    shared = jnp.where(in_qf, pltpu.roll(packed, _AUG_QF, 1), -pltpu.roll(packed, _AUG_KF, 1))
    q_bias = jnp.where(in_qf, shared, jnp.where(in_kf, 1.0, jnp.where(
        lane == _AUG_SHIFT, -shift_ref[...], 0.0)))
    v_const = jnp.where(lane == _AUG_ONE, 1.0, 0.0)
    for h in range(_N_HEADS):
        own = (lane & (_N_HEADS - 1)) == h
        k_bias = jnp.where(in_kf & own, shared,
                           jnp.where((in_qf & own) | (lane == _AUG_SHIFT), 1.0, 0.0))
        g0, c0 = divmod(h * _HEAD_DIM, _RET_GROUP)
        c0 = (c0 // _LANES) * _LANES
        qb = qn[g0][:, c0:c0 + _LANES]
        kb = kn[g0][:, c0:c0 + _LANES]
        vb = fv[:, (h // 2) * _LANES:(h // 2 + 1) * _LANES]
        if h % 2 == 1:
            qb = pltpu.roll(qb, _HEAD_DIM, 1)
            kb = pltpu.roll(kb, _HEAD_DIM, 1)
            vb = pltpu.roll(vb, _HEAD_DIM, 1)
        qa_ref[0, h] = jnp.where(low, qb, q_bias).astype(_BF16)
        ka_ref[0, h] = jnp.where(low, kb, k_bias).astype(_BF16)
        va_ref[0, h] = jnp.where(low, vb, v_const).astype(_BF16)

    for r0 in range(0, tm, tb):
        _retention_block(rq_ref, rk_ref, rv_ref, gate_ref, slice(r0, r0 + tb), dmask_ref, qdec_ref, kdec_ref,
                         sdec_ref, bd_ref, avg_ref, gret_ref, state_ref, ret_ref)


def _in_proj(x, g_mix, w_in_t, b_ff, cos_t, sin_t, gq, gk, shift, tri, ret_tables, g_ret, *, tm, tb):
    B, T, D = x.shape
    width = _N_HEADS * _HEAD_DIM
    assert w_in_t.shape == (7 * width + _N_HEADS, D) and tm % tb == 0
    dmask, qdec, kdec, sdec, bd, avg = ret_tables
    tok = lambda b, i: (b, i, 0)
    head = lambda b, i: (b, 0, i, 0)
    bf_tok = jax.ShapeDtypeStruct((B, T, width), _BF16)
    bf_head = jax.ShapeDtypeStruct((B, _N_HEADS, T, _LANES), _BF16)
    return pl.pallas_call(
        functools.partial(_in_proj_body, tm=tm, tb=tb, width=width),
        grid=(B, T // tm),
        in_specs=[
            pl.BlockSpec((1, tm, D), tok),
            _const_spec((1, D)),
            _const_spec(w_in_t.shape),
            _const_spec((1, _LANES)),
            pl.BlockSpec((tm, _LANES), lambda b, i: (i, 0)),
            pl.BlockSpec((tm, _LANES), lambda b, i: (i, 0)),
            _const_spec((1, width)),
            _const_spec((1, width)),
            _const_spec((1, _LANES)),
            _const_spec((tm, tm)),
            _const_spec(avg.shape),
            _const_spec(dmask.shape),
            _const_spec(qdec.shape),
            _const_spec(kdec.shape),
            _const_spec(sdec.shape),
            _const_spec(bd.shape),
            _const_spec(g_ret.shape),
        ],
        out_specs=[pl.BlockSpec((1, tm, width), tok)]
        + [pl.BlockSpec((1, _N_HEADS, tm, _LANES), head)] * 3
        + [pl.BlockSpec((1, 1, 8, _LANES), lambda b, i: (b, i, 0, 0))],
        out_shape=[bf_tok] + [bf_head] * 3 + [jax.ShapeDtypeStruct((B, T // tm, 8, _LANES), _F32)],
        scratch_shapes=[pltpu.VMEM((D, 7 * width), _BF16), pltpu.VMEM((D, _LANES), _BF16),
                        pltpu.VMEM((8, _LANES), _F32)]
        + [pltpu.VMEM((tm, width), _BF16)] * 4 + [pltpu.VMEM(sdec.shape, _F32)],
        compiler_params=_params(("arbitrary", "arbitrary")),
        name="in_proj",
    )(x, g_mix, w_in_t, b_ff, cos_t, sin_t, gq, gk, shift, tri, avg, dmask, qdec, kdec, sdec, bd, g_ret)


def _retention_block(rq_ref, rk_ref, rv_ref, gate_ref, rows, dmask_ref, qdec_ref, kdec_ref, sdec_ref, bd_ref,
                     avg_ref, g_ref, state_ref, o_ref):
    tb = rows.stop - rows.start
    nt = (((1,), (1,)), ((), ()))
    tn = (((0,), (0,)), ((), ()))
    gw = state_ref.shape[1]
    lane = lax.broadcasted_iota(jnp.int32, (1, _LANES), 1)
    low = lax.broadcasted_iota(jnp.int32, (tb, _LANES), 1) < _HEAD_DIM
    head_lanes = [jnp.where(lane < _HEAD_DIM, 1.0, 0.0).astype(_BF16),
                  jnp.where(lane < _HEAD_DIM, 0.0, 1.0).astype(_BF16)]
    n_groups = state_ref.shape[0]
    groups = [slice(g * gw, (g + 1) * gw) for g in range(n_groups)]
    scores = []
    for h in range(_N_HEADS):
        ps = slice((h // 2) * _LANES, (h // 2 + 1) * _LANES)
        scores.append(lax.dot_general(rq_ref[rows, ps] * head_lanes[h % 2], rk_ref[rows, ps], nt,
                                      preferred_element_type=_F32))
    inter = []
    for g, gs in enumerate(groups):
        state = state_ref[g]
        inter.append(jnp.dot(rq_ref[rows, gs], state.astype(_BF16),
                             preferred_element_type=_F32) * qdec_ref[:, gs])
        kd = (rk_ref[rows, gs].astype(_F32) * kdec_ref[:, gs]).astype(_BF16)
        state_ref[g] = state * sdec_ref[g] + lax.dot_general(
            kd, rv_ref[rows, gs], tn, preferred_element_type=_F32) * bd_ref[...]
    intra = []
    for h in range(_N_HEADS):
        ps = slice((h // 2) * _LANES, (h // 2 + 1) * _LANES)
        intra.append(jnp.dot((scores[h] * dmask_ref[h]).astype(_BF16), rv_ref[rows, ps],
                             preferred_element_type=_F32))
    outs = []
    for g, gs in enumerate(groups):
        pairs = [jnp.where(low, intra[2 * c], intra[2 * c + 1])
                 for c in range(g * gw // _LANES, (g + 1) * gw // _LANES)]
        outs.append(jnp.concatenate(pairs, axis=1) + inter[g])
    mus = [jnp.dot(o.astype(_BF16), avg_ref[...], preferred_element_type=_F32) for o in outs]
    cents = [o - mu for o, mu in zip(outs, mus)]
    vars_ = [jnp.dot((oc * oc).astype(_BF16), avg_ref[...], preferred_element_type=_F32) for oc in cents]
    for gs, oc, var in zip(groups, cents, vars_):
        y = oc * lax.rsqrt(var + _EPS) * g_ref[:, gs]
        o_ref[0, rows, gs] = (y * gate_ref[rows, gs].astype(_F32)).astype(_BF16)


def _causal_mask(tq):
    row = lax.broadcasted_iota(jnp.int32, (tq, tq), 0)
    col = lax.broadcasted_iota(jnp.int32, (tq, tq), 1)
    return row >= col


def _fox_finish(acc_refs, o_ref, tq):
    lane = lax.broadcasted_iota(jnp.int32, (tq, _LANES), 1)
    for c in range(len(acc_refs) // 2):
        pair = []
        for hh in (2 * c, 2 * c + 1):
            acc = acc_refs[hh][...]
            pair.append(acc / acc[:, _AUG_ONE:_AUG_ONE + 1])
        o_ref[0, :, c * _LANES:(c + 1) * _LANES] = jnp.where(
            lane < _HEAD_DIM, pair[0], pltpu.roll(pair[1], _HEAD_DIM, 1)).astype(_BF16)


def _fox_shifted_body(jlo_ref, qa_ref, ka_ref, va_ref, o_ref, acc_ref, it_h, it_i, it_j, it_slot, it_keep,
                      *, tq, nq, unroll):
    n_pair = qa_ref.shape[1]
    base = (pl.program_id(0) * pl.num_programs(1) + pl.program_id(1)) * (n_pair * nq)
    nt = (((1,), (1,)), ((), ()))
    half = tq // 2
    diff_top = (lax.broadcasted_iota(jnp.int32, (half, half), 1)
                - lax.broadcasted_iota(jnp.int32, (half, half), 0))
    diff_bot = (lax.broadcasted_iota(jnp.int32, (half, tq), 1)
                - lax.broadcasted_iota(jnp.int32, (half, tq), 0))

    def interleave(n_chains, qk, finish):
        s_prev = qk(0)
        for u in range(1, n_chains):
            s_next = qk(u)
            finish(u - 1, s_prev)
            s_prev = s_next
        finish(n_chains - 1, s_prev)

    n = jnp.int32(0)
    for hh in range(n_pair):
        for i in range(1, nq):
            lo = jlo_ref[base + hh * nq + i]

            def add(j, n, hh=hh, i=i, lo=lo):
                it_h[n] = jnp.int32(hh)
                it_i[n] = jnp.int32(i)
                it_j[n] = j
                it_slot[n] = jnp.int32(hh * nq + i)
                it_keep[n] = jnp.where(j == lo, 0, 1)
                return n + 1

            n = lax.fori_loop(lo, i, add, n)
    n_trips = (n + (unroll - 1)) // unroll

    def pad(m, carry):
        it_h[m] = jnp.int32(0)
        it_i[m] = jnp.int32(0)
        it_j[m] = jnp.int32(0)
        it_slot[m] = jnp.int32(n_pair * nq)
        it_keep[m] = jnp.int32(0)
        return carry

    lax.fori_loop(n, n_trips * unroll, pad, 0)

    def trip(t, carry):
        items = [tuple(ref[t * unroll + u] for ref in (it_h, it_i, it_j, it_slot, it_keep))
                 for u in range(unroll)]

        def qk(u):
            h, i, j = items[u][:3]
            return lax.dot_general(qa_ref[0, h, pl.ds(pl.multiple_of(i * tq, tq), tq), :],
                                   ka_ref[0, h, pl.ds(pl.multiple_of(j * tq, tq), tq), :], nt,
                                   preferred_element_type=_F32)

        def finish(u, s):
            h, _, j, slot, keep = items[u]
            pv = jnp.dot(jnp.exp2(s).astype(_BF16), va_ref[0, h, pl.ds(pl.multiple_of(j * tq, tq), tq), :],
                         preferred_element_type=_F32)
            acc_ref[slot] = jnp.where(keep > 0, acc_ref[slot], 0.0) + pv

        interleave(unroll, qk, finish)
        return carry

    lax.fori_loop(0, n_trips, trip, 0)

    chains = [(hh, i, bottom) for hh in range(n_pair) for i in range(nq) for bottom in (0, 1)]

    def diag_qk(u):
        hh, i, bottom = chains[u]
        r0 = i * tq + bottom * half
        width = tq if bottom else half
        return lax.dot_general(qa_ref[0, hh, r0:r0 + half, :], ka_ref[0, hh, i * tq:i * tq + width, :], nt,
                               preferred_element_type=_F32)

    def diag_finish(u, s):
        hh, i, bottom = chains[u]
        r0 = i * tq + bottom * half
        width = tq if bottom else half
        live = (diff_bot <= half) if bottom else (diff_top <= 0)
        p = jnp.exp2(jnp.where(live, s, _NEG_INF)).astype(_BF16)
        pv = jnp.dot(p, va_ref[0, hh, i * tq:i * tq + width, :], preferred_element_type=_F32)
        if i == 0:
            acc = pv
        else:
            has_off_diagonal = jlo_ref[base + hh * nq + i] < i
            acc = jnp.where(has_off_diagonal,
                            acc_ref[hh * nq + i, bottom * half:(bottom + 1) * half, :], 0.0) + pv
        o = acc / acc[:, _AUG_ONE:_AUG_ONE + 1]
        lanes = slice((hh % 2) * _HEAD_DIM, (hh % 2 + 1) * _HEAD_DIM)
        if hh % 2 == 1:
            o = pltpu.roll(o, _HEAD_DIM, 1)
        o_ref[0, r0:r0 + half, hh * _HEAD_DIM:(hh + 1) * _HEAD_DIM] = o[:, lanes].astype(_BF16)

    interleave(len(chains), diag_qk, diag_finish)


def _fox_online_body(qa_ref, ka_ref, va_ref, o_ref, *scratch, tq):
    qi = pl.program_id(2)
    nt = (((1,), (1,)), ((), ()))
    n_heads = qa_ref.shape[1]
    m_refs, acc_refs = scratch[:n_heads], scratch[n_heads:]
    for hh in range(n_heads):
        m_refs[hh][...] = jnp.full_like(m_refs[hh], _NEG_INF)
        acc_refs[hh][...] = jnp.zeros_like(acc_refs[hh])

    def step(j, masked):
        start = pl.multiple_of(j * tq, tq)
        logits = [lax.dot_general(qa_ref[0, hh], ka_ref[0, hh, pl.ds(start, tq), :], nt,
                                  preferred_element_type=_F32) for hh in range(n_heads)]
        for hh in range(n_heads):
            m_ref, acc_ref = m_refs[hh], acc_refs[hh]
            s = logits[hh]
            if masked:
                s = jnp.where(_causal_mask(tq), s, _NEG_INF)
            m_old = m_ref[...]
            m_new = jnp.maximum(m_old, jnp.max(s, axis=-1, keepdims=True))
            p = jnp.exp2(s - m_new[:, 0:1])
            acc_ref[...] = jnp.exp2(m_old - m_new) * acc_ref[...] + jnp.dot(
                p.astype(_BF16), va_ref[0, hh, pl.ds(start, tq), :], preferred_element_type=_F32)
            m_ref[...] = m_new

    def off_diag(j, carry):
        step(j, False)
        return carry

    lax.fori_loop(0, qi, off_diag, 0)
    step(qi, True)
    _fox_finish(acc_refs, o_ref, tq)


def _fox_shifted(jlo, qa, ka, va, *, tq, hg, unroll):
    B, H, T, L = qa.shape
    nq = T // tq
    assert tq % (2 * _MXU_TILE) == 0 and hg % 2 == 0 and H % hg == 0
    max_items = hg * nq * (nq - 1) // 2 + unroll
    blk = pl.BlockSpec((1, hg, T, L), lambda b, p, jlo_ref: (b, p, 0, 0))
    return pl.pallas_call(
        functools.partial(_fox_shifted_body, tq=tq, nq=nq, unroll=unroll),
        grid_spec=pltpu.PrefetchScalarGridSpec(
            num_scalar_prefetch=1,
            grid=(B, H // hg),
            in_specs=[blk, blk, blk],
            out_specs=pl.BlockSpec((1, T, hg * _HEAD_DIM), lambda b, p, jlo_ref: (b, 0, p)),
            scratch_shapes=[pltpu.VMEM((hg * nq + 1, tq, L), _F32)]
            + [pltpu.SMEM((max_items,), jnp.int32)] * 5,
        ),
        out_shape=jax.ShapeDtypeStruct((B, T, H * _HEAD_DIM), _BF16),
        compiler_params=_params(("arbitrary", "arbitrary")),
        name="fox_shifted",
    )(jlo, qa, ka, va)


def _fox_online(qa, ka, va, *, tq, hg):
    B, H, T, L = qa.shape
    return pl.pallas_call(
        functools.partial(_fox_online_body, tq=tq),
        grid=(B, H // hg, T // tq),
        in_specs=[
            pl.BlockSpec((1, hg, tq, L), lambda b, p, i: (b, p, i, 0)),
            pl.BlockSpec((1, hg, T, L), lambda b, p, i: (b, p, 0, 0)),
            pl.BlockSpec((1, hg, T, L), lambda b, p, i: (b, p, 0, 0)),
        ],
        out_specs=pl.BlockSpec((1, tq, hg * _HEAD_DIM), lambda b, p, i: (b, i, p)),
        out_shape=jax.ShapeDtypeStruct((B, T, H * _HEAD_DIM), _BF16),
        scratch_shapes=[pltpu.VMEM((tq, L), _F32)] * (2 * hg),
        compiler_params=_params(("arbitrary", "arbitrary", "arbitrary")),
        name="fox_online",
    )(qa, ka, va)


def _first_live_block(fend):
    B, nq, H = fend.shape
    f = jnp.transpose(fend, (0, 2, 1))
    top = jnp.concatenate([jnp.zeros((B, H, 1), _F32), f[:, :, :-1]], axis=-1)
    dead = (top[:, :, :, None] - f[:, :, None, :]) <= _UNDERFLOW_LOG2
    j = jnp.arange(nq, dtype=jnp.int32)
    before = j[None, :] < j[:, None]
    first_live = jnp.min(jnp.where(dead & before, nq, j), axis=-1)
    return first_live.astype(jnp.int32).reshape(-1)


def _mem_kv_body(mem_ref, gm_ref, w32_ref, gk_ref, k_ref, v_ref, w_ref, *, d_model, xd):
    @pl.when(pl.program_id(0) == 0)
    def _():
        w_ref[...] = w32_ref[...].astype(_BF16)

    mn = _rms(mem_ref[0], gm_ref[...]).astype(_BF16)
    kv = jnp.dot(mn, w_ref[...], preferred_element_type=_F32)
    for h in range(_N_XHEADS):
        sl = slice(h * xd, (h + 1) * xd)
        k_ref[0, :, sl] = (_rms(kv[:, sl], gk_ref[...]) * (xd ** -0.5)).astype(_BF16)
    v_ref[0] = kv[:, d_model:].astype(_BF16)


def _mem_kv(mem, g_mem, w_xkv, g_xk):
    B, M, D = mem.shape
    xd = D // _N_XHEADS
    blk = pl.BlockSpec((1, M, D), lambda b: (b, 0, 0))
    out = jax.ShapeDtypeStruct((B, M, D), _BF16)
    return pl.pallas_call(
        functools.partial(_mem_kv_body, d_model=D, xd=xd),
        grid=(B,),
        in_specs=[blk, _const_spec((1, D)), _const_spec(w_xkv.shape), _const_spec((1, xd))],
        out_specs=[blk, blk],
        out_shape=[out, out],
        scratch_shapes=[pltpu.VMEM(w_xkv.shape, _BF16)],
        compiler_params=_params(("arbitrary",)),
        name="mem_kv",
    )(mem, g_mem, w_xkv, g_xk)


def _mix_xattn_body(x_ref, ret_ref, fox_ref, wo32_ref, gx_ref, wq32_ref, gq_ref, k_ref, v_ref, wxo32_ref,
                    o_ref, wo_ref, wq_ref, wxo_ref, *, width, xd, sub):
    @pl.when((pl.program_id(0) == 0) & (pl.program_id(1) == 0))
    def _():
        wo_ref[...] = wo32_ref[...].astype(_BF16)
        wq_ref[...] = wq32_ref[...].astype(_BF16)
        wxo_ref[...] = wxo32_ref[...].astype(_BF16)

    nt = (((1,), (1,)), ((), ()))
    head_slices = [slice(h * xd, (h + 1) * xd) for h in range(_N_XHEADS)]

    def front(rows):
        h1 = (x_ref[0, rows, :]
              + jnp.dot(ret_ref[0, rows, :], wo_ref[:width, :], preferred_element_type=_F32)
              + jnp.dot(fox_ref[0, rows, :], wo_ref[width:, :], preferred_element_type=_F32))
        hn, inv_h = _rms_split(h1, gx_ref[...])
        y = jnp.dot(hn, wq_ref[...], preferred_element_type=_F32)
        logits = []
        for sl in head_slices:
            yh = y[:, sl]
            inv_q = lax.rsqrt(inv_h * inv_h * jnp.mean(yh * yh, axis=-1, keepdims=True) + _EPS)
            logits.append(lax.dot_general((yh * gq_ref[...]).astype(_BF16), k_ref[0, :, sl], nt,
                                          preferred_element_type=_F32) * (inv_q * inv_h * _LOG2E))
        return h1, logits

    def back(rows, h1, logits):
        probs = []
        for lg in logits:
            p = jnp.exp2(lg - jnp.max(lg, axis=-1, keepdims=True))
            probs.append((p / jnp.sum(p, axis=-1, keepdims=True)).astype(_BF16))
        heads = [jnp.dot(p, v_ref[0, :, sl], preferred_element_type=_F32).astype(_BF16)
                 for p, sl in zip(probs, head_slices)]
        o = jnp.concatenate(heads, axis=-1)
        o_ref[0, rows, :] = h1 + jnp.dot(o, wxo_ref[...], preferred_element_type=_F32)

    tm = x_ref.shape[1]
    subs = [slice(r0, r0 + sub) for r0 in range(0, tm, sub)]
    fronts = [front(rows) for rows in subs]
    for rows, (h1, logits) in zip(subs, fronts):
        back(rows, h1, logits)


def _mix_xattn(x, ret, fox, w_out, g_xattn, w_xq, g_xq, k, v, w_xo, *, tm, sub):
    B, T, D = x.shape
    W = ret.shape[-1]
    M = k.shape[1]
    xd = D // _N_XHEADS
    assert tm % sub == 0 and T % tm == 0
    tok = lambda b, i: (b, i, 0)
    return pl.pallas_call(
        functools.partial(_mix_xattn_body, width=W, xd=xd, sub=sub),
        grid=(B, T // tm),
        in_specs=[
            pl.BlockSpec((1, tm, D), tok),
            pl.BlockSpec((1, tm, W), tok),
            pl.BlockSpec((1, tm, W), tok),
            _const_spec(w_out.shape),
            _const_spec((1, D)),
            _const_spec(w_xq.shape),
            _const_spec((1, xd)),
            pl.BlockSpec((1, M, D), lambda b, i: (b, 0, 0)),
            pl.BlockSpec((1, M, D), lambda b, i: (b, 0, 0)),
            _const_spec(w_xo.shape),
        ],
        out_specs=pl.BlockSpec((1, tm, D), tok),
        out_shape=jax.ShapeDtypeStruct((B, T, D), _F32),
        scratch_shapes=[pltpu.VMEM(w.shape, _BF16) for w in (w_out, w_xq, w_xo)],
        compiler_params=_params(("arbitrary", "arbitrary")),
        name="mix_xattn",
    )(x, ret, fox, w_out, g_xattn, w_xq, g_xq, k, v, w_xo)


def _ffn_chunks(d_ff, n_chunks):
    tiles = -(-d_ff // _MXU_TILE)
    bounds = [min(d_ff, _MXU_TILE * ((tiles * c) // n_chunks)) for c in range(n_chunks)] + [d_ff]
    return [(lo, hi) for lo, hi in zip(bounds[:-1], bounds[1:]) if hi > lo]


def _ffn_body(h_ref, g_ref, wg32_ref, wu32_ref, wd32_ref, o_ref, wg_ref, wu_ref, wd_ref, acc0_ref,
              *, n_cast, ck, chunks, sub):
    step = pl.program_id(0)
    tm = h_ref.shape[1]
    subs = [slice(r0, r0 + sub) for r0 in range(0, tm, sub)]

    def swiglu(hn, inv_rms, sl):
        gate = jnp.dot(hn, wg_ref[:, sl], preferred_element_type=_F32) * inv_rms
        up = jnp.dot(hn, wu_ref[:, sl], preferred_element_type=_F32) * inv_rms
        a = (gate * jax.nn.sigmoid(gate) * up).astype(_BF16)
        return jnp.dot(a, wd_ref[sl, :], preferred_element_type=_F32)

    for c in range(n_cast):
        @pl.when(step == c)
        def _(c=c):
            sl = slice(c * ck, (c + 1) * ck)
            wg_ref[:, sl] = wg32_ref[...].astype(_BF16)
            wu_ref[:, sl] = wu32_ref[...].astype(_BF16)
            wd_ref[sl, :] = wd32_ref[...].astype(_BF16)
            for rows in subs:
                h = h_ref[0, rows, :]
                part = swiglu(*_rms_split(h, g_ref[...]), sl)
                if c == 0:
                    acc0_ref[rows, :] = h + part
                elif c < n_cast - 1:
                    acc0_ref[rows, :] += part
                else:
                    o_ref[0, rows, :] = acc0_ref[rows, :] + part

    @pl.when(step >= n_cast)
    def _():
        for rows in subs:
            h = h_ref[0, rows, :]
            hn, inv_rms = _rms_split(h, g_ref[...])
            acc = h
            for lo, hi in chunks:
                acc = acc + swiglu(hn, inv_rms, slice(lo, hi))
            o_ref[0, rows, :] = acc


def _ffn(h, g_ffn, w_gate, w_up, w_down, *, tm, sub, n_split):
    B, T, D = h.shape
    F = w_gate.shape[1]
    ck = _MXU_TILE
    assert F % ck == 0 and T % tm == 0 and tm % sub == 0
    n_cast = F // ck
    n_tiles = B * T // tm
    tile = lambda s: (jnp.maximum(s - (n_cast - 1), 0), 0, 0)
    chunk = lambda s: jnp.minimum(s, n_cast - 1)
    tok = pl.BlockSpec((1, tm, D), tile)
    out = pl.pallas_call(
        functools.partial(_ffn_body, n_cast=n_cast, ck=ck, chunks=_ffn_chunks(F, n_split), sub=sub),
        grid=(n_cast - 1 + n_tiles,),
        in_specs=[tok, _const_spec((1, D)),
                  pl.BlockSpec((D, ck), lambda s: (0, chunk(s))),
                  pl.BlockSpec((D, ck), lambda s: (0, chunk(s))),
                  pl.BlockSpec((ck, D), lambda s: (chunk(s), 0))],
        out_specs=tok,
        out_shape=jax.ShapeDtypeStruct((n_tiles, tm, D), _F32),
        scratch_shapes=[pltpu.VMEM((D, F), _BF16), pltpu.VMEM((D, F), _BF16), pltpu.VMEM((F, D), _BF16),
                        pltpu.VMEM((tm, D), _F32)],
        compiler_params=_params(("arbitrary",)),
        name="ffn",
    )(h.reshape(n_tiles, tm, D), g_ffn, w_gate, w_up, w_down)
    return out.reshape(B, T, D)


def _rope_tables(T):
    half = _HEAD_DIM // 2
    inv_freq = (_ROPE_BASE ** (-np.arange(0, _HEAD_DIM, 2, dtype=np.float32) / _HEAD_DIM)).astype(np.float32)
    ang = (np.arange(T, dtype=np.float32)[:, None] * inv_freq[None, :]).astype(np.float32).astype(np.float64)
    cos, sin = np.cos(ang), np.sin(ang)
    reps = _LANES // _HEAD_DIM
    cos_t = np.tile(np.concatenate([cos, cos], axis=1), (1, reps))
    sin_t = np.tile(np.concatenate([-sin, sin], axis=1), (1, reps))
    return jnp.asarray(cos_t, _F32), jnp.asarray(sin_t, _F32)


def _retention_tables(tb):
    log_g = np.log(1.0 - 2.0 ** (-5.0 - np.arange(_N_HEADS, dtype=np.float64)))
    idx = np.arange(tb, dtype=np.float64)
    dist = np.abs(idx[:, None] - idx[None, :])
    chunk = np.arange(tb) // _RET_CHUNK
    visible = chunk[None, :] <= chunk[:, None]
    dmask = np.where(visible[None], np.exp(log_g[:, None, None] * dist[None]), 0.0)
    qdec = np.repeat(np.exp(log_g[None, :] * (idx[:, None] + 1.0)), _HEAD_DIM, axis=1)
    kdec = np.repeat(np.exp(log_g[None, :] * (tb - 1.0 - idx[:, None])), _HEAD_DIM, axis=1)
    heads_per_group = _RET_GROUP // _HEAD_DIM
    head_of = np.arange(_RET_GROUP) // _HEAD_DIM
    bd = (head_of[:, None] == head_of[None, :]).astype(np.float64)
    step_decay = np.exp(log_g * tb).reshape(-1, heads_per_group)
    sdec = bd[None] * np.repeat(step_decay, _HEAD_DIM, axis=1)[:, None, :]
    f = lambda a: jnp.asarray(a, _F32)
    return f(dmask), f(qdec), f(kdec), f(sdec), f(bd), jnp.asarray(bd / _HEAD_DIM, _BF16)


def _pad_lanes(a):
    return jnp.pad(a, [(0, 0)] * (a.ndim - 1) + [(0, _LANES - a.shape[-1])])


def kernel(x, mem, g_mix, w_in, b_forget, g_ret_out, g_fox_q, g_fox_k, w_out, g_xattn, w_xq, w_xkv,
           g_mem, g_xq, g_xk, w_xo, g_ffn, w_gate, w_up, w_down):
    B, T, D = x.shape
    width = _N_HEADS * _HEAD_DIM
    tb, tq, tm_mix, tm_ffn = 256, 512, 1024, 1024
    cos_t, sin_t = _rope_tables(T)
    ret_tables = _retention_tables(tb)
    tri = jnp.asarray(np.tril(np.ones((tq, tq), np.float32)), _BF16)
    row = lambda a: a.reshape(1, -1).astype(_F32)

    h = x
    for l in range(w_in.shape[0]):
        b_ff = _pad_lanes(row(b_forget[l]))
        gq = jnp.tile(row(g_fox_q[l]), (1, _N_HEADS)) * (_LOG2E * _HEAD_DIM ** -0.5)
        gk = jnp.tile(row(g_fox_k[l]), (1, _N_HEADS))
        bound = (_HEAD_DIM ** 0.5 * _NORM_ROUNDING_SLACK) * jnp.max(jnp.abs(g_fox_q[l])) * jnp.max(
            jnp.abs(g_fox_k[l]))
        use_shift = bound <= _MAX_FIXED_SHIFT
        shift = jnp.where(use_shift, jnp.ceil(bound * (4.0 * _LOG2E)) * 0.25, 0.0).astype(_F32)
        ret, qa, ka, va, fend = _in_proj(h, row(g_mix[l]), w_in[l].T, b_ff, cos_t, sin_t, gq, gk,
                                         jnp.full((1, _LANES), shift), tri, ret_tables,
                                         row(g_ret_out[l]), tm=tq, tb=tb)
        jlo = _first_live_block(fend[:, :, 0, :_N_HEADS])
        fox = lax.cond(use_shift,
                       lambda jlo, qa, ka, va: _fox_shifted(jlo, qa, ka, va, tq=tq, hg=4, unroll=6),
                       lambda jlo, qa, ka, va: _fox_online(qa, ka, va, tq=tq, hg=4), jlo, qa, ka, va)
        k, v = _mem_kv(mem, row(g_mem[l]), w_xkv[l], row(g_xk[l]))
        h = _mix_xattn(h, ret, fox, w_out[l], row(g_xattn[l]), w_xq[l], row(g_xq[l]), k, v, w_xo[l],
                       tm=tm_mix, sub=512)
        h = _ffn(h, row(g_ffn[l]), w_gate[l], w_up[l], w_down[l], tm=tm_ffn, sub=512, n_split=2)
    return h
```

```python
import functools

import numpy as np
import jax
import jax.numpy as jnp
from jax import lax
from jax.experimental import pallas as pl
from jax.experimental.pallas import tpu as pltpu

_BF16 = jnp.bfloat16
_F32 = jnp.float32

_EPS = 1e-6
_NEG_INF = -1e30
_ROPE_BASE = 10000.0
_HEAD_DIM = 64
_N_HEADS = 8
_RET_CHUNK = 64
_RET_GROUP = 256
_N_XHEADS = 4
_LANES = 128
_MXU_TILE = 256
_VMEM_LIMIT = 56 * 1024 * 1024

_AUG_QF = 64
_AUG_KF = 88
_AUG_SHIFT = 112
_AUG_ONE = 64

_LOG2E = 1.4426950408889634
_MAX_FIXED_SHIFT = 32.0
_UNDERFLOW_LOG2 = -152.0
_NORM_ROUNDING_SLACK = 1.01


def _params(sem):
    return pltpu.CompilerParams(dimension_semantics=sem, vmem_limit_bytes=_VMEM_LIMIT)


def _const_spec(shape):
    nd = len(shape)
    return pl.BlockSpec(shape, lambda *_: (0,) * nd, pipeline_mode=pl.Buffered(1))


def _rms(x, g):
    return x * lax.rsqrt(jnp.mean(x * x, axis=-1, keepdims=True) + _EPS) * g


def _rms_split(x, g):
    return (x * g).astype(_BF16), lax.rsqrt(jnp.mean(x * x, axis=-1, keepdims=True) + _EPS)


def _split3(v):
    hi = v.astype(_BF16).astype(_F32)
    r = v - hi
    mid = r.astype(_BF16).astype(_F32)
    return hi, mid, r - mid


def _in_proj_body(x_ref, g_ref, w32_ref, bf_ref, cos_ref, sin_ref, gq_ref, gk_ref, shift_ref,
                  tri_ref, avg_ref, dmask_ref, qdec_ref, kdec_ref, sdec_ref, bd_ref, gret_ref,
                  ret_ref, qa_ref, ka_ref, va_ref, fend_ref,
                  w_ref, wff_ref, carry_ref, rq_ref, rk_ref, rv_ref, gate_ref, state_ref, *, tm, tb, width):
    i = pl.program_id(1)
    n_main = 7 * width

    @pl.when((pl.program_id(0) == 0) & (i == 0))
    def _():
        for j in range(n_main // width):
            w_ref[:, j * width:(j + 1) * width] = jnp.transpose(
                w32_ref[j * width:(j + 1) * width, :]).astype(_BF16)
        tail = jnp.concatenate([w32_ref[n_main:, :],
                                jnp.zeros((_LANES - _N_HEADS, w32_ref.shape[1]), _F32)], axis=0)
        wff_ref[...] = jnp.transpose(tail).astype(_BF16)

    @pl.when(i == 0)
    def _():
        state_ref[...] = jnp.zeros_like(state_ref)
        carry_ref[...] = jnp.zeros_like(carry_ref)
    hb, inv_rms = _rms_split(x_ref[0], g_ref[...])

    def proj(j):
        return jnp.dot(hb, w_ref[:, j * width:(j + 1) * width], preferred_element_type=_F32) * inv_rms

    lane = lax.broadcasted_iota(jnp.int32, (tm, _LANES), 1)
    low = lane < _HEAD_DIM
    first_half = (lane & (_HEAD_DIM // 2)) == 0
    n_pairs = width // _LANES

    cos = cos_ref[...]
    sin = sin_ref[...]
    for j, out_ref, scale in ((0, rq_ref, _HEAD_DIM ** -0.5), (1, rk_ref, None)):
        y = proj(j)
        for c in range(n_pairs):
            blk = y[:, c * _LANES:(c + 1) * _LANES]
            swapped = jnp.where(first_half, pltpu.roll(blk, _LANES - _HEAD_DIM // 2, 1),
                                pltpu.roll(blk, _HEAD_DIM // 2, 1))
            r = blk * cos + swapped * sin
            if scale is not None:
                r = r * scale
            out_ref[:, c * _LANES:(c + 1) * _LANES] = r.astype(_BF16)
    rv_ref[...] = proj(2).astype(_BF16)
    gate = proj(3)
    gate_ref[...] = (gate * jax.nn.sigmoid(gate)).astype(_BF16)


    half = tm // 2
    z = jnp.concatenate([jnp.dot(hb[:half], wff_ref[...], preferred_element_type=_F32),
                         jnp.dot(hb[half:], wff_ref[...], preferred_element_type=_F32)],
                        axis=0) * inv_rms + bf_ref[...]
    logf = (jnp.minimum(z, 0.0) - jnp.log(1.0 + jnp.exp(-jnp.abs(z)))) * _LOG2E
    hi, mid, lo = _split3(logf)
    tri = tri_ref[...]
    split = jnp.concatenate([hi, mid, lo], axis=1).astype(_BF16)
    parts = jnp.concatenate(
        [jnp.dot(tri[:half, :half], split[:half], preferred_element_type=_F32),
         jnp.dot(tri[half:], split, preferred_element_type=_F32)], axis=0)
    csum = parts[:, :_LANES] + parts[:, _LANES:2 * _LANES] + parts[:, 2 * _LANES:]

    fcum = csum + carry_ref[0:1, :]
    carry_ref[...] = jnp.broadcast_to(fcum[tm - 1:tm, :], carry_ref.shape)
    fend_ref[0, 0] = carry_ref[...]

    def head_rms(y, g_row):
        out = []
        for g0 in range(0, width, _RET_GROUP):
            blk = y[:, g0:g0 + _RET_GROUP]
            ms = jnp.dot((blk * blk).astype(_BF16), avg_ref[...], preferred_element_type=_F32)
            out.append(blk * lax.rsqrt(ms + _EPS) * g_row[:, g0:g0 + _RET_GROUP])
        return out

    qn = head_rms(proj(4), gq_ref[...])
    kn = head_rms(proj(5), gk_ref[...])
    fv = proj(6)

    fh, fm, fl = _split3(fcum)
    n_parts = 3 * _N_HEADS
    packed = jnp.where(lane < _N_HEADS, fh, jnp.where(
        lane < 2 * _N_HEADS, pltpu.roll(fm, _N_HEADS, 1), pltpu.roll(fl, 2 * _N_HEADS, 1)))
    in_qf = (lane >= _AUG_QF) & (lane < _AUG_QF + n_parts)
    in_kf = (lane >= _AUG_KF) & (lane < _AUG_KF + n_parts)
    shared = jnp.where(in_qf, pltpu.roll(packed, _AUG_QF, 1), -pltpu.roll(packed, _AUG_KF, 1))
    q_bias = jnp.where(in_qf, shared, jnp.where(in_kf, 1.0, jnp.where(
        lane == _AUG_SHIFT, -shift_ref[...], 0.0)))
    v_const = jnp.where(lane == _AUG_ONE, 1.0, 0.0)
    for h in range(_N_HEADS):
        own = (lane & (_N_HEADS - 1)) == h
        k_bias = jnp.where(in_kf & own, shared,
                           jnp.where((in_qf & own) | (lane == _AUG_SHIFT), 1.0, 0.0))
        g0, c0 = divmod(h * _HEAD_DIM, _RET_GROUP)
        c0 = (c0 // _LANES) * _LANES
        qb = qn[g0][:, c0:c0 + _LANES]
        kb = kn[g0][:, c0:c0 + _LANES]
        vb = fv[:, (h // 2) * _LANES:(h // 2 + 1) * _LANES]
        if h % 2 == 1:
            qb = pltpu.roll(qb, _HEAD_DIM, 1)
            kb = pltpu.roll(kb, _HEAD_DIM, 1)
            vb = pltpu.roll(vb, _HEAD_DIM, 1)
        qa_ref[0, h] = jnp.where(low, qb, q_bias).astype(_BF16)
        ka_ref[0, h] = jnp.where(low, kb, k_bias).astype(_BF16)
        va_ref[0, h] = jnp.where(low, vb, v_const).astype(_BF16)

    for r0 in range(0, tm, tb):
        _retention_block(rq_ref, rk_ref, rv_ref, gate_ref, slice(r0, r0 + tb), dmask_ref, qdec_ref, kdec_ref,
                         sdec_ref, bd_ref, avg_ref, gret_ref, state_ref, ret_ref)


def _in_proj(x, g_mix, w_in_t, b_ff, cos_t, sin_t, gq, gk, shift, tri, ret_tables, g_ret, *, tm, tb):
    B, T, D = x.shape
    width = _N_HEADS * _HEAD_DIM
    assert w_in_t.shape == (7 * width + _N_HEADS, D) and tm % tb == 0
    dmask, qdec, kdec, sdec, bd, avg = ret_tables
    tok = lambda b, i: (b, i, 0)
    head = lambda b, i: (b, 0, i, 0)
    bf_tok = jax.ShapeDtypeStruct((B, T, width), _BF16)
    bf_head = jax.ShapeDtypeStruct((B, _N_HEADS, T, _LANES), _BF16)
    return pl.pallas_call(
        functools.partial(_in_proj_body, tm=tm, tb=tb, width=width),
        grid=(B, T // tm),
        in_specs=[
            pl.BlockSpec((1, tm, D), tok),
            _const_spec((1, D)),
            _const_spec(w_in_t.shape),
            _const_spec((1, _LANES)),
            pl.BlockSpec((tm, _LANES), lambda b, i: (i, 0)),
            pl.BlockSpec((tm, _LANES), lambda b, i: (i, 0)),
            _const_spec((1, width)),
            _const_spec((1, width)),
            _const_spec((1, _LANES)),
            _const_spec((tm, tm)),
            _const_spec(avg.shape),
            _const_spec(dmask.shape),
            _const_spec(qdec.shape),
            _const_spec(kdec.shape),
            _const_spec(sdec.shape),
            _const_spec(bd.shape),
            _const_spec(g_ret.shape),
        ],
        out_specs=[pl.BlockSpec((1, tm, width), tok)]
        + [pl.BlockSpec((1, _N_HEADS, tm, _LANES), head)] * 3
        + [pl.BlockSpec((1, 1, 8, _LANES), lambda b, i: (b, i, 0, 0))],
        out_shape=[bf_tok] + [bf_head] * 3 + [jax.ShapeDtypeStruct((B, T // tm, 8, _LANES), _F32)],
        scratch_shapes=[pltpu.VMEM((D, 7 * width), _BF16), pltpu.VMEM((D, _LANES), _BF16),
                        pltpu.VMEM((8, _LANES), _F32)]
        + [pltpu.VMEM((tm, width), _BF16)] * 4 + [pltpu.VMEM(sdec.shape, _F32)],
        compiler_params=_params(("arbitrary", "arbitrary")),
        name="in_proj",
    )(x, g_mix, w_in_t, b_ff, cos_t, sin_t, gq, gk, shift, tri, avg, dmask, qdec, kdec, sdec, bd, g_ret)


def _retention_block(rq_ref, rk_ref, rv_ref, gate_ref, rows, dmask_ref, qdec_ref, kdec_ref, sdec_ref, bd_ref,
                     avg_ref, g_ref, state_ref, o_ref):
    tb = rows.stop - rows.start
    nt = (((1,), (1,)), ((), ()))
    tn = (((0,), (0,)), ((), ()))
    gw = state_ref.shape[1]
    lane = lax.broadcasted_iota(jnp.int32, (1, _LANES), 1)
    low = lax.broadcasted_iota(jnp.int32, (tb, _LANES), 1) < _HEAD_DIM
    head_lanes = [jnp.where(lane < _HEAD_DIM, 1.0, 0.0).astype(_BF16),
                  jnp.where(lane < _HEAD_DIM, 0.0, 1.0).astype(_BF16)]
    n_groups = state_ref.shape[0]
    groups = [slice(g * gw, (g + 1) * gw) for g in range(n_groups)]
    scores = []
    for h in range(_N_HEADS):
        ps = slice((h // 2) * _LANES, (h // 2 + 1) * _LANES)
        scores.append(lax.dot_general(rq_ref[rows, ps] * head_lanes[h % 2], rk_ref[rows, ps], nt,
                                      preferred_element_type=_F32))
    inter = []
    for g, gs in enumerate(groups):
        state = state_ref[g]
        inter.append(jnp.dot(rq_ref[rows, gs], state.astype(_BF16),
                             preferred_element_type=_F32) * qdec_ref[:, gs])
        kd = (rk_ref[rows, gs].astype(_F32) * kdec_ref[:, gs]).astype(_BF16)
        state_ref[g] = state * sdec_ref[g] + lax.dot_general(
            kd, rv_ref[rows, gs], tn, preferred_element_type=_F32) * bd_ref[...]
    intra = []
    for h in range(_N_HEADS):
        ps = slice((h // 2) * _LANES, (h // 2 + 1) * _LANES)
        intra.append(jnp.dot((scores[h] * dmask_ref[h]).astype(_BF16), rv_ref[rows, ps],
                             preferred_element_type=_F32))
    outs = []
    for g, gs in enumerate(groups):
        pairs = [jnp.where(low, intra[2 * c], intra[2 * c + 1])
                 for c in range(g * gw // _LANES, (g + 1) * gw // _LANES)]
        outs.append(jnp.concatenate(pairs, axis=1) + inter[g])
    mus = [jnp.dot(o.astype(_BF16), avg_ref[...], preferred_element_type=_F32) for o in outs]
    cents = [o - mu for o, mu in zip(outs, mus)]
    vars_ = [jnp.dot((oc * oc).astype(_BF16), avg_ref[...], preferred_element_type=_F32) for oc in cents]
    for gs, oc, var in zip(groups, cents, vars_):
        y = oc * lax.rsqrt(var + _EPS) * g_ref[:, gs]
        o_ref[0, rows, gs] = (y * gate_ref[rows, gs].astype(_F32)).astype(_BF16)


def _causal_mask(tq):
    row = lax.broadcasted_iota(jnp.int32, (tq, tq), 0)
    col = lax.broadcasted_iota(jnp.int32, (tq, tq), 1)
    return row >= col


def _fox_finish(acc_refs, o_ref, tq):
    lane = lax.broadcasted_iota(jnp.int32, (tq, _LANES), 1)
    for c in range(len(acc_refs) // 2):
        pair = []
        for hh in (2 * c, 2 * c + 1):
            acc = acc_refs[hh][...]
            pair.append(acc / acc[:, _AUG_ONE:_AUG_ONE + 1])
        o_ref[0, :, c * _LANES:(c + 1) * _LANES] = jnp.where(
            lane < _HEAD_DIM, pair[0], pltpu.roll(pair[1], _HEAD_DIM, 1)).astype(_BF16)


def _fox_shifted_body(jlo_ref, qa_ref, ka_ref, va_ref, o_ref, acc_ref, it_h, it_i, it_j, it_slot, it_keep,
                      *, tq, nq, unroll):
    n_pair = qa_ref.shape[1]
    base = (pl.program_id(0) * pl.num_programs(1) + pl.program_id(1)) * (n_pair * nq)
    nt = (((1,), (1,)), ((), ()))
    half = tq // 2
    diff_top = (lax.broadcasted_iota(jnp.int32, (half, half), 1)
                - lax.broadcasted_iota(jnp.int32, (half, half), 0))
    diff_bot = (lax.broadcasted_iota(jnp.int32, (half, tq), 1)
                - lax.broadcasted_iota(jnp.int32, (half, tq), 0))

    def interleave(n_chains, qk, finish):
        s_prev = qk(0)
        for u in range(1, n_chains):
            s_next = qk(u)
            finish(u - 1, s_prev)
            s_prev = s_next
        finish(n_chains - 1, s_prev)

    n = jnp.int32(0)
    for hh in range(n_pair):
        for i in range(1, nq):
            lo = jlo_ref[base + hh * nq + i]

            def add(j, n, hh=hh, i=i, lo=lo):
                it_h[n] = jnp.int32(hh)
                it_i[n] = jnp.int32(i)
                it_j[n] = j
                it_slot[n] = jnp.int32(hh * nq + i)
                it_keep[n] = jnp.where(j == lo, 0, 1)
                return n + 1

            n = lax.fori_loop(lo, i, add, n)
    n_trips = (n + (unroll - 1)) // unroll

    def pad(m, carry):
        it_h[m] = jnp.int32(0)
        it_i[m] = jnp.int32(0)
        it_j[m] = jnp.int32(0)
        it_slot[m] = jnp.int32(n_pair * nq)
        it_keep[m] = jnp.int32(0)
        return carry

    lax.fori_loop(n, n_trips * unroll, pad, 0)

    def trip(t, carry):
        items = [tuple(ref[t * unroll + u] for ref in (it_h, it_i, it_j, it_slot, it_keep))
                 for u in range(unroll)]

        def qk(u):
            h, i, j = items[u][:3]
            return lax.dot_general(qa_ref[0, h, pl.ds(pl.multiple_of(i * tq, tq), tq), :],
                                   ka_ref[0, h, pl.ds(pl.multiple_of(j * tq, tq), tq), :], nt,
                                   preferred_element_type=_F32)

        def finish(u, s):
            h, _, j, slot, keep = items[u]
            pv = jnp.dot(jnp.exp2(s).astype(_BF16), va_ref[0, h, pl.ds(pl.multiple_of(j * tq, tq), tq), :],
                         preferred_element_type=_F32)
            acc_ref[slot] = jnp.where(keep > 0, acc_ref[slot], 0.0) + pv

        interleave(unroll, qk, finish)
        return carry

    lax.fori_loop(0, n_trips, trip, 0)

    chains = [(hh, i, bottom) for hh in range(n_pair) for i in range(nq) for bottom in (0, 1)]

    def diag_qk(u):
        hh, i, bottom = chains[u]
        r0 = i * tq + bottom * half
        width = tq if bottom else half
        return lax.dot_general(qa_ref[0, hh, r0:r0 + half, :], ka_ref[0, hh, i * tq:i * tq + width, :], nt,
                               preferred_element_type=_F32)

    def diag_finish(u, s):
        hh, i, bottom = chains[u]
        r0 = i * tq + bottom * half
        width = tq if bottom else half
        live = (diff_bot <= half) if bottom else (diff_top <= 0)
        p = jnp.exp2(jnp.where(live, s, _NEG_INF)).astype(_BF16)
        pv = jnp.dot(p, va_ref[0, hh, i * tq:i * tq + width, :], preferred_element_type=_F32)
        if i == 0:
            acc = pv
        else:
            has_off_diagonal = jlo_ref[base + hh * nq + i] < i
            acc = jnp.where(has_off_diagonal,
                            acc_ref[hh * nq + i, bottom * half:(bottom + 1) * half, :], 0.0) + pv
        o = acc / acc[:, _AUG_ONE:_AUG_ONE + 1]
        lanes = slice((hh % 2) * _HEAD_DIM, (hh % 2 + 1) * _HEAD_DIM)
        if hh % 2 == 1:
            o = pltpu.roll(o, _HEAD_DIM, 1)
        o_ref[0, r0:r0 + half, hh * _HEAD_DIM:(hh + 1) * _HEAD_DIM] = o[:, lanes].astype(_BF16)

    interleave(len(chains), diag_qk, diag_finish)


def _fox_online_body(qa_ref, ka_ref, va_ref, o_ref, *scratch, tq):
    qi = pl.program_id(2)
    nt = (((1,), (1,)), ((), ()))
    n_heads = qa_ref.shape[1]
    m_refs, acc_refs = scratch[:n_heads], scratch[n_heads:]
    for hh in range(n_heads):
        m_refs[hh][...] = jnp.full_like(m_refs[hh], _NEG_INF)
        acc_refs[hh][...] = jnp.zeros_like(acc_refs[hh])

    def step(j, masked):
        start = pl.multiple_of(j * tq, tq)
        logits = [lax.dot_general(qa_ref[0, hh], ka_ref[0, hh, pl.ds(start, tq), :], nt,
                                  preferred_element_type=_F32) for hh in range(n_heads)]
        for hh in range(n_heads):
            m_ref, acc_ref = m_refs[hh], acc_refs[hh]
            s = logits[hh]
            if masked:
                s = jnp.where(_causal_mask(tq), s, _NEG_INF)
            m_old = m_ref[...]
            m_new = jnp.maximum(m_old, jnp.max(s, axis=-1, keepdims=True))
            p = jnp.exp2(s - m_new[:, 0:1])
            acc_ref[...] = jnp.exp2(m_old - m_new) * acc_ref[...] + jnp.dot(
                p.astype(_BF16), va_ref[0, hh, pl.ds(start, tq), :], preferred_element_type=_F32)
            m_ref[...] = m_new

    def off_diag(j, carry):
        step(j, False)
        return carry

    lax.fori_loop(0, qi, off_diag, 0)
    step(qi, True)
    _fox_finish(acc_refs, o_ref, tq)


def _fox_shifted(jlo, qa, ka, va, *, tq, hg, unroll):
    B, H, T, L = qa.shape
    nq = T // tq
    assert tq % (2 * _MXU_TILE) == 0 and hg % 2 == 0 and H % hg == 0
    max_items = hg * nq * (nq - 1) // 2 + unroll
    blk = pl.BlockSpec((1, hg, T, L), lambda b, p, jlo_ref: (b, p, 0, 0))
    return pl.pallas_call(
        functools.partial(_fox_shifted_body, tq=tq, nq=nq, unroll=unroll),
        grid_spec=pltpu.PrefetchScalarGridSpec(
            num_scalar_prefetch=1,
            grid=(B, H // hg),
            in_specs=[blk, blk, blk],
            out_specs=pl.BlockSpec((1, T, hg * _HEAD_DIM), lambda b, p, jlo_ref: (b, 0, p)),
            scratch_shapes=[pltpu.VMEM((hg * nq + 1, tq, L), _F32)]
            + [pltpu.SMEM((max_items,), jnp.int32)] * 5,
        ),
        out_shape=jax.ShapeDtypeStruct((B, T, H * _HEAD_DIM), _BF16),
        compiler_params=_params(("arbitrary", "arbitrary")),
        name="fox_shifted",
    )(jlo, qa, ka, va)


def _fox_online(qa, ka, va, *, tq, hg):
    B, H, T, L = qa.shape
    return pl.pallas_call(
        functools.partial(_fox_online_body, tq=tq),
        grid=(B, H // hg, T // tq),
        in_specs=[
            pl.BlockSpec((1, hg, tq, L), lambda b, p, i: (b, p, i, 0)),
            pl.BlockSpec((1, hg, T, L), lambda b, p, i: (b, p, 0, 0)),
            pl.BlockSpec((1, hg, T, L), lambda b, p, i: (b, p, 0, 0)),
        ],
        out_specs=pl.BlockSpec((1, tq, hg * _HEAD_DIM), lambda b, p, i: (b, i, p)),
        out_shape=jax.ShapeDtypeStruct((B, T, H * _HEAD_DIM), _BF16),
        scratch_shapes=[pltpu.VMEM((tq, L), _F32)] * (2 * hg),
        compiler_params=_params(("arbitrary", "arbitrary", "arbitrary")),
        name="fox_online",
    )(qa, ka, va)


def _first_live_block(fend):
    B, nq, H = fend.shape
    f = jnp.transpose(fend, (0, 2, 1))
    top = jnp.concatenate([jnp.zeros((B, H, 1), _F32), f[:, :, :-1]], axis=-1)
    dead = (top[:, :, :, None] - f[:, :, None, :]) <= _UNDERFLOW_LOG2
    j = jnp.arange(nq, dtype=jnp.int32)
    before = j[None, :] < j[:, None]
    first_live = jnp.min(jnp.where(dead & before, nq, j), axis=-1)
    return first_live.astype(jnp.int32).reshape(-1)


def _mem_kv_body(mem_ref, gm_ref, w32_ref, gk_ref, k_ref, v_ref, w_ref, *, d_model, xd):
    @pl.when(pl.program_id(0) == 0)
    def _():
        w_ref[...] = w32_ref[...].astype(_BF16)

    mn = _rms(mem_ref[0], gm_ref[...]).astype(_BF16)
    kv = jnp.dot(mn, w_ref[...], preferred_element_type=_F32)
    for h in range(_N_XHEADS):
        sl = slice(h * xd, (h + 1) * xd)
        k_ref[0, :, sl] = (_rms(kv[:, sl], gk_ref[...]) * (xd ** -0.5)).astype(_BF16)
    v_ref[0] = kv[:, d_model:].astype(_BF16)


def _mem_kv(mem, g_mem, w_xkv, g_xk):
    B, M, D = mem.shape
    xd = D // _N_XHEADS
    blk = pl.BlockSpec((1, M, D), lambda b: (b, 0, 0))
    out = jax.ShapeDtypeStruct((B, M, D), _BF16)
    return pl.pallas_call(
        functools.partial(_mem_kv_body, d_model=D, xd=xd),
        grid=(B,),
        in_specs=[blk, _const_spec((1, D)), _const_spec(w_xkv.shape), _const_spec((1, xd))],
        out_specs=[blk, blk],
        out_shape=[out, out],
        scratch_shapes=[pltpu.VMEM(w_xkv.shape, _BF16)],
        compiler_params=_params(("arbitrary",)),
        name="mem_kv",
    )(mem, g_mem, w_xkv, g_xk)


def _mix_xattn_body(x_ref, ret_ref, fox_ref, wo32_ref, gx_ref, wq32_ref, gq_ref, k_ref, v_ref, wxo32_ref,
                    o_ref, wo_ref, wq_ref, wxo_ref, *, width, xd, sub):
    @pl.when((pl.program_id(0) == 0) & (pl.program_id(1) == 0))
    def _():
        wo_ref[...] = wo32_ref[...].astype(_BF16)
        wq_ref[...] = wq32_ref[...].astype(_BF16)
        wxo_ref[...] = wxo32_ref[...].astype(_BF16)

    nt = (((1,), (1,)), ((), ()))
    head_slices = [slice(h * xd, (h + 1) * xd) for h in range(_N_XHEADS)]

    def front(rows):
        h1 = (x_ref[0, rows, :]
              + jnp.dot(ret_ref[0, rows, :], wo_ref[:width, :], preferred_element_type=_F32)
              + jnp.dot(fox_ref[0, rows, :], wo_ref[width:, :], preferred_element_type=_F32))
        hn, inv_h = _rms_split(h1, gx_ref[...])
        y = jnp.dot(hn, wq_ref[...], preferred_element_type=_F32)
        logits = []
        for sl in head_slices:
            yh = y[:, sl]
            inv_q = lax.rsqrt(inv_h * inv_h * jnp.mean(yh * yh, axis=-1, keepdims=True) + _EPS)
            logits.append(lax.dot_general((yh * gq_ref[...]).astype(_BF16), k_ref[0, :, sl], nt,
                                          preferred_element_type=_F32) * (inv_q * inv_h * _LOG2E))
        return h1, logits

    def back(rows, h1, logits):
        probs = []
        for lg in logits:
            p = jnp.exp2(lg - jnp.max(lg, axis=-1, keepdims=True))
            probs.append((p / jnp.sum(p, axis=-1, keepdims=True)).astype(_BF16))
        heads = [jnp.dot(p, v_ref[0, :, sl], preferred_element_type=_F32).astype(_BF16)
                 for p, sl in zip(probs, head_slices)]
        o = jnp.concatenate(heads, axis=-1)
        o_ref[0, rows, :] = h1 + jnp.dot(o, wxo_ref[...], preferred_element_type=_F32)

    tm = x_ref.shape[1]
    subs = [slice(r0, r0 + sub) for r0 in range(0, tm, sub)]
    fronts = [front(rows) for rows in subs]
    for rows, (h1, logits) in zip(subs, fronts):
        back(rows, h1, logits)


def _mix_xattn(x, ret, fox, w_out, g_xattn, w_xq, g_xq, k, v, w_xo, *, tm, sub):
    B, T, D = x.shape
    W = ret.shape[-1]
    M = k.shape[1]
    xd = D // _N_XHEADS
    assert tm % sub == 0 and T % tm == 0
    tok = lambda b, i: (b, i, 0)
    return pl.pallas_call(
        functools.partial(_mix_xattn_body, width=W, xd=xd, sub=sub),
        grid=(B, T // tm),
        in_specs=[
            pl.BlockSpec((1, tm, D), tok),
            pl.BlockSpec((1, tm, W), tok),
            pl.BlockSpec((1, tm, W), tok),
            _const_spec(w_out.shape),
            _const_spec((1, D)),
            _const_spec(w_xq.shape),
            _const_spec((1, xd)),
            pl.BlockSpec((1, M, D), lambda b, i: (b, 0, 0)),
            pl.BlockSpec((1, M, D), lambda b, i: (b, 0, 0)),
            _const_spec(w_xo.shape),
        ],
        out_specs=pl.BlockSpec((1, tm, D), tok),
        out_shape=jax.ShapeDtypeStruct((B, T, D), _F32),
        scratch_shapes=[pltpu.VMEM(w.shape, _BF16) for w in (w_out, w_xq, w_xo)],
        compiler_params=_params(("arbitrary", "arbitrary")),
        name="mix_xattn",
    )(x, ret, fox, w_out, g_xattn, w_xq, g_xq, k, v, w_xo)


def _ffn_chunks(d_ff, n_chunks):
    tiles = -(-d_ff // _MXU_TILE)
    bounds = [min(d_ff, _MXU_TILE * ((tiles * c) // n_chunks)) for c in range(n_chunks)] + [d_ff]
    return [(lo, hi) for lo, hi in zip(bounds[:-1], bounds[1:]) if hi > lo]


def _ffn_body(h_ref, g_ref, wg32_ref, wu32_ref, wd32_ref, o_ref, wg_ref, wu_ref, wd_ref, acc0_ref,
              *, n_cast, ck, chunks):
    step = pl.program_id(0)

    def swiglu(hn, inv_rms, sl):
        gate = jnp.dot(hn, wg_ref[:, sl], preferred_element_type=_F32) * inv_rms
        up = jnp.dot(hn, wu_ref[:, sl], preferred_element_type=_F32) * inv_rms
        a = (gate * jax.nn.sigmoid(gate) * up).astype(_BF16)
        return jnp.dot(a, wd_ref[sl, :], preferred_element_type=_F32)

    for c in range(n_cast):
        @pl.when(step == c)
        def _(c=c):
            sl = slice(c * ck, (c + 1) * ck)
            wg_ref[:, sl] = wg32_ref[...].astype(_BF16)
            wu_ref[:, sl] = wu32_ref[...].astype(_BF16)
            wd_ref[sl, :] = wd32_ref[...].astype(_BF16)
            h = h_ref[0]
            part = swiglu(*_rms_split(h, g_ref[...]), sl)
            if c == 0:
                acc0_ref[...] = h + part
            elif c < n_cast - 1:
                acc0_ref[...] += part
            else:
                o_ref[0] = acc0_ref[...] + part

    @pl.when(step >= n_cast)
    def _():
        h = h_ref[0]
        hn, inv_rms = _rms_split(h, g_ref[...])
        acc = h
        for lo, hi in chunks:
            acc = acc + swiglu(hn, inv_rms, slice(lo, hi))
        o_ref[0] = acc


def _ffn(h, g_ffn, w_gate, w_up, w_down, *, tm, n_split):
    B, T, D = h.shape
    F = w_gate.shape[1]
    ck = _MXU_TILE
    assert F % ck == 0 and T % tm == 0
    n_cast = F // ck
    n_tiles = B * T // tm
    tile = lambda s: (jnp.maximum(s - (n_cast - 1), 0), 0, 0)
    chunk = lambda s: jnp.minimum(s, n_cast - 1)
    tok = pl.BlockSpec((1, tm, D), tile)
    out = pl.pallas_call(
        functools.partial(_ffn_body, n_cast=n_cast, ck=ck, chunks=_ffn_chunks(F, n_split)),
        grid=(n_cast - 1 + n_tiles,),
        in_specs=[tok, _const_spec((1, D)),
                  pl.BlockSpec((D, ck), lambda s: (0, chunk(s))),
                  pl.BlockSpec((D, ck), lambda s: (0, chunk(s))),
                  pl.BlockSpec((ck, D), lambda s: (chunk(s), 0))],
        out_specs=tok,
        out_shape=jax.ShapeDtypeStruct((n_tiles, tm, D), _F32),
        scratch_shapes=[pltpu.VMEM((D, F), _BF16), pltpu.VMEM((D, F), _BF16), pltpu.VMEM((F, D), _BF16),
                        pltpu.VMEM((tm, D), _F32)],
        compiler_params=_params(("arbitrary",)),
        name="ffn",
    )(h.reshape(n_tiles, tm, D), g_ffn, w_gate, w_up, w_down)
    return out.reshape(B, T, D)


def _rope_tables(T):
    half = _HEAD_DIM // 2
    inv_freq = (_ROPE_BASE ** (-np.arange(0, _HEAD_DIM, 2, dtype=np.float32) / _HEAD_DIM)).astype(np.float32)
    ang = (np.arange(T, dtype=np.float32)[:, None] * inv_freq[None, :]).astype(np.float32).astype(np.float64)
    cos, sin = np.cos(ang), np.sin(ang)
    reps = _LANES // _HEAD_DIM
    cos_t = np.tile(np.concatenate([cos, cos], axis=1), (1, reps))
    sin_t = np.tile(np.concatenate([-sin, sin], axis=1), (1, reps))
    return jnp.asarray(cos_t, _F32), jnp.asarray(sin_t, _F32)


def _retention_tables(tb):
    log_g = np.log(1.0 - 2.0 ** (-5.0 - np.arange(_N_HEADS, dtype=np.float64)))
    idx = np.arange(tb, dtype=np.float64)
    dist = np.abs(idx[:, None] - idx[None, :])
    chunk = np.arange(tb) // _RET_CHUNK
    visible = chunk[None, :] <= chunk[:, None]
    dmask = np.where(visible[None], np.exp(log_g[:, None, None] * dist[None]), 0.0)
    qdec = np.repeat(np.exp(log_g[None, :] * (idx[:, None] + 1.0)), _HEAD_DIM, axis=1)
    kdec = np.repeat(np.exp(log_g[None, :] * (tb - 1.0 - idx[:, None])), _HEAD_DIM, axis=1)
    heads_per_group = _RET_GROUP // _HEAD_DIM
    head_of = np.arange(_RET_GROUP) // _HEAD_DIM
    bd = (head_of[:, None] == head_of[None, :]).astype(np.float64)
    step_decay = np.exp(log_g * tb).reshape(-1, heads_per_group)
    sdec = bd[None] * np.repeat(step_decay, _HEAD_DIM, axis=1)[:, None, :]
    f = lambda a: jnp.asarray(a, _F32)
    return f(dmask), f(qdec), f(kdec), f(sdec), f(bd), jnp.asarray(bd / _HEAD_DIM, _BF16)


def _pad_lanes(a):
    return jnp.pad(a, [(0, 0)] * (a.ndim - 1) + [(0, _LANES - a.shape[-1])])


def kernel(x, mem, g_mix, w_in, b_forget, g_ret_out, g_fox_q, g_fox_k, w_out, g_xattn, w_xq, w_xkv,
           g_mem, g_xq, g_xk, w_xo, g_ffn, w_gate, w_up, w_down):
    B, T, D = x.shape
    width = _N_HEADS * _HEAD_DIM
    tb, tq, tm_mix, tm_ffn = 256, 512, 512, 512
    cos_t, sin_t = _rope_tables(T)
    ret_tables = _retention_tables(tb)
    tri = jnp.asarray(np.tril(np.ones((tq, tq), np.float32)), _BF16)
    row = lambda a: a.reshape(1, -1).astype(_F32)

    h = x
    for l in range(w_in.shape[0]):
        b_ff = _pad_lanes(row(b_forget[l]))
        gq = jnp.tile(row(g_fox_q[l]), (1, _N_HEADS)) * (_LOG2E * _HEAD_DIM ** -0.5)
        gk = jnp.tile(row(g_fox_k[l]), (1, _N_HEADS))
        bound = (_HEAD_DIM ** 0.5 * _NORM_ROUNDING_SLACK) * jnp.max(jnp.abs(g_fox_q[l])) * jnp.max(
            jnp.abs(g_fox_k[l]))
        use_shift = bound <= _MAX_FIXED_SHIFT
        shift = jnp.where(use_shift, jnp.ceil(bound * (4.0 * _LOG2E)) * 0.25, 0.0).astype(_F32)
        ret, qa, ka, va, fend = _in_proj(h, row(g_mix[l]), w_in[l].T, b_ff, cos_t, sin_t, gq, gk,
                                         jnp.full((1, _LANES), shift), tri, ret_tables,
                                         row(g_ret_out[l]), tm=tq, tb=tb)
        jlo = _first_live_block(fend[:, :, 0, :_N_HEADS])
        fox = lax.cond(use_shift,
                       lambda jlo, qa, ka, va: _fox_shifted(jlo, qa, ka, va, tq=tq, hg=4, unroll=6),
                       lambda jlo, qa, ka, va: _fox_online(qa, ka, va, tq=tq, hg=4), jlo, qa, ka, va)
        k, v = _mem_kv(mem, row(g_mem[l]), w_xkv[l], row(g_xk[l]))
        h = _mix_xattn(h, ret, fox, w_out[l], row(g_xattn[l]), w_xq[l], row(g_xq[l]), k, v, w_xo[l],
                       tm=tm_mix, sub=256)
        h = _ffn(h, row(g_ffn[l]), w_gate[l], w_up[l], w_down[l], tm=tm_ffn, n_split=2)
    return h
```

```python
import functools

import numpy as np
import jax
import jax.numpy as jnp
from jax import lax
from jax.experimental import pallas as pl
from jax.experimental.pallas import tpu as pltpu

_BF16 = jnp.bfloat16
_F32 = jnp.float32

_EPS = 1e-6
_NEG_INF = -1e30
_ROPE_BASE = 10000.0
_HEAD_DIM = 64
_N_HEADS = 8
_RET_CHUNK = 64
_RET_GROUP = 256
_N_XHEADS = 4
_LANES = 128
_MXU_TILE = 256
_VMEM_LIMIT = 56 * 1024 * 1024

_AUG_QF = 64
_AUG_KF = 88
_AUG_SHIFT = 112
_AUG_ONE = 64

_LOG2E = 1.4426950408889634
_MAX_FIXED_SHIFT = 32.0
_UNDERFLOW_LOG2 = -152.0
_NORM_ROUNDING_SLACK = 1.01


def _params(sem):
    return pltpu.CompilerParams(dimension_semantics=sem, vmem_limit_bytes=_VMEM_LIMIT)


def _const_spec(shape):
    nd = len(shape)
    return pl.BlockSpec(shape, lambda *_: (0,) * nd, pipeline_mode=pl.Buffered(1))


def _rms(x, g):
    return x * lax.rsqrt(jnp.mean(x * x, axis=-1, keepdims=True) + _EPS) * g


def _rms_split(x, g):
    return (x * g).astype(_BF16), lax.rsqrt(jnp.mean(x * x, axis=-1, keepdims=True) + _EPS)


def _split3(v):
    hi = v.astype(_BF16).astype(_F32)
    r = v - hi
    mid = r.astype(_BF16).astype(_F32)
    return hi, mid, r - mid


def _in_proj_body(x_ref, g_ref, w32_ref, bf_ref, cos_ref, sin_ref, gq_ref, gk_ref, shift_ref,
                  tri_ref, avg_ref, dmask_ref, qdec_ref, kdec_ref, sdec_ref, bd_ref, gret_ref,
                  ret_ref, qa_ref, ka_ref, va_ref, fend_ref,
                  w_ref, wff_ref, carry_ref, rq_ref, rk_ref, rv_ref, gate_ref, state_ref, *, tm, tb, width):
    i = pl.program_id(1)
    n_main = 7 * width

    @pl.when((pl.program_id(0) == 0) & (i == 0))
    def _():
        for j in range(n_main // width):
            w_ref[:, j * width:(j + 1) * width] = jnp.transpose(
                w32_ref[j * width:(j + 1) * width, :]).astype(_BF16)
        tail = jnp.concatenate([w32_ref[n_main:, :],
                                jnp.zeros((_LANES - _N_HEADS, w32_ref.shape[1]), _F32)], axis=0)
        wff_ref[...] = jnp.transpose(tail).astype(_BF16)

    @pl.when(i == 0)
    def _():
        state_ref[...] = jnp.zeros_like(state_ref)
        carry_ref[...] = jnp.zeros_like(carry_ref)
    hb, inv_rms = _rms_split(x_ref[0], g_ref[...])

    def proj(j):
        return jnp.dot(hb, w_ref[:, j * width:(j + 1) * width], preferred_element_type=_F32) * inv_rms

    lane = lax.broadcasted_iota(jnp.int32, (tm, _LANES), 1)
    low = lane < _HEAD_DIM
    first_half = (lane & (_HEAD_DIM // 2)) == 0
    n_pairs = width // _LANES

    cos = cos_ref[...]
    sin = sin_ref[...]
    for j, out_ref, scale in ((0, rq_ref, _HEAD_DIM ** -0.5), (1, rk_ref, None)):
        y = proj(j)
        for c in range(n_pairs):
            blk = y[:, c * _LANES:(c + 1) * _LANES]
            swapped = jnp.where(first_half, pltpu.roll(blk, _LANES - _HEAD_DIM // 2, 1),
                                pltpu.roll(blk, _HEAD_DIM // 2, 1))
            r = blk * cos + swapped * sin
            if scale is not None:
                r = r * scale
            out_ref[:, c * _LANES:(c + 1) * _LANES] = r.astype(_BF16)
    rv_ref[...] = proj(2).astype(_BF16)
    gate = proj(3)
    gate_ref[...] = (gate * jax.nn.sigmoid(gate)).astype(_BF16)


    half = tm // 2
    z = jnp.concatenate([jnp.dot(hb[:half], wff_ref[...], preferred_element_type=_F32),
                         jnp.dot(hb[half:], wff_ref[...], preferred_element_type=_F32)],
                        axis=0) * inv_rms + bf_ref[...]
    logf = (jnp.minimum(z, 0.0) - jnp.log(1.0 + jnp.exp(-jnp.abs(z)))) * _LOG2E
    hi, mid, lo = _split3(logf)
    tri = tri_ref[...]
    split = jnp.concatenate([hi, mid, lo], axis=1).astype(_BF16)
    parts = jnp.concatenate(
        [jnp.dot(tri[:half, :half], split[:half], preferred_element_type=_F32),
         jnp.dot(tri[half:], split, preferred_element_type=_F32)], axis=0)
    csum = parts[:, :_LANES] + parts[:, _LANES:2 * _LANES] + parts[:, 2 * _LANES:]

    fcum = csum + carry_ref[0:1, :]
    carry_ref[...] = jnp.broadcast_to(fcum[tm - 1:tm, :], carry_ref.shape)
    fend_ref[0, 0] = carry_ref[...]

    def head_rms(y, g_row):
        out = []
        for g0 in range(0, width, _RET_GROUP):
            blk = y[:, g0:g0 + _RET_GROUP]
            ms = jnp.dot((blk * blk).astype(_BF16), avg_ref[...], preferred_element_type=_F32)
            out.append(blk * lax.rsqrt(ms + _EPS) * g_row[:, g0:g0 + _RET_GROUP])
        return out

    qn = head_rms(proj(4), gq_ref[...])
    kn = head_rms(proj(5), gk_ref[...])
    fv = proj(6)

    fh, fm, fl = _split3(fcum)
    n_parts = 3 * _N_HEADS
    packed = jnp.where(lane < _N_HEADS, fh, jnp.where(
        lane < 2 * _N_HEADS, pltpu.roll(fm, _N_HEADS, 1), pltpu.roll(fl, 2 * _N_HEADS, 1)))
    in_qf = (lane >= _AUG_QF) & (lane < _AUG_QF + n_parts)
    in_kf = (lane >= _AUG_KF) & (lane < _AUG_KF + n_parts)
    shared = jnp.where(in_qf, pltpu.roll(packed, _AUG_QF, 1), -pltpu.roll(packed, _AUG_KF, 1))
    q_bias = jnp.where(in_qf, shared, jnp.where(in_kf, 1.0, jnp.where(
        lane == _AUG_SHIFT, -shift_ref[...], 0.0)))
    v_const = jnp.where(lane == _AUG_ONE, 1.0, 0.0)
    for h in range(_N_HEADS):
        own = (lane & (_N_HEADS - 1)) == h
        k_bias = jnp.where(in_kf & own, shared,
                           jnp.where((in_qf & own) | (lane == _AUG_SHIFT), 1.0, 0.0))
        g0, c0 = divmod(h * _HEAD_DIM, _RET_GROUP)
        c0 = (c0 // _LANES) * _LANES
        qb = qn[g0][:, c0:c0 + _LANES]
        kb = kn[g0][:, c0:c0 + _LANES]
        vb = fv[:, (h // 2) * _LANES:(h // 2 + 1) * _LANES]
        if h % 2 == 1:
            qb = pltpu.roll(qb, _HEAD_DIM, 1)
            kb = pltpu.roll(kb, _HEAD_DIM, 1)
            vb = pltpu.roll(vb, _HEAD_DIM, 1)
        qa_ref[0, h] = jnp.where(low, qb, q_bias).astype(_BF16)
        ka_ref[0, h] = jnp.where(low, kb, k_bias).astype(_BF16)
        va_ref[0, h] = jnp.where(low, vb, v_const).astype(_BF16)

    for r0 in range(0, tm, tb):
        _retention_block(rq_ref, rk_ref, rv_ref, gate_ref, slice(r0, r0 + tb), dmask_ref, qdec_ref, kdec_ref,
                         sdec_ref, bd_ref, avg_ref, gret_ref, state_ref, ret_ref)


def _in_proj(x, g_mix, w_in_t, b_ff, cos_t, sin_t, gq, gk, shift, tri, ret_tables, g_ret, *, tm, tb):
    B, T, D = x.shape
    width = _N_HEADS * _HEAD_DIM
    assert w_in_t.shape == (7 * width + _N_HEADS, D) and tm % tb == 0
    dmask, qdec, kdec, sdec, bd, avg = ret_tables
    tok = lambda b, i: (b, i, 0)
    head = lambda b, i: (b, 0, i, 0)
    bf_tok = jax.ShapeDtypeStruct((B, T, width), _BF16)
    bf_head = jax.ShapeDtypeStruct((B, _N_HEADS, T, _LANES), _BF16)
    return pl.pallas_call(
        functools.partial(_in_proj_body, tm=tm, tb=tb, width=width),
        grid=(B, T // tm),
        in_specs=[
            pl.BlockSpec((1, tm, D), tok),
            _const_spec((1, D)),
            _const_spec(w_in_t.shape),
            _const_spec((1, _LANES)),
            pl.BlockSpec((tm, _LANES), lambda b, i: (i, 0)),
            pl.BlockSpec((tm, _LANES), lambda b, i: (i, 0)),
            _const_spec((1, width)),
            _const_spec((1, width)),
            _const_spec((1, _LANES)),
            _const_spec((tm, tm)),
            _const_spec(avg.shape),
            _const_spec(dmask.shape),
            _const_spec(qdec.shape),
            _const_spec(kdec.shape),
            _const_spec(sdec.shape),
            _const_spec(bd.shape),
            _const_spec(g_ret.shape),
        ],
        out_specs=[pl.BlockSpec((1, tm, width), tok)]
        + [pl.BlockSpec((1, _N_HEADS, tm, _LANES), head)] * 3
        + [pl.BlockSpec((1, 1, 8, _LANES), lambda b, i: (b, i, 0, 0))],
        out_shape=[bf_tok] + [bf_head] * 3 + [jax.ShapeDtypeStruct((B, T // tm, 8, _LANES), _F32)],
        scratch_shapes=[pltpu.VMEM((D, 7 * width), _BF16), pltpu.VMEM((D, _LANES), _BF16),
                        pltpu.VMEM((8, _LANES), _F32)]
        + [pltpu.VMEM((tm, width), _BF16)] * 4 + [pltpu.VMEM(sdec.shape, _F32)],
        compiler_params=_params(("arbitrary", "arbitrary")),
        name="in_proj",
    )(x, g_mix, w_in_t, b_ff, cos_t, sin_t, gq, gk, shift, tri, avg, dmask, qdec, kdec, sdec, bd, g_ret)


def _retention_block(rq_ref, rk_ref, rv_ref, gate_ref, rows, dmask_ref, qdec_ref, kdec_ref, sdec_ref, bd_ref,
                     avg_ref, g_ref, state_ref, o_ref):
    tb = rows.stop - rows.start
    nt = (((1,), (1,)), ((), ()))
    tn = (((0,), (0,)), ((), ()))
    gw = state_ref.shape[1]
    lane = lax.broadcasted_iota(jnp.int32, (1, _LANES), 1)
    low = lax.broadcasted_iota(jnp.int32, (tb, _LANES), 1) < _HEAD_DIM
    head_lanes = [jnp.where(lane < _HEAD_DIM, 1.0, 0.0).astype(_BF16),
                  jnp.where(lane < _HEAD_DIM, 0.0, 1.0).astype(_BF16)]
    n_groups = state_ref.shape[0]
    groups = [slice(g * gw, (g + 1) * gw) for g in range(n_groups)]
    scores = []
    for h in range(_N_HEADS):
        ps = slice((h // 2) * _LANES, (h // 2 + 1) * _LANES)
        scores.append(lax.dot_general(rq_ref[rows, ps] * head_lanes[h % 2], rk_ref[rows, ps], nt,
                                      preferred_element_type=_F32))
    inter = []
    for g, gs in enumerate(groups):
        state = state_ref[g]
        inter.append(jnp.dot(rq_ref[rows, gs], state.astype(_BF16),
                             preferred_element_type=_F32) * qdec_ref[:, gs])
        kd = (rk_ref[rows, gs].astype(_F32) * kdec_ref[:, gs]).astype(_BF16)
        state_ref[g] = state * sdec_ref[g] + lax.dot_general(
            kd, rv_ref[rows, gs], tn, preferred_element_type=_F32) * bd_ref[...]
    intra = []
    for h in range(_N_HEADS):
        ps = slice((h // 2) * _LANES, (h // 2 + 1) * _LANES)
        intra.append(jnp.dot((scores[h] * dmask_ref[h]).astype(_BF16), rv_ref[rows, ps],
                             preferred_element_type=_F32))
    outs = []
    for g, gs in enumerate(groups):
        pairs = [jnp.where(low, intra[2 * c], intra[2 * c + 1])
                 for c in range(g * gw // _LANES, (g + 1) * gw // _LANES)]
        outs.append(jnp.concatenate(pairs, axis=1) + inter[g])
    mus = [jnp.dot(o.astype(_BF16), avg_ref[...], preferred_element_type=_F32) for o in outs]
    cents = [o - mu for o, mu in zip(outs, mus)]
    vars_ = [jnp.dot((oc * oc).astype(_BF16), avg_ref[...], preferred_element_type=_F32) for oc in cents]
    for gs, oc, var in zip(groups, cents, vars_):
        y = oc * lax.rsqrt(var + _EPS) * g_ref[:, gs]
        o_ref[0, rows, gs] = (y * gate_ref[rows, gs].astype(_F32)).astype(_BF16)


def _causal_mask(tq):
    row = lax.broadcasted_iota(jnp.int32, (tq, tq), 0)
    col = lax.broadcasted_iota(jnp.int32, (tq, tq), 1)
    return row >= col


def _fox_finish(acc_refs, o_ref, tq):
    lane = lax.broadcasted_iota(jnp.int32, (tq, _LANES), 1)
    for c in range(len(acc_refs) // 2):
        pair = []
        for hh in (2 * c, 2 * c + 1):
            acc = acc_refs[hh][...]
            pair.append(acc / acc[:, _AUG_ONE:_AUG_ONE + 1])
        o_ref[0, :, c * _LANES:(c + 1) * _LANES] = jnp.where(
            lane < _HEAD_DIM, pair[0], pltpu.roll(pair[1], _HEAD_DIM, 1)).astype(_BF16)


def _fox_shifted_body(jlo_ref, qa_ref, ka_ref, va_ref, o_ref, acc_ref, it_h, it_i, it_j, it_slot, it_keep,
                      *, tq, nq, unroll):
    n_pair = qa_ref.shape[1]
    base = (pl.program_id(0) * pl.num_programs(1) + pl.program_id(1)) * (n_pair * nq)
    nt = (((1,), (1,)), ((), ()))
    half = tq // 2
    diff_top = (lax.broadcasted_iota(jnp.int32, (half, half), 1)
                - lax.broadcasted_iota(jnp.int32, (half, half), 0))
    diff_bot = (lax.broadcasted_iota(jnp.int32, (half, tq), 1)
                - lax.broadcasted_iota(jnp.int32, (half, tq), 0))

    def interleave(n_chains, qk, finish):
        s_prev = qk(0)
        for u in range(1, n_chains):
            s_next = qk(u)
            finish(u - 1, s_prev)
            s_prev = s_next
        finish(n_chains - 1, s_prev)

    n = jnp.int32(0)
    for hh in range(n_pair):
        for i in range(1, nq):
            lo = jlo_ref[base + hh * nq + i]

            def add(j, n, hh=hh, i=i, lo=lo):
                it_h[n] = jnp.int32(hh)
                it_i[n] = jnp.int32(i)
                it_j[n] = j
                it_slot[n] = jnp.int32(hh * nq + i)
                it_keep[n] = jnp.where(j == lo, 0, 1)
                return n + 1

            n = lax.fori_loop(lo, i, add, n)
    n_trips = (n + (unroll - 1)) // unroll

    def pad(m, carry):
        it_h[m] = jnp.int32(0)
        it_i[m] = jnp.int32(0)
        it_j[m] = jnp.int32(0)
        it_slot[m] = jnp.int32(n_pair * nq)
        it_keep[m] = jnp.int32(0)
        return carry

    lax.fori_loop(n, n_trips * unroll, pad, 0)

    def trip(t, carry):
        items = [tuple(ref[t * unroll + u] for ref in (it_h, it_i, it_j, it_slot, it_keep))
                 for u in range(unroll)]

        def qk(u):
            h, i, j = items[u][:3]
            return lax.dot_general(qa_ref[0, h, pl.ds(pl.multiple_of(i * tq, tq), tq), :],
                                   ka_ref[0, h, pl.ds(pl.multiple_of(j * tq, tq), tq), :], nt,
                                   preferred_element_type=_F32)

        def finish(u, s):
            h, _, j, slot, keep = items[u]
            pv = jnp.dot(jnp.exp2(s).astype(_BF16), va_ref[0, h, pl.ds(pl.multiple_of(j * tq, tq), tq), :],
                         preferred_element_type=_F32)
            acc_ref[slot] = jnp.where(keep > 0, acc_ref[slot], 0.0) + pv

        interleave(unroll, qk, finish)
        return carry

    lax.fori_loop(0, n_trips, trip, 0)

    chains = [(hh, i, bottom) for hh in range(n_pair) for i in range(nq) for bottom in (0, 1)]

    def diag_qk(u):
        hh, i, bottom = chains[u]
        r0 = i * tq + bottom * half
        width = tq if bottom else half
        return lax.dot_general(qa_ref[0, hh, r0:r0 + half, :], ka_ref[0, hh, i * tq:i * tq + width, :], nt,
                               preferred_element_type=_F32)

    def diag_finish(u, s):
        hh, i, bottom = chains[u]
        r0 = i * tq + bottom * half
        width = tq if bottom else half
        live = (diff_bot <= half) if bottom else (diff_top <= 0)
        p = jnp.exp2(jnp.where(live, s, _NEG_INF)).astype(_BF16)
        pv = jnp.dot(p, va_ref[0, hh, i * tq:i * tq + width, :], preferred_element_type=_F32)
        if i == 0:
            acc = pv
        else:
            has_off_diagonal = jlo_ref[base + hh * nq + i] < i
            acc = jnp.where(has_off_diagonal,
                            acc_ref[hh * nq + i, bottom * half:(bottom + 1) * half, :], 0.0) + pv
        o = acc / acc[:, _AUG_ONE:_AUG_ONE + 1]
        lanes = slice((hh % 2) * _HEAD_DIM, (hh % 2 + 1) * _HEAD_DIM)
        if hh % 2 == 1:
            o = pltpu.roll(o, _HEAD_DIM, 1)
        o_ref[0, r0:r0 + half, hh * _HEAD_DIM:(hh + 1) * _HEAD_DIM] = o[:, lanes].astype(_BF16)

    interleave(len(chains), diag_qk, diag_finish)


def _fox_online_body(qa_ref, ka_ref, va_ref, o_ref, *scratch, tq):
    qi = pl.program_id(2)
    nt = (((1,), (1,)), ((), ()))
    n_heads = qa_ref.shape[1]
    m_refs, acc_refs = scratch[:n_heads], scratch[n_heads:]
    for hh in range(n_heads):
        m_refs[hh][...] = jnp.full_like(m_refs[hh], _NEG_INF)
        acc_refs[hh][...] = jnp.zeros_like(acc_refs[hh])

    def step(j, masked):
        start = pl.multiple_of(j * tq, tq)
        logits = [lax.dot_general(qa_ref[0, hh], ka_ref[0, hh, pl.ds(start, tq), :], nt,
                                  preferred_element_type=_F32) for hh in range(n_heads)]
        for hh in range(n_heads):
            m_ref, acc_ref = m_refs[hh], acc_refs[hh]
            s = logits[hh]
            if masked:
                s = jnp.where(_causal_mask(tq), s, _NEG_INF)
            m_old = m_ref[...]
            m_new = jnp.maximum(m_old, jnp.max(s, axis=-1, keepdims=True))
            p = jnp.exp2(s - m_new[:, 0:1])
            acc_ref[...] = jnp.exp2(m_old - m_new) * acc_ref[...] + jnp.dot(
                p.astype(_BF16), va_ref[0, hh, pl.ds(start, tq), :], preferred_element_type=_F32)
            m_ref[...] = m_new

    def off_diag(j, carry):
        step(j, False)
        return carry

    lax.fori_loop(0, qi, off_diag, 0)
    step(qi, True)
    _fox_finish(acc_refs, o_ref, tq)


def _fox_shifted(jlo, qa, ka, va, *, tq, hg, unroll):
    B, H, T, L = qa.shape
    nq = T // tq
    assert tq % (2 * _MXU_TILE) == 0 and hg % 2 == 0 and H % hg == 0
    max_items = hg * nq * (nq - 1) // 2 + unroll
    blk = pl.BlockSpec((1, hg, T, L), lambda b, p, jlo_ref: (b, p, 0, 0))
    return pl.pallas_call(
        functools.partial(_fox_shifted_body, tq=tq, nq=nq, unroll=unroll),
        grid_spec=pltpu.PrefetchScalarGridSpec(
            num_scalar_prefetch=1,
            grid=(B, H // hg),
            in_specs=[blk, blk, blk],
            out_specs=pl.BlockSpec((1, T, hg * _HEAD_DIM), lambda b, p, jlo_ref: (b, 0, p)),
            scratch_shapes=[pltpu.VMEM((hg * nq + 1, tq, L), _F32)]
            + [pltpu.SMEM((max_items,), jnp.int32)] * 5,
        ),
        out_shape=jax.ShapeDtypeStruct((B, T, H * _HEAD_DIM), _BF16),
        compiler_params=_params(("arbitrary", "arbitrary")),
        name="fox_shifted",
    )(jlo, qa, ka, va)


def _fox_online(qa, ka, va, *, tq, hg):
    B, H, T, L = qa.shape
    return pl.pallas_call(
        functools.partial(_fox_online_body, tq=tq),
        grid=(B, H // hg, T // tq),
        in_specs=[
            pl.BlockSpec((1, hg, tq, L), lambda b, p, i: (b, p, i, 0)),
            pl.BlockSpec((1, hg, T, L), lambda b, p, i: (b, p, 0, 0)),
            pl.BlockSpec((1, hg, T, L), lambda b, p, i: (b, p, 0, 0)),
        ],
        out_specs=pl.BlockSpec((1, tq, hg * _HEAD_DIM), lambda b, p, i: (b, i, p)),
        out_shape=jax.ShapeDtypeStruct((B, T, H * _HEAD_DIM), _BF16),
        scratch_shapes=[pltpu.VMEM((tq, L), _F32)] * (2 * hg),
        compiler_params=_params(("arbitrary", "arbitrary", "arbitrary")),
        name="fox_online",
    )(qa, ka, va)


def _first_live_block(fend):
    B, nq, H = fend.shape
    f = jnp.transpose(fend, (0, 2, 1))
    top = jnp.concatenate([jnp.zeros((B, H, 1), _F32), f[:, :, :-1]], axis=-1)
    dead = (top[:, :, :, None] - f[:, :, None, :]) <= _UNDERFLOW_LOG2
    j = jnp.arange(nq, dtype=jnp.int32)
    before = j[None, :] < j[:, None]
    first_live = jnp.min(jnp.where(dead & before, nq, j), axis=-1)
    return first_live.astype(jnp.int32).reshape(-1)


def _mem_kv_body(mem_ref, gm_ref, w32_ref, gk_ref, k_ref, v_ref, w_ref, *, d_model, xd):
    @pl.when(pl.program_id(0) == 0)
    def _():
        w_ref[...] = w32_ref[...].astype(_BF16)

    mn = _rms(mem_ref[0], gm_ref[...]).astype(_BF16)
    kv = jnp.dot(mn, w_ref[...], preferred_element_type=_F32)
    for h in range(_N_XHEADS):
        sl = slice(h * xd, (h + 1) * xd)
        k_ref[0, :, sl] = (_rms(kv[:, sl], gk_ref[...]) * (xd ** -0.5)).astype(_BF16)
    v_ref[0] = kv[:, d_model:].astype(_BF16)


def _mem_kv(mem, g_mem, w_xkv, g_xk):
    B, M, D = mem.shape
    xd = D // _N_XHEADS
    blk = pl.BlockSpec((1, M, D), lambda b: (b, 0, 0))
    out = jax.ShapeDtypeStruct((B, M, D), _BF16)
    return pl.pallas_call(
        functools.partial(_mem_kv_body, d_model=D, xd=xd),
        grid=(B,),
        in_specs=[blk, _const_spec((1, D)), _const_spec(w_xkv.shape), _const_spec((1, xd))],
        out_specs=[blk, blk],
        out_shape=[out, out],
        scratch_shapes=[pltpu.VMEM(w_xkv.shape, _BF16)],
        compiler_params=_params(("arbitrary",)),
        name="mem_kv",
    )(mem, g_mem, w_xkv, g_xk)


def _mix_xattn_body(x_ref, ret_ref, fox_ref, wo32_ref, gx_ref, wq32_ref, gq_ref, k_ref, v_ref, wxo32_ref,
                    o_ref, wo_ref, wq_ref, wxo_ref, *, width, xd, sub):
    @pl.when((pl.program_id(0) == 0) & (pl.program_id(1) == 0))
    def _():
        wo_ref[...] = wo32_ref[...].astype(_BF16)
        wq_ref[...] = wq32_ref[...].astype(_BF16)
        wxo_ref[...] = wxo32_ref[...].astype(_BF16)

    nt = (((1,), (1,)), ((), ()))
    head_slices = [slice(h * xd, (h + 1) * xd) for h in range(_N_XHEADS)]

    def front(rows):
        h1 = (x_ref[0, rows, :]
              + jnp.dot(ret_ref[0, rows, :], wo_ref[:width, :], preferred_element_type=_F32)
              + jnp.dot(fox_ref[0, rows, :], wo_ref[width:, :], preferred_element_type=_F32))
        hn, inv_h = _rms_split(h1, gx_ref[...])
        y = jnp.dot(hn, wq_ref[...], preferred_element_type=_F32)
        logits = []
        for sl in head_slices:
            yh = y[:, sl]
            inv_q = lax.rsqrt(inv_h * inv_h * jnp.mean(yh * yh, axis=-1, keepdims=True) + _EPS)
            logits.append(lax.dot_general((yh * gq_ref[...]).astype(_BF16), k_ref[0, :, sl], nt,
                                          preferred_element_type=_F32) * (inv_q * inv_h * _LOG2E))
        return h1, logits

    def back(rows, h1, logits):
        probs = []
        for lg in logits:
            p = jnp.exp2(lg - jnp.max(lg, axis=-1, keepdims=True))
            probs.append((p / jnp.sum(p, axis=-1, keepdims=True)).astype(_BF16))
        heads = [jnp.dot(p, v_ref[0, :, sl], preferred_element_type=_F32).astype(_BF16)
                 for p, sl in zip(probs, head_slices)]
        o = jnp.concatenate(heads, axis=-1)
        o_ref[0, rows, :] = h1 + jnp.dot(o, wxo_ref[...], preferred_element_type=_F32)

    tm = x_ref.shape[1]
    subs = [slice(r0, r0 + sub) for r0 in range(0, tm, sub)]
    fronts = [front(rows) for rows in subs]
    for rows, (h1, logits) in zip(subs, fronts):
        back(rows, h1, logits)


def _mix_xattn(x, ret, fox, w_out, g_xattn, w_xq, g_xq, k, v, w_xo, *, tm, sub):
    B, T, D = x.shape
    W = ret.shape[-1]
    M = k.shape[1]
    xd = D // _N_XHEADS
    assert tm % sub == 0 and T % tm == 0
    tok = lambda b, i: (b, i, 0)
    return pl.pallas_call(
        functools.partial(_mix_xattn_body, width=W, xd=xd, sub=sub),
        grid=(B, T // tm),
        in_specs=[
            pl.BlockSpec((1, tm, D), tok),
            pl.BlockSpec((1, tm, W), tok),
            pl.BlockSpec((1, tm, W), tok),
            _const_spec(w_out.shape),
            _const_spec((1, D)),
            _const_spec(w_xq.shape),
            _const_spec((1, xd)),
            pl.BlockSpec((1, M, D), lambda b, i: (b, 0, 0)),
            pl.BlockSpec((1, M, D), lambda b, i: (b, 0, 0)),
            _const_spec(w_xo.shape),
        ],
        out_specs=pl.BlockSpec((1, tm, D), tok),
        out_shape=jax.ShapeDtypeStruct((B, T, D), _F32),
        scratch_shapes=[pltpu.VMEM(w.shape, _BF16) for w in (w_out, w_xq, w_xo)],
        compiler_params=_params(("arbitrary", "arbitrary")),
        name="mix_xattn",
    )(x, ret, fox, w_out, g_xattn, w_xq, g_xq, k, v, w_xo)


def _ffn_chunks(d_ff, n_chunks):
    tiles = -(-d_ff // _MXU_TILE)
    bounds = [min(d_ff, _MXU_TILE * ((tiles * c) // n_chunks)) for c in range(n_chunks)] + [d_ff]
    return [(lo, hi) for lo, hi in zip(bounds[:-1], bounds[1:]) if hi > lo]


def _ffn_body(h_ref, g_ref, wg32_ref, wu32_ref, wd32_ref, o_ref, wg_ref, wu_ref, wd_ref, acc0_ref,
              *, n_cast, ck, chunks):
    step = pl.program_id(0)

    def swiglu(hn, inv_rms, sl):
        gate = jnp.dot(hn, wg_ref[:, sl], preferred_element_type=_F32) * inv_rms
        up = jnp.dot(hn, wu_ref[:, sl], preferred_element_type=_F32) * inv_rms
        a = (gate * jax.nn.sigmoid(gate) * up).astype(_BF16)
        return jnp.dot(a, wd_ref[sl, :], preferred_element_type=_F32)

    for c in range(n_cast):
        @pl.when(step == c)
        def _(c=c):
            sl = slice(c * ck, (c + 1) * ck)
            wg_ref[:, sl] = wg32_ref[...].astype(_BF16)
            wu_ref[:, sl] = wu32_ref[...].astype(_BF16)
            wd_ref[sl, :] = wd32_ref[...].astype(_BF16)
            h = h_ref[0]
            part = swiglu(*_rms_split(h, g_ref[...]), sl)
            if c == 0:
                acc0_ref[...] = h + part
            elif c < n_cast - 1:
                acc0_ref[...] += part
            else:
                o_ref[0] = acc0_ref[...] + part

    @pl.when(step >= n_cast)
    def _():
        h = h_ref[0]
        hn, inv_rms = _rms_split(h, g_ref[...])
        acc = h
        for lo, hi in chunks:
            acc = acc + swiglu(hn, inv_rms, slice(lo, hi))
        o_ref[0] = acc


def _ffn(h, g_ffn, w_gate, w_up, w_down, *, tm, n_split):
    B, T, D = h.shape
    F = w_gate.shape[1]
    ck = _MXU_TILE
    assert F % ck == 0 and T % tm == 0
    n_cast = F // ck
    n_tiles = B * T // tm
    tile = lambda s: (jnp.maximum(s - (n_cast - 1), 0), 0, 0)
    chunk = lambda s: jnp.minimum(s, n_cast - 1)
    tok = pl.BlockSpec((1, tm, D), tile)
    out = pl.pallas_call(
        functools.partial(_ffn_body, n_cast=n_cast, ck=ck, chunks=_ffn_chunks(F, n_split)),
        grid=(n_cast - 1 + n_tiles,),
        in_specs=[tok, _const_spec((1, D)),
                  pl.BlockSpec((D, ck), lambda s: (0, chunk(s))),
                  pl.BlockSpec((D, ck), lambda s: (0, chunk(s))),
                  pl.BlockSpec((ck, D), lambda s: (chunk(s), 0))],
        out_specs=tok,
        out_shape=jax.ShapeDtypeStruct((n_tiles, tm, D), _F32),
        scratch_shapes=[pltpu.VMEM((D, F), _BF16), pltpu.VMEM((D, F), _BF16), pltpu.VMEM((F, D), _BF16),
                        pltpu.VMEM((tm, D), _F32)],
        compiler_params=_params(("arbitrary",)),
        name="ffn",
    )(h.reshape(n_tiles, tm, D), g_ffn, w_gate, w_up, w_down)
    return out.reshape(B, T, D)


def _rope_tables(T):
    half = _HEAD_DIM // 2
    inv_freq = (_ROPE_BASE ** (-np.arange(0, _HEAD_DIM, 2, dtype=np.float32) / _HEAD_DIM)).astype(np.float32)
    ang = (np.arange(T, dtype=np.float32)[:, None] * inv_freq[None, :]).astype(np.float32).astype(np.float64)
    cos, sin = np.cos(ang), np.sin(ang)
    reps = _LANES // _HEAD_DIM
    cos_t = np.tile(np.concatenate([cos, cos], axis=1), (1, reps))
    sin_t = np.tile(np.concatenate([-sin, sin], axis=1), (1, reps))
    return jnp.asarray(cos_t, _F32), jnp.asarray(sin_t, _F32)


def _retention_tables(tb):
    log_g = np.log(1.0 - 2.0 ** (-5.0 - np.arange(_N_HEADS, dtype=np.float64)))
    idx = np.arange(tb, dtype=np.float64)
    dist = np.abs(idx[:, None] - idx[None, :])
    chunk = np.arange(tb) // _RET_CHUNK
    visible = chunk[None, :] <= chunk[:, None]
    dmask = np.where(visible[None], np.exp(log_g[:, None, None] * dist[None]), 0.0)
    qdec = np.repeat(np.exp(log_g[None, :] * (idx[:, None] + 1.0)), _HEAD_DIM, axis=1)
    kdec = np.repeat(np.exp(log_g[None, :] * (tb - 1.0 - idx[:, None])), _HEAD_DIM, axis=1)
    heads_per_group = _RET_GROUP // _HEAD_DIM
    head_of = np.arange(_RET_GROUP) // _HEAD_DIM
    bd = (head_of[:, None] == head_of[None, :]).astype(np.float64)
    step_decay = np.exp(log_g * tb).reshape(-1, heads_per_group)
    sdec = bd[None] * np.repeat(step_decay, _HEAD_DIM, axis=1)[:, None, :]
    f = lambda a: jnp.asarray(a, _F32)
    return f(dmask), f(qdec), f(kdec), f(sdec), f(bd), jnp.asarray(bd / _HEAD_DIM, _BF16)


def _pad_lanes(a):
    return jnp.pad(a, [(0, 0)] * (a.ndim - 1) + [(0, _LANES - a.shape[-1])])


def kernel(x, mem, g_mix, w_in, b_forget, g_ret_out, g_fox_q, g_fox_k, w_out, g_xattn, w_xq, w_xkv,
           g_mem, g_xq, g_xk, w_xo, g_ffn, w_gate, w_up, w_down):
    B, T, D = x.shape
    width = _N_HEADS * _HEAD_DIM
    tb, tq, tm_mix, tm_ffn = 256, 512, 512, 256
    cos_t, sin_t = _rope_tables(T)
    ret_tables = _retention_tables(tb)
    tri = jnp.asarray(np.tril(np.ones((tq, tq), np.float32)), _BF16)
    row = lambda a: a.reshape(1, -1).astype(_F32)

    h = x
    for l in range(w_in.shape[0]):
        b_ff = _pad_lanes(row(b_forget[l]))
        gq = jnp.tile(row(g_fox_q[l]), (1, _N_HEADS)) * (_LOG2E * _HEAD_DIM ** -0.5)
        gk = jnp.tile(row(g_fox_k[l]), (1, _N_HEADS))
        bound = (_HEAD_DIM ** 0.5 * _NORM_ROUNDING_SLACK) * jnp.max(jnp.abs(g_fox_q[l])) * jnp.max(
            jnp.abs(g_fox_k[l]))
        use_shift = bound <= _MAX_FIXED_SHIFT
        shift = jnp.where(use_shift, jnp.ceil(bound * (4.0 * _LOG2E)) * 0.25, 0.0).astype(_F32)
        ret, qa, ka, va, fend = _in_proj(h, row(g_mix[l]), w_in[l].T, b_ff, cos_t, sin_t, gq, gk,
                                         jnp.full((1, _LANES), shift), tri, ret_tables,
                                         row(g_ret_out[l]), tm=tq, tb=tb)
        jlo = _first_live_block(fend[:, :, 0, :_N_HEADS])
        fox = lax.cond(use_shift,
                       lambda jlo, qa, ka, va: _fox_shifted(jlo, qa, ka, va, tq=tq, hg=4, unroll=6),
                       lambda jlo, qa, ka, va: _fox_online(qa, ka, va, tq=tq, hg=4), jlo, qa, ka, va)
        k, v = _mem_kv(mem, row(g_mem[l]), w_xkv[l], row(g_xk[l]))
        h = _mix_xattn(h, ret, fox, w_out[l], row(g_xattn[l]), w_xq[l], row(g_xq[l]), k, v, w_xo[l],
                       tm=tm_mix, sub=256)
        h = _ffn(h, row(g_ffn[l]), w_gate[l], w_up[l], w_down[l], tm=tm_ffn, n_split=2)
    return h
```

```python
import functools

import numpy as np
import jax
import jax.numpy as jnp
from jax import lax
from jax.experimental import pallas as pl
from jax.experimental.pallas import tpu as pltpu

_BF16 = jnp.bfloat16
_F32 = jnp.float32

_EPS = 1e-6
_NEG_INF = -1e30
_ROPE_BASE = 10000.0
_HEAD_DIM = 64
_N_HEADS = 8
_RET_CHUNK = 64
_RET_GROUP = 256
_N_XHEADS = 4
_LANES = 128
_MXU_TILE = 256
_VMEM_LIMIT = 56 * 1024 * 1024

_AUG_QF = 64
_AUG_KF = 88
_AUG_SHIFT = 112
_AUG_ONE = 64

_LOG2E = 1.4426950408889634
_MAX_FIXED_SHIFT = 32.0
_UNDERFLOW_LOG2 = -152.0
_NORM_ROUNDING_SLACK = 1.01


def _params(sem):
    return pltpu.CompilerParams(dimension_semantics=sem, vmem_limit_bytes=_VMEM_LIMIT)


def _const_spec(shape):
    nd = len(shape)
    return pl.BlockSpec(shape, lambda *_: (0,) * nd, pipeline_mode=pl.Buffered(1))


def _rms(x, g):
    return x * lax.rsqrt(jnp.mean(x * x, axis=-1, keepdims=True) + _EPS) * g


def _rms_split(x, g):
    return (x * g).astype(_BF16), lax.rsqrt(jnp.mean(x * x, axis=-1, keepdims=True) + _EPS)


def _split3(v):
    hi = v.astype(_BF16).astype(_F32)
    r = v - hi
    mid = r.astype(_BF16).astype(_F32)
    return hi, mid, r - mid


def _in_proj_body(x_ref, g_ref, w32_ref, bf_ref, cos_ref, sin_ref, gq_ref, gk_ref, shift_ref,
                  tri_ref, avg_ref, dmask_ref, qdec_ref, kdec_ref, sdec_ref, bd_ref, gret_ref,
                  ret_ref, qa_ref, ka_ref, va_ref, fend_ref,
                  w_ref, wff_ref, carry_ref, rq_ref, rk_ref, rv_ref, gate_ref, state_ref, *, tm, tb, width):
    i = pl.program_id(1)
    n_main = 7 * width

    @pl.when((pl.program_id(0) == 0) & (i == 0))
    def _():
        for j in range(n_main // width):
            w_ref[:, j * width:(j + 1) * width] = jnp.transpose(
                w32_ref[j * width:(j + 1) * width, :]).astype(_BF16)
        tail = jnp.concatenate([w32_ref[n_main:, :],
                                jnp.zeros((_LANES - _N_HEADS, w32_ref.shape[1]), _F32)], axis=0)
        wff_ref[...] = jnp.transpose(tail).astype(_BF16)

    @pl.when(i == 0)
    def _():
        state_ref[...] = jnp.zeros_like(state_ref)
        carry_ref[...] = jnp.zeros_like(carry_ref)
    hb, inv_rms = _rms_split(x_ref[0], g_ref[...])

    def proj(j):
        return jnp.dot(hb, w_ref[:, j * width:(j + 1) * width], preferred_element_type=_F32) * inv_rms

    lane = lax.broadcasted_iota(jnp.int32, (tm, _LANES), 1)
    low = lane < _HEAD_DIM
    first_half = (lane & (_HEAD_DIM // 2)) == 0
    n_pairs = width // _LANES

    cos = cos_ref[...]
    sin = sin_ref[...]
    for j, out_ref, scale in ((0, rq_ref, _HEAD_DIM ** -0.5), (1, rk_ref, None)):
        y = proj(j)
        for c in range(n_pairs):
            blk = y[:, c * _LANES:(c + 1) * _LANES]
            swapped = jnp.where(first_half, pltpu.roll(blk, _LANES - _HEAD_DIM // 2, 1),
                                pltpu.roll(blk, _HEAD_DIM // 2, 1))
            r = blk * cos + swapped * sin
            if scale is not None:
                r = r * scale
            out_ref[:, c * _LANES:(c + 1) * _LANES] = r.astype(_BF16)
    rv_ref[...] = proj(2).astype(_BF16)
    gate = proj(3)
    gate_ref[...] = (gate * jax.nn.sigmoid(gate)).astype(_BF16)


    half = tm // 2
    z = jnp.concatenate([jnp.dot(hb[:half], wff_ref[...], preferred_element_type=_F32),
                         jnp.dot(hb[half:], wff_ref[...], preferred_element_type=_F32)],
                        axis=0) * inv_rms + bf_ref[...]
    logf = (jnp.minimum(z, 0.0) - jnp.log(1.0 + jnp.exp(-jnp.abs(z)))) * _LOG2E
    hi, mid, lo = _split3(logf)
    tri = tri_ref[...]
    split = jnp.concatenate([hi, mid, lo], axis=1).astype(_BF16)
    parts = jnp.concatenate(
        [jnp.dot(tri[:half, :half], split[:half], preferred_element_type=_F32),
         jnp.dot(tri[half:], split, preferred_element_type=_F32)], axis=0)
    csum = parts[:, :_LANES] + parts[:, _LANES:2 * _LANES] + parts[:, 2 * _LANES:]

    fcum = csum + carry_ref[0:1, :]
    carry_ref[...] = jnp.broadcast_to(fcum[tm - 1:tm, :], carry_ref.shape)
    fend_ref[0, 0] = carry_ref[...]

    def head_rms(y, g_row):
        out = []
        for g0 in range(0, width, _RET_GROUP):
            blk = y[:, g0:g0 + _RET_GROUP]
            ms = jnp.dot((blk * blk).astype(_BF16), avg_ref[...], preferred_element_type=_F32)
            out.append(blk * lax.rsqrt(ms + _EPS) * g_row[:, g0:g0 + _RET_GROUP])
        return out

    qn = head_rms(proj(4), gq_ref[...])
    kn = head_rms(proj(5), gk_ref[...])
    fv = proj(6)

    fh, fm, fl = _split3(fcum)
    n_parts = 3 * _N_HEADS
    packed = jnp.where(lane < _N_HEADS, fh, jnp.where(
        lane < 2 * _N_HEADS, pltpu.roll(fm, _N_HEADS, 1), pltpu.roll(fl, 2 * _N_HEADS, 1)))
    in_qf = (lane >= _AUG_QF) & (lane < _AUG_QF + n_parts)
    in_kf = (lane >= _AUG_KF) & (lane < _AUG_KF + n_parts)
    shared = jnp.where(in_qf, pltpu.roll(packed, _AUG_QF, 1), -pltpu.roll(packed, _AUG_KF, 1))
    q_bias = jnp.where(in_qf, shared, jnp.where(in_kf, 1.0, jnp.where(
        lane == _AUG_SHIFT, -shift_ref[...], 0.0)))
    v_const = jnp.where(lane == _AUG_ONE, 1.0, 0.0)
    for h in range(_N_HEADS):
        own = (lane & (_N_HEADS - 1)) == h
        k_bias = jnp.where(in_kf & own, shared,
                           jnp.where((in_qf & own) | (lane == _AUG_SHIFT), 1.0, 0.0))
        g0, c0 = divmod(h * _HEAD_DIM, _RET_GROUP)
        c0 = (c0 // _LANES) * _LANES
        qb = qn[g0][:, c0:c0 + _LANES]
        kb = kn[g0][:, c0:c0 + _LANES]
        vb = fv[:, (h // 2) * _LANES:(h // 2 + 1) * _LANES]
        if h % 2 == 1:
            qb = pltpu.roll(qb, _HEAD_DIM, 1)
            kb = pltpu.roll(kb, _HEAD_DIM, 1)
            vb = pltpu.roll(vb, _HEAD_DIM, 1)
        qa_ref[0, h] = jnp.where(low, qb, q_bias).astype(_BF16)
        ka_ref[0, h] = jnp.where(low, kb, k_bias).astype(_BF16)
        va_ref[0, h] = jnp.where(low, vb, v_const).astype(_BF16)

    for r0 in range(0, tm, tb):
        _retention_block(rq_ref, rk_ref, rv_ref, gate_ref, slice(r0, r0 + tb), dmask_ref, qdec_ref, kdec_ref,
                         sdec_ref, bd_ref, avg_ref, gret_ref, state_ref, ret_ref)


def _in_proj(x, g_mix, w_in_t, b_ff, cos_t, sin_t, gq, gk, shift, tri, ret_tables, g_ret, *, tm, tb):
    B, T, D = x.shape
    width = _N_HEADS * _HEAD_DIM
    assert w_in_t.shape == (7 * width + _N_HEADS, D) and tm % tb == 0
    dmask, qdec, kdec, sdec, bd, avg = ret_tables
    tok = lambda b, i: (b, i, 0)
    head = lambda b, i: (b, 0, i, 0)
    bf_tok = jax.ShapeDtypeStruct((B, T, width), _BF16)
    bf_head = jax.ShapeDtypeStruct((B, _N_HEADS, T, _LANES), _BF16)
    return pl.pallas_call(
        functools.partial(_in_proj_body, tm=tm, tb=tb, width=width),
        grid=(B, T // tm),
        in_specs=[
            pl.BlockSpec((1, tm, D), tok),
            _const_spec((1, D)),
            _const_spec(w_in_t.shape),
            _const_spec((1, _LANES)),
            pl.BlockSpec((tm, _LANES), lambda b, i: (i, 0)),
            pl.BlockSpec((tm, _LANES), lambda b, i: (i, 0)),
            _const_spec((1, width)),
            _const_spec((1, width)),
            _const_spec((1, _LANES)),
            _const_spec((tm, tm)),
            _const_spec(avg.shape),
            _const_spec(dmask.shape),
            _const_spec(qdec.shape),
            _const_spec(kdec.shape),
            _const_spec(sdec.shape),
            _const_spec(bd.shape),
            _const_spec(g_ret.shape),
        ],
        out_specs=[pl.BlockSpec((1, tm, width), tok)]
        + [pl.BlockSpec((1, _N_HEADS, tm, _LANES), head)] * 3
        + [pl.BlockSpec((1, 1, 8, _LANES), lambda b, i: (b, i, 0, 0))],
        out_shape=[bf_tok] + [bf_head] * 3 + [jax.ShapeDtypeStruct((B, T // tm, 8, _LANES), _F32)],
        scratch_shapes=[pltpu.VMEM((D, 7 * width), _BF16), pltpu.VMEM((D, _LANES), _BF16),
                        pltpu.VMEM((8, _LANES), _F32)]
        + [pltpu.VMEM((tm, width), _BF16)] * 4 + [pltpu.VMEM(sdec.shape, _F32)],
        compiler_params=_params(("arbitrary", "arbitrary")),
        name="in_proj",
    )(x, g_mix, w_in_t, b_ff, cos_t, sin_t, gq, gk, shift, tri, avg, dmask, qdec, kdec, sdec, bd, g_ret)


def _retention_block(rq_ref, rk_ref, rv_ref, gate_ref, rows, dmask_ref, qdec_ref, kdec_ref, sdec_ref, bd_ref,
                     avg_ref, g_ref, state_ref, o_ref):
    tb = rows.stop - rows.start
    nt = (((1,), (1,)), ((), ()))
    tn = (((0,), (0,)), ((), ()))
    gw = state_ref.shape[1]
    lane = lax.broadcasted_iota(jnp.int32, (1, _LANES), 1)
    low = lax.broadcasted_iota(jnp.int32, (tb, _LANES), 1) < _HEAD_DIM
    head_lanes = [jnp.where(lane < _HEAD_DIM, 1.0, 0.0).astype(_BF16),
                  jnp.where(lane < _HEAD_DIM, 0.0, 1.0).astype(_BF16)]
    n_groups = state_ref.shape[0]
    groups = [slice(g * gw, (g + 1) * gw) for g in range(n_groups)]
    scores = []
    for h in range(_N_HEADS):
        ps = slice((h // 2) * _LANES, (h // 2 + 1) * _LANES)
        scores.append(lax.dot_general(rq_ref[rows, ps] * head_lanes[h % 2], rk_ref[rows, ps], nt,
                                      preferred_element_type=_F32))
    inter = []
    for g, gs in enumerate(groups):
        state = state_ref[g]
        inter.append(jnp.dot(rq_ref[rows, gs], state.astype(_BF16),
                             preferred_element_type=_F32) * qdec_ref[:, gs])
        kd = (rk_ref[rows, gs].astype(_F32) * kdec_ref[:, gs]).astype(_BF16)
        state_ref[g] = state * sdec_ref[g] + lax.dot_general(
            kd, rv_ref[rows, gs], tn, preferred_element_type=_F32) * bd_ref[...]
    intra = []
    for h in range(_N_HEADS):
        ps = slice((h // 2) * _LANES, (h // 2 + 1) * _LANES)
        intra.append(jnp.dot((scores[h] * dmask_ref[h]).astype(_BF16), rv_ref[rows, ps],
                             preferred_element_type=_F32))
    outs = []
    for g, gs in enumerate(groups):
        pairs = [jnp.where(low, intra[2 * c], intra[2 * c + 1])
                 for c in range(g * gw // _LANES, (g + 1) * gw // _LANES)]
        outs.append(jnp.concatenate(pairs, axis=1) + inter[g])
    mus = [jnp.dot(o.astype(_BF16), avg_ref[...], preferred_element_type=_F32) for o in outs]
    cents = [o - mu for o, mu in zip(outs, mus)]
    vars_ = [jnp.dot((oc * oc).astype(_BF16), avg_ref[...], preferred_element_type=_F32) for oc in cents]
    for gs, oc, var in zip(groups, cents, vars_):
        y = oc * lax.rsqrt(var + _EPS) * g_ref[:, gs]
        o_ref[0, rows, gs] = (y * gate_ref[rows, gs].astype(_F32)).astype(_BF16)


def _causal_mask(tq):
    row = lax.broadcasted_iota(jnp.int32, (tq, tq), 0)
    col = lax.broadcasted_iota(jnp.int32, (tq, tq), 1)
    return row >= col


def _fox_finish(acc_refs, o_ref, tq):
    lane = lax.broadcasted_iota(jnp.int32, (tq, _LANES), 1)
    for c in range(len(acc_refs) // 2):
        pair = []
        for hh in (2 * c, 2 * c + 1):
            acc = acc_refs[hh][...]
            pair.append(acc / acc[:, _AUG_ONE:_AUG_ONE + 1])
        o_ref[0, :, c * _LANES:(c + 1) * _LANES] = jnp.where(
            lane < _HEAD_DIM, pair[0], pltpu.roll(pair[1], _HEAD_DIM, 1)).astype(_BF16)


def _fox_shifted_body(jlo_ref, qa_ref, ka_ref, va_ref, o_ref, acc_ref, it_h, it_i, it_j, it_slot, it_keep,
                      *, tq, nq, unroll):
    n_pair = qa_ref.shape[1]
    base = (pl.program_id(0) * pl.num_programs(1) + pl.program_id(1)) * (n_pair * nq)
    nt = (((1,), (1,)), ((), ()))
    half = tq // 2
    diff_top = (lax.broadcasted_iota(jnp.int32, (half, half), 1)
                - lax.broadcasted_iota(jnp.int32, (half, half), 0))
    diff_bot = (lax.broadcasted_iota(jnp.int32, (half, tq), 1)
                - lax.broadcasted_iota(jnp.int32, (half, tq), 0))

    def interleave(n_chains, qk, finish):
        s_prev = qk(0)
        for u in range(1, n_chains):
            s_next = qk(u)
            finish(u - 1, s_prev)
            s_prev = s_next
        finish(n_chains - 1, s_prev)

    @pl.when((pl.program_id(0) == 0) & (pl.program_id(1) == 0))
    def _():
        acc_ref[...] = jnp.zeros_like(acc_ref)

    n = jnp.int32(0)
    for hh in range(n_pair):
        for i in range(1, nq):
            lo = jlo_ref[base + hh * nq + i]

            def add(j, n, hh=hh, i=i, lo=lo):
                it_h[n] = jnp.int32(hh)
                it_i[n] = jnp.int32(i)
                it_j[n] = j
                it_slot[n] = jnp.int32(hh * nq + i)
                it_keep[n] = jnp.where(j == lo, 0, 1)
                return n + 1

            n = lax.fori_loop(lo, i, add, n)
    n_trips = (n + (unroll - 1)) // unroll

    def pad(m, carry):
        it_h[m] = jnp.int32(0)
        it_i[m] = jnp.int32(0)
        it_j[m] = jnp.int32(0)
        it_slot[m] = jnp.int32(n_pair * nq)
        it_keep[m] = jnp.int32(0)
        return carry

    lax.fori_loop(n, n_trips * unroll, pad, 0)

    def trip(t, carry):
        items = [tuple(ref[t * unroll + u] for ref in (it_h, it_i, it_j, it_slot, it_keep))
                 for u in range(unroll)]

        def qk(u):
            h, i, j = items[u][:3]
            return lax.dot_general(qa_ref[0, h, pl.ds(pl.multiple_of(i * tq, tq), tq), :],
                                   ka_ref[0, h, pl.ds(pl.multiple_of(j * tq, tq), tq), :], nt,
                                   preferred_element_type=_F32)

        def finish(u, s):
            h, _, j, slot, keep = items[u]
            pv = jnp.dot(jnp.exp2(s).astype(_BF16), va_ref[0, h, pl.ds(pl.multiple_of(j * tq, tq), tq), :],
                         preferred_element_type=_F32)
            acc_ref[slot] = jnp.where(keep > 0, acc_ref[slot], 0.0) + pv

        interleave(unroll, qk, finish)
        return carry

    lax.fori_loop(0, n_trips, trip, 0)

    chains = [(hh, i, bottom) for hh in range(n_pair) for i in range(nq) for bottom in (0, 1)]

    def diag_qk(u):
        hh, i, bottom = chains[u]
        r0 = i * tq + bottom * half
        width = tq if bottom else half
        return lax.dot_general(qa_ref[0, hh, r0:r0 + half, :], ka_ref[0, hh, i * tq:i * tq + width, :], nt,
                               preferred_element_type=_F32)

    def diag_finish(u, s):
        hh, i, bottom = chains[u]
        r0 = i * tq + bottom * half
        width = tq if bottom else half
        live = (diff_bot <= half) if bottom else (diff_top <= 0)
        p = jnp.exp2(jnp.where(live, s, _NEG_INF)).astype(_BF16)
        pv = jnp.dot(p, va_ref[0, hh, i * tq:i * tq + width, :], preferred_element_type=_F32)
        if i == 0:
            acc = pv
        else:
            has_off_diagonal = jlo_ref[base + hh * nq + i] < i
            acc = jnp.where(has_off_diagonal,
                            acc_ref[hh * nq + i, bottom * half:(bottom + 1) * half, :], 0.0) + pv
        o = acc / acc[:, _AUG_ONE:_AUG_ONE + 1]
        lanes = slice((hh % 2) * _HEAD_DIM, (hh % 2 + 1) * _HEAD_DIM)
        if hh % 2 == 1:
            o = pltpu.roll(o, _HEAD_DIM, 1)
        o_ref[0, r0:r0 + half, hh * _HEAD_DIM:(hh + 1) * _HEAD_DIM] = o[:, lanes].astype(_BF16)

    interleave(len(chains), diag_qk, diag_finish)


def _fox_online_body(qa_ref, ka_ref, va_ref, o_ref, *scratch, tq):
    qi = pl.program_id(2)
    nt = (((1,), (1,)), ((), ()))
    n_heads = qa_ref.shape[1]
    m_refs, acc_refs = scratch[:n_heads], scratch[n_heads:]
    for hh in range(n_heads):
        m_refs[hh][...] = jnp.full_like(m_refs[hh], _NEG_INF)
        acc_refs[hh][...] = jnp.zeros_like(acc_refs[hh])

    def step(j, masked):
        start = pl.multiple_of(j * tq, tq)
        logits = [lax.dot_general(qa_ref[0, hh], ka_ref[0, hh, pl.ds(start, tq), :], nt,
                                  preferred_element_type=_F32) for hh in range(n_heads)]
        for hh in range(n_heads):
            m_ref, acc_ref = m_refs[hh], acc_refs[hh]
            s = logits[hh]
            if masked:
                s = jnp.where(_causal_mask(tq), s, _NEG_INF)
            m_old = m_ref[...]
            m_new = jnp.maximum(m_old, jnp.max(s, axis=-1, keepdims=True))
            p = jnp.exp2(s - m_new[:, 0:1])
            acc_ref[...] = jnp.exp2(m_old - m_new) * acc_ref[...] + jnp.dot(
                p.astype(_BF16), va_ref[0, hh, pl.ds(start, tq), :], preferred_element_type=_F32)
            m_ref[...] = m_new

    def off_diag(j, carry):
        step(j, False)
        return carry

    lax.fori_loop(0, qi, off_diag, 0)
    step(qi, True)
    _fox_finish(acc_refs, o_ref, tq)


def _fox_shifted(jlo, qa, ka, va, *, tq, hg, unroll):
    B, H, T, L = qa.shape
    nq = T // tq
    assert tq % (2 * _MXU_TILE) == 0 and hg % 2 == 0 and H % hg == 0
    max_items = hg * nq * (nq - 1) // 2 + unroll
    blk = pl.BlockSpec((1, hg, T, L), lambda b, p, jlo_ref: (b, p, 0, 0))
    return pl.pallas_call(
        functools.partial(_fox_shifted_body, tq=tq, nq=nq, unroll=unroll),
        grid_spec=pltpu.PrefetchScalarGridSpec(
            num_scalar_prefetch=1,
            grid=(B, H // hg),
            in_specs=[blk, blk, blk],
            out_specs=pl.BlockSpec((1, T, hg * _HEAD_DIM), lambda b, p, jlo_ref: (b, 0, p)),
            scratch_shapes=[pltpu.VMEM((hg * nq + 1, tq, L), _F32)]
            + [pltpu.SMEM((max_items,), jnp.int32)] * 5,
        ),
        out_shape=jax.ShapeDtypeStruct((B, T, H * _HEAD_DIM), _BF16),
        compiler_params=_params(("arbitrary", "arbitrary")),
        name="fox_shifted",
    )(jlo, qa, ka, va)


def _fox_online(qa, ka, va, *, tq, hg):
    B, H, T, L = qa.shape
    return pl.pallas_call(
        functools.partial(_fox_online_body, tq=tq),
        grid=(B, H // hg, T // tq),
        in_specs=[
            pl.BlockSpec((1, hg, tq, L), lambda b, p, i: (b, p, i, 0)),
            pl.BlockSpec((1, hg, T, L), lambda b, p, i: (b, p, 0, 0)),
            pl.BlockSpec((1, hg, T, L), lambda b, p, i: (b, p, 0, 0)),
        ],
        out_specs=pl.BlockSpec((1, tq, hg * _HEAD_DIM), lambda b, p, i: (b, i, p)),
        out_shape=jax.ShapeDtypeStruct((B, T, H * _HEAD_DIM), _BF16),
        scratch_shapes=[pltpu.VMEM((tq, L), _F32)] * (2 * hg),
        compiler_params=_params(("arbitrary", "arbitrary", "arbitrary")),
        name="fox_online",
    )(qa, ka, va)


def _first_live_block(fend):
    B, nq, H = fend.shape
    f = jnp.transpose(fend, (0, 2, 1))
    top = jnp.concatenate([jnp.zeros((B, H, 1), _F32), f[:, :, :-1]], axis=-1)
    dead = (top[:, :, :, None] - f[:, :, None, :]) <= _UNDERFLOW_LOG2
    j = jnp.arange(nq, dtype=jnp.int32)
    before = j[None, :] < j[:, None]
    first_live = jnp.min(jnp.where(dead & before, nq, j), axis=-1)
    return first_live.astype(jnp.int32).reshape(-1)


def _mem_kv_body(mem_ref, gm_ref, w32_ref, gk_ref, k_ref, v_ref, w_ref, *, d_model, xd):
    @pl.when(pl.program_id(0) == 0)
    def _():
        w_ref[...] = w32_ref[...].astype(_BF16)

    mn = _rms(mem_ref[0], gm_ref[...]).astype(_BF16)
    kv = jnp.dot(mn, w_ref[...], preferred_element_type=_F32)
    for h in range(_N_XHEADS):
        sl = slice(h * xd, (h + 1) * xd)
        k_ref[0, :, sl] = (_rms(kv[:, sl], gk_ref[...]) * (xd ** -0.5)).astype(_BF16)
    v_ref[0] = kv[:, d_model:].astype(_BF16)


def _mem_kv(mem, g_mem, w_xkv, g_xk):
    B, M, D = mem.shape
    xd = D // _N_XHEADS
    blk = pl.BlockSpec((1, M, D), lambda b: (b, 0, 0))
    out = jax.ShapeDtypeStruct((B, M, D), _BF16)
    return pl.pallas_call(
        functools.partial(_mem_kv_body, d_model=D, xd=xd),
        grid=(B,),
        in_specs=[blk, _const_spec((1, D)), _const_spec(w_xkv.shape), _const_spec((1, xd))],
        out_specs=[blk, blk],
        out_shape=[out, out],
        scratch_shapes=[pltpu.VMEM(w_xkv.shape, _BF16)],
        compiler_params=_params(("arbitrary",)),
        name="mem_kv",
    )(mem, g_mem, w_xkv, g_xk)


def _mix_xattn_body(x_ref, ret_ref, fox_ref, wo32_ref, gx_ref, wq32_ref, gq_ref, k_ref, v_ref, wxo32_ref,
                    o_ref, wo_ref, wq_ref, wxo_ref, *, width, xd, sub):
    @pl.when((pl.program_id(0) == 0) & (pl.program_id(1) == 0))
    def _():
        wo_ref[...] = wo32_ref[...].astype(_BF16)
        wq_ref[...] = wq32_ref[...].astype(_BF16)
        wxo_ref[...] = wxo32_ref[...].astype(_BF16)

    nt = (((1,), (1,)), ((), ()))
    head_slices = [slice(h * xd, (h + 1) * xd) for h in range(_N_XHEADS)]

    def front(rows):
        h1 = (x_ref[0, rows, :]
              + jnp.dot(ret_ref[0, rows, :], wo_ref[:width, :], preferred_element_type=_F32)
              + jnp.dot(fox_ref[0, rows, :], wo_ref[width:, :], preferred_element_type=_F32))
        hn, inv_h = _rms_split(h1, gx_ref[...])
        y = jnp.dot(hn, wq_ref[...], preferred_element_type=_F32)
        logits = []
        for sl in head_slices:
            yh = y[:, sl]
            inv_q = lax.rsqrt(inv_h * inv_h * jnp.mean(yh * yh, axis=-1, keepdims=True) + _EPS)
            logits.append(lax.dot_general((yh * gq_ref[...]).astype(_BF16), k_ref[0, :, sl], nt,
                                          preferred_element_type=_F32) * (inv_q * inv_h * _LOG2E))
        return h1, logits

    def back(rows, h1, logits):
        probs = []
        for lg in logits:
            p = jnp.exp2(lg - jnp.max(lg, axis=-1, keepdims=True))
            probs.append((p / jnp.sum(p, axis=-1, keepdims=True)).astype(_BF16))
        heads = [jnp.dot(p, v_ref[0, :, sl], preferred_element_type=_F32).astype(_BF16)
                 for p, sl in zip(probs, head_slices)]
        o = jnp.concatenate(heads, axis=-1)
        o_ref[0, rows, :] = h1 + jnp.dot(o, wxo_ref[...], preferred_element_type=_F32)

    tm = x_ref.shape[1]
    subs = [slice(r0, r0 + sub) for r0 in range(0, tm, sub)]
    fronts = [front(rows) for rows in subs]
    for rows, (h1, logits) in zip(subs, fronts):
        back(rows, h1, logits)


def _mix_xattn(x, ret, fox, w_out, g_xattn, w_xq, g_xq, k, v, w_xo, *, tm, sub):
    B, T, D = x.shape
    W = ret.shape[-1]
    M = k.shape[1]
    xd = D // _N_XHEADS
    assert tm % sub == 0 and T % tm == 0
    tok = lambda b, i: (b, i, 0)
    return pl.pallas_call(
        functools.partial(_mix_xattn_body, width=W, xd=xd, sub=sub),
        grid=(B, T // tm),
        in_specs=[
            pl.BlockSpec((1, tm, D), tok),
            pl.BlockSpec((1, tm, W), tok),
            pl.BlockSpec((1, tm, W), tok),
            _const_spec(w_out.shape),
            _const_spec((1, D)),
            _const_spec(w_xq.shape),
            _const_spec((1, xd)),
            pl.BlockSpec((1, M, D), lambda b, i: (b, 0, 0)),
            pl.BlockSpec((1, M, D), lambda b, i: (b, 0, 0)),
            _const_spec(w_xo.shape),
        ],
        out_specs=pl.BlockSpec((1, tm, D), tok),
        out_shape=jax.ShapeDtypeStruct((B, T, D), _F32),
        scratch_shapes=[pltpu.VMEM(w.shape, _BF16) for w in (w_out, w_xq, w_xo)],
        compiler_params=_params(("arbitrary", "arbitrary")),
        name="mix_xattn",
    )(x, ret, fox, w_out, g_xattn, w_xq, g_xq, k, v, w_xo)


def _ffn_chunks(d_ff, n_chunks):
    tiles = -(-d_ff // _MXU_TILE)
    bounds = [min(d_ff, _MXU_TILE * ((tiles * c) // n_chunks)) for c in range(n_chunks)] + [d_ff]
    return [(lo, hi) for lo, hi in zip(bounds[:-1], bounds[1:]) if hi > lo]


def _ffn_body(h_ref, g_ref, wg32_ref, wu32_ref, wd32_ref, o_ref, wg_ref, wu_ref, wd_ref, acc0_ref,
              *, n_cast, ck, chunks):
    step = pl.program_id(0)

    def swiglu(hn, inv_rms, sl):
        gate = jnp.dot(hn, wg_ref[:, sl], preferred_element_type=_F32) * inv_rms
        up = jnp.dot(hn, wu_ref[:, sl], preferred_element_type=_F32) * inv_rms
        a = (gate * jax.nn.sigmoid(gate) * up).astype(_BF16)
        return jnp.dot(a, wd_ref[sl, :], preferred_element_type=_F32)

    for c in range(n_cast):
        @pl.when(step == c)
        def _(c=c):
            sl = slice(c * ck, (c + 1) * ck)
            wg_ref[:, sl] = wg32_ref[...].astype(_BF16)
            wu_ref[:, sl] = wu32_ref[...].astype(_BF16)
            wd_ref[sl, :] = wd32_ref[...].astype(_BF16)
            h = h_ref[0]
            part = swiglu(*_rms_split(h, g_ref[...]), sl)
            if c == 0:
                acc0_ref[...] = h + part
            elif c < n_cast - 1:
                acc0_ref[...] += part
            else:
                o_ref[0] = acc0_ref[...] + part

    @pl.when(step >= n_cast)
    def _():
        h = h_ref[0]
        hn, inv_rms = _rms_split(h, g_ref[...])
        acc = h
        for lo, hi in chunks:
            acc = acc + swiglu(hn, inv_rms, slice(lo, hi))
        o_ref[0] = acc


def _ffn(h, g_ffn, w_gate, w_up, w_down, *, tm, n_split):
    B, T, D = h.shape
    F = w_gate.shape[1]
    ck = _MXU_TILE
    assert F % ck == 0 and T % tm == 0
    n_cast = F // ck
    n_tiles = B * T // tm
    tile = lambda s: (jnp.maximum(s - (n_cast - 1), 0), 0, 0)
    chunk = lambda s: jnp.minimum(s, n_cast - 1)
    tok = pl.BlockSpec((1, tm, D), tile)
    out = pl.pallas_call(
        functools.partial(_ffn_body, n_cast=n_cast, ck=ck, chunks=_ffn_chunks(F, n_split)),
        grid=(n_cast - 1 + n_tiles,),
        in_specs=[tok, _const_spec((1, D)),
                  pl.BlockSpec((D, ck), lambda s: (0, chunk(s))),
                  pl.BlockSpec((D, ck), lambda s: (0, chunk(s))),
                  pl.BlockSpec((ck, D), lambda s: (chunk(s), 0))],
        out_specs=tok,
        out_shape=jax.ShapeDtypeStruct((n_tiles, tm, D), _F32),
        scratch_shapes=[pltpu.VMEM((D, F), _BF16), pltpu.VMEM((D, F), _BF16), pltpu.VMEM((F, D), _BF16),
                        pltpu.VMEM((tm, D), _F32)],
        compiler_params=_params(("arbitrary",)),
        name="ffn",
    )(h.reshape(n_tiles, tm, D), g_ffn, w_gate, w_up, w_down)
    return out.reshape(B, T, D)


def _rope_tables(T):
    half = _HEAD_DIM // 2
    inv_freq = (_ROPE_BASE ** (-np.arange(0, _HEAD_DIM, 2, dtype=np.float32) / _HEAD_DIM)).astype(np.float32)
    ang = (np.arange(T, dtype=np.float32)[:, None] * inv_freq[None, :]).astype(np.float32).astype(np.float64)
    cos, sin = np.cos(ang), np.sin(ang)
    reps = _LANES // _HEAD_DIM
    cos_t = np.tile(np.concatenate([cos, cos], axis=1), (1, reps))
    sin_t = np.tile(np.concatenate([-sin, sin], axis=1), (1, reps))
    return jnp.asarray(cos_t, _F32), jnp.asarray(sin_t, _F32)


def _retention_tables(tb):
    log_g = np.log(1.0 - 2.0 ** (-5.0 - np.arange(_N_HEADS, dtype=np.float64)))
    idx = np.arange(tb, dtype=np.float64)
    dist = np.abs(idx[:, None] - idx[None, :])
    chunk = np.arange(tb) // _RET_CHUNK
    visible = chunk[None, :] <= chunk[:, None]
    dmask = np.where(visible[None], np.exp(log_g[:, None, None] * dist[None]), 0.0)
    qdec = np.repeat(np.exp(log_g[None, :] * (idx[:, None] + 1.0)), _HEAD_DIM, axis=1)
    kdec = np.repeat(np.exp(log_g[None, :] * (tb - 1.0 - idx[:, None])), _HEAD_DIM, axis=1)
    heads_per_group = _RET_GROUP // _HEAD_DIM
    head_of = np.arange(_RET_GROUP) // _HEAD_DIM
    bd = (head_of[:, None] == head_of[None, :]).astype(np.float64)
    step_decay = np.exp(log_g * tb).reshape(-1, heads_per_group)
    sdec = bd[None] * np.repeat(step_decay, _HEAD_DIM, axis=1)[:, None, :]
    f = lambda a: jnp.asarray(a, _F32)
    return f(dmask), f(qdec), f(kdec), f(sdec), f(bd), jnp.asarray(bd / _HEAD_DIM, _BF16)


def _pad_lanes(a):
    return jnp.pad(a, [(0, 0)] * (a.ndim - 1) + [(0, _LANES - a.shape[-1])])


def kernel(x, mem, g_mix, w_in, b_forget, g_ret_out, g_fox_q, g_fox_k, w_out, g_xattn, w_xq, w_xkv,
           g_mem, g_xq, g_xk, w_xo, g_ffn, w_gate, w_up, w_down):
    B, T, D = x.shape
    width = _N_HEADS * _HEAD_DIM
    tb, tq, tm_mix, tm_ffn = 256, 512, 512, 512
    cos_t, sin_t = _rope_tables(T)
    ret_tables = _retention_tables(tb)
    tri = jnp.asarray(np.tril(np.ones((tq, tq), np.float32)), _BF16)
    row = lambda a: a.reshape(1, -1).astype(_F32)

    h = x
    for l in range(w_in.shape[0]):
        b_ff = _pad_lanes(row(b_forget[l]))
        gq = jnp.tile(row(g_fox_q[l]), (1, _N_HEADS)) * (_LOG2E * _HEAD_DIM ** -0.5)
        gk = jnp.tile(row(g_fox_k[l]), (1, _N_HEADS))
        bound = (_HEAD_DIM ** 0.5 * _NORM_ROUNDING_SLACK) * jnp.max(jnp.abs(g_fox_q[l])) * jnp.max(
            jnp.abs(g_fox_k[l]))
        use_shift = bound <= _MAX_FIXED_SHIFT
        shift = jnp.where(use_shift, jnp.ceil(bound * (4.0 * _LOG2E)) * 0.25, 0.0).astype(_F32)
        ret, qa, ka, va, fend = _in_proj(h, row(g_mix[l]), w_in[l].T, b_ff, cos_t, sin_t, gq, gk,
                                         jnp.full((1, _LANES), shift), tri, ret_tables,
                                         row(g_ret_out[l]), tm=tq, tb=tb)
        jlo = _first_live_block(fend[:, :, 0, :_N_HEADS])
        fox = lax.cond(use_shift,
                       lambda jlo, qa, ka, va: _fox_shifted(jlo, qa, ka, va, tq=tq, hg=4, unroll=6),
                       lambda jlo, qa, ka, va: _fox_online(qa, ka, va, tq=tq, hg=4), jlo, qa, ka, va)
        k, v = _mem_kv(mem, row(g_mem[l]), w_xkv[l], row(g_xk[l]))
        h = _mix_xattn(h, ret, fox, w_out[l], row(g_xattn[l]), w_xq[l], row(g_xq[l]), k, v, w_xo[l],
                       tm=tm_mix, sub=256)
        h = _ffn(h, row(g_ffn[l]), w_gate[l], w_up[l], w_down[l], tm=tm_ffn, n_split=2)
    return h
```

```python
import functools

import numpy as np
import jax
import jax.numpy as jnp
from jax import lax
from jax.experimental import pallas as pl
from jax.experimental.pallas import tpu as pltpu

_BF16 = jnp.bfloat16
_F32 = jnp.float32

_EPS = 1e-6
_NEG_INF = -1e30
_ROPE_BASE = 10000.0
_HEAD_DIM = 64
_N_HEADS = 8
_RET_CHUNK = 64
_RET_GROUP = 256
_N_XHEADS = 4
_LANES = 128
_MXU_TILE = 256
_VMEM_LIMIT = 56 * 1024 * 1024

_AUG_QF = 64
_AUG_KF = 88
_AUG_SHIFT = 112
_AUG_ONE = 64

_LOG2E = 1.4426950408889634
_MAX_FIXED_SHIFT = 32.0
_KSPLIT = 2
_UNDERFLOW_LOG2 = -152.0
_NORM_ROUNDING_SLACK = 1.01


def _params(sem):
    return pltpu.CompilerParams(dimension_semantics=sem, vmem_limit_bytes=_VMEM_LIMIT)


def _const_spec(shape):
    nd = len(shape)
    return pl.BlockSpec(shape, lambda *_: (0,) * nd, pipeline_mode=pl.Buffered(1))


def _rms(x, g):
    return x * lax.rsqrt(jnp.mean(x * x, axis=-1, keepdims=True) + _EPS) * g


def _rms_split(x, g):
    return (x * g).astype(_BF16), lax.rsqrt(jnp.mean(x * x, axis=-1, keepdims=True) + _EPS)


def _split3(v):
    hi = v.astype(_BF16).astype(_F32)
    r = v - hi
    mid = r.astype(_BF16).astype(_F32)
    return hi, mid, r - mid


def _in_proj_body(x_ref, g_ref, w32_ref, bf_ref, cos_ref, sin_ref, gq_ref, gk_ref, shift_ref,
                  tri_ref, avg_ref, dmask_ref, qdec_ref, kdec_ref, sdec_ref, bd_ref, gret_ref,
                  ret_ref, qa_ref, ka_ref, va_ref, fend_ref,
                  w_ref, wff_ref, carry_ref, rq_ref, rk_ref, rv_ref, gate_ref, state_ref, *, tm, tb, width):
    i = pl.program_id(1)
    n_main = 7 * width

    @pl.when((pl.program_id(0) == 0) & (i == 0))
    def _():
        for j in range(n_main // width):
            w_ref[:, j * width:(j + 1) * width] = jnp.transpose(
                w32_ref[j * width:(j + 1) * width, :]).astype(_BF16)
        tail = jnp.concatenate([w32_ref[n_main:, :],
                                jnp.zeros((_LANES - _N_HEADS, w32_ref.shape[1]), _F32)], axis=0)
        wff_ref[...] = jnp.transpose(tail).astype(_BF16)

    @pl.when(i == 0)
    def _():
        state_ref[...] = jnp.zeros_like(state_ref)
        carry_ref[...] = jnp.zeros_like(carry_ref)
    hb, inv_rms = _rms_split(x_ref[0], g_ref[...])

    def proj(j):
        return jnp.dot(hb, w_ref[:, j * width:(j + 1) * width], preferred_element_type=_F32) * inv_rms

    lane = lax.broadcasted_iota(jnp.int32, (tm, _LANES), 1)
    low = lane < _HEAD_DIM
    first_half = (lane & (_HEAD_DIM // 2)) == 0
    n_pairs = width // _LANES

    cos = cos_ref[...]
    sin = sin_ref[...]
    for j, out_ref, scale in ((0, rq_ref, _HEAD_DIM ** -0.5), (1, rk_ref, None)):
        y = proj(j)
        for c in range(n_pairs):
            blk = y[:, c * _LANES:(c + 1) * _LANES]
            swapped = jnp.where(first_half, pltpu.roll(blk, _LANES - _HEAD_DIM // 2, 1),
                                pltpu.roll(blk, _HEAD_DIM // 2, 1))
            r = blk * cos + swapped * sin
            if scale is not None:
                r = r * scale
            out_ref[:, c * _LANES:(c + 1) * _LANES] = r.astype(_BF16)
    rv_ref[...] = proj(2).astype(_BF16)
    gate = proj(3)
    gate_ref[...] = (gate * jax.nn.sigmoid(gate)).astype(_BF16)


    half = tm // 2
    z = jnp.concatenate([jnp.dot(hb[:half], wff_ref[...], preferred_element_type=_F32),
                         jnp.dot(hb[half:], wff_ref[...], preferred_element_type=_F32)],
                        axis=0) * inv_rms + bf_ref[...]
    logf = (jnp.minimum(z, 0.0) - jnp.log(1.0 + jnp.exp(-jnp.abs(z)))) * _LOG2E
    hi, mid, lo = _split3(logf)
    tri = tri_ref[...]
    split = jnp.concatenate([hi, mid, lo], axis=1).astype(_BF16)
    parts = jnp.concatenate(
        [jnp.dot(tri[:half, :half], split[:half], preferred_element_type=_F32),
         jnp.dot(tri[half:], split, preferred_element_type=_F32)], axis=0)
    csum = parts[:, :_LANES] + parts[:, _LANES:2 * _LANES] + parts[:, 2 * _LANES:]

    fcum = csum + carry_ref[0:1, :]
    carry_ref[...] = jnp.broadcast_to(fcum[tm - 1:tm, :], carry_ref.shape)
    sub_id = lax.broadcasted_iota(jnp.int32, carry_ref.shape, 0) // (carry_ref.shape[0] // _KSPLIT)
    ends = carry_ref[...]
    for part in range(_KSPLIT - 1):
        row = (part + 1) * (tm // _KSPLIT) - 1
        ends = jnp.where(sub_id == part, jnp.broadcast_to(fcum[row:row + 1, :], carry_ref.shape), ends)
    fend_ref[0, 0] = ends

    def head_rms(y, g_row):
        out = []
        for g0 in range(0, width, _RET_GROUP):
            blk = y[:, g0:g0 + _RET_GROUP]
            ms = jnp.dot((blk * blk).astype(_BF16), avg_ref[...], preferred_element_type=_F32)
            out.append(blk * lax.rsqrt(ms + _EPS) * g_row[:, g0:g0 + _RET_GROUP])
        return out

    qn = head_rms(proj(4), gq_ref[...])
    kn = head_rms(proj(5), gk_ref[...])
    fv = proj(6)

    fh, fm, fl = _split3(fcum)
    n_parts = 3 * _N_HEADS
    packed = jnp.where(lane < _N_HEADS, fh, jnp.where(
        lane < 2 * _N_HEADS, pltpu.roll(fm, _N_HEADS, 1), pltpu.roll(fl, 2 * _N_HEADS, 1)))
    in_qf = (lane >= _AUG_QF) & (lane < _AUG_QF + n_parts)
    in_kf = (lane >= _AUG_KF) & (lane < _AUG_KF + n_parts)
    shared = jnp.where(in_qf, pltpu.roll(packed, _AUG_QF, 1), -pltpu.roll(packed, _AUG_KF, 1))
    q_bias = jnp.where(in_qf, shared, jnp.where(in_kf, 1.0, jnp.where(
        lane == _AUG_SHIFT, -shift_ref[...], 0.0)))
    v_const = jnp.where(lane == _AUG_ONE, 1.0, 0.0)
    for h in range(_N_HEADS):
        own = (lane & (_N_HEADS - 1)) == h
        k_bias = jnp.where(in_kf & own, shared,
                           jnp.where((in_qf & own) | (lane == _AUG_SHIFT), 1.0, 0.0))
        g0, c0 = divmod(h * _HEAD_DIM, _RET_GROUP)
        c0 = (c0 // _LANES) * _LANES
        qb = qn[g0][:, c0:c0 + _LANES]
        kb = kn[g0][:, c0:c0 + _LANES]
        vb = fv[:, (h // 2) * _LANES:(h // 2 + 1) * _LANES]
        if h % 2 == 1:
            qb = pltpu.roll(qb, _HEAD_DIM, 1)
            kb = pltpu.roll(kb, _HEAD_DIM, 1)
            vb = pltpu.roll(vb, _HEAD_DIM, 1)
        qa_ref[0, h] = jnp.where(low, qb, q_bias).astype(_BF16)
        ka_ref[0, h] = jnp.where(low, kb, k_bias).astype(_BF16)
        va_ref[0, h] = jnp.where(low, vb, v_const).astype(_BF16)

    for r0 in range(0, tm, tb):
        _retention_block(rq_ref, rk_ref, rv_ref, gate_ref, slice(r0, r0 + tb), dmask_ref, qdec_ref, kdec_ref,
                         sdec_ref, bd_ref, avg_ref, gret_ref, state_ref, ret_ref)


def _in_proj(x, g_mix, w_in_t, b_ff, cos_t, sin_t, gq, gk, shift, tri, ret_tables, g_ret, *, tm, tb):
    B, T, D = x.shape
    width = _N_HEADS * _HEAD_DIM
    assert w_in_t.shape == (7 * width + _N_HEADS, D) and tm % tb == 0
    dmask, qdec, kdec, sdec, bd, avg = ret_tables
    tok = lambda b, i: (b, i, 0)
    head = lambda b, i: (b, 0, i, 0)
    bf_tok = jax.ShapeDtypeStruct((B, T, width), _BF16)
    bf_head = jax.ShapeDtypeStruct((B, _N_HEADS, T, _LANES), _BF16)
    return pl.pallas_call(
        functools.partial(_in_proj_body, tm=tm, tb=tb, width=width),
        grid=(B, T // tm),
        in_specs=[
            pl.BlockSpec((1, tm, D), tok),
            _const_spec((1, D)),
            _const_spec(w_in_t.shape),
            _const_spec((1, _LANES)),
            pl.BlockSpec((tm, _LANES), lambda b, i: (i, 0)),
            pl.BlockSpec((tm, _LANES), lambda b, i: (i, 0)),
            _const_spec((1, width)),
            _const_spec((1, width)),
            _const_spec((1, _LANES)),
            _const_spec((tm, tm)),
            _const_spec(avg.shape),
            _const_spec(dmask.shape),
            _const_spec(qdec.shape),
            _const_spec(kdec.shape),
            _const_spec(sdec.shape),
            _const_spec(bd.shape),
            _const_spec(g_ret.shape),
        ],
        out_specs=[pl.BlockSpec((1, tm, width), tok)]
        + [pl.BlockSpec((1, _N_HEADS, tm, _LANES), head)] * 3
        + [pl.BlockSpec((1, 1, 8, _LANES), lambda b, i: (b, i, 0, 0))],
        out_shape=[bf_tok] + [bf_head] * 3 + [jax.ShapeDtypeStruct((B, T // tm, 8, _LANES), _F32)],
        scratch_shapes=[pltpu.VMEM((D, 7 * width), _BF16), pltpu.VMEM((D, _LANES), _BF16),
                        pltpu.VMEM((8, _LANES), _F32)]
        + [pltpu.VMEM((tm, width), _BF16)] * 4 + [pltpu.VMEM(sdec.shape, _F32)],
        compiler_params=_params(("arbitrary", "arbitrary")),
        name="in_proj",
    )(x, g_mix, w_in_t, b_ff, cos_t, sin_t, gq, gk, shift, tri, avg, dmask, qdec, kdec, sdec, bd, g_ret)


def _retention_block(rq_ref, rk_ref, rv_ref, gate_ref, rows, dmask_ref, qdec_ref, kdec_ref, sdec_ref, bd_ref,
                     avg_ref, g_ref, state_ref, o_ref):
    tb = rows.stop - rows.start
    nt = (((1,), (1,)), ((), ()))
    tn = (((0,), (0,)), ((), ()))
    gw = state_ref.shape[1]
    lane = lax.broadcasted_iota(jnp.int32, (1, _LANES), 1)
    low = lax.broadcasted_iota(jnp.int32, (tb, _LANES), 1) < _HEAD_DIM
    head_lanes = [jnp.where(lane < _HEAD_DIM, 1.0, 0.0).astype(_BF16),
                  jnp.where(lane < _HEAD_DIM, 0.0, 1.0).astype(_BF16)]
    n_groups = state_ref.shape[0]
    groups = [slice(g * gw, (g + 1) * gw) for g in range(n_groups)]
    scores = []
    for h in range(_N_HEADS):
        ps = slice((h // 2) * _LANES, (h // 2 + 1) * _LANES)
        scores.append(lax.dot_general(rq_ref[rows, ps] * head_lanes[h % 2], rk_ref[rows, ps], nt,
                                      preferred_element_type=_F32))
    inter = []
    for g, gs in enumerate(groups):
        state = state_ref[g]
        inter.append(jnp.dot(rq_ref[rows, gs], state.astype(_BF16),
                             preferred_element_type=_F32) * qdec_ref[:, gs])
        kd = (rk_ref[rows, gs].astype(_F32) * kdec_ref[:, gs]).astype(_BF16)
        state_ref[g] = state * sdec_ref[g] + lax.dot_general(
            kd, rv_ref[rows, gs], tn, preferred_element_type=_F32) * bd_ref[...]
    intra = []
    for h in range(_N_HEADS):
        ps = slice((h // 2) * _LANES, (h // 2 + 1) * _LANES)
        intra.append(jnp.dot((scores[h] * dmask_ref[h]).astype(_BF16), rv_ref[rows, ps],
                             preferred_element_type=_F32))
    outs = []
    for g, gs in enumerate(groups):
        pairs = [jnp.where(low, intra[2 * c], intra[2 * c + 1])
                 for c in range(g * gw // _LANES, (g + 1) * gw // _LANES)]
        outs.append(jnp.concatenate(pairs, axis=1) + inter[g])
    mus = [jnp.dot(o.astype(_BF16), avg_ref[...], preferred_element_type=_F32) for o in outs]
    cents = [o - mu for o, mu in zip(outs, mus)]
    vars_ = [jnp.dot((oc * oc).astype(_BF16), avg_ref[...], preferred_element_type=_F32) for oc in cents]
    for gs, oc, var in zip(groups, cents, vars_):
        y = oc * lax.rsqrt(var + _EPS) * g_ref[:, gs]
        o_ref[0, rows, gs] = (y * gate_ref[rows, gs].astype(_F32)).astype(_BF16)


def _causal_mask(tq):
    row = lax.broadcasted_iota(jnp.int32, (tq, tq), 0)
    col = lax.broadcasted_iota(jnp.int32, (tq, tq), 1)
    return row >= col


def _fox_finish(acc_refs, o_ref, tq):
    lane = lax.broadcasted_iota(jnp.int32, (tq, _LANES), 1)
    for c in range(len(acc_refs) // 2):
        pair = []
        for hh in (2 * c, 2 * c + 1):
            acc = acc_refs[hh][...]
            pair.append(acc / acc[:, _AUG_ONE:_AUG_ONE + 1])
        o_ref[0, :, c * _LANES:(c + 1) * _LANES] = jnp.where(
            lane < _HEAD_DIM, pair[0], pltpu.roll(pair[1], _HEAD_DIM, 1)).astype(_BF16)


def _fox_shifted_body(jlo_ref, qa_ref, ka_ref, va_ref, o_ref, acc_ref, it_code, *, tq, nq, unroll):
    n_pair = qa_ref.shape[1]
    base = (pl.program_id(0) * pl.num_programs(1) + pl.program_id(1)) * (n_pair * nq)
    nt = (((1,), (1,)), ((), ()))
    half = tq // 2
    tk = tq // _KSPLIT
    diff_top = (lax.broadcasted_iota(jnp.int32, (half, half), 1)
                - lax.broadcasted_iota(jnp.int32, (half, half), 0))
    diff_bot = (lax.broadcasted_iota(jnp.int32, (half, tq), 1)
                - lax.broadcasted_iota(jnp.int32, (half, tq), 0))

    def interleave(n_chains, qk, finish):
        s_prev = qk(0)
        for u in range(1, n_chains):
            s_next = qk(u)
            finish(u - 1, s_prev)
            s_prev = s_next
        finish(n_chains - 1, s_prev)

    @pl.when((pl.program_id(0) == 0) & (pl.program_id(1) == 0))
    def _():
        acc_ref[...] = jnp.zeros_like(acc_ref)

    def pack(h, i, slot, keep, j):
        return h | (i << 4) | (slot << 8) | (keep << 16) | (j << 17)

    def unpack(code):
        return (code & 15, (code >> 4) & 15, code >> 17, (code >> 8) & 255, (code >> 16) & 1)

    spare = it_code.shape[0] - 1
    n = jnp.int32(0)
    for hh in range(n_pair):
        for i in range(1, nq):
            lo = jlo_ref[base + hh * nq + i]
            for j in range(_KSPLIT * i):
                code = pack(hh, i, hh * nq + i, jnp.where(j == lo, 0, 1), j)
                it_code[jnp.where(j >= lo, n + (j - lo), spare)] = code
            n = n + (_KSPLIT * i - lo)
    n_trips = (n + (unroll - 1)) // unroll

    def pad(m, carry):
        it_code[m] = jnp.int32(pack(0, 0, n_pair * nq, 0, 0))
        return carry

    lax.fori_loop(n, n_trips * unroll, pad, 0)

    def trip(t, carry):
        items = [unpack(it_code[t * unroll + u]) for u in range(unroll)]

        def qk(u):
            h, i, j = items[u][:3]
            return lax.dot_general(qa_ref[0, h, pl.ds(pl.multiple_of(i * tq, tq), tq), :],
                                   ka_ref[0, h, pl.ds(pl.multiple_of(j * tk, tk), tk), :], nt,
                                   preferred_element_type=_F32)

        def finish(u, s):
            h, _, j, slot, keep = items[u]
            pv = jnp.dot(jnp.exp2(s).astype(_BF16), va_ref[0, h, pl.ds(pl.multiple_of(j * tk, tk), tk), :],
                         preferred_element_type=_F32)
            acc_ref[slot] = jnp.where(keep > 0, acc_ref[slot], 0.0) + pv

        interleave(unroll, qk, finish)
        return carry

    lax.fori_loop(0, n_trips, trip, 0)

    chains = [(hh, i, bottom) for hh in range(n_pair) for i in range(nq) for bottom in (0, 1)]

    def diag_qk(u):
        hh, i, bottom = chains[u]
        r0 = i * tq + bottom * half
        width = tq if bottom else half
        return lax.dot_general(qa_ref[0, hh, r0:r0 + half, :], ka_ref[0, hh, i * tq:i * tq + width, :], nt,
                               preferred_element_type=_F32)

    def diag_finish(u, s):
        hh, i, bottom = chains[u]
        r0 = i * tq + bottom * half
        width = tq if bottom else half
        live = (diff_bot <= half) if bottom else (diff_top <= 0)
        p = jnp.exp2(jnp.where(live, s, _NEG_INF)).astype(_BF16)
        pv = jnp.dot(p, va_ref[0, hh, i * tq:i * tq + width, :], preferred_element_type=_F32)
        if i == 0:
            acc = pv
        else:
            has_off_diagonal = jlo_ref[base + hh * nq + i] < _KSPLIT * i
            acc = jnp.where(has_off_diagonal,
                            acc_ref[hh * nq + i, bottom * half:(bottom + 1) * half, :], 0.0) + pv
        o = acc / acc[:, _AUG_ONE:_AUG_ONE + 1]
        lanes = slice((hh % 2) * _HEAD_DIM, (hh % 2 + 1) * _HEAD_DIM)
        if hh % 2 == 1:
            o = pltpu.roll(o, _HEAD_DIM, 1)
        o_ref[0, r0:r0 + half, hh * _HEAD_DIM:(hh + 1) * _HEAD_DIM] = o[:, lanes].astype(_BF16)

    interleave(len(chains), diag_qk, diag_finish)


def _fox_online_body(qa_ref, ka_ref, va_ref, o_ref, *scratch, tq):
    qi = pl.program_id(2)
    nt = (((1,), (1,)), ((), ()))
    n_heads = qa_ref.shape[1]
    m_refs, acc_refs = scratch[:n_heads], scratch[n_heads:]
    for hh in range(n_heads):
        m_refs[hh][...] = jnp.full_like(m_refs[hh], _NEG_INF)
        acc_refs[hh][...] = jnp.zeros_like(acc_refs[hh])

    def step(j, masked):
        start = pl.multiple_of(j * tq, tq)
        logits = [lax.dot_general(qa_ref[0, hh], ka_ref[0, hh, pl.ds(start, tq), :], nt,
                                  preferred_element_type=_F32) for hh in range(n_heads)]
        for hh in range(n_heads):
            m_ref, acc_ref = m_refs[hh], acc_refs[hh]
            s = logits[hh]
            if masked:
                s = jnp.where(_causal_mask(tq), s, _NEG_INF)
            m_old = m_ref[...]
            m_new = jnp.maximum(m_old, jnp.max(s, axis=-1, keepdims=True))
            p = jnp.exp2(s - m_new[:, 0:1])
            acc_ref[...] = jnp.exp2(m_old - m_new) * acc_ref[...] + jnp.dot(
                p.astype(_BF16), va_ref[0, hh, pl.ds(start, tq), :], preferred_element_type=_F32)
            m_ref[...] = m_new

    def off_diag(j, carry):
        step(j, False)
        return carry

    lax.fori_loop(0, qi, off_diag, 0)
    step(qi, True)
    _fox_finish(acc_refs, o_ref, tq)


def _fox_shifted(jlo, qa, ka, va, *, tq, hg, unroll):
    B, H, T, L = qa.shape
    nq = T // tq
    assert tq % (2 * _MXU_TILE) == 0 and hg % 2 == 0 and H % hg == 0
    max_items = _KSPLIT * hg * nq * (nq - 1) // 2 + unroll + 1
    blk = pl.BlockSpec((1, hg, T, L), lambda b, p, jlo_ref: (b, p, 0, 0))
    return pl.pallas_call(
        functools.partial(_fox_shifted_body, tq=tq, nq=nq, unroll=unroll),
        grid_spec=pltpu.PrefetchScalarGridSpec(
            num_scalar_prefetch=1,
            grid=(B, H // hg),
            in_specs=[blk, blk, blk],
            out_specs=pl.BlockSpec((1, T, hg * _HEAD_DIM), lambda b, p, jlo_ref: (b, 0, p)),
            scratch_shapes=[pltpu.VMEM((hg * nq + 1, tq, L), _F32)]
            + [pltpu.SMEM((max_items,), jnp.int32)],
        ),
        out_shape=jax.ShapeDtypeStruct((B, T, H * _HEAD_DIM), _BF16),
        compiler_params=_params(("arbitrary", "arbitrary")),
        name="fox_shifted",
    )(jlo, qa, ka, va)


def _fox_online(qa, ka, va, *, tq, hg):
    B, H, T, L = qa.shape
    return pl.pallas_call(
        functools.partial(_fox_online_body, tq=tq),
        grid=(B, H // hg, T // tq),
        in_specs=[
            pl.BlockSpec((1, hg, tq, L), lambda b, p, i: (b, p, i, 0)),
            pl.BlockSpec((1, hg, T, L), lambda b, p, i: (b, p, 0, 0)),
            pl.BlockSpec((1, hg, T, L), lambda b, p, i: (b, p, 0, 0)),
        ],
        out_specs=pl.BlockSpec((1, tq, hg * _HEAD_DIM), lambda b, p, i: (b, i, p)),
        out_shape=jax.ShapeDtypeStruct((B, T, H * _HEAD_DIM), _BF16),
        scratch_shapes=[pltpu.VMEM((tq, L), _F32)] * (2 * hg),
        compiler_params=_params(("arbitrary", "arbitrary", "arbitrary")),
        name="fox_online",
    )(qa, ka, va)


def _first_live_block(fend):
    B, nk, H = fend.shape
    nq = nk // _KSPLIT
    f = jnp.transpose(fend, (0, 2, 1))
    f_q = f[:, :, _KSPLIT - 1::_KSPLIT]
    top = jnp.concatenate([jnp.zeros((B, H, 1), _F32), f_q[:, :, :-1]], axis=-1)
    dead = (top[:, :, :, None] - f[:, :, None, :]) <= _UNDERFLOW_LOG2
    j = jnp.arange(nk, dtype=jnp.int32)
    before = j[None, :] < _KSPLIT * jnp.arange(nq, dtype=jnp.int32)[:, None]
    first_live = jnp.min(jnp.where(dead & before, nk, j), axis=-1)
    return first_live.astype(jnp.int32).reshape(-1)


def _mem_kv_body(mem_ref, gm_ref, w32_ref, gk_ref, k_ref, v_ref, w_ref, *, d_model, xd):
    @pl.when(pl.program_id(0) == 0)
    def _():
        w_ref[...] = w32_ref[...].astype(_BF16)

    mn = _rms(mem_ref[0], gm_ref[...]).astype(_BF16)
    kv = jnp.dot(mn, w_ref[...], preferred_element_type=_F32)
    for h in range(_N_XHEADS):
        sl = slice(h * xd, (h + 1) * xd)
        k_ref[0, :, sl] = (_rms(kv[:, sl], gk_ref[...]) * (xd ** -0.5)).astype(_BF16)
    v_ref[0] = kv[:, d_model:].astype(_BF16)


def _mem_kv(mem, g_mem, w_xkv, g_xk):
    B, M, D = mem.shape
    xd = D // _N_XHEADS
    blk = pl.BlockSpec((1, M, D), lambda b: (b, 0, 0))
    out = jax.ShapeDtypeStruct((B, M, D), _BF16)
    return pl.pallas_call(
        functools.partial(_mem_kv_body, d_model=D, xd=xd),
        grid=(B,),
        in_specs=[blk, _const_spec((1, D)), _const_spec(w_xkv.shape), _const_spec((1, xd))],
        out_specs=[blk, blk],
        out_shape=[out, out],
        scratch_shapes=[pltpu.VMEM(w_xkv.shape, _BF16)],
        compiler_params=_params(("arbitrary",)),
        name="mem_kv",
    )(mem, g_mem, w_xkv, g_xk)


def _mix_xattn_body(x_ref, ret_ref, fox_ref, wo32_ref, gx_ref, wq32_ref, gq_ref, k_ref, v_ref, wxo32_ref,
                    o_ref, wo_ref, wq_ref, wxo_ref, *, width, xd, sub):
    @pl.when((pl.program_id(0) == 0) & (pl.program_id(1) == 0))
    def _():
        wo_ref[...] = wo32_ref[...].astype(_BF16)
        wq_ref[...] = wq32_ref[...].astype(_BF16)
        wxo_ref[...] = wxo32_ref[...].astype(_BF16)

    nt = (((1,), (1,)), ((), ()))
    head_slices = [slice(h * xd, (h + 1) * xd) for h in range(_N_XHEADS)]

    def front(rows):
        h1 = (x_ref[0, rows, :]
              + jnp.dot(ret_ref[0, rows, :], wo_ref[:width, :], preferred_element_type=_F32)
              + jnp.dot(fox_ref[0, rows, :], wo_ref[width:, :], preferred_element_type=_F32))
        hn, inv_h = _rms_split(h1, gx_ref[...])
        y = jnp.dot(hn, wq_ref[...], preferred_element_type=_F32)
        logits = []
        for sl in head_slices:
            yh = y[:, sl]
            inv_q = lax.rsqrt(inv_h * inv_h * jnp.mean(yh * yh, axis=-1, keepdims=True) + _EPS)
            logits.append(lax.dot_general((yh * gq_ref[...]).astype(_BF16), k_ref[0, :, sl], nt,
                                          preferred_element_type=_F32) * (inv_q * inv_h * _LOG2E))
        return h1, logits

    def back(rows, h1, logits):
        probs = []
        for lg in logits:
            p = jnp.exp2(lg - jnp.max(lg, axis=-1, keepdims=True))
            probs.append((p / jnp.sum(p, axis=-1, keepdims=True)).astype(_BF16))
        heads = [jnp.dot(p, v_ref[0, :, sl], preferred_element_type=_F32).astype(_BF16)
                 for p, sl in zip(probs, head_slices)]
        o = jnp.concatenate(heads, axis=-1)
        o_ref[0, rows, :] = h1 + jnp.dot(o, wxo_ref[...], preferred_element_type=_F32)

    tm = x_ref.shape[1]
    subs = [slice(r0, r0 + sub) for r0 in range(0, tm, sub)]
    fronts = [front(rows) for rows in subs]
    for rows, (h1, logits) in zip(subs, fronts):
        back(rows, h1, logits)


def _mix_xattn(x, ret, fox, w_out, g_xattn, w_xq, g_xq, k, v, w_xo, *, tm, sub):
    B, T, D = x.shape
    W = ret.shape[-1]
    M = k.shape[1]
    xd = D // _N_XHEADS
    assert tm % sub == 0 and T % tm == 0
    tok = lambda b, i: (b, i, 0)
    return pl.pallas_call(
        functools.partial(_mix_xattn_body, width=W, xd=xd, sub=sub),
        grid=(B, T // tm),
        in_specs=[
            pl.BlockSpec((1, tm, D), tok),
            pl.BlockSpec((1, tm, W), tok),
            pl.BlockSpec((1, tm, W), tok),
            _const_spec(w_out.shape),
            _const_spec((1, D)),
            _const_spec(w_xq.shape),
            _const_spec((1, xd)),
            pl.BlockSpec((1, M, D), lambda b, i: (b, 0, 0)),
            pl.BlockSpec((1, M, D), lambda b, i: (b, 0, 0)),
            _const_spec(w_xo.shape),
        ],
        out_specs=pl.BlockSpec((1, tm, D), tok),
        out_shape=jax.ShapeDtypeStruct((B, T, D), _F32),
        scratch_shapes=[pltpu.VMEM(w.shape, _BF16) for w in (w_out, w_xq, w_xo)],
        compiler_params=_params(("arbitrary", "arbitrary")),
        name="mix_xattn",
    )(x, ret, fox, w_out, g_xattn, w_xq, g_xq, k, v, w_xo)


def _ffn_chunks(d_ff, n_chunks):
    tiles = -(-d_ff // _MXU_TILE)
    bounds = [min(d_ff, _MXU_TILE * ((tiles * c) // n_chunks)) for c in range(n_chunks)] + [d_ff]
    return [(lo, hi) for lo, hi in zip(bounds[:-1], bounds[1:]) if hi > lo]


def _ffn_body(h_ref, g_ref, wg32_ref, wu32_ref, wd32_ref, o_ref, wg_ref, wu_ref, wd_ref, acc0_ref,
              *, n_cast, ck, chunks):
    step = pl.program_id(0)

    def swiglu(hn, inv_rms, sl):
        gate = jnp.dot(hn, wg_ref[:, sl], preferred_element_type=_F32) * inv_rms
        up = jnp.dot(hn, wu_ref[:, sl], preferred_element_type=_F32) * inv_rms
        a = (gate * jax.nn.sigmoid(gate) * up).astype(_BF16)
        return jnp.dot(a, wd_ref[sl, :], preferred_element_type=_F32)

    for c in range(n_cast):
        @pl.when(step == c)
        def _(c=c):
            sl = slice(c * ck, (c + 1) * ck)
            wg_ref[:, sl] = wg32_ref[...].astype(_BF16)
            wu_ref[:, sl] = wu32_ref[...].astype(_BF16)
            wd_ref[sl, :] = wd32_ref[...].astype(_BF16)
            h = h_ref[0]
            part = swiglu(*_rms_split(h, g_ref[...]), sl)
            if c == 0:
                acc0_ref[...] = h + part
            elif c < n_cast - 1:
                acc0_ref[...] += part
            else:
                o_ref[0] = acc0_ref[...] + part

    @pl.when(step >= n_cast)
    def _():
        h = h_ref[0]
        hn, inv_rms = _rms_split(h, g_ref[...])
        acc = h
        for lo, hi in chunks:
            acc = acc + swiglu(hn, inv_rms, slice(lo, hi))
        o_ref[0] = acc


def _ffn(h, g_ffn, w_gate, w_up, w_down, *, tm, n_split):
    B, T, D = h.shape
    F = w_gate.shape[1]
    ck = _MXU_TILE
    assert F % ck == 0 and T % tm == 0
    n_cast = F // ck
    n_tiles = B * T // tm
    tile = lambda s: (jnp.maximum(s - (n_cast - 1), 0), 0, 0)
    chunk = lambda s: jnp.minimum(s, n_cast - 1)
    tok = pl.BlockSpec((1, tm, D), tile)
    out = pl.pallas_call(
        functools.partial(_ffn_body, n_cast=n_cast, ck=ck, chunks=_ffn_chunks(F, n_split)),
        grid=(n_cast - 1 + n_tiles,),
        in_specs=[tok, _const_spec((1, D)),
                  pl.BlockSpec((D, ck), lambda s: (0, chunk(s))),
                  pl.BlockSpec((D, ck), lambda s: (0, chunk(s))),
                  pl.BlockSpec((ck, D), lambda s: (chunk(s), 0))],
        out_specs=tok,
        out_shape=jax.ShapeDtypeStruct((n_tiles, tm, D), _F32),
        scratch_shapes=[pltpu.VMEM((D, F), _BF16), pltpu.VMEM((D, F), _BF16), pltpu.VMEM((F, D), _BF16),
                        pltpu.VMEM((tm, D), _F32)],
        compiler_params=_params(("arbitrary",)),
        name="ffn",
    )(h.reshape(n_tiles, tm, D), g_ffn, w_gate, w_up, w_down)
    return out.reshape(B, T, D)


def _rope_tables(T):
    half = _HEAD_DIM // 2
    inv_freq = (_ROPE_BASE ** (-np.arange(0, _HEAD_DIM, 2, dtype=np.float32) / _HEAD_DIM)).astype(np.float32)
    ang = (np.arange(T, dtype=np.float32)[:, None] * inv_freq[None, :]).astype(np.float32).astype(np.float64)
    cos, sin = np.cos(ang), np.sin(ang)
    reps = _LANES // _HEAD_DIM
    cos_t = np.tile(np.concatenate([cos, cos], axis=1), (1, reps))
    sin_t = np.tile(np.concatenate([-sin, sin], axis=1), (1, reps))
    return jnp.asarray(cos_t, _F32), jnp.asarray(sin_t, _F32)


def _retention_tables(tb):
    log_g = np.log(1.0 - 2.0 ** (-5.0 - np.arange(_N_HEADS, dtype=np.float64)))
    idx = np.arange(tb, dtype=np.float64)
    dist = np.abs(idx[:, None] - idx[None, :])
    chunk = np.arange(tb) // _RET_CHUNK
    visible = chunk[None, :] <= chunk[:, None]
    dmask = np.where(visible[None], np.exp(log_g[:, None, None] * dist[None]), 0.0)
    qdec = np.repeat(np.exp(log_g[None, :] * (idx[:, None] + 1.0)), _HEAD_DIM, axis=1)
    kdec = np.repeat(np.exp(log_g[None, :] * (tb - 1.0 - idx[:, None])), _HEAD_DIM, axis=1)
    heads_per_group = _RET_GROUP // _HEAD_DIM
    head_of = np.arange(_RET_GROUP) // _HEAD_DIM
    bd = (head_of[:, None] == head_of[None, :]).astype(np.float64)
    step_decay = np.exp(log_g * tb).reshape(-1, heads_per_group)
    sdec = bd[None] * np.repeat(step_decay, _HEAD_DIM, axis=1)[:, None, :]
    f = lambda a: jnp.asarray(a, _F32)
    return f(dmask), f(qdec), f(kdec), f(sdec), f(bd), jnp.asarray(bd / _HEAD_DIM, _BF16)


def _pad_lanes(a):
    return jnp.pad(a, [(0, 0)] * (a.ndim - 1) + [(0, _LANES - a.shape[-1])])


def kernel(x, mem, g_mix, w_in, b_forget, g_ret_out, g_fox_q, g_fox_k, w_out, g_xattn, w_xq, w_xkv,
           g_mem, g_xq, g_xk, w_xo, g_ffn, w_gate, w_up, w_down):
    B, T, D = x.shape
    width = _N_HEADS * _HEAD_DIM
    tb, tq, tm_mix, tm_ffn = 256, 512, 512, 512
    cos_t, sin_t = _rope_tables(T)
    ret_tables = _retention_tables(tb)
    tri = jnp.asarray(np.tril(np.ones((tq, tq), np.float32)), _BF16)
    row = lambda a: a.reshape(1, -1).astype(_F32)

    h = x
    for l in range(w_in.shape[0]):
        b_ff = _pad_lanes(row(b_forget[l]))
        gq = jnp.tile(row(g_fox_q[l]), (1, _N_HEADS)) * (_LOG2E * _HEAD_DIM ** -0.5)
        gk = jnp.tile(row(g_fox_k[l]), (1, _N_HEADS))
        bound = (_HEAD_DIM ** 0.5 * _NORM_ROUNDING_SLACK) * jnp.max(jnp.abs(g_fox_q[l])) * jnp.max(
            jnp.abs(g_fox_k[l]))
        use_shift = bound <= _MAX_FIXED_SHIFT
        shift = jnp.where(use_shift, jnp.ceil(bound * (4.0 * _LOG2E)) * 0.25, 0.0).astype(_F32)
        ret, qa, ka, va, fend = _in_proj(h, row(g_mix[l]), w_in[l].T, b_ff, cos_t, sin_t, gq, gk,
                                         jnp.full((1, _LANES), shift), tri, ret_tables,
                                         row(g_ret_out[l]), tm=tq, tb=tb)
        f_ends = fend[:, :, ::8 // _KSPLIT, :_N_HEADS].reshape(B, -1, _N_HEADS)
        jlo = _first_live_block(f_ends)
        fox = lax.cond(use_shift,
                       lambda jlo, qa, ka, va: _fox_shifted(jlo, qa, ka, va, tq=tq, hg=4, unroll=12),
                       lambda jlo, qa, ka, va: _fox_online(qa, ka, va, tq=tq, hg=4), jlo, qa, ka, va)
        k, v = _mem_kv(mem, row(g_mem[l]), w_xkv[l], row(g_xk[l]))
        h = _mix_xattn(h, ret, fox, w_out[l], row(g_xattn[l]), w_xq[l], row(g_xq[l]), k, v, w_xo[l],
                       tm=tm_mix, sub=256)
        h = _ffn(h, row(g_ffn[l]), w_gate[l], w_up[l], w_down[l], tm=tm_ffn, n_split=2)
    return h
```

```python
import functools

import numpy as np
import jax
import jax.numpy as jnp
from jax import lax
from jax.experimental import pallas as pl
from jax.experimental.pallas import tpu as pltpu

_BF16 = jnp.bfloat16
_F32 = jnp.float32

_EPS = 1e-6
_NEG_INF = -1e30
_ROPE_BASE = 10000.0
_HEAD_DIM = 64
_N_HEADS = 8
_RET_CHUNK = 64
_RET_GROUP = 256
_N_XHEADS = 4
_LANES = 128
_MXU_TILE = 256
_VMEM_LIMIT = 56 * 1024 * 1024

_AUG_QF = 64
_AUG_KF = 88
_AUG_SHIFT = 112
_AUG_ONE = 64

_LOG2E = 1.4426950408889634
_MAX_FIXED_SHIFT = 32.0
_KSPLIT = 2
_UNDERFLOW_LOG2 = -152.0
_NORM_ROUNDING_SLACK = 1.01


def _params(sem):
    return pltpu.CompilerParams(dimension_semantics=sem, vmem_limit_bytes=_VMEM_LIMIT)


def _const_spec(shape):
    nd = len(shape)
    return pl.BlockSpec(shape, lambda *_: (0,) * nd, pipeline_mode=pl.Buffered(1))


def _rms(x, g):
    return x * lax.rsqrt(jnp.mean(x * x, axis=-1, keepdims=True) + _EPS) * g


def _rms_split(x, g):
    return (x * g).astype(_BF16), lax.rsqrt(jnp.mean(x * x, axis=-1, keepdims=True) + _EPS)


def _split3(v):
    hi = v.astype(_BF16).astype(_F32)
    r = v - hi
    mid = r.astype(_BF16).astype(_F32)
    return hi, mid, r - mid


def _in_proj_body(x_ref, g_ref, w32_ref, bf_ref, cos_ref, sin_ref, gq_ref, gk_ref, shift_ref,
                  tri_ref, avg_ref, dmask_ref, qdec_ref, kdec_ref, sdec_ref, bd_ref, gret_ref,
                  ret_ref, qa_ref, ka_ref, va_ref, fend_ref,
                  w_ref, wff_ref, carry_ref, rq_ref, rk_ref, rv_ref, gate_ref, state_ref, *, tm, tb, width):
    i = pl.program_id(1)
    n_main = 7 * width

    @pl.when((pl.program_id(0) == 0) & (i == 0))
    def _():
        for j in range(n_main // width):
            w_ref[:, j * width:(j + 1) * width] = jnp.transpose(
                w32_ref[j * width:(j + 1) * width, :]).astype(_BF16)
        tail = jnp.concatenate([w32_ref[n_main:, :],
                                jnp.zeros((_LANES - _N_HEADS, w32_ref.shape[1]), _F32)], axis=0)
        wff_ref[...] = jnp.transpose(tail).astype(_BF16)

    @pl.when(i == 0)
    def _():
        state_ref[...] = jnp.zeros_like(state_ref)
        carry_ref[...] = jnp.zeros_like(carry_ref)
    hb, inv_rms = _rms_split(x_ref[0], g_ref[...])

    def proj(j):
        return jnp.dot(hb, w_ref[:, j * width:(j + 1) * width], preferred_element_type=_F32) * inv_rms

    lane = lax.broadcasted_iota(jnp.int32, (tm, _LANES), 1)
    low = lane < _HEAD_DIM
    first_half = (lane & (_HEAD_DIM // 2)) == 0
    n_pairs = width // _LANES

    cos = cos_ref[...]
    sin = sin_ref[...]
    for j, out_ref, scale in ((0, rq_ref, _HEAD_DIM ** -0.5), (1, rk_ref, None)):
        y = proj(j)
        for c in range(n_pairs):
            blk = y[:, c * _LANES:(c + 1) * _LANES]
            swapped = jnp.where(first_half, pltpu.roll(blk, _LANES - _HEAD_DIM // 2, 1),
                                pltpu.roll(blk, _HEAD_DIM // 2, 1))
            r = blk * cos + swapped * sin
            if scale is not None:
                r = r * scale
            out_ref[:, c * _LANES:(c + 1) * _LANES] = r.astype(_BF16)
    rv_ref[...] = proj(2).astype(_BF16)
    gate = proj(3)
    gate_ref[...] = (gate * jax.nn.sigmoid(gate)).astype(_BF16)


    half = tm // 2
    z = jnp.concatenate([jnp.dot(hb[:half], wff_ref[...], preferred_element_type=_F32),
                         jnp.dot(hb[half:], wff_ref[...], preferred_element_type=_F32)],
                        axis=0) * inv_rms + bf_ref[...]
    logf = (jnp.minimum(z, 0.0) - jnp.log(1.0 + jnp.exp(-jnp.abs(z)))) * _LOG2E
    hi, mid, lo = _split3(logf)
    tri = tri_ref[...]
    split = jnp.concatenate([hi, mid, lo], axis=1).astype(_BF16)
    parts = jnp.concatenate(
        [jnp.dot(tri[:half, :half], split[:half], preferred_element_type=_F32),
         jnp.dot(tri[half:], split, preferred_element_type=_F32)], axis=0)
    csum = parts[:, :_LANES] + parts[:, _LANES:2 * _LANES] + parts[:, 2 * _LANES:]

    fcum = csum + carry_ref[0:1, :]
    carry_ref[...] = jnp.broadcast_to(fcum[tm - 1:tm, :], carry_ref.shape)
    sub_id = lax.broadcasted_iota(jnp.int32, carry_ref.shape, 0) // (carry_ref.shape[0] // _KSPLIT)
    ends = carry_ref[...]
    for part in range(_KSPLIT - 1):
        row = (part + 1) * (tm // _KSPLIT) - 1
        ends = jnp.where(sub_id == part, jnp.broadcast_to(fcum[row:row + 1, :], carry_ref.shape), ends)
    fend_ref[0, 0] = ends

    def head_rms(y, g_row):
        out = []
        for g0 in range(0, width, _RET_GROUP):
            blk = y[:, g0:g0 + _RET_GROUP]
            ms = jnp.dot((blk * blk).astype(_BF16), avg_ref[...], preferred_element_type=_F32)
            out.append(blk * lax.rsqrt(ms + _EPS) * g_row[:, g0:g0 + _RET_GROUP])
        return out

    qn = head_rms(proj(4), gq_ref[...])
    kn = head_rms(proj(5), gk_ref[...])
    fv = proj(6)

    fh, fm, fl = _split3(fcum)
    n_parts = 3 * _N_HEADS
    packed = jnp.where(lane < _N_HEADS, fh, jnp.where(
        lane < 2 * _N_HEADS, pltpu.roll(fm, _N_HEADS, 1), pltpu.roll(fl, 2 * _N_HEADS, 1)))
    in_qf = (lane >= _AUG_QF) & (lane < _AUG_QF + n_parts)
    in_kf = (lane >= _AUG_KF) & (lane < _AUG_KF + n_parts)
    shared = jnp.where(in_qf, pltpu.roll(packed, _AUG_QF, 1), -pltpu.roll(packed, _AUG_KF, 1))
    q_bias = jnp.where(in_qf, shared, jnp.where(in_kf, 1.0, jnp.where(
        lane == _AUG_SHIFT, -shift_ref[...], 0.0)))
    v_const = jnp.where(lane == _AUG_ONE, 1.0, 0.0)
    for h in range(_N_HEADS):
        own = (lane & (_N_HEADS - 1)) == h
        k_bias = jnp.where(in_kf & own, shared,
                           jnp.where((in_qf & own) | (lane == _AUG_SHIFT), 1.0, 0.0))
        g0, c0 = divmod(h * _HEAD_DIM, _RET_GROUP)
        c0 = (c0 // _LANES) * _LANES
        qb = qn[g0][:, c0:c0 + _LANES]
        kb = kn[g0][:, c0:c0 + _LANES]
        vb = fv[:, (h // 2) * _LANES:(h // 2 + 1) * _LANES]
        if h % 2 == 1:
            qb = pltpu.roll(qb, _HEAD_DIM, 1)
            kb = pltpu.roll(kb, _HEAD_DIM, 1)
            vb = pltpu.roll(vb, _HEAD_DIM, 1)
        qa_ref[0, h] = jnp.where(low, qb, q_bias).astype(_BF16)
        ka_ref[0, h] = jnp.where(low, kb, k_bias).astype(_BF16)
        va_ref[0, h] = jnp.where(low, vb, v_const).astype(_BF16)

    for r0 in range(0, tm, tb):
        _retention_block(rq_ref, rk_ref, rv_ref, gate_ref, slice(r0, r0 + tb), dmask_ref, qdec_ref, kdec_ref,
                         sdec_ref, bd_ref, avg_ref, gret_ref, state_ref, ret_ref)


def _in_proj(x, g_mix, w_in_t, b_ff, cos_t, sin_t, gq, gk, shift, tri, ret_tables, g_ret, *, tm, tb):
    B, T, D = x.shape
    width = _N_HEADS * _HEAD_DIM
    assert w_in_t.shape == (7 * width + _N_HEADS, D) and tm % tb == 0
    dmask, qdec, kdec, sdec, bd, avg = ret_tables
    tok = lambda b, i: (b, i, 0)
    head = lambda b, i: (b, 0, i, 0)
    bf_tok = jax.ShapeDtypeStruct((B, T, width), _BF16)
    bf_head = jax.ShapeDtypeStruct((B, _N_HEADS, T, _LANES), _BF16)
    return pl.pallas_call(
        functools.partial(_in_proj_body, tm=tm, tb=tb, width=width),
        grid=(B, T // tm),
        in_specs=[
            pl.BlockSpec((1, tm, D), tok),
            _const_spec((1, D)),
            _const_spec(w_in_t.shape),
            _const_spec((1, _LANES)),
            pl.BlockSpec((tm, _LANES), lambda b, i: (i, 0)),
            pl.BlockSpec((tm, _LANES), lambda b, i: (i, 0)),
            _const_spec((1, width)),
            _const_spec((1, width)),
            _const_spec((1, _LANES)),
            _const_spec((tm, tm)),
            _const_spec(avg.shape),
            _const_spec(dmask.shape),
            _const_spec(qdec.shape),
            _const_spec(kdec.shape),
            _const_spec(sdec.shape),
            _const_spec(bd.shape),
            _const_spec(g_ret.shape),
        ],
        out_specs=[pl.BlockSpec((1, tm, width), tok)]
        + [pl.BlockSpec((1, _N_HEADS, tm, _LANES), head)] * 3
        + [pl.BlockSpec((1, 1, 8, _LANES), lambda b, i: (b, i, 0, 0))],
        out_shape=[bf_tok] + [bf_head] * 3 + [jax.ShapeDtypeStruct((B, T // tm, 8, _LANES), _F32)],
        scratch_shapes=[pltpu.VMEM((D, 7 * width), _BF16), pltpu.VMEM((D, _LANES), _BF16),
                        pltpu.VMEM((8, _LANES), _F32)]
        + [pltpu.VMEM((tm, width), _BF16)] * 4 + [pltpu.VMEM(sdec.shape, _F32)],
        compiler_params=_params(("arbitrary", "arbitrary")),
        name="in_proj",
    )(x, g_mix, w_in_t, b_ff, cos_t, sin_t, gq, gk, shift, tri, avg, dmask, qdec, kdec, sdec, bd, g_ret)


def _retention_block(rq_ref, rk_ref, rv_ref, gate_ref, rows, dmask_ref, qdec_ref, kdec_ref, sdec_ref, bd_ref,
                     avg_ref, g_ref, state_ref, o_ref):
    tb = rows.stop - rows.start
    nt = (((1,), (1,)), ((), ()))
    tn = (((0,), (0,)), ((), ()))
    gw = state_ref.shape[1]
    lane = lax.broadcasted_iota(jnp.int32, (1, _LANES), 1)
    low = lax.broadcasted_iota(jnp.int32, (tb, _LANES), 1) < _HEAD_DIM
    head_lanes = [jnp.where(lane < _HEAD_DIM, 1.0, 0.0).astype(_BF16),
                  jnp.where(lane < _HEAD_DIM, 0.0, 1.0).astype(_BF16)]
    n_groups = state_ref.shape[0]
    groups = [slice(g * gw, (g + 1) * gw) for g in range(n_groups)]
    scores = []
    for h in range(_N_HEADS):
        ps = slice((h // 2) * _LANES, (h // 2 + 1) * _LANES)
        scores.append(lax.dot_general(rq_ref[rows, ps] * head_lanes[h % 2], rk_ref[rows, ps], nt,
                                      preferred_element_type=_F32))
    inter = []
    for g, gs in enumerate(groups):
        state = state_ref[g]
        inter.append(jnp.dot(rq_ref[rows, gs], state.astype(_BF16),
                             preferred_element_type=_F32) * qdec_ref[:, gs])
        kd = (rk_ref[rows, gs].astype(_F32) * kdec_ref[:, gs]).astype(_BF16)
        state_ref[g] = state * sdec_ref[g] + lax.dot_general(
            kd, rv_ref[rows, gs], tn, preferred_element_type=_F32) * bd_ref[...]
    intra = []
    for h in range(_N_HEADS):
        ps = slice((h // 2) * _LANES, (h // 2 + 1) * _LANES)
        intra.append(jnp.dot((scores[h] * dmask_ref[h]).astype(_BF16), rv_ref[rows, ps],
                             preferred_element_type=_F32))
    outs = []
    for g, gs in enumerate(groups):
        pairs = [jnp.where(low, intra[2 * c], intra[2 * c + 1])
                 for c in range(g * gw // _LANES, (g + 1) * gw // _LANES)]
        outs.append(jnp.concatenate(pairs, axis=1) + inter[g])
    mus = [jnp.dot(o.astype(_BF16), avg_ref[...], preferred_element_type=_F32) for o in outs]
    cents = [o - mu for o, mu in zip(outs, mus)]
    vars_ = [jnp.dot((oc * oc).astype(_BF16), avg_ref[...], preferred_element_type=_F32) for oc in cents]
    for gs, oc, var in zip(groups, cents, vars_):
        y = oc * lax.rsqrt(var + _EPS) * g_ref[:, gs]
        o_ref[0, rows, gs] = (y * gate_ref[rows, gs].astype(_F32)).astype(_BF16)


def _causal_mask(tq):
    row = lax.broadcasted_iota(jnp.int32, (tq, tq), 0)
    col = lax.broadcasted_iota(jnp.int32, (tq, tq), 1)
    return row >= col


def _fox_finish(acc_refs, o_ref, tq):
    lane = lax.broadcasted_iota(jnp.int32, (tq, _LANES), 1)
    for c in range(len(acc_refs) // 2):
        pair = []
        for hh in (2 * c, 2 * c + 1):
            acc = acc_refs[hh][...]
            pair.append(acc / acc[:, _AUG_ONE:_AUG_ONE + 1])
        o_ref[0, :, c * _LANES:(c + 1) * _LANES] = jnp.where(
            lane < _HEAD_DIM, pair[0], pltpu.roll(pair[1], _HEAD_DIM, 1)).astype(_BF16)


def _fox_shifted_body(jlo_ref, qa_ref, ka_ref, va_ref, o_ref, acc_ref, it_code, *, tq, nq, unroll):
    n_pair = qa_ref.shape[1]
    base = (pl.program_id(0) * pl.num_programs(1) + pl.program_id(1)) * (n_pair * nq)
    nt = (((1,), (1,)), ((), ()))
    half = tq // 2
    tk = tq // _KSPLIT
    diff_top = (lax.broadcasted_iota(jnp.int32, (half, half), 1)
                - lax.broadcasted_iota(jnp.int32, (half, half), 0))
    diff_bot = (lax.broadcasted_iota(jnp.int32, (half, tq), 1)
                - lax.broadcasted_iota(jnp.int32, (half, tq), 0))

    def interleave(n_chains, qk, finish):
        s_prev = qk(0)
        for u in range(1, n_chains):
            s_next = qk(u)
            finish(u - 1, s_prev)
            s_prev = s_next
        finish(n_chains - 1, s_prev)

    @pl.when((pl.program_id(0) == 0) & (pl.program_id(1) == 0))
    def _():
        acc_ref[...] = jnp.zeros_like(acc_ref)

    def pack(h, i, slot, keep, j):
        return h | (i << 4) | (slot << 8) | (keep << 16) | (j << 17)

    def unpack(code):
        return (code & 15, (code >> 4) & 15, code >> 17, (code >> 8) & 255, (code >> 16) & 1)

    n = jnp.int32(0)
    for hh in range(n_pair):
        for i in range(1, nq):
            lo = jlo_ref[base + hh * nq + i]
            for j in range(_KSPLIT * i):
                it_code[n + jnp.maximum(j - lo, 0)] = pack(hh, i, hh * nq + i, jnp.where(j == lo, 0, 1), j)
            n = n + (_KSPLIT * i - lo)
    n_trips = (n + (unroll - 1)) // unroll

    def pad(m, carry):
        it_code[m] = jnp.int32(pack(0, 0, n_pair * nq, 0, 0))
        return carry

    lax.fori_loop(n, n_trips * unroll, pad, 0)

    def trip(t, carry):
        items = [unpack(it_code[t * unroll + u]) for u in range(unroll)]

        def qk(u):
            h, i, j = items[u][:3]
            return lax.dot_general(qa_ref[0, h, pl.ds(pl.multiple_of(i * tq, tq), tq), :],
                                   ka_ref[0, h, pl.ds(pl.multiple_of(j * tk, tk), tk), :], nt,
                                   preferred_element_type=_F32)

        def finish(u, s):
            h, _, j, slot, keep = items[u]
            pv = jnp.dot(jnp.exp2(s).astype(_BF16), va_ref[0, h, pl.ds(pl.multiple_of(j * tk, tk), tk), :],
                         preferred_element_type=_F32)
            acc_ref[slot] = jnp.where(keep > 0, acc_ref[slot], 0.0) + pv

        interleave(unroll, qk, finish)
        return carry

    lax.fori_loop(0, n_trips, trip, 0)

    chains = [(hh, i, bottom) for hh in range(n_pair) for i in range(nq) for bottom in (0, 1)]

    def diag_qk(u):
        hh, i, bottom = chains[u]
        r0 = i * tq + bottom * half
        width = tq if bottom else half
        return lax.dot_general(qa_ref[0, hh, r0:r0 + half, :], ka_ref[0, hh, i * tq:i * tq + width, :], nt,
                               preferred_element_type=_F32)

    def diag_finish(u, s):
        hh, i, bottom = chains[u]
        r0 = i * tq + bottom * half
        width = tq if bottom else half
        live = (diff_bot <= half) if bottom else (diff_top <= 0)
        p = jnp.exp2(jnp.where(live, s, _NEG_INF)).astype(_BF16)
        pv = jnp.dot(p, va_ref[0, hh, i * tq:i * tq + width, :], preferred_element_type=_F32)
        if i == 0:
            acc = pv
        else:
            has_off_diagonal = jlo_ref[base + hh * nq + i] < _KSPLIT * i
            acc = jnp.where(has_off_diagonal,
                            acc_ref[hh * nq + i, bottom * half:(bottom + 1) * half, :], 0.0) + pv
        o = acc / acc[:, _AUG_ONE:_AUG_ONE + 1]
        lanes = slice((hh % 2) * _HEAD_DIM, (hh % 2 + 1) * _HEAD_DIM)
        if hh % 2 == 1:
            o = pltpu.roll(o, _HEAD_DIM, 1)
        o_ref[0, r0:r0 + half, hh * _HEAD_DIM:(hh + 1) * _HEAD_DIM] = o[:, lanes].astype(_BF16)

    interleave(len(chains), diag_qk, diag_finish)


def _fox_online_body(qa_ref, ka_ref, va_ref, o_ref, *scratch, tq):
    qi = pl.program_id(2)
    nt = (((1,), (1,)), ((), ()))
    n_heads = qa_ref.shape[1]
    m_refs, acc_refs = scratch[:n_heads], scratch[n_heads:]
    for hh in range(n_heads):
        m_refs[hh][...] = jnp.full_like(m_refs[hh], _NEG_INF)
        acc_refs[hh][...] = jnp.zeros_like(acc_refs[hh])

    def step(j, masked):
        start = pl.multiple_of(j * tq, tq)
        logits = [lax.dot_general(qa_ref[0, hh], ka_ref[0, hh, pl.ds(start, tq), :], nt,
                                  preferred_element_type=_F32) for hh in range(n_heads)]
        for hh in range(n_heads):
            m_ref, acc_ref = m_refs[hh], acc_refs[hh]
            s = logits[hh]
            if masked:
                s = jnp.where(_causal_mask(tq), s, _NEG_INF)
            m_old = m_ref[...]
            m_new = jnp.maximum(m_old, jnp.max(s, axis=-1, keepdims=True))
            p = jnp.exp2(s - m_new[:, 0:1])
            acc_ref[...] = jnp.exp2(m_old - m_new) * acc_ref[...] + jnp.dot(
                p.astype(_BF16), va_ref[0, hh, pl.ds(start, tq), :], preferred_element_type=_F32)
            m_ref[...] = m_new

    def off_diag(j, carry):
        step(j, False)
        return carry

    lax.fori_loop(0, qi, off_diag, 0)
    step(qi, True)
    _fox_finish(acc_refs, o_ref, tq)


def _fox_shifted(jlo, qa, ka, va, *, tq, hg, unroll):
    B, H, T, L = qa.shape
    nq = T // tq
    assert tq % (2 * _MXU_TILE) == 0 and hg % 2 == 0 and H % hg == 0
    assert hg <= 16 and nq <= 16 and hg * nq < 256
    max_items = _KSPLIT * hg * nq * (nq - 1) // 2 + unroll + 1
    blk = pl.BlockSpec((1, hg, T, L), lambda b, p, jlo_ref: (b, p, 0, 0))
    return pl.pallas_call(
        functools.partial(_fox_shifted_body, tq=tq, nq=nq, unroll=unroll),
        grid_spec=pltpu.PrefetchScalarGridSpec(
            num_scalar_prefetch=1,
            grid=(B, H // hg),
            in_specs=[blk, blk, blk],
            out_specs=pl.BlockSpec((1, T, hg * _HEAD_DIM), lambda b, p, jlo_ref: (b, 0, p)),
            scratch_shapes=[pltpu.VMEM((hg * nq + 1, tq, L), _F32)]
            + [pltpu.SMEM((max_items,), jnp.int32)],
        ),
        out_shape=jax.ShapeDtypeStruct((B, T, H * _HEAD_DIM), _BF16),
        compiler_params=_params(("arbitrary", "arbitrary")),
        name="fox_shifted",
    )(jlo, qa, ka, va)


def _fox_online(qa, ka, va, *, tq, hg):
    B, H, T, L = qa.shape
    return pl.pallas_call(
        functools.partial(_fox_online_body, tq=tq),
        grid=(B, H // hg, T // tq),
        in_specs=[
            pl.BlockSpec((1, hg, tq, L), lambda b, p, i: (b, p, i, 0)),
            pl.BlockSpec((1, hg, T, L), lambda b, p, i: (b, p, 0, 0)),
            pl.BlockSpec((1, hg, T, L), lambda b, p, i: (b, p, 0, 0)),
        ],
        out_specs=pl.BlockSpec((1, tq, hg * _HEAD_DIM), lambda b, p, i: (b, i, p)),
        out_shape=jax.ShapeDtypeStruct((B, T, H * _HEAD_DIM), _BF16),
        scratch_shapes=[pltpu.VMEM((tq, L), _F32)] * (2 * hg),
        compiler_params=_params(("arbitrary", "arbitrary", "arbitrary")),
        name="fox_online",
    )(qa, ka, va)


def _first_live_block(fend):
    B, nk, H = fend.shape
    nq = nk // _KSPLIT
    f = jnp.transpose(fend, (0, 2, 1))
    f_q = f[:, :, _KSPLIT - 1::_KSPLIT]
    top = jnp.concatenate([jnp.zeros((B, H, 1), _F32), f_q[:, :, :-1]], axis=-1)
    dead = (top[:, :, :, None] - f[:, :, None, :]) <= _UNDERFLOW_LOG2
    j = jnp.arange(nk, dtype=jnp.int32)
    before = j[None, :] < _KSPLIT * jnp.arange(nq, dtype=jnp.int32)[:, None]
    first_live = jnp.min(jnp.where(dead & before, nk, j), axis=-1)
    return first_live.astype(jnp.int32).reshape(-1)


def _mem_kv_body(mem_ref, gm_ref, w32_ref, gk_ref, k_ref, v_ref, w_ref, *, d_model, xd):
    @pl.when(pl.program_id(0) == 0)
    def _():
        w_ref[...] = w32_ref[...].astype(_BF16)

    mn = _rms(mem_ref[0], gm_ref[...]).astype(_BF16)
    kv = jnp.dot(mn, w_ref[...], preferred_element_type=_F32)
    for h in range(_N_XHEADS):
        sl = slice(h * xd, (h + 1) * xd)
        k_ref[0, :, sl] = (_rms(kv[:, sl], gk_ref[...]) * (xd ** -0.5)).astype(_BF16)
    v_ref[0] = kv[:, d_model:].astype(_BF16)


def _mem_kv(mem, g_mem, w_xkv, g_xk):
    B, M, D = mem.shape
    xd = D // _N_XHEADS
    blk = pl.BlockSpec((1, M, D), lambda b: (b, 0, 0))
    out = jax.ShapeDtypeStruct((B, M, D), _BF16)
    return pl.pallas_call(
        functools.partial(_mem_kv_body, d_model=D, xd=xd),
        grid=(B,),
        in_specs=[blk, _const_spec((1, D)), _const_spec(w_xkv.shape), _const_spec((1, xd))],
        out_specs=[blk, blk],
        out_shape=[out, out],
        scratch_shapes=[pltpu.VMEM(w_xkv.shape, _BF16)],
        compiler_params=_params(("arbitrary",)),
        name="mem_kv",
    )(mem, g_mem, w_xkv, g_xk)


def _mix_xattn_body(x_ref, ret_ref, fox_ref, wo32_ref, gx_ref, wq32_ref, gq_ref, k_ref, v_ref, wxo32_ref,
                    o_ref, wo_ref, wq_ref, wxo_ref, *, width, xd, sub):
    @pl.when((pl.program_id(0) == 0) & (pl.program_id(1) == 0))
    def _():
        wo_ref[...] = wo32_ref[...].astype(_BF16)
        wq_ref[...] = wq32_ref[...].astype(_BF16)
        wxo_ref[...] = wxo32_ref[...].astype(_BF16)

    nt = (((1,), (1,)), ((), ()))
    head_slices = [slice(h * xd, (h + 1) * xd) for h in range(_N_XHEADS)]

    def front(rows):
        h1 = (x_ref[0, rows, :]
              + jnp.dot(ret_ref[0, rows, :], wo_ref[:width, :], preferred_element_type=_F32)
              + jnp.dot(fox_ref[0, rows, :], wo_ref[width:, :], preferred_element_type=_F32))
        hn, inv_h = _rms_split(h1, gx_ref[...])
        y = jnp.dot(hn, wq_ref[...], preferred_element_type=_F32)
        logits = []
        for sl in head_slices:
            yh = y[:, sl]
            inv_q = lax.rsqrt(inv_h * inv_h * jnp.mean(yh * yh, axis=-1, keepdims=True) + _EPS)
            logits.append(lax.dot_general((yh * gq_ref[...]).astype(_BF16), k_ref[0, :, sl], nt,
                                          preferred_element_type=_F32) * (inv_q * inv_h * _LOG2E))
        return h1, logits

    def back(rows, h1, logits):
        probs = []
        for lg in logits:
            p = jnp.exp2(lg - jnp.max(lg, axis=-1, keepdims=True))
            probs.append((p / jnp.sum(p, axis=-1, keepdims=True)).astype(_BF16))
        heads = [jnp.dot(p, v_ref[0, :, sl], preferred_element_type=_F32).astype(_BF16)
                 for p, sl in zip(probs, head_slices)]
        o = jnp.concatenate(heads, axis=-1)
        o_ref[0, rows, :] = h1 + jnp.dot(o, wxo_ref[...], preferred_element_type=_F32)

    tm = x_ref.shape[1]
    subs = [slice(r0, r0 + sub) for r0 in range(0, tm, sub)]
    fronts = [front(rows) for rows in subs]
    for rows, (h1, logits) in zip(subs, fronts):
        back(rows, h1, logits)


def _mix_xattn(x, ret, fox, w_out, g_xattn, w_xq, g_xq, k, v, w_xo, *, tm, sub):
    B, T, D = x.shape
    W = ret.shape[-1]
    M = k.shape[1]
    xd = D // _N_XHEADS
    assert tm % sub == 0 and T % tm == 0
    tok = lambda b, i: (b, i, 0)
    return pl.pallas_call(
        functools.partial(_mix_xattn_body, width=W, xd=xd, sub=sub),
        grid=(B, T // tm),
        in_specs=[
            pl.BlockSpec((1, tm, D), tok),
            pl.BlockSpec((1, tm, W), tok),
            pl.BlockSpec((1, tm, W), tok),
            _const_spec(w_out.shape),
            _const_spec((1, D)),
            _const_spec(w_xq.shape),
            _const_spec((1, xd)),
            pl.BlockSpec((1, M, D), lambda b, i: (b, 0, 0)),
            pl.BlockSpec((1, M, D), lambda b, i: (b, 0, 0)),
            _const_spec(w_xo.shape),
        ],
        out_specs=pl.BlockSpec((1, tm, D), tok),
        out_shape=jax.ShapeDtypeStruct((B, T, D), _F32),
        scratch_shapes=[pltpu.VMEM(w.shape, _BF16) for w in (w_out, w_xq, w_xo)],
        compiler_params=_params(("arbitrary", "arbitrary")),
        name="mix_xattn",
    )(x, ret, fox, w_out, g_xattn, w_xq, g_xq, k, v, w_xo)


def _ffn_chunks(d_ff, n_chunks):
    tiles = -(-d_ff // _MXU_TILE)
    bounds = [min(d_ff, _MXU_TILE * ((tiles * c) // n_chunks)) for c in range(n_chunks)] + [d_ff]
    return [(lo, hi) for lo, hi in zip(bounds[:-1], bounds[1:]) if hi > lo]


def _ffn_body(h_ref, g_ref, wg32_ref, wu32_ref, wd32_ref, o_ref, wg_ref, wu_ref, wd_ref, acc0_ref,
              *, n_cast, ck, chunks):
    step = pl.program_id(0)

    def swiglu(hn, inv_rms, sl):
        gate = jnp.dot(hn, wg_ref[:, sl], preferred_element_type=_F32) * inv_rms
        up = jnp.dot(hn, wu_ref[:, sl], preferred_element_type=_F32) * inv_rms
        a = (gate * jax.nn.sigmoid(gate) * up).astype(_BF16)
        return jnp.dot(a, wd_ref[sl, :], preferred_element_type=_F32)

    for c in range(n_cast):
        @pl.when(step == c)
        def _(c=c):
            sl = slice(c * ck, (c + 1) * ck)
            wg_ref[:, sl] = wg32_ref[...].astype(_BF16)
            wu_ref[:, sl] = wu32_ref[...].astype(_BF16)
            wd_ref[sl, :] = wd32_ref[...].astype(_BF16)
            h = h_ref[0]
            part = swiglu(*_rms_split(h, g_ref[...]), sl)
            if c == 0:
                acc0_ref[...] = h + part
            elif c < n_cast - 1:
                acc0_ref[...] += part
            else:
                o_ref[0] = acc0_ref[...] + part

    @pl.when(step >= n_cast)
    def _():
        h = h_ref[0]
        hn, inv_rms = _rms_split(h, g_ref[...])
        acc = h
        for lo, hi in chunks:
            acc = acc + swiglu(hn, inv_rms, slice(lo, hi))
        o_ref[0] = acc


def _ffn(h, g_ffn, w_gate, w_up, w_down, *, tm, n_split):
    B, T, D = h.shape
    F = w_gate.shape[1]
    ck = _MXU_TILE
    assert F % ck == 0 and T % tm == 0
    n_cast = F // ck
    n_tiles = B * T // tm
    tile = lambda s: (jnp.maximum(s - (n_cast - 1), 0), 0, 0)
    chunk = lambda s: jnp.minimum(s, n_cast - 1)
    tok = pl.BlockSpec((1, tm, D), tile)
    out = pl.pallas_call(
        functools.partial(_ffn_body, n_cast=n_cast, ck=ck, chunks=_ffn_chunks(F, n_split)),
        grid=(n_cast - 1 + n_tiles,),
        in_specs=[tok, _const_spec((1, D)),
                  pl.BlockSpec((D, ck), lambda s: (0, chunk(s))),
                  pl.BlockSpec((D, ck), lambda s: (0, chunk(s))),
                  pl.BlockSpec((ck, D), lambda s: (chunk(s), 0))],
        out_specs=tok,
        out_shape=jax.ShapeDtypeStruct((n_tiles, tm, D), _F32),
        scratch_shapes=[pltpu.VMEM((D, F), _BF16), pltpu.VMEM((D, F), _BF16), pltpu.VMEM((F, D), _BF16),
                        pltpu.VMEM((tm, D), _F32)],
        compiler_params=_params(("arbitrary",)),
        name="ffn",
    )(h.reshape(n_tiles, tm, D), g_ffn, w_gate, w_up, w_down)
    return out.reshape(B, T, D)


def _rope_tables(T):
    half = _HEAD_DIM // 2
    inv_freq = (_ROPE_BASE ** (-np.arange(0, _HEAD_DIM, 2, dtype=np.float32) / _HEAD_DIM)).astype(np.float32)
    ang = (np.arange(T, dtype=np.float32)[:, None] * inv_freq[None, :]).astype(np.float32).astype(np.float64)
    cos, sin = np.cos(ang), np.sin(ang)
    reps = _LANES // _HEAD_DIM
    cos_t = np.tile(np.concatenate([cos, cos], axis=1), (1, reps))
    sin_t = np.tile(np.concatenate([-sin, sin], axis=1), (1, reps))
    return jnp.asarray(cos_t, _F32), jnp.asarray(sin_t, _F32)


def _retention_tables(tb):
    log_g = np.log(1.0 - 2.0 ** (-5.0 - np.arange(_N_HEADS, dtype=np.float64)))
    idx = np.arange(tb, dtype=np.float64)
    dist = np.abs(idx[:, None] - idx[None, :])
    chunk = np.arange(tb) // _RET_CHUNK
    visible = chunk[None, :] <= chunk[:, None]
    dmask = np.where(visible[None], np.exp(log_g[:, None, None] * dist[None]), 0.0)
    qdec = np.repeat(np.exp(log_g[None, :] * (idx[:, None] + 1.0)), _HEAD_DIM, axis=1)
    kdec = np.repeat(np.exp(log_g[None, :] * (tb - 1.0 - idx[:, None])), _HEAD_DIM, axis=1)
    heads_per_group = _RET_GROUP // _HEAD_DIM
    head_of = np.arange(_RET_GROUP) // _HEAD_DIM
    bd = (head_of[:, None] == head_of[None, :]).astype(np.float64)
    step_decay = np.exp(log_g * tb).reshape(-1, heads_per_group)
    sdec = bd[None] * np.repeat(step_decay, _HEAD_DIM, axis=1)[:, None, :]
    f = lambda a: jnp.asarray(a, _F32)
    return f(dmask), f(qdec), f(kdec), f(sdec), f(bd), jnp.asarray(bd / _HEAD_DIM, _BF16)


def _pad_lanes(a):
    return jnp.pad(a, [(0, 0)] * (a.ndim - 1) + [(0, _LANES - a.shape[-1])])


def kernel(x, mem, g_mix, w_in, b_forget, g_ret_out, g_fox_q, g_fox_k, w_out, g_xattn, w_xq, w_xkv,
           g_mem, g_xq, g_xk, w_xo, g_ffn, w_gate, w_up, w_down):
    B, T, D = x.shape
    width = _N_HEADS * _HEAD_DIM
    tb, tq, tm_mix, tm_ffn = 256, 512, 512, 512
    cos_t, sin_t = _rope_tables(T)
    ret_tables = _retention_tables(tb)
    tri = jnp.asarray(np.tril(np.ones((tq, tq), np.float32)), _BF16)
    row = lambda a: a.reshape(1, -1).astype(_F32)

    h = x
    for l in range(w_in.shape[0]):
        b_ff = _pad_lanes(row(b_forget[l]))
        gq = jnp.tile(row(g_fox_q[l]), (1, _N_HEADS)) * (_LOG2E * _HEAD_DIM ** -0.5)
        gk = jnp.tile(row(g_fox_k[l]), (1, _N_HEADS))
        bound = (_HEAD_DIM ** 0.5 * _NORM_ROUNDING_SLACK) * jnp.max(jnp.abs(g_fox_q[l])) * jnp.max(
            jnp.abs(g_fox_k[l]))
        use_shift = bound <= _MAX_FIXED_SHIFT
        shift = jnp.where(use_shift, jnp.ceil(bound * (4.0 * _LOG2E)) * 0.25, 0.0).astype(_F32)
        ret, qa, ka, va, fend = _in_proj(h, row(g_mix[l]), w_in[l].T, b_ff, cos_t, sin_t, gq, gk,
                                         jnp.full((1, _LANES), shift), tri, ret_tables,
                                         row(g_ret_out[l]), tm=tq, tb=tb)
        f_ends = fend[:, :, ::8 // _KSPLIT, :_N_HEADS].reshape(B, -1, _N_HEADS)
        jlo = _first_live_block(f_ends)
        fox = lax.cond(use_shift,
                       lambda jlo, qa, ka, va: _fox_shifted(jlo, qa, ka, va, tq=tq, hg=4, unroll=12),
                       lambda jlo, qa, ka, va: _fox_online(qa, ka, va, tq=tq, hg=4), jlo, qa, ka, va)
        k, v = _mem_kv(mem, row(g_mem[l]), w_xkv[l], row(g_xk[l]))
        h = _mix_xattn(h, ret, fox, w_out[l], row(g_xattn[l]), w_xq[l], row(g_xq[l]), k, v, w_xo[l],
                       tm=tm_mix, sub=256)
        h = _ffn(h, row(g_ffn[l]), w_gate[l], w_up[l], w_down[l], tm=tm_ffn, n_split=2)
    return h
```

```python
import functools

import numpy as np
import jax
import jax.numpy as jnp
from jax import lax
from jax.experimental import pallas as pl
from jax.experimental.pallas import tpu as pltpu

_BF16 = jnp.bfloat16
_F32 = jnp.float32

_EPS = 1e-6
_NEG_INF = -1e30
_ROPE_BASE = 10000.0
_HEAD_DIM = 64
_N_HEADS = 8
_RET_CHUNK = 64
_RET_GROUP = 256
_N_XHEADS = 4
_LANES = 128
_MXU_TILE = 256
_VMEM_LIMIT = 56 * 1024 * 1024

_AUG_QF = 64
_AUG_KF = 88
_AUG_SHIFT = 112
_AUG_ONE = 64

_LOG2E = 1.4426950408889634
_MAX_FIXED_SHIFT = 32.0
_KSPLIT = 2
_UNDERFLOW_LOG2 = -152.0
_NORM_ROUNDING_SLACK = 1.01


def _params(sem):
    return pltpu.CompilerParams(dimension_semantics=sem, vmem_limit_bytes=_VMEM_LIMIT)


def _const_spec(shape):
    nd = len(shape)
    return pl.BlockSpec(shape, lambda *_: (0,) * nd, pipeline_mode=pl.Buffered(1))


def _rms(x, g):
    return x * lax.rsqrt(jnp.mean(x * x, axis=-1, keepdims=True) + _EPS) * g


def _rms_split(x, g):
    return (x * g).astype(_BF16), lax.rsqrt(jnp.mean(x * x, axis=-1, keepdims=True) + _EPS)


def _split3(v):
    hi = v.astype(_BF16).astype(_F32)
    r = v - hi
    mid = r.astype(_BF16).astype(_F32)
    return hi, mid, r - mid


def _in_proj_body(x_ref, g_ref, w32_ref, bf_ref, cos_ref, sin_ref, gq_ref, gk_ref, shift_ref,
                  tri_ref, avg_ref, dmask_ref, qdec_ref, kdec_ref, sdec_ref, bd_ref, gret_ref,
                  ret_ref, qa_ref, ka_ref, va_ref, fend_ref,
                  w_ref, wff_ref, carry_ref, rq_ref, rk_ref, rv_ref, gate_ref, state_ref, *, tm, tb, width):
    i = pl.program_id(1)
    n_main = 7 * width

    @pl.when((pl.program_id(0) == 0) & (i == 0))
    def _():
        for j in range(n_main // width):
            w_ref[:, j * width:(j + 1) * width] = jnp.transpose(
                w32_ref[j * width:(j + 1) * width, :]).astype(_BF16)
        tail = jnp.concatenate([w32_ref[n_main:, :],
                                jnp.zeros((_LANES - _N_HEADS, w32_ref.shape[1]), _F32)], axis=0)
        wff_ref[...] = jnp.transpose(tail).astype(_BF16)

    @pl.when(i == 0)
    def _():
        state_ref[...] = jnp.zeros_like(state_ref)
        carry_ref[...] = jnp.zeros_like(carry_ref)
    hb, inv_rms = _rms_split(x_ref[0], g_ref[...])

    def proj(j):
        return jnp.dot(hb, w_ref[:, j * width:(j + 1) * width], preferred_element_type=_F32) * inv_rms

    lane = lax.broadcasted_iota(jnp.int32, (tm, _LANES), 1)
    low = lane < _HEAD_DIM
    first_half = (lane & (_HEAD_DIM // 2)) == 0
    n_pairs = width // _LANES

    cos = cos_ref[...]
    sin = sin_ref[...]
    for j, out_ref, scale in ((0, rq_ref, _HEAD_DIM ** -0.5), (1, rk_ref, None)):
        y = proj(j)
        for c in range(n_pairs):
            blk = y[:, c * _LANES:(c + 1) * _LANES]
            swapped = jnp.where(first_half, pltpu.roll(blk, _LANES - _HEAD_DIM // 2, 1),
                                pltpu.roll(blk, _HEAD_DIM // 2, 1))
            r = blk * cos + swapped * sin
            if scale is not None:
                r = r * scale
            out_ref[:, c * _LANES:(c + 1) * _LANES] = r.astype(_BF16)
    rv_ref[...] = proj(2).astype(_BF16)
    gate = proj(3)
    gate_ref[...] = (gate * jax.nn.sigmoid(gate)).astype(_BF16)


    half = tm // 2
    z = jnp.concatenate([jnp.dot(hb[:half], wff_ref[...], preferred_element_type=_F32),
                         jnp.dot(hb[half:], wff_ref[...], preferred_element_type=_F32)],
                        axis=0) * inv_rms + bf_ref[...]
    logf = (jnp.minimum(z, 0.0) - jnp.log(1.0 + jnp.exp(-jnp.abs(z)))) * _LOG2E
    hi, mid, lo = _split3(logf)
    tri = tri_ref[...]
    split = jnp.concatenate([hi, mid, lo], axis=1).astype(_BF16)
    parts = jnp.concatenate(
        [jnp.dot(tri[:half, :half], split[:half], preferred_element_type=_F32),
         jnp.dot(tri[half:], split, preferred_element_type=_F32)], axis=0)
    csum = parts[:, :_LANES] + parts[:, _LANES:2 * _LANES] + parts[:, 2 * _LANES:]

    fcum = csum + carry_ref[0:1, :]
    carry_ref[...] = jnp.broadcast_to(fcum[tm - 1:tm, :], carry_ref.shape)
    sub_id = lax.broadcasted_iota(jnp.int32, carry_ref.shape, 0) // (carry_ref.shape[0] // _KSPLIT)
    ends = carry_ref[...]
    for part in range(_KSPLIT - 1):
        row = (part + 1) * (tm // _KSPLIT) - 1
        ends = jnp.where(sub_id == part, jnp.broadcast_to(fcum[row:row + 1, :], carry_ref.shape), ends)
    fend_ref[0, 0] = ends

    def head_rms(y, g_row):
        out = []
        for g0 in range(0, width, _RET_GROUP):
            blk = y[:, g0:g0 + _RET_GROUP]
            ms = jnp.dot((blk * blk).astype(_BF16), avg_ref[...], preferred_element_type=_F32)
            out.append(blk * lax.rsqrt(ms + _EPS) * g_row[:, g0:g0 + _RET_GROUP])
        return out

    qn = head_rms(proj(4), gq_ref[...])
    kn = head_rms(proj(5), gk_ref[...])
    fv = proj(6)

    fh, fm, fl = _split3(fcum)
    n_parts = 3 * _N_HEADS
    packed = jnp.where(lane < _N_HEADS, fh, jnp.where(
        lane < 2 * _N_HEADS, pltpu.roll(fm, _N_HEADS, 1), pltpu.roll(fl, 2 * _N_HEADS, 1)))
    in_qf = (lane >= _AUG_QF) & (lane < _AUG_QF + n_parts)
    in_kf = (lane >= _AUG_KF) & (lane < _AUG_KF + n_parts)
    shared = jnp.where(in_qf, pltpu.roll(packed, _AUG_QF, 1), -pltpu.roll(packed, _AUG_KF, 1))
    q_bias = jnp.where(in_qf, shared, jnp.where(in_kf, 1.0, jnp.where(
        lane == _AUG_SHIFT, -shift_ref[...], 0.0)))
    v_const = jnp.where(lane == _AUG_ONE, 1.0, 0.0)
    for h in range(_N_HEADS):
        own = (lane & (_N_HEADS - 1)) == h
        k_bias = jnp.where(in_kf & own, shared,
                           jnp.where((in_qf & own) | (lane == _AUG_SHIFT), 1.0, 0.0))
        g0, c0 = divmod(h * _HEAD_DIM, _RET_GROUP)
        c0 = (c0 // _LANES) * _LANES
        qb = qn[g0][:, c0:c0 + _LANES]
        kb = kn[g0][:, c0:c0 + _LANES]
        vb = fv[:, (h // 2) * _LANES:(h // 2 + 1) * _LANES]
        if h % 2 == 1:
            qb = pltpu.roll(qb, _HEAD_DIM, 1)
            kb = pltpu.roll(kb, _HEAD_DIM, 1)
            vb = pltpu.roll(vb, _HEAD_DIM, 1)
        qa_ref[0, h] = jnp.where(low, qb, q_bias).astype(_BF16)
        ka_ref[0, h] = jnp.where(low, kb, k_bias).astype(_BF16)
        va_ref[0, h] = jnp.where(low, vb, v_const).astype(_BF16)

    for r0 in range(0, tm, tb):
        _retention_block(rq_ref, rk_ref, rv_ref, gate_ref, slice(r0, r0 + tb), dmask_ref, qdec_ref, kdec_ref,
                         sdec_ref, bd_ref, avg_ref, gret_ref, state_ref, ret_ref)


def _in_proj(x, g_mix, w_in_t, b_ff, cos_t, sin_t, gq, gk, shift, tri, ret_tables, g_ret, *, tm, tb):
    B, T, D = x.shape
    width = _N_HEADS * _HEAD_DIM
    assert w_in_t.shape == (7 * width + _N_HEADS, D) and tm % tb == 0
    dmask, qdec, kdec, sdec, bd, avg = ret_tables
    tok = lambda b, i: (b, i, 0)
    head = lambda b, i: (b, 0, i, 0)
    bf_tok = jax.ShapeDtypeStruct((B, T, width), _BF16)
    bf_head = jax.ShapeDtypeStruct((B, _N_HEADS, T, _LANES), _BF16)
    return pl.pallas_call(
        functools.partial(_in_proj_body, tm=tm, tb=tb, width=width),
        grid=(B, T // tm),
        in_specs=[
            pl.BlockSpec((1, tm, D), tok),
            _const_spec((1, D)),
            _const_spec(w_in_t.shape),
            _const_spec((1, _LANES)),
            pl.BlockSpec((tm, _LANES), lambda b, i: (i, 0)),
            pl.BlockSpec((tm, _LANES), lambda b, i: (i, 0)),
            _const_spec((1, width)),
            _const_spec((1, width)),
            _const_spec((1, _LANES)),
            _const_spec((tm, tm)),
            _const_spec(avg.shape),
            _const_spec(dmask.shape),
            _const_spec(qdec.shape),
            _const_spec(kdec.shape),
            _const_spec(sdec.shape),
            _const_spec(bd.shape),
            _const_spec(g_ret.shape),
        ],
        out_specs=[pl.BlockSpec((1, tm, width), tok)]
        + [pl.BlockSpec((1, _N_HEADS, tm, _LANES), head)] * 3
        + [pl.BlockSpec((1, 1, 8, _LANES), lambda b, i: (b, i, 0, 0))],
        out_shape=[bf_tok] + [bf_head] * 3 + [jax.ShapeDtypeStruct((B, T // tm, 8, _LANES), _F32)],
        scratch_shapes=[pltpu.VMEM((D, 7 * width), _BF16), pltpu.VMEM((D, _LANES), _BF16),
                        pltpu.VMEM((8, _LANES), _F32)]
        + [pltpu.VMEM((tm, width), _BF16)] * 4 + [pltpu.VMEM(sdec.shape, _F32)],
        compiler_params=_params(("arbitrary", "arbitrary")),
        name="in_proj",
    )(x, g_mix, w_in_t, b_ff, cos_t, sin_t, gq, gk, shift, tri, avg, dmask, qdec, kdec, sdec, bd, g_ret)


def _retention_block(rq_ref, rk_ref, rv_ref, gate_ref, rows, dmask_ref, qdec_ref, kdec_ref, sdec_ref, bd_ref,
                     avg_ref, g_ref, state_ref, o_ref):
    tb = rows.stop - rows.start
    nt = (((1,), (1,)), ((), ()))
    tn = (((0,), (0,)), ((), ()))
    gw = state_ref.shape[1]
    lane = lax.broadcasted_iota(jnp.int32, (1, _LANES), 1)
    low = lax.broadcasted_iota(jnp.int32, (tb, _LANES), 1) < _HEAD_DIM
    head_lanes = [jnp.where(lane < _HEAD_DIM, 1.0, 0.0).astype(_BF16),
                  jnp.where(lane < _HEAD_DIM, 0.0, 1.0).astype(_BF16)]
    n_groups = state_ref.shape[0]
    groups = [slice(g * gw, (g + 1) * gw) for g in range(n_groups)]
    scores = []
    for h in range(_N_HEADS):
        ps = slice((h // 2) * _LANES, (h // 2 + 1) * _LANES)
        scores.append(lax.dot_general(rq_ref[rows, ps] * head_lanes[h % 2], rk_ref[rows, ps], nt,
                                      preferred_element_type=_F32))
    inter = []
    for g, gs in enumerate(groups):
        state = state_ref[g]
        inter.append(jnp.dot(rq_ref[rows, gs], state.astype(_BF16),
                             preferred_element_type=_F32) * qdec_ref[:, gs])
        kd = (rk_ref[rows, gs].astype(_F32) * kdec_ref[:, gs]).astype(_BF16)
        state_ref[g] = state * sdec_ref[g] + lax.dot_general(
            kd, rv_ref[rows, gs], tn, preferred_element_type=_F32) * bd_ref[...]
    intra = []
    for h in range(_N_HEADS):
        ps = slice((h // 2) * _LANES, (h // 2 + 1) * _LANES)
        intra.append(jnp.dot((scores[h] * dmask_ref[h]).astype(_BF16), rv_ref[rows, ps],
                             preferred_element_type=_F32))
    outs = []
    for g, gs in enumerate(groups):
        pairs = [jnp.where(low, intra[2 * c], intra[2 * c + 1])
                 for c in range(g * gw // _LANES, (g + 1) * gw // _LANES)]
        outs.append(jnp.concatenate(pairs, axis=1) + inter[g])
    mus = [jnp.dot(o.astype(_BF16), avg_ref[...], preferred_element_type=_F32) for o in outs]
    cents = [o - mu for o, mu in zip(outs, mus)]
    vars_ = [jnp.dot((oc * oc).astype(_BF16), avg_ref[...], preferred_element_type=_F32) for oc in cents]
    for gs, oc, var in zip(groups, cents, vars_):
        y = oc * lax.rsqrt(var + _EPS) * g_ref[:, gs]
        o_ref[0, rows, gs] = (y * gate_ref[rows, gs].astype(_F32)).astype(_BF16)


def _causal_mask(tq):
    row = lax.broadcasted_iota(jnp.int32, (tq, tq), 0)
    col = lax.broadcasted_iota(jnp.int32, (tq, tq), 1)
    return row >= col


def _fox_finish(acc_refs, o_ref, tq):
    lane = lax.broadcasted_iota(jnp.int32, (tq, _LANES), 1)
    for c in range(len(acc_refs) // 2):
        pair = []
        for hh in (2 * c, 2 * c + 1):
            acc = acc_refs[hh][...]
            pair.append(acc / acc[:, _AUG_ONE:_AUG_ONE + 1])
        o_ref[0, :, c * _LANES:(c + 1) * _LANES] = jnp.where(
            lane < _HEAD_DIM, pair[0], pltpu.roll(pair[1], _HEAD_DIM, 1)).astype(_BF16)


def _fox_shifted_body(jlo_ref, qa_ref, ka_ref, va_ref, o_ref, acc_ref, it_code, *, tq, nq, unroll):
    n_pair = qa_ref.shape[1]
    base = (pl.program_id(0) * pl.num_programs(1) + pl.program_id(1)) * (n_pair * nq)
    nt = (((1,), (1,)), ((), ()))
    half = tq // 2
    tk = tq // _KSPLIT
    diff_top = (lax.broadcasted_iota(jnp.int32, (half, half), 1)
                - lax.broadcasted_iota(jnp.int32, (half, half), 0))
    diff_bot = (lax.broadcasted_iota(jnp.int32, (half, tq), 1)
                - lax.broadcasted_iota(jnp.int32, (half, tq), 0))

    def interleave(n_chains, qk, finish):
        s_prev = qk(0)
        for u in range(1, n_chains):
            s_next = qk(u)
            finish(u - 1, s_prev)
            s_prev = s_next
        finish(n_chains - 1, s_prev)

    @pl.when((pl.program_id(0) == 0) & (pl.program_id(1) == 0))
    def _():
        acc_ref[...] = jnp.zeros_like(acc_ref)

    def pack(h, i, slot, keep, j):
        return h | (i << 4) | (slot << 8) | (keep << 16) | (j << 17)

    def unpack(code):
        return (code & 15, (code >> 4) & 15, code >> 17, (code >> 8) & 255, (code >> 16) & 1)

    n = jnp.int32(0)
    for hh in range(n_pair):
        for i in range(1, nq):
            lo = jlo_ref[base + hh * nq + i]
            for j in range(_KSPLIT * i):
                it_code[n + jnp.maximum(j - lo, 0)] = pack(hh, i, hh * nq + i, jnp.where(j == lo, 0, 1), j)
            n = n + (_KSPLIT * i - lo)
    n_trips = (n + (unroll - 1)) // unroll

    def pad(m, carry):
        it_code[m] = jnp.int32(pack(0, 0, n_pair * nq, 0, 0))
        return carry

    lax.fori_loop(n, n_trips * unroll, pad, 0)

    def trip(t, carry):
        items = [unpack(it_code[t * unroll + u]) for u in range(unroll)]

        def qk(u):
            h, i, j = items[u][:3]
            return lax.dot_general(qa_ref[0, h, pl.ds(pl.multiple_of(i * tq, tq), tq), :],
                                   ka_ref[0, h, pl.ds(pl.multiple_of(j * tk, tk), tk), :], nt,
                                   preferred_element_type=_F32)

        def finish(u, s):
            h, _, j, slot, keep = items[u]
            pv = jnp.dot(jnp.exp2(s).astype(_BF16), va_ref[0, h, pl.ds(pl.multiple_of(j * tk, tk), tk), :],
                         preferred_element_type=_F32)
            acc_ref[slot] = jnp.where(keep > 0, acc_ref[slot], 0.0) + pv

        interleave(unroll, qk, finish)
        return carry

    lax.fori_loop(0, n_trips, trip, 0)

    chains = [(hh, i, bottom) for hh in range(n_pair) for i in range(nq) for bottom in (0, 1)]

    def diag_qk(u):
        hh, i, bottom = chains[u]
        r0 = i * tq + bottom * half
        width = tq if bottom else half
        return lax.dot_general(qa_ref[0, hh, r0:r0 + half, :], ka_ref[0, hh, i * tq:i * tq + width, :], nt,
                               preferred_element_type=_F32)

    def diag_finish(u, s):
        hh, i, bottom = chains[u]
        r0 = i * tq + bottom * half
        width = tq if bottom else half
        live = (diff_bot <= half) if bottom else (diff_top <= 0)
        p = jnp.exp2(jnp.where(live, s, _NEG_INF)).astype(_BF16)
        pv = jnp.dot(p, va_ref[0, hh, i * tq:i * tq + width, :], preferred_element_type=_F32)
        if i == 0:
            acc = pv
        else:
            has_off_diagonal = jlo_ref[base + hh * nq + i] < _KSPLIT * i
            acc = jnp.where(has_off_diagonal,
                            acc_ref[hh * nq + i, bottom * half:(bottom + 1) * half, :], 0.0) + pv
        o = acc / acc[:, _AUG_ONE:_AUG_ONE + 1]
        lanes = slice((hh % 2) * _HEAD_DIM, (hh % 2 + 1) * _HEAD_DIM)
        if hh % 2 == 1:
            o = pltpu.roll(o, _HEAD_DIM, 1)
        o_ref[0, r0:r0 + half, hh * _HEAD_DIM:(hh + 1) * _HEAD_DIM] = o[:, lanes].astype(_BF16)

    interleave(len(chains), diag_qk, diag_finish)


def _fox_online_body(qa_ref, ka_ref, va_ref, o_ref, *scratch, tq):
    qi = pl.program_id(2)
    nt = (((1,), (1,)), ((), ()))
    n_heads = qa_ref.shape[1]
    m_refs, acc_refs = scratch[:n_heads], scratch[n_heads:]
    for hh in range(n_heads):
        m_refs[hh][...] = jnp.full_like(m_refs[hh], _NEG_INF)
        acc_refs[hh][...] = jnp.zeros_like(acc_refs[hh])

    def step(j, masked):
        start = pl.multiple_of(j * tq, tq)
        logits = [lax.dot_general(qa_ref[0, hh], ka_ref[0, hh, pl.ds(start, tq), :], nt,
                                  preferred_element_type=_F32) for hh in range(n_heads)]
        for hh in range(n_heads):
            m_ref, acc_ref = m_refs[hh], acc_refs[hh]
            s = logits[hh]
            if masked:
                s = jnp.where(_causal_mask(tq), s, _NEG_INF)
            m_old = m_ref[...]
            m_new = jnp.maximum(m_old, jnp.max(s, axis=-1, keepdims=True))
            p = jnp.exp2(s - m_new[:, 0:1])
            acc_ref[...] = jnp.exp2(m_old - m_new) * acc_ref[...] + jnp.dot(
                p.astype(_BF16), va_ref[0, hh, pl.ds(start, tq), :], preferred_element_type=_F32)
            m_ref[...] = m_new

    def off_diag(j, carry):
        step(j, False)
        return carry

    lax.fori_loop(0, qi, off_diag, 0)
    step(qi, True)
    _fox_finish(acc_refs, o_ref, tq)


def _fox_shifted(jlo, qa, ka, va, *, tq, hg, unroll):
    B, H, T, L = qa.shape
    nq = T // tq
    assert tq % (2 * _MXU_TILE) == 0 and hg % 2 == 0 and H % hg == 0
    assert hg <= 16 and nq <= 16 and hg * nq < 256
    max_items = _KSPLIT * hg * nq * (nq - 1) // 2 + unroll + 1
    blk = pl.BlockSpec((1, hg, T, L), lambda b, p, jlo_ref: (b, p, 0, 0))
    return pl.pallas_call(
        functools.partial(_fox_shifted_body, tq=tq, nq=nq, unroll=unroll),
        grid_spec=pltpu.PrefetchScalarGridSpec(
            num_scalar_prefetch=1,
            grid=(B, H // hg),
            in_specs=[blk, blk, blk],
            out_specs=pl.BlockSpec((1, T, hg * _HEAD_DIM), lambda b, p, jlo_ref: (b, 0, p)),
            scratch_shapes=[pltpu.VMEM((hg * nq + 1, tq, L), _F32)]
            + [pltpu.SMEM((max_items,), jnp.int32)],
        ),
        out_shape=jax.ShapeDtypeStruct((B, T, H * _HEAD_DIM), _BF16),
        compiler_params=_params(("arbitrary", "arbitrary")),
        name="fox_shifted",
    )(jlo, qa, ka, va)


def _fox_online(qa, ka, va, *, tq, hg):
    B, H, T, L = qa.shape
    return pl.pallas_call(
        functools.partial(_fox_online_body, tq=tq),
        grid=(B, H // hg, T // tq),
        in_specs=[
            pl.BlockSpec((1, hg, tq, L), lambda b, p, i: (b, p, i, 0)),
            pl.BlockSpec((1, hg, T, L), lambda b, p, i: (b, p, 0, 0)),
            pl.BlockSpec((1, hg, T, L), lambda b, p, i: (b, p, 0, 0)),
        ],
        out_specs=pl.BlockSpec((1, tq, hg * _HEAD_DIM), lambda b, p, i: (b, i, p)),
        out_shape=jax.ShapeDtypeStruct((B, T, H * _HEAD_DIM), _BF16),
        scratch_shapes=[pltpu.VMEM((tq, L), _F32)] * (2 * hg),
        compiler_params=_params(("arbitrary", "arbitrary", "arbitrary")),
        name="fox_online",
    )(qa, ka, va)


def _first_live_block(fend):
    B, nk, H = fend.shape
    nq = nk // _KSPLIT
    f = jnp.transpose(fend, (0, 2, 1))
    f_q = f[:, :, _KSPLIT - 1::_KSPLIT]
    top = jnp.concatenate([jnp.zeros((B, H, 1), _F32), f_q[:, :, :-1]], axis=-1)
    dead = (top[:, :, :, None] - f[:, :, None, :]) <= _UNDERFLOW_LOG2
    j = jnp.arange(nk, dtype=jnp.int32)
    before = j[None, :] < _KSPLIT * jnp.arange(nq, dtype=jnp.int32)[:, None]
    first_live = jnp.min(jnp.where(dead & before, nk, j), axis=-1)
    return first_live.astype(jnp.int32).reshape(-1)


def _mem_kv_body(mem_ref, gm_ref, w32_ref, gk_ref, k_ref, v_ref, w_ref, *, d_model, xd):
    @pl.when(pl.program_id(0) == 0)
    def _():
        w_ref[...] = w32_ref[...].astype(_BF16)

    mn = _rms(mem_ref[0], gm_ref[...]).astype(_BF16)
    kv = jnp.dot(mn, w_ref[...], preferred_element_type=_F32)
    for h in range(_N_XHEADS):
        sl = slice(h * xd, (h + 1) * xd)
        k_ref[0, :, sl] = (_rms(kv[:, sl], gk_ref[...]) * (xd ** -0.5)).astype(_BF16)
    v_ref[0] = kv[:, d_model:].astype(_BF16)


def _mem_kv(mem, g_mem, w_xkv, g_xk):
    B, M, D = mem.shape
    xd = D // _N_XHEADS
    blk = pl.BlockSpec((1, M, D), lambda b: (b, 0, 0))
    out = jax.ShapeDtypeStruct((B, M, D), _BF16)
    return pl.pallas_call(
        functools.partial(_mem_kv_body, d_model=D, xd=xd),
        grid=(B,),
        in_specs=[blk, _const_spec((1, D)), _const_spec(w_xkv.shape), _const_spec((1, xd))],
        out_specs=[blk, blk],
        out_shape=[out, out],
        scratch_shapes=[pltpu.VMEM(w_xkv.shape, _BF16)],
        compiler_params=_params(("arbitrary",)),
        name="mem_kv",
    )(mem, g_mem, w_xkv, g_xk)


def _mix_xattn_body(x_ref, ret_ref, fox_ref, wo32_ref, gx_ref, wq32_ref, gq_ref, k_ref, v_ref, wxo32_ref,
                    o_ref, wo_ref, wq_ref, wxo_ref, *, width, xd, sub):
    @pl.when((pl.program_id(0) == 0) & (pl.program_id(1) == 0))
    def _():
        wo_ref[...] = wo32_ref[...].astype(_BF16)
        wq_ref[...] = wq32_ref[...].astype(_BF16)
        wxo_ref[...] = wxo32_ref[...].astype(_BF16)

    nt = (((1,), (1,)), ((), ()))
    head_slices = [slice(h * xd, (h + 1) * xd) for h in range(_N_XHEADS)]

    def front(rows):
        h1 = (x_ref[0, rows, :]
              + jnp.dot(ret_ref[0, rows, :], wo_ref[:width, :], preferred_element_type=_F32)
              + jnp.dot(fox_ref[0, rows, :], wo_ref[width:, :], preferred_element_type=_F32))
        hn, inv_h = _rms_split(h1, gx_ref[...])
        y = jnp.dot(hn, wq_ref[...], preferred_element_type=_F32)
        logits = []
        for sl in head_slices:
            yh = y[:, sl]
            inv_q = lax.rsqrt(inv_h * inv_h * jnp.mean(yh * yh, axis=-1, keepdims=True) + _EPS)
            logits.append(lax.dot_general((yh * gq_ref[...]).astype(_BF16), k_ref[0, :, sl], nt,
                                          preferred_element_type=_F32) * (inv_q * inv_h * _LOG2E))
        return h1, logits

    def back(rows, h1, logits):
        probs = []
        for lg in logits:
            p = jnp.exp2(lg - jnp.max(lg, axis=-1, keepdims=True))
            probs.append((p / jnp.sum(p, axis=-1, keepdims=True)).astype(_BF16))
        heads = [jnp.dot(p, v_ref[0, :, sl], preferred_element_type=_F32).astype(_BF16)
                 for p, sl in zip(probs, head_slices)]
        o = jnp.concatenate(heads, axis=-1)
        o_ref[0, rows, :] = h1 + jnp.dot(o, wxo_ref[...], preferred_element_type=_F32)

    tm = x_ref.shape[1]
    subs = [slice(r0, r0 + sub) for r0 in range(0, tm, sub)]
    fronts = [front(rows) for rows in subs]
    for rows, (h1, logits) in zip(subs, fronts):
        back(rows, h1, logits)


def _mix_xattn(x, ret, fox, w_out, g_xattn, w_xq, g_xq, k, v, w_xo, *, tm, sub):
    B, T, D = x.shape
    W = ret.shape[-1]
    M = k.shape[1]
    xd = D // _N_XHEADS
    assert tm % sub == 0 and T % tm == 0
    tok = lambda b, i: (b, i, 0)
    return pl.pallas_call(
        functools.partial(_mix_xattn_body, width=W, xd=xd, sub=sub),
        grid=(B, T // tm),
        in_specs=[
            pl.BlockSpec((1, tm, D), tok),
            pl.BlockSpec((1, tm, W), tok),
            pl.BlockSpec((1, tm, W), tok),
            _const_spec(w_out.shape),
            _const_spec((1, D)),
            _const_spec(w_xq.shape),
            _const_spec((1, xd)),
            pl.BlockSpec((1, M, D), lambda b, i: (b, 0, 0)),
            pl.BlockSpec((1, M, D), lambda b, i: (b, 0, 0)),
            _const_spec(w_xo.shape),
        ],
        out_specs=pl.BlockSpec((1, tm, D), tok),
        out_shape=jax.ShapeDtypeStruct((B, T, D), _F32),
        scratch_shapes=[pltpu.VMEM(w.shape, _BF16) for w in (w_out, w_xq, w_xo)],
        compiler_params=_params(("arbitrary", "arbitrary")),
        name="mix_xattn",
    )(x, ret, fox, w_out, g_xattn, w_xq, g_xq, k, v, w_xo)


def _ffn_chunks(d_ff, n_chunks):
    tiles = -(-d_ff // _MXU_TILE)
    bounds = [min(d_ff, _MXU_TILE * ((tiles * c) // n_chunks)) for c in range(n_chunks)] + [d_ff]
    return [(lo, hi) for lo, hi in zip(bounds[:-1], bounds[1:]) if hi > lo]


def _ffn_body(h_ref, g_ref, wg32_ref, wu32_ref, wd32_ref, o_ref, wg_ref, wu_ref, wd_ref, acc0_ref,
              *, n_cast, ck, chunks):
    step = pl.program_id(0)

    def swiglu(hn, inv_rms, sl):
        gate = jnp.dot(hn, wg_ref[:, sl], preferred_element_type=_F32) * inv_rms
        up = jnp.dot(hn, wu_ref[:, sl], preferred_element_type=_F32) * inv_rms
        a = (gate * jax.nn.sigmoid(gate) * up).astype(_BF16)
        return jnp.dot(a, wd_ref[sl, :], preferred_element_type=_F32)

    for c in range(n_cast):
        @pl.when(step == c)
        def _(c=c):
            sl = slice(c * ck, (c + 1) * ck)
            wg_ref[:, sl] = wg32_ref[...].astype(_BF16)
            wu_ref[:, sl] = wu32_ref[...].astype(_BF16)
            wd_ref[sl, :] = wd32_ref[...].astype(_BF16)
            h = h_ref[0]
            part = swiglu(*_rms_split(h, g_ref[...]), sl)
            if c == 0:
                acc0_ref[...] = h + part
            elif c < n_cast - 1:
                acc0_ref[...] += part
            else:
                o_ref[0] = acc0_ref[...] + part

    @pl.when(step >= n_cast)
    def _():
        h = h_ref[0]
        hn, inv_rms = _rms_split(h, g_ref[...])
        acc = h
        for lo, hi in chunks:
            acc = acc + swiglu(hn, inv_rms, slice(lo, hi))
        o_ref[0] = acc


def _ffn(h, g_ffn, w_gate, w_up, w_down, *, tm, n_split):
    B, T, D = h.shape
    F = w_gate.shape[1]
    ck = _MXU_TILE
    assert F % ck == 0 and T % tm == 0
    n_cast = F // ck
    n_tiles = B * T // tm
    tile = lambda s: (jnp.maximum(s - (n_cast - 1), 0), 0, 0)
    chunk = lambda s: jnp.minimum(s, n_cast - 1)
    tok = pl.BlockSpec((1, tm, D), tile)
    out = pl.pallas_call(
        functools.partial(_ffn_body, n_cast=n_cast, ck=ck, chunks=_ffn_chunks(F, n_split)),
        grid=(n_cast - 1 + n_tiles,),
        in_specs=[tok, _const_spec((1, D)),
                  pl.BlockSpec((D, ck), lambda s: (0, chunk(s))),
                  pl.BlockSpec((D, ck), lambda s: (0, chunk(s))),
                  pl.BlockSpec((ck, D), lambda s: (chunk(s), 0))],
        out_specs=tok,
        out_shape=jax.ShapeDtypeStruct((n_tiles, tm, D), _F32),
        scratch_shapes=[pltpu.VMEM((D, F), _BF16), pltpu.VMEM((D, F), _BF16), pltpu.VMEM((F, D), _BF16),
                        pltpu.VMEM((tm, D), _F32)],
        compiler_params=_params(("arbitrary",)),
        name="ffn",
    )(h.reshape(n_tiles, tm, D), g_ffn, w_gate, w_up, w_down)
    return out.reshape(B, T, D)


def _rope_tables(T):
    half = _HEAD_DIM // 2
    inv_freq = (_ROPE_BASE ** (-np.arange(0, _HEAD_DIM, 2, dtype=np.float32) / _HEAD_DIM)).astype(np.float32)
    ang = (np.arange(T, dtype=np.float32)[:, None] * inv_freq[None, :]).astype(np.float32).astype(np.float64)
    cos, sin = np.cos(ang), np.sin(ang)
    reps = _LANES // _HEAD_DIM
    cos_t = np.tile(np.concatenate([cos, cos], axis=1), (1, reps))
    sin_t = np.tile(np.concatenate([-sin, sin], axis=1), (1, reps))
    return jnp.asarray(cos_t, _F32), jnp.asarray(sin_t, _F32)


def _retention_tables(tb):
    log_g = np.log(1.0 - 2.0 ** (-5.0 - np.arange(_N_HEADS, dtype=np.float64)))
    idx = np.arange(tb, dtype=np.float64)
    dist = np.abs(idx[:, None] - idx[None, :])
    chunk = np.arange(tb) // _RET_CHUNK
    visible = chunk[None, :] <= chunk[:, None]
    dmask = np.where(visible[None], np.exp(log_g[:, None, None] * dist[None]), 0.0)
    qdec = np.repeat(np.exp(log_g[None, :] * (idx[:, None] + 1.0)), _HEAD_DIM, axis=1)
    kdec = np.repeat(np.exp(log_g[None, :] * (tb - 1.0 - idx[:, None])), _HEAD_DIM, axis=1)
    heads_per_group = _RET_GROUP // _HEAD_DIM
    head_of = np.arange(_RET_GROUP) // _HEAD_DIM
    bd = (head_of[:, None] == head_of[None, :]).astype(np.float64)
    step_decay = np.exp(log_g * tb).reshape(-1, heads_per_group)
    sdec = bd[None] * np.repeat(step_decay, _HEAD_DIM, axis=1)[:, None, :]
    f = lambda a: jnp.asarray(a, _F32)
    return f(dmask), f(qdec), f(kdec), f(sdec), f(bd), jnp.asarray(bd / _HEAD_DIM, _BF16)


def _pad_lanes(a):
    return jnp.pad(a, [(0, 0)] * (a.ndim - 1) + [(0, _LANES - a.shape[-1])])


def kernel(x, mem, g_mix, w_in, b_forget, g_ret_out, g_fox_q, g_fox_k, w_out, g_xattn, w_xq, w_xkv,
           g_mem, g_xq, g_xk, w_xo, g_ffn, w_gate, w_up, w_down):
    B, T, D = x.shape
    width = _N_HEADS * _HEAD_DIM
    tb, tq, tm_mix, tm_ffn = 256, 512, 1024, 512
    cos_t, sin_t = _rope_tables(T)
    ret_tables = _retention_tables(tb)
    tri = jnp.asarray(np.tril(np.ones((tq, tq), np.float32)), _BF16)
    row = lambda a: a.reshape(1, -1).astype(_F32)

    h = x
    for l in range(w_in.shape[0]):
        b_ff = _pad_lanes(row(b_forget[l]))
        gq = jnp.tile(row(g_fox_q[l]), (1, _N_HEADS)) * (_LOG2E * _HEAD_DIM ** -0.5)
        gk = jnp.tile(row(g_fox_k[l]), (1, _N_HEADS))
        bound = (_HEAD_DIM ** 0.5 * _NORM_ROUNDING_SLACK) * jnp.max(jnp.abs(g_fox_q[l])) * jnp.max(
            jnp.abs(g_fox_k[l]))
        use_shift = bound <= _MAX_FIXED_SHIFT
        shift = jnp.where(use_shift, jnp.ceil(bound * (4.0 * _LOG2E)) * 0.25, 0.0).astype(_F32)
        ret, qa, ka, va, fend = _in_proj(h, row(g_mix[l]), w_in[l].T, b_ff, cos_t, sin_t, gq, gk,
                                         jnp.full((1, _LANES), shift), tri, ret_tables,
                                         row(g_ret_out[l]), tm=tq, tb=tb)
        f_ends = fend[:, :, ::8 // _KSPLIT, :_N_HEADS].reshape(B, -1, _N_HEADS)
        jlo = _first_live_block(f_ends)
        fox = lax.cond(use_shift,
                       lambda jlo, qa, ka, va: _fox_shifted(jlo, qa, ka, va, tq=tq, hg=4, unroll=12),
                       lambda jlo, qa, ka, va: _fox_online(qa, ka, va, tq=tq, hg=4), jlo, qa, ka, va)
        k, v = _mem_kv(mem, row(g_mem[l]), w_xkv[l], row(g_xk[l]))
        h = _mix_xattn(h, ret, fox, w_out[l], row(g_xattn[l]), w_xq[l], row(g_xq[l]), k, v, w_xo[l],
                       tm=tm_mix, sub=256)
        h = _ffn(h, row(g_ffn[l]), w_gate[l], w_up[l], w_down[l], tm=tm_ffn, n_split=2)
    return h
```

```python
import functools

import numpy as np
import jax
import jax.numpy as jnp
from jax import lax
from jax.experimental import pallas as pl
from jax.experimental.pallas import tpu as pltpu

_BF16 = jnp.bfloat16
_F32 = jnp.float32

_EPS = 1e-6
_NEG_INF = -1e30
_ROPE_BASE = 10000.0
_HEAD_DIM = 64
_N_HEADS = 8
_RET_CHUNK = 64
_RET_GROUP = 256
_N_XHEADS = 4
_LANES = 128
_MXU_TILE = 256
_VMEM_LIMIT = 56 * 1024 * 1024

_AUG_QF = 64
_AUG_KF = 88
_AUG_SHIFT = 112
_AUG_ONE = 64

_LOG2E = 1.4426950408889634
_MAX_FIXED_SHIFT = 32.0
_KSPLIT = 2
_UNDERFLOW_LOG2 = -152.0
_NORM_ROUNDING_SLACK = 1.01


def _params(sem):
    return pltpu.CompilerParams(dimension_semantics=sem, vmem_limit_bytes=_VMEM_LIMIT)


def _const_spec(shape):
    nd = len(shape)
    return pl.BlockSpec(shape, lambda *_: (0,) * nd, pipeline_mode=pl.Buffered(1))


def _rms(x, g):
    return x * lax.rsqrt(jnp.mean(x * x, axis=-1, keepdims=True) + _EPS) * g


def _rms_split(x, g):
    return (x * g).astype(_BF16), lax.rsqrt(jnp.mean(x * x, axis=-1, keepdims=True) + _EPS)


def _split3(v):
    hi = v.astype(_BF16).astype(_F32)
    r = v - hi
    mid = r.astype(_BF16).astype(_F32)
    return hi, mid, r - mid


def _in_proj_body(x_ref, g_ref, w32_ref, bf_ref, cos_ref, sin_ref, gq_ref, gk_ref, shift_ref,
                  tri_ref, avg_ref, dmask_ref, qdec_ref, kdec_ref, sdec_ref, bd_ref, gret_ref,
                  ret_ref, qa_ref, ka_ref, va_ref, fend_ref,
                  w_ref, wff_ref, carry_ref, rq_ref, rk_ref, rv_ref, gate_ref, state_ref, *, tm, tb, width):
    i = pl.program_id(1)
    n_main = 7 * width

    @pl.when((pl.program_id(0) == 0) & (i == 0))
    def _():
        for j in range(n_main // width):
            w_ref[:, j * width:(j + 1) * width] = jnp.transpose(
                w32_ref[j * width:(j + 1) * width, :]).astype(_BF16)
        tail = jnp.concatenate([w32_ref[n_main:, :],
                                jnp.zeros((_LANES - _N_HEADS, w32_ref.shape[1]), _F32)], axis=0)
        wff_ref[...] = jnp.transpose(tail).astype(_BF16)

    @pl.when(i == 0)
    def _():
        state_ref[...] = jnp.zeros_like(state_ref)
        carry_ref[...] = jnp.zeros_like(carry_ref)
    hb, inv_rms = _rms_split(x_ref[0], g_ref[...])

    def proj(j):
        return jnp.dot(hb, w_ref[:, j * width:(j + 1) * width], preferred_element_type=_F32) * inv_rms

    lane = lax.broadcasted_iota(jnp.int32, (tm, _LANES), 1)
    low = lane < _HEAD_DIM
    first_half = (lane & (_HEAD_DIM // 2)) == 0
    n_pairs = width // _LANES

    cos = cos_ref[...]
    sin = sin_ref[...]
    for j, out_ref, scale in ((0, rq_ref, _HEAD_DIM ** -0.5), (1, rk_ref, None)):
        y = proj(j)
        for c in range(n_pairs):
            blk = y[:, c * _LANES:(c + 1) * _LANES]
            swapped = jnp.where(first_half, pltpu.roll(blk, _LANES - _HEAD_DIM // 2, 1),
                                pltpu.roll(blk, _HEAD_DIM // 2, 1))
            r = blk * cos + swapped * sin
            if scale is not None:
                r = r * scale
            out_ref[:, c * _LANES:(c + 1) * _LANES] = r.astype(_BF16)
    rv_ref[...] = proj(2).astype(_BF16)
    gate = proj(3)
    gate_ref[...] = (gate * jax.nn.sigmoid(gate)).astype(_BF16)


    half = tm // 2
    z = jnp.concatenate([jnp.dot(hb[:half], wff_ref[...], preferred_element_type=_F32),
                         jnp.dot(hb[half:], wff_ref[...], preferred_element_type=_F32)],
                        axis=0) * inv_rms + bf_ref[...]
    logf = (jnp.minimum(z, 0.0) - jnp.log(1.0 + jnp.exp(-jnp.abs(z)))) * _LOG2E
    hi, mid, lo = _split3(logf)
    tri = tri_ref[...]
    split = jnp.concatenate([hi, mid, lo], axis=1).astype(_BF16)
    parts = jnp.concatenate(
        [jnp.dot(tri[:half, :half], split[:half], preferred_element_type=_F32),
         jnp.dot(tri[half:], split, preferred_element_type=_F32)], axis=0)
    csum = parts[:, :_LANES] + parts[:, _LANES:2 * _LANES] + parts[:, 2 * _LANES:]

    fcum = csum + carry_ref[0:1, :]
    carry_ref[...] = jnp.broadcast_to(fcum[tm - 1:tm, :], carry_ref.shape)
    sub_id = lax.broadcasted_iota(jnp.int32, carry_ref.shape, 0) // (carry_ref.shape[0] // _KSPLIT)
    ends = carry_ref[...]
    for part in range(_KSPLIT - 1):
        row = (part + 1) * (tm // _KSPLIT) - 1
        ends = jnp.where(sub_id == part, jnp.broadcast_to(fcum[row:row + 1, :], carry_ref.shape), ends)
    fend_ref[0, 0] = ends

    def head_rms(y, g_row):
        out = []
        for g0 in range(0, width, _RET_GROUP):
            blk = y[:, g0:g0 + _RET_GROUP]
            ms = jnp.dot((blk * blk).astype(_BF16), avg_ref[...], preferred_element_type=_F32)
            out.append(blk * lax.rsqrt(ms + _EPS) * g_row[:, g0:g0 + _RET_GROUP])
        return out

    qn = head_rms(proj(4), gq_ref[...])
    kn = head_rms(proj(5), gk_ref[...])
    fv = proj(6)

    fh, fm, fl = _split3(fcum)
    n_parts = 3 * _N_HEADS
    packed = jnp.where(lane < _N_HEADS, fh, jnp.where(
        lane < 2 * _N_HEADS, pltpu.roll(fm, _N_HEADS, 1), pltpu.roll(fl, 2 * _N_HEADS, 1)))
    in_qf = (lane >= _AUG_QF) & (lane < _AUG_QF + n_parts)
    in_kf = (lane >= _AUG_KF) & (lane < _AUG_KF + n_parts)
    shared = jnp.where(in_qf, pltpu.roll(packed, _AUG_QF, 1), -pltpu.roll(packed, _AUG_KF, 1))
    q_bias = jnp.where(in_qf, shared, jnp.where(in_kf, 1.0, jnp.where(
        lane == _AUG_SHIFT, -shift_ref[...], 0.0)))
    v_const = jnp.where(lane == _AUG_ONE, 1.0, 0.0)
    for h in range(_N_HEADS):
        own = (lane & (_N_HEADS - 1)) == h
        k_bias = jnp.where(in_kf & own, shared,
                           jnp.where((in_qf & own) | (lane == _AUG_SHIFT), 1.0, 0.0))
        g0, c0 = divmod(h * _HEAD_DIM, _RET_GROUP)
        c0 = (c0 // _LANES) * _LANES
        qb = qn[g0][:, c0:c0 + _LANES]
        kb = kn[g0][:, c0:c0 + _LANES]
        vb = fv[:, (h // 2) * _LANES:(h // 2 + 1) * _LANES]
        if h % 2 == 1:
            qb = pltpu.roll(qb, _HEAD_DIM, 1)
            kb = pltpu.roll(kb, _HEAD_DIM, 1)
            vb = pltpu.roll(vb, _HEAD_DIM, 1)
        qa_ref[0, h] = jnp.where(low, qb, q_bias).astype(_BF16)
        ka_ref[0, h] = jnp.where(low, kb, k_bias).astype(_BF16)
        va_ref[0, h] = jnp.where(low, vb, v_const).astype(_BF16)

    _retention_blocks(rq_ref, rk_ref, rv_ref, gate_ref, [slice(r0, r0 + tb) for r0 in range(0, tm, tb)],
                      dmask_ref, qdec_ref, kdec_ref, sdec_ref, bd_ref, avg_ref, gret_ref, state_ref, ret_ref)


def _in_proj(x, g_mix, w_in_t, b_ff, cos_t, sin_t, gq, gk, shift, tri, ret_tables, g_ret, *, tm, tb):
    B, T, D = x.shape
    width = _N_HEADS * _HEAD_DIM
    assert w_in_t.shape == (7 * width + _N_HEADS, D) and tm % tb == 0
    dmask, qdec, kdec, sdec, bd, avg = ret_tables
    tok = lambda b, i: (b, i, 0)
    head = lambda b, i: (b, 0, i, 0)
    bf_tok = jax.ShapeDtypeStruct((B, T, width), _BF16)
    bf_head = jax.ShapeDtypeStruct((B, _N_HEADS, T, _LANES), _BF16)
    return pl.pallas_call(
        functools.partial(_in_proj_body, tm=tm, tb=tb, width=width),
        grid=(B, T // tm),
        in_specs=[
            pl.BlockSpec((1, tm, D), tok),
            _const_spec((1, D)),
            _const_spec(w_in_t.shape),
            _const_spec((1, _LANES)),
            pl.BlockSpec((tm, _LANES), lambda b, i: (i, 0)),
            pl.BlockSpec((tm, _LANES), lambda b, i: (i, 0)),
            _const_spec((1, width)),
            _const_spec((1, width)),
            _const_spec((1, _LANES)),
            _const_spec((tm, tm)),
            _const_spec(avg.shape),
            _const_spec(dmask.shape),
            _const_spec(qdec.shape),
            _const_spec(kdec.shape),
            _const_spec(sdec.shape),
            _const_spec(bd.shape),
            _const_spec(g_ret.shape),
        ],
        out_specs=[pl.BlockSpec((1, tm, width), tok)]
        + [pl.BlockSpec((1, _N_HEADS, tm, _LANES), head)] * 3
        + [pl.BlockSpec((1, 1, 8, _LANES), lambda b, i: (b, i, 0, 0))],
        out_shape=[bf_tok] + [bf_head] * 3 + [jax.ShapeDtypeStruct((B, T // tm, 8, _LANES), _F32)],
        scratch_shapes=[pltpu.VMEM((D, 7 * width), _BF16), pltpu.VMEM((D, _LANES), _BF16),
                        pltpu.VMEM((8, _LANES), _F32)]
        + [pltpu.VMEM((tm, width), _BF16)] * 4 + [pltpu.VMEM(sdec.shape, _F32)],
        compiler_params=_params(("arbitrary", "arbitrary")),
        name="in_proj",
    )(x, g_mix, w_in_t, b_ff, cos_t, sin_t, gq, gk, shift, tri, avg, dmask, qdec, kdec, sdec, bd, g_ret)


def _retention_blocks(rq_ref, rk_ref, rv_ref, gate_ref, blocks, dmask_ref, qdec_ref, kdec_ref, sdec_ref, bd_ref,
                      avg_ref, g_ref, state_ref, o_ref):
    tb = blocks[0].stop - blocks[0].start
    pair_lanes = [slice((h // 2) * _LANES, (h // 2 + 1) * _LANES) for h in range(_N_HEADS)]
    nt = (((1,), (1,)), ((), ()))
    tn = (((0,), (0,)), ((), ()))
    gw = state_ref.shape[1]
    lane = lax.broadcasted_iota(jnp.int32, (1, _LANES), 1)
    low = lax.broadcasted_iota(jnp.int32, (tb, _LANES), 1) < _HEAD_DIM
    head_lanes = [jnp.where(lane < _HEAD_DIM, 1.0, 0.0).astype(_BF16),
                  jnp.where(lane < _HEAD_DIM, 0.0, 1.0).astype(_BF16)]
    n_groups = state_ref.shape[0]
    groups = [slice(g * gw, (g + 1) * gw) for g in range(n_groups)]
    scores = [[lax.dot_general(rq_ref[rows, ps] * head_lanes[h % 2], rk_ref[rows, ps], nt,
                               preferred_element_type=_F32)
               for h, ps in enumerate(pair_lanes)] for rows in blocks]
    inter = []
    for rows in blocks:
        inter.append([])
        for g, gs in enumerate(groups):
            state = state_ref[g]
            inter[-1].append(jnp.dot(rq_ref[rows, gs], state.astype(_BF16),
                                     preferred_element_type=_F32) * qdec_ref[:, gs])
            kd = (rk_ref[rows, gs].astype(_F32) * kdec_ref[:, gs]).astype(_BF16)
            state_ref[g] = state * sdec_ref[g] + lax.dot_general(
                kd, rv_ref[rows, gs], tn, preferred_element_type=_F32) * bd_ref[...]
    intra = [[jnp.dot((sc[h] * dmask_ref[h]).astype(_BF16), rv_ref[rows, ps], preferred_element_type=_F32)
              for h, ps in enumerate(pair_lanes)] for rows, sc in zip(blocks, scores)]
    outs = []
    for b in range(len(blocks)):
        for g in range(n_groups):
            pairs = [jnp.where(low, intra[b][2 * c], intra[b][2 * c + 1])
                     for c in range(g * gw // _LANES, (g + 1) * gw // _LANES)]
            outs.append(jnp.concatenate(pairs, axis=1) + inter[b][g])
    mus = [jnp.dot(o.astype(_BF16), avg_ref[...], preferred_element_type=_F32) for o in outs]
    cents = [o - mu for o, mu in zip(outs, mus)]
    vars_ = [jnp.dot((oc * oc).astype(_BF16), avg_ref[...], preferred_element_type=_F32) for oc in cents]
    for k, (oc, var) in enumerate(zip(cents, vars_)):
        rows, gs = blocks[k // n_groups], groups[k % n_groups]
        y = oc * lax.rsqrt(var + _EPS) * g_ref[:, gs]
        o_ref[0, rows, gs] = (y * gate_ref[rows, gs].astype(_F32)).astype(_BF16)


def _causal_mask(tq):
    row = lax.broadcasted_iota(jnp.int32, (tq, tq), 0)
    col = lax.broadcasted_iota(jnp.int32, (tq, tq), 1)
    return row >= col


def _fox_finish(acc_refs, o_ref, tq):
    lane = lax.broadcasted_iota(jnp.int32, (tq, _LANES), 1)
    for c in range(len(acc_refs) // 2):
        pair = []
        for hh in (2 * c, 2 * c + 1):
            acc = acc_refs[hh][...]
            pair.append(acc / acc[:, _AUG_ONE:_AUG_ONE + 1])
        o_ref[0, :, c * _LANES:(c + 1) * _LANES] = jnp.where(
            lane < _HEAD_DIM, pair[0], pltpu.roll(pair[1], _HEAD_DIM, 1)).astype(_BF16)


def _fox_shifted_body(jlo_ref, qa_ref, ka_ref, va_ref, o_ref, acc_ref, it_code, *, tq, nq, unroll):
    n_pair = qa_ref.shape[1]
    base = (pl.program_id(0) * pl.num_programs(1) + pl.program_id(1)) * (n_pair * nq)
    nt = (((1,), (1,)), ((), ()))
    half = tq // 2
    tk = tq // _KSPLIT
    diff_top = (lax.broadcasted_iota(jnp.int32, (half, half), 1)
                - lax.broadcasted_iota(jnp.int32, (half, half), 0))
    diff_bot = (lax.broadcasted_iota(jnp.int32, (half, tq), 1)
                - lax.broadcasted_iota(jnp.int32, (half, tq), 0))

    def interleave(n_chains, qk, finish):
        s_prev = qk(0)
        for u in range(1, n_chains):
            s_next = qk(u)
            finish(u - 1, s_prev)
            s_prev = s_next
        finish(n_chains - 1, s_prev)

    @pl.when((pl.program_id(0) == 0) & (pl.program_id(1) == 0))
    def _():
        acc_ref[...] = jnp.zeros_like(acc_ref)

    def pack(h, i, slot, keep, j):
        return h | (i << 4) | (slot << 8) | (keep << 16) | (j << 17)

    def unpack(code):
        return (code & 15, (code >> 4) & 15, code >> 17, (code >> 8) & 255, (code >> 16) & 1)

    n = jnp.int32(0)
    for hh in range(n_pair):
        for i in range(1, nq):
            lo = jlo_ref[base + hh * nq + i]
            for j in range(_KSPLIT * i):
                it_code[n + jnp.maximum(j - lo, 0)] = pack(hh, i, hh * nq + i, jnp.where(j == lo, 0, 1), j)
            n = n + (_KSPLIT * i - lo)
    n_trips = (n + (unroll - 1)) // unroll

    def pad(m, carry):
        it_code[m] = jnp.int32(pack(0, 0, n_pair * nq, 0, 0))
        return carry

    lax.fori_loop(n, n_trips * unroll, pad, 0)

    def trip(t, carry):
        items = [unpack(it_code[t * unroll + u]) for u in range(unroll)]

        def qk(u):
            h, i, j = items[u][:3]
            return lax.dot_general(qa_ref[0, h, pl.ds(pl.multiple_of(i * tq, tq), tq), :],
                                   ka_ref[0, h, pl.ds(pl.multiple_of(j * tk, tk), tk), :], nt,
                                   preferred_element_type=_F32)

        def finish(u, s):
            h, _, j, slot, keep = items[u]
            pv = jnp.dot(jnp.exp2(s).astype(_BF16), va_ref[0, h, pl.ds(pl.multiple_of(j * tk, tk), tk), :],
                         preferred_element_type=_F32)
            acc_ref[slot] = jnp.where(keep > 0, acc_ref[slot], 0.0) + pv

        interleave(unroll, qk, finish)
        return carry

    lax.fori_loop(0, n_trips, trip, 0)

    chains = [(hh, i, bottom) for hh in range(n_pair) for i in range(nq) for bottom in (0, 1)]

    def diag_qk(u):
        hh, i, bottom = chains[u]
        r0 = i * tq + bottom * half
        width = tq if bottom else half
        return lax.dot_general(qa_ref[0, hh, r0:r0 + half, :], ka_ref[0, hh, i * tq:i * tq + width, :], nt,
                               preferred_element_type=_F32)

    def diag_finish(u, s):
        hh, i, bottom = chains[u]
        r0 = i * tq + bottom * half
        width = tq if bottom else half
        live = (diff_bot <= half) if bottom else (diff_top <= 0)
        p = jnp.exp2(jnp.where(live, s, _NEG_INF)).astype(_BF16)
        pv = jnp.dot(p, va_ref[0, hh, i * tq:i * tq + width, :], preferred_element_type=_F32)
        if i == 0:
            acc = pv
        else:
            has_off_diagonal = jlo_ref[base + hh * nq + i] < _KSPLIT * i
            acc = jnp.where(has_off_diagonal,
                            acc_ref[hh * nq + i, bottom * half:(bottom + 1) * half, :], 0.0) + pv
        o = acc / acc[:, _AUG_ONE:_AUG_ONE + 1]
        lanes = slice((hh % 2) * _HEAD_DIM, (hh % 2 + 1) * _HEAD_DIM)
        if hh % 2 == 1:
            o = pltpu.roll(o, _HEAD_DIM, 1)
        o_ref[0, r0:r0 + half, hh * _HEAD_DIM:(hh + 1) * _HEAD_DIM] = o[:, lanes].astype(_BF16)

    interleave(len(chains), diag_qk, diag_finish)


def _fox_online_body(qa_ref, ka_ref, va_ref, o_ref, *scratch, tq):
    qi = pl.program_id(2)
    nt = (((1,), (1,)), ((), ()))
    n_heads = qa_ref.shape[1]
    m_refs, acc_refs = scratch[:n_heads], scratch[n_heads:]
    for hh in range(n_heads):
        m_refs[hh][...] = jnp.full_like(m_refs[hh], _NEG_INF)
        acc_refs[hh][...] = jnp.zeros_like(acc_refs[hh])

    def step(j, masked):
        start = pl.multiple_of(j * tq, tq)
        logits = [lax.dot_general(qa_ref[0, hh], ka_ref[0, hh, pl.ds(start, tq), :], nt,
                                  preferred_element_type=_F32) for hh in range(n_heads)]
        for hh in range(n_heads):
            m_ref, acc_ref = m_refs[hh], acc_refs[hh]
            s = logits[hh]
            if masked:
                s = jnp.where(_causal_mask(tq), s, _NEG_INF)
            m_old = m_ref[...]
            m_new = jnp.maximum(m_old, jnp.max(s, axis=-1, keepdims=True))
            p = jnp.exp2(s - m_new[:, 0:1])
            acc_ref[...] = jnp.exp2(m_old - m_new) * acc_ref[...] + jnp.dot(
                p.astype(_BF16), va_ref[0, hh, pl.ds(start, tq), :], preferred_element_type=_F32)
            m_ref[...] = m_new

    def off_diag(j, carry):
        step(j, False)
        return carry

    lax.fori_loop(0, qi, off_diag, 0)
    step(qi, True)
    _fox_finish(acc_refs, o_ref, tq)


def _fox_shifted(jlo, qa, ka, va, *, tq, hg, unroll):
    B, H, T, L = qa.shape
    nq = T // tq
    assert tq % (2 * _MXU_TILE) == 0 and hg % 2 == 0 and H % hg == 0
    assert hg <= 16 and nq <= 16 and hg * nq < 256
    max_items = _KSPLIT * hg * nq * (nq - 1) // 2 + unroll + 1
    blk = pl.BlockSpec((1, hg, T, L), lambda b, p, jlo_ref: (b, p, 0, 0))
    return pl.pallas_call(
        functools.partial(_fox_shifted_body, tq=tq, nq=nq, unroll=unroll),
        grid_spec=pltpu.PrefetchScalarGridSpec(
            num_scalar_prefetch=1,
            grid=(B, H // hg),
            in_specs=[blk, blk, blk],
            out_specs=pl.BlockSpec((1, T, hg * _HEAD_DIM), lambda b, p, jlo_ref: (b, 0, p)),
            scratch_shapes=[pltpu.VMEM((hg * nq + 1, tq, L), _F32)]
            + [pltpu.SMEM((max_items,), jnp.int32)],
        ),
        out_shape=jax.ShapeDtypeStruct((B, T, H * _HEAD_DIM), _BF16),
        compiler_params=_params(("arbitrary", "arbitrary")),
        name="fox_shifted",
    )(jlo, qa, ka, va)


def _fox_online(qa, ka, va, *, tq, hg):
    B, H, T, L = qa.shape
    return pl.pallas_call(
        functools.partial(_fox_online_body, tq=tq),
        grid=(B, H // hg, T // tq),
        in_specs=[
            pl.BlockSpec((1, hg, tq, L), lambda b, p, i: (b, p, i, 0)),
            pl.BlockSpec((1, hg, T, L), lambda b, p, i: (b, p, 0, 0)),
            pl.BlockSpec((1, hg, T, L), lambda b, p, i: (b, p, 0, 0)),
        ],
        out_specs=pl.BlockSpec((1, tq, hg * _HEAD_DIM), lambda b, p, i: (b, i, p)),
        out_shape=jax.ShapeDtypeStruct((B, T, H * _HEAD_DIM), _BF16),
        scratch_shapes=[pltpu.VMEM((tq, L), _F32)] * (2 * hg),
        compiler_params=_params(("arbitrary", "arbitrary", "arbitrary")),
        name="fox_online",
    )(qa, ka, va)


def _first_live_block(fend):
    B, nk, H = fend.shape
    nq = nk // _KSPLIT
    f = jnp.transpose(fend, (0, 2, 1))
    f_q = f[:, :, _KSPLIT - 1::_KSPLIT]
    top = jnp.concatenate([jnp.zeros((B, H, 1), _F32), f_q[:, :, :-1]], axis=-1)
    dead = (top[:, :, :, None] - f[:, :, None, :]) <= _UNDERFLOW_LOG2
    j = jnp.arange(nk, dtype=jnp.int32)
    before = j[None, :] < _KSPLIT * jnp.arange(nq, dtype=jnp.int32)[:, None]
    first_live = jnp.min(jnp.where(dead & before, nk, j), axis=-1)
    return first_live.astype(jnp.int32).reshape(-1)


def _mem_kv_body(mem_ref, gm_ref, w32_ref, gk_ref, k_ref, v_ref, w_ref, *, d_model, xd):
    @pl.when(pl.program_id(0) == 0)
    def _():
        w_ref[...] = w32_ref[...].astype(_BF16)

    mn = _rms(mem_ref[0], gm_ref[...]).astype(_BF16)
    kv = jnp.dot(mn, w_ref[...], preferred_element_type=_F32)
    for h in range(_N_XHEADS):
        sl = slice(h * xd, (h + 1) * xd)
        k_ref[0, :, sl] = (_rms(kv[:, sl], gk_ref[...]) * (xd ** -0.5)).astype(_BF16)
    v_ref[0] = kv[:, d_model:].astype(_BF16)


def _mem_kv(mem, g_mem, w_xkv, g_xk):
    B, M, D = mem.shape
    xd = D // _N_XHEADS
    blk = pl.BlockSpec((1, M, D), lambda b: (b, 0, 0))
    out = jax.ShapeDtypeStruct((B, M, D), _BF16)
    return pl.pallas_call(
        functools.partial(_mem_kv_body, d_model=D, xd=xd),
        grid=(B,),
        in_specs=[blk, _const_spec((1, D)), _const_spec(w_xkv.shape), _const_spec((1, xd))],
        out_specs=[blk, blk],
        out_shape=[out, out],
        scratch_shapes=[pltpu.VMEM(w_xkv.shape, _BF16)],
        compiler_params=_params(("arbitrary",)),
        name="mem_kv",
    )(mem, g_mem, w_xkv, g_xk)


def _mix_xattn_body(x_ref, ret_ref, fox_ref, wo32_ref, gx_ref, wq32_ref, gq_ref, k_ref, v_ref, wxo32_ref,
                    o_ref, wo_ref, wq_ref, wxo_ref, *, width, xd, sub):
    @pl.when((pl.program_id(0) == 0) & (pl.program_id(1) == 0))
    def _():
        wo_ref[...] = wo32_ref[...].astype(_BF16)
        wq_ref[...] = wq32_ref[...].astype(_BF16)
        wxo_ref[...] = wxo32_ref[...].astype(_BF16)

    nt = (((1,), (1,)), ((), ()))
    head_slices = [slice(h * xd, (h + 1) * xd) for h in range(_N_XHEADS)]

    def front(rows):
        h1 = (x_ref[0, rows, :]
              + jnp.dot(ret_ref[0, rows, :], wo_ref[:width, :], preferred_element_type=_F32)
              + jnp.dot(fox_ref[0, rows, :], wo_ref[width:, :], preferred_element_type=_F32))
        hn, inv_h = _rms_split(h1, gx_ref[...])
        y = jnp.dot(hn, wq_ref[...], preferred_element_type=_F32)
        logits = []
        for sl in head_slices:
            yh = y[:, sl]
            inv_q = lax.rsqrt(inv_h * inv_h * jnp.mean(yh * yh, axis=-1, keepdims=True) + _EPS)
            logits.append(lax.dot_general((yh * gq_ref[...]).astype(_BF16), k_ref[0, :, sl], nt,
                                          preferred_element_type=_F32) * (inv_q * inv_h * _LOG2E))
        return h1, logits

    def back(rows, h1, logits):
        probs = []
        for lg in logits:
            p = jnp.exp2(lg - jnp.max(lg, axis=-1, keepdims=True))
            probs.append((p / jnp.sum(p, axis=-1, keepdims=True)).astype(_BF16))
        heads = [jnp.dot(p, v_ref[0, :, sl], preferred_element_type=_F32).astype(_BF16)
                 for p, sl in zip(probs, head_slices)]
        o = jnp.concatenate(heads, axis=-1)
        o_ref[0, rows, :] = h1 + jnp.dot(o, wxo_ref[...], preferred_element_type=_F32)

    tm = x_ref.shape[1]
    subs = [slice(r0, r0 + sub) for r0 in range(0, tm, sub)]
    fronts = [front(rows) for rows in subs]
    for rows, (h1, logits) in zip(subs, fronts):
        back(rows, h1, logits)


def _mix_xattn(x, ret, fox, w_out, g_xattn, w_xq, g_xq, k, v, w_xo, *, tm, sub):
    B, T, D = x.shape
    W = ret.shape[-1]
    M = k.shape[1]
    xd = D // _N_XHEADS
    assert tm % sub == 0 and T % tm == 0
    tok = lambda b, i: (b, i, 0)
    return pl.pallas_call(
        functools.partial(_mix_xattn_body, width=W, xd=xd, sub=sub),
        grid=(B, T // tm),
        in_specs=[
            pl.BlockSpec((1, tm, D), tok),
            pl.BlockSpec((1, tm, W), tok),
            pl.BlockSpec((1, tm, W), tok),
            _const_spec(w_out.shape),
            _const_spec((1, D)),
            _const_spec(w_xq.shape),
            _const_spec((1, xd)),
            pl.BlockSpec((1, M, D), lambda b, i: (b, 0, 0)),
            pl.BlockSpec((1, M, D), lambda b, i: (b, 0, 0)),
            _const_spec(w_xo.shape),
        ],
        out_specs=pl.BlockSpec((1, tm, D), tok),
        out_shape=jax.ShapeDtypeStruct((B, T, D), _F32),
        scratch_shapes=[pltpu.VMEM(w.shape, _BF16) for w in (w_out, w_xq, w_xo)],
        compiler_params=_params(("arbitrary", "arbitrary")),
        name="mix_xattn",
    )(x, ret, fox, w_out, g_xattn, w_xq, g_xq, k, v, w_xo)


def _ffn_chunks(d_ff, n_chunks):
    tiles = -(-d_ff // _MXU_TILE)
    bounds = [min(d_ff, _MXU_TILE * ((tiles * c) // n_chunks)) for c in range(n_chunks)] + [d_ff]
    return [(lo, hi) for lo, hi in zip(bounds[:-1], bounds[1:]) if hi > lo]


def _ffn_body(h_ref, g_ref, wg32_ref, wu32_ref, wd32_ref, o_ref, wg_ref, wu_ref, wd_ref, acc0_ref,
              *, n_cast, ck, chunks):
    step = pl.program_id(0)

    def swiglu(hn, inv_rms, sl):
        gate = jnp.dot(hn, wg_ref[:, sl], preferred_element_type=_F32) * inv_rms
        up = jnp.dot(hn, wu_ref[:, sl], preferred_element_type=_F32) * inv_rms
        a = (gate * jax.nn.sigmoid(gate) * up).astype(_BF16)
        return jnp.dot(a, wd_ref[sl, :], preferred_element_type=_F32)

    for c in range(n_cast):
        @pl.when(step == c)
        def _(c=c):
            sl = slice(c * ck, (c + 1) * ck)
            wg_ref[:, sl] = wg32_ref[...].astype(_BF16)
            wu_ref[:, sl] = wu32_ref[...].astype(_BF16)
            wd_ref[sl, :] = wd32_ref[...].astype(_BF16)
            h = h_ref[0]
            part = swiglu(*_rms_split(h, g_ref[...]), sl)
            if c == 0:
                acc0_ref[...] = h + part
            elif c < n_cast - 1:
                acc0_ref[...] += part
            else:
                o_ref[0] = acc0_ref[...] + part

    @pl.when(step >= n_cast)
    def _():
        h = h_ref[0]
        hn, inv_rms = _rms_split(h, g_ref[...])
        acc = h
        for lo, hi in chunks:
            acc = acc + swiglu(hn, inv_rms, slice(lo, hi))
        o_ref[0] = acc


def _ffn(h, g_ffn, w_gate, w_up, w_down, *, tm, n_split):
    B, T, D = h.shape
    F = w_gate.shape[1]
    ck = _MXU_TILE
    assert F % ck == 0 and T % tm == 0
    n_cast = F // ck
    n_tiles = B * T // tm
    tile = lambda s: (jnp.maximum(s - (n_cast - 1), 0), 0, 0)
    chunk = lambda s: jnp.minimum(s, n_cast - 1)
    tok = pl.BlockSpec((1, tm, D), tile)
    out = pl.pallas_call(
        functools.partial(_ffn_body, n_cast=n_cast, ck=ck, chunks=_ffn_chunks(F, n_split)),
        grid=(n_cast - 1 + n_tiles,),
        in_specs=[tok, _const_spec((1, D)),
                  pl.BlockSpec((D, ck), lambda s: (0, chunk(s))),
                  pl.BlockSpec((D, ck), lambda s: (0, chunk(s))),
                  pl.BlockSpec((ck, D), lambda s: (chunk(s), 0))],
        out_specs=tok,
        out_shape=jax.ShapeDtypeStruct((n_tiles, tm, D), _F32),
        scratch_shapes=[pltpu.VMEM((D, F), _BF16), pltpu.VMEM((D, F), _BF16), pltpu.VMEM((F, D), _BF16),
                        pltpu.VMEM((tm, D), _F32)],
        compiler_params=_params(("arbitrary",)),
        name="ffn",
    )(h.reshape(n_tiles, tm, D), g_ffn, w_gate, w_up, w_down)
    return out.reshape(B, T, D)


def _rope_tables(T):
    half = _HEAD_DIM // 2
    inv_freq = (_ROPE_BASE ** (-np.arange(0, _HEAD_DIM, 2, dtype=np.float32) / _HEAD_DIM)).astype(np.float32)
    ang = (np.arange(T, dtype=np.float32)[:, None] * inv_freq[None, :]).astype(np.float32).astype(np.float64)
    cos, sin = np.cos(ang), np.sin(ang)
    reps = _LANES // _HEAD_DIM
    cos_t = np.tile(np.concatenate([cos, cos], axis=1), (1, reps))
    sin_t = np.tile(np.concatenate([-sin, sin], axis=1), (1, reps))
    return jnp.asarray(cos_t, _F32), jnp.asarray(sin_t, _F32)


def _retention_tables(tb):
    log_g = np.log(1.0 - 2.0 ** (-5.0 - np.arange(_N_HEADS, dtype=np.float64)))
    idx = np.arange(tb, dtype=np.float64)
    dist = np.abs(idx[:, None] - idx[None, :])
    chunk = np.arange(tb) // _RET_CHUNK
    visible = chunk[None, :] <= chunk[:, None]
    dmask = np.where(visible[None], np.exp(log_g[:, None, None] * dist[None]), 0.0)
    qdec = np.repeat(np.exp(log_g[None, :] * (idx[:, None] + 1.0)), _HEAD_DIM, axis=1)
    kdec = np.repeat(np.exp(log_g[None, :] * (tb - 1.0 - idx[:, None])), _HEAD_DIM, axis=1)
    heads_per_group = _RET_GROUP // _HEAD_DIM
    head_of = np.arange(_RET_GROUP) // _HEAD_DIM
    bd = (head_of[:, None] == head_of[None, :]).astype(np.float64)
    step_decay = np.exp(log_g * tb).reshape(-1, heads_per_group)
    sdec = bd[None] * np.repeat(step_decay, _HEAD_DIM, axis=1)[:, None, :]
    f = lambda a: jnp.asarray(a, _F32)
    return f(dmask), f(qdec), f(kdec), f(sdec), f(bd), jnp.asarray(bd / _HEAD_DIM, _BF16)


def _pad_lanes(a):
    return jnp.pad(a, [(0, 0)] * (a.ndim - 1) + [(0, _LANES - a.shape[-1])])


def kernel(x, mem, g_mix, w_in, b_forget, g_ret_out, g_fox_q, g_fox_k, w_out, g_xattn, w_xq, w_xkv,
           g_mem, g_xq, g_xk, w_xo, g_ffn, w_gate, w_up, w_down):
    B, T, D = x.shape
    width = _N_HEADS * _HEAD_DIM
    tb, tq, tm_mix, tm_ffn = 256, 512, 1024, 512
    cos_t, sin_t = _rope_tables(T)
    ret_tables = _retention_tables(tb)
    tri = jnp.asarray(np.tril(np.ones((tq, tq), np.float32)), _BF16)
    row = lambda a: a.reshape(1, -1).astype(_F32)

    h = x
    for l in range(w_in.shape[0]):
        b_ff = _pad_lanes(row(b_forget[l]))
        gq = jnp.tile(row(g_fox_q[l]), (1, _N_HEADS)) * (_LOG2E * _HEAD_DIM ** -0.5)
        gk = jnp.tile(row(g_fox_k[l]), (1, _N_HEADS))
        bound = (_HEAD_DIM ** 0.5 * _NORM_ROUNDING_SLACK) * jnp.max(jnp.abs(g_fox_q[l])) * jnp.max(
            jnp.abs(g_fox_k[l]))
        use_shift = bound <= _MAX_FIXED_SHIFT
        shift = jnp.where(use_shift, jnp.ceil(bound * (4.0 * _LOG2E)) * 0.25, 0.0).astype(_F32)
        ret, qa, ka, va, fend = _in_proj(h, row(g_mix[l]), w_in[l].T, b_ff, cos_t, sin_t, gq, gk,
                                         jnp.full((1, _LANES), shift), tri, ret_tables,
                                         row(g_ret_out[l]), tm=tq, tb=tb)
        f_ends = fend[:, :, ::8 // _KSPLIT, :_N_HEADS].reshape(B, -1, _N_HEADS)
        jlo = _first_live_block(f_ends)
        fox = lax.cond(use_shift,
                       lambda jlo, qa, ka, va: _fox_shifted(jlo, qa, ka, va, tq=tq, hg=4, unroll=12),
                       lambda jlo, qa, ka, va: _fox_online(qa, ka, va, tq=tq, hg=4), jlo, qa, ka, va)
        k, v = _mem_kv(mem, row(g_mem[l]), w_xkv[l], row(g_xk[l]))
        h = _mix_xattn(h, ret, fox, w_out[l], row(g_xattn[l]), w_xq[l], row(g_xq[l]), k, v, w_xo[l],
                       tm=tm_mix, sub=256)
        h = _ffn(h, row(g_ffn[l]), w_gate[l], w_up[l], w_down[l], tm=tm_ffn, n_split=2)
    return h
```
